```python
import jax, jax.numpy as jnp
from jax import lax
import numpy as np

D_MODEL = 2048
BATCH = 2
SEQ = 8192
DEPTH = 1

HEAD_DIM = 128
DIL_CONFIGS = ((128, 1), (512, 4), (2048, 16))
DIL_HEADS_PER_GROUP = 4
DIL_HEADS = DIL_HEADS_PER_GROUP * len(DIL_CONFIGS)
DIL_WIDTH = DIL_HEADS * HEAD_DIM
DIL_OUT = DIL_HEADS_PER_GROUP * HEAD_DIM
MOBA_HEADS = 8
MOBA_WIDTH = MOBA_HEADS * HEAD_DIM
MOBA_BLOCK = 256
MOBA_TOPK = 3
MOBA_QCHUNK = 32
IN_WIDTH = 3 * (DIL_WIDTH + MOBA_WIDTH)
IN_SPLITS = (DIL_WIDTH, 2 * DIL_WIDTH, 3 * DIL_WIDTH,
             3 * DIL_WIDTH + MOBA_WIDTH, 3 * DIL_WIDTH + 2 * MOBA_WIDTH)
MEM_LEN = 256
MEM_HEADS = 4
MEM_WIDTH = MEM_HEADS * HEAD_DIM
N_GROUPS = 4
EXPERTS_PER_GROUP = 8
N_EXPERTS = N_GROUPS * EXPERTS_PER_GROUP
EXPERT_FF = D_MODEL // 8
MOE_TOPK = 2
RMS_EPS = 1e-6

kernel_name = "hybrid_dilated_moba_hmoe_block"


def rms_norm(x, g):
    xf = x.astype(jnp.float32)
    y = xf * lax.rsqrt(jnp.mean(xf * xf, axis=-1, keepdims=True) + RMS_EPS)
    return (y * g.astype(jnp.float32)).astype(x.dtype)


def alibi_slopes(n):
    return jnp.asarray(2.0 ** (-8.0 * np.arange(1, n + 1) / n), dtype=jnp.float32)


def dilated_group_attention(q, k, v, window, dilation, slopes):
    B, S, H, hd = q.shape
    n = window // dilation
    L = S // dilation
    nb = -(-L // n)
    lp = nb * n

    def to_sub(t):
        t = t.reshape(B, L, dilation, H, hd).transpose(0, 2, 3, 1, 4)
        t = jnp.pad(t, ((0, 0), (0, 0), (0, 0), (0, lp - L), (0, 0)))
        return t.reshape(B, dilation, H, nb, n, hd)

    def with_prev(t):
        prev = jnp.pad(t, ((0, 0), (0, 0), (0, 0), (1, 0), (0, 0), (0, 0)))[:, :, :, :-1]
        return jnp.concatenate([prev, t], axis=4)

    qs = to_sub(q)
    kk = with_prev(to_sub(k))
    vv = with_prev(to_sub(v))
    s = jnp.einsum("bdhnqe,bdhnke->bdhnqk", qs, kk).astype(jnp.float32) * (hd ** -0.5)
    qi = jnp.arange(n)[:, None] + n
    kj = jnp.arange(2 * n)[None, :]
    delta = qi - kj
    key_sub = jnp.arange(nb)[:, None, None] * n + kj - n
    valid = (delta >= 0) & (delta <= n) & (key_sub >= 0)
    bias = -slopes[:, None, None, None] * (delta * dilation).astype(jnp.float32)
    s = jnp.where(valid, s + bias, -jnp.inf)
    m = jnp.max(s, axis=-1, keepdims=True)
    p = jnp.exp(s - m)
    den = jnp.sum(p, axis=-1, keepdims=True)
    o = jnp.einsum("bdhnqk,bdhnke->bdhnqe", p, vv.astype(jnp.float32)) / den
    lse = (m + jnp.log(den))[..., 0]
    o = o.reshape(B, dilation, H, lp, hd)[:, :, :, :L].transpose(0, 3, 1, 2, 4).reshape(B, S, H, hd)
    lse = lse.reshape(B, dilation, H, lp)[..., :L].transpose(0, 3, 1, 2).reshape(B, S, H)
    return o, lse


def dilated_mixture_attention(q, k, v):
    B, S = q.shape[0], q.shape[1]
    slopes = alibi_slopes(DIL_HEADS)
    outs, lses = [], []
    for g, (window, dilation) in enumerate(DIL_CONFIGS):
        sl = slice(g * DIL_HEADS_PER_GROUP, (g + 1) * DIL_HEADS_PER_GROUP)
        o, lse = dilated_group_attention(q[:, :, sl], k[:, :, sl], v[:, :, sl], window, dilation, slopes[sl])
        outs.append(o)
        lses.append(lse)
    w = jax.nn.softmax(jnp.stack(lses, axis=0), axis=0)
    o = jnp.einsum("gbsh,gbshe->bshe", w, jnp.stack(outs, axis=0))
    return o.astype(q.dtype).reshape(B, S, DIL_OUT)


def moba_attention(q, k, v):
    B, S, H, hd = q.shape
    slopes = alibi_slopes(H)
    nblk = -(-S // MOBA_BLOCK)
    sp = nblk * MOBA_BLOCK
    topk = min(MOBA_TOPK, nblk)
    n_sel = topk * MOBA_BLOCK
    pad = ((0, 0), (0, sp - S), (0, 0), (0, 0))
    qh = jnp.pad(q, pad).transpose(0, 2, 1, 3)
    kb = jnp.pad(k, pad).transpose(0, 2, 1, 3).reshape(B, H, nblk, MOBA_BLOCK, hd)
    vb = jnp.pad(v, pad).transpose(0, 2, 1, 3).reshape(B, H, nblk, MOBA_BLOCK, hd)
    kmean = jnp.mean(kb.astype(jnp.float32), axis=3)
    bi = jnp.arange(B)[:, None, None, None]
    hi = jnp.arange(H)[None, :, None, None]
    blk_ids = jnp.arange(nblk)
    key_off = jnp.arange(MOBA_BLOCK)
    scale = hd ** -0.5

    def one_chunk(start):
        qc = lax.dynamic_slice_in_dim(qh, start, MOBA_QCHUNK, axis=2)
        t = start + jnp.arange(MOBA_QCHUNK)
        own = start // MOBA_BLOCK
        gate = jnp.einsum("bhqe,bhne->bhqn", qc.astype(jnp.float32), kmean)
        gate = jnp.where(blk_ids < own, gate, -jnp.inf)
        _, idx = lax.top_k(gate, topk)
        sel_ok = jnp.arange(topk) < own
        ksel = kb[bi, hi, idx]
        vsel = vb[bi, hi, idx]
        s_sel = jnp.einsum("bhqe,bhqkje->bhqkj", qc, ksel).astype(jnp.float32) * scale
        dist_sel = (t[:, None, None] - (idx[..., None] * MOBA_BLOCK + key_off)).astype(jnp.float32)
        s_sel = jnp.where(sel_ok[:, None], s_sel - slopes[:, None, None, None] * dist_sel, -jnp.inf)
        kown = lax.dynamic_index_in_dim(kb, own, axis=2, keepdims=False)
        vown = lax.dynamic_index_in_dim(vb, own, axis=2, keepdims=False)
        dist_own = t[:, None] - (own * MOBA_BLOCK + key_off)[None, :]
        s_own = jnp.einsum("bhqe,bhje->bhqj", qc, kown).astype(jnp.float32) * scale
        s_own = jnp.where(dist_own >= 0, s_own - slopes[:, None, None] * dist_own.astype(jnp.float32), -jnp.inf)
        p = jax.nn.softmax(jnp.concatenate([s_sel.reshape(B, H, MOBA_QCHUNK, n_sel), s_own], axis=-1), axis=-1)
        o = (jnp.einsum("bhqk,bhqke->bhqe", p[..., :n_sel],
                        vsel.reshape(B, H, MOBA_QCHUNK, n_sel, hd).astype(jnp.float32))
             + jnp.einsum("bhqj,bhje->bhqe", p[..., n_sel:], vown.astype(jnp.float32)))
        return o.astype(q.dtype)

    starts = jnp.arange(sp // MOBA_QCHUNK) * MOBA_QCHUNK
    o = lax.map(one_chunk, starts)
    o = o.transpose(1, 0, 3, 2, 4).reshape(B, sp, H, hd)[:, :S]
    return o.reshape(B, S, H * hd)


def memory_cross_attention(h, memn, w_q, w_kv, w_o):
    B, S, _ = h.shape
    M = memn.shape[1]
    q = (h @ w_q).reshape(B, S, MEM_HEADS, HEAD_DIM)
    kv = (memn @ w_kv).reshape(B, M, 2, MEM_HEADS, HEAD_DIM)
    s = jnp.einsum("bshe,bmhe->bhsm", q, kv[:, :, 0]).astype(jnp.float32) * (HEAD_DIM ** -0.5)
    p = jax.nn.softmax(s, axis=-1)
    o = jnp.einsum("bhsm,bmhe->bshe", p, kv[:, :, 1].astype(jnp.float32)).astype(h.dtype)
    return o.reshape(B, S, MEM_WIDTH) @ w_o


def hierarchical_moe(h, w_rg, b_rg, w_re, b_re, w_gate, w_up, w_down):
    B, S, D = h.shape
    T = B * S
    t = h.reshape(T, D)
    glog = (t @ w_rg).astype(jnp.float32) + b_rg
    gsel = jnp.argmax(glog, axis=-1)
    pg = jnp.take_along_axis(jax.nn.softmax(glog, axis=-1), gsel[:, None], axis=1)
    elog = ((t @ w_re).astype(jnp.float32) + b_re).reshape(T, N_GROUPS, EXPERTS_PER_GROUP)
    elog_g = jnp.take_along_axis(elog, gsel[:, None, None], axis=1)[:, 0]
    top_l, top_i = lax.top_k(elog_g, MOE_TOPK)
    top_w = jax.nn.softmax(top_l, axis=-1) * pg
    comb = jnp.einsum("tk,tke->te", top_w, jax.nn.one_hot(top_i, EXPERTS_PER_GROUP, dtype=jnp.float32))
    y = jnp.zeros((T, D), dtype=h.dtype)
    for g in range(N_GROUPS):
        sl = slice(g * EXPERTS_PER_GROUP, (g + 1) * EXPERTS_PER_GROUP)
        wg = jnp.where((gsel == g)[:, None], comb, 0.0).astype(h.dtype)
        a = jax.nn.silu(jnp.einsum("td,edf->tef", t, w_gate[sl])) * jnp.einsum("td,edf->tef", t, w_up[sl])
        y = y + jnp.einsum("tef,efd->td", a * wg[:, :, None], w_down[sl])
    return y.reshape(B, S, D)


def setup_inputs(seed: int = 0) -> dict:
    key = jax.random.key(seed)
    ks = jax.random.split(key, 24)
    f32 = jnp.float32

    def dense(k, shape, fan_in):
        return jax.random.normal(k, shape, f32) * (fan_in ** -0.5)

    def gain(k, shape):
        return 1.0 + 0.02 * jax.random.normal(k, shape, f32)

    L = DEPTH
    D = D_MODEL
    return {
        "x": jax.random.normal(ks[0], (BATCH, SEQ, D), f32),
        "mem": jax.random.normal(ks[1], (BATCH, MEM_LEN, D), f32),
        "attn_norm": gain(ks[2], (L, D)),
        "w_in": dense(ks[3], (L, D, IN_WIDTH), D),
        "w_up_dil": dense(ks[4], (L, DIL_OUT, D), DIL_OUT),
        "w_up_moba": dense(ks[5], (L, MOBA_WIDTH, D), MOBA_WIDTH),
        "w_branch_gate": dense(ks[6], (L, D, 2 * D), D),
        "w_out": dense(ks[7], (L, D, D), D),
        "cross_norm": gain(ks[8], (L, D)),
        "mem_norm": gain(ks[9], (L, D)),
        "w_q_mem": dense(ks[10], (L, D, MEM_WIDTH), D),
        "w_kv_mem": dense(ks[11], (L, D, 2 * MEM_WIDTH), D),
        "w_o_mem": dense(ks[12], (L, MEM_WIDTH, D), MEM_WIDTH),
        "ffn_norm": gain(ks[13], (L, D)),
        "w_router_group": dense(ks[14], (L, D, N_GROUPS), D),
        "b_router_group": 0.01 * jax.random.normal(ks[15], (L, N_GROUPS), f32),
        "w_router_expert": dense(ks[16], (L, D, N_EXPERTS), D),
        "b_router_expert": 0.01 * jax.random.normal(ks[17], (L, N_EXPERTS), f32),
        "w_expert_gate": dense(ks[18], (L, N_EXPERTS, D, EXPERT_FF), D),
        "w_expert_up": dense(ks[19], (L, N_EXPERTS, D, EXPERT_FF), D),
        "w_expert_down": dense(ks[20], (L, N_EXPERTS, EXPERT_FF, D), EXPERT_FF),
        "final_norm": gain(ks[21], (D,)),
    }


def reference(x, mem, attn_norm, w_in, w_up_dil, w_up_moba, w_branch_gate, w_out,
              cross_norm, mem_norm, w_q_mem, w_kv_mem, w_o_mem, ffn_norm,
              w_router_group, b_router_group, w_router_expert, b_router_expert,
              w_expert_gate, w_expert_up, w_expert_down, final_norm):
    B, S, D = x.shape
    for l in range(DEPTH):
        h = rms_norm(x, attn_norm[l])
        proj = h @ w_in[l]
        qa, ka, va, qb, kb, vb = jnp.split(proj, IN_SPLITS, axis=-1)
        y_dil = dilated_mixture_attention(qa.reshape(B, S, DIL_HEADS, HEAD_DIM),
                                          ka.reshape(B, S, DIL_HEADS, HEAD_DIM),
                                          va.reshape(B, S, DIL_HEADS, HEAD_DIM))
        y_moba = moba_attention(qb.reshape(B, S, MOBA_HEADS, HEAD_DIM),
                                kb.reshape(B, S, MOBA_HEADS, HEAD_DIM),
                                vb.reshape(B, S, MOBA_HEADS, HEAD_DIM))
        gates = jax.nn.sigmoid((h @ w_branch_gate[l]).astype(jnp.float32)).astype(h.dtype)
        g_dil, g_moba = jnp.split(gates, 2, axis=-1)
        merged = g_dil * (y_dil @ w_up_dil[l]) + g_moba * (y_moba @ w_up_moba[l])
        x = x + merged @ w_out[l]
        x = x + memory_cross_attention(rms_norm(x, cross_norm[l]), rms_norm(mem, mem_norm[l]),
                                       w_q_mem[l], w_kv_mem[l], w_o_mem[l])
        x = x + hierarchical_moe(rms_norm(x, ffn_norm[l]), w_router_group[l], b_router_group[l],
                                 w_router_expert[l], b_router_expert[l],
                                 w_expert_gate[l], w_expert_up[l], w_expert_down[l])
    return rms_norm(x, final_norm)
```

```python
import functools

import numpy as np
import jax
import jax.numpy as jnp
from jax import lax
from jax.experimental import pallas as pl
from jax.experimental.pallas import tpu as pltpu

F32 = jnp.float32
BF16 = jnp.bfloat16

HEAD_DIM = 128
DIL_CONFIGS = ((128, 1), (512, 4), (2048, 16))
DIL_HEADS_PER_GROUP = 4
DIL_HEADS = DIL_HEADS_PER_GROUP * len(DIL_CONFIGS)
DIL_WIDTH = DIL_HEADS * HEAD_DIM
DIL_OUT = DIL_HEADS_PER_GROUP * HEAD_DIM
DIL_STEPS = 128
MOBA_HEADS = 8
MOBA_WIDTH = MOBA_HEADS * HEAD_DIM
MOBA_BLOCK = 256
MOBA_TOPK = 3
IN_WIDTH = 3 * (DIL_WIDTH + MOBA_WIDTH)
MEM_HEADS = 4
MEM_WIDTH = MEM_HEADS * HEAD_DIM
N_GROUPS = 4
EXPERTS_PER_GROUP = 8
N_EXPERTS = N_GROUPS * EXPERTS_PER_GROUP
MOE_TOPK = 2
RMS_EPS = 1e-6
SCALE = HEAD_DIM ** -0.5
NEG_INF = float("-inf")

VMEM_LIMIT_BYTES = 56 * 1024 * 1024
_NT = (((1,), (1,)), ((), ()))


def _alibi_slopes(n):
    return jnp.asarray(2.0 ** (-8.0 * np.arange(1, n + 1) / n), dtype=F32)


def _rms(x, g):
    return x * lax.rsqrt(jnp.mean(x * x, axis=-1, keepdims=True) + RMS_EPS) * g


def _params(*sem):
    return pltpu.CompilerParams(dimension_semantics=sem, vmem_limit_bytes=VMEM_LIMIT_BYTES)


def _resident(shape):
    nd = len(shape)
    return pl.BlockSpec(shape, lambda *_: (0,) * nd, pipeline_mode=pl.Buffered(1))


def _norm_matmul_body(x_ref, g_ref, w_ref, o_ref, h_ref, *, sigmoid):
    @pl.when(pl.program_id(1) == 0)
    def _():
        h_ref[...] = _rms(x_ref[...], g_ref[...]).astype(BF16)

    acc = jnp.dot(h_ref[...], w_ref[...], preferred_element_type=F32)
    if sigmoid:
        acc = jax.nn.sigmoid(acc)
    o_ref[...] = acc.astype(o_ref.dtype)


def _norm_matmul(x, g, w, *, tm, tn, sigmoid=False, name):
    m, d = x.shape
    n = w.shape[1]
    assert m % tm == 0 and n % tn == 0
    return pl.pallas_call(
        functools.partial(_norm_matmul_body, sigmoid=sigmoid),
        grid=(m // tm, n // tn),
        in_specs=[
            pl.BlockSpec((tm, d), lambda i, j: (i, 0)),
            pl.BlockSpec((1, d), lambda i, j: (0, 0)),
            pl.BlockSpec((d, tn), lambda i, j: (0, j)),
        ],
        out_specs=pl.BlockSpec((tm, tn), lambda i, j: (i, j)),
        out_shape=jax.ShapeDtypeStruct((m, n), BF16),
        scratch_shapes=[pltpu.VMEM((tm, d), BF16)],
        compiler_params=_params("parallel", "arbitrary"),
        name=name,
    )(x, g.reshape(1, d), w)


def _dilated_body(slope_ref, q_ref, kp_ref, kc_ref, vp_ref, vc_ref, o_ref, lse_ref,
                  *, dilation, group):
    n = DIL_STEPS
    qi = lax.broadcasted_iota(jnp.int32, (n, n), 0)
    kj = lax.broadcasted_iota(jnp.int32, (n, n), 1)
    steps_cur = qi - kj
    valid_cur = steps_cur >= 0
    prev_limit = jnp.where(pl.program_id(2) > 0, 0, -n)
    valid_prev = steps_cur <= prev_limit
    dist_cur = (steps_cur * dilation).astype(F32)
    dist_prev = ((steps_cur + n) * dilation).astype(F32)
    for h in range(DIL_HEADS_PER_GROUP):
        cols = slice(h * HEAD_DIM, (h + 1) * HEAD_DIM)
        slope = slope_ref[group * DIL_HEADS_PER_GROUP + h]
        q = q_ref[:, cols]
        s_cur = lax.dot_general(q, kc_ref[:, cols], _NT, preferred_element_type=F32) * SCALE
        s_prev = lax.dot_general(q, kp_ref[:, cols], _NT, preferred_element_type=F32) * SCALE
        s_cur = jnp.where(valid_cur, s_cur - slope * dist_cur, NEG_INF)
        s_prev = jnp.where(valid_prev, s_prev - slope * dist_prev, NEG_INF)
        m = jnp.maximum(jnp.max(s_cur, axis=-1, keepdims=True),
                        jnp.max(s_prev, axis=-1, keepdims=True))
        p_cur = jnp.exp(s_cur - m)
        p_prev = jnp.exp(s_prev - m)
        den = jnp.sum(p_cur, axis=-1, keepdims=True) + jnp.sum(p_prev, axis=-1, keepdims=True)
        o = (jnp.dot(p_cur.astype(BF16), vc_ref[:, cols], preferred_element_type=F32)
             + jnp.dot(p_prev.astype(BF16), vp_ref[:, cols], preferred_element_type=F32))
        o_ref[:, cols] = o / den
        lse_ref[:, cols] = jnp.broadcast_to(m + jnp.log(den), (n, HEAD_DIM))


def _dilated_group(proj, slopes, group, dilation):
    b, s, _ = proj.shape
    n = DIL_STEPS
    sub_len = s // dilation
    assert s % dilation == 0 and sub_len % n == 0
    nb = sub_len // n
    view = proj.reshape(b, sub_len, dilation * IN_WIDTH)
    cb = IN_WIDTH // DIL_OUT
    kb = DIL_WIDTH // DIL_OUT
    blk = (None, n, DIL_OUT)

    def spec(section, prev):
        def index(bi, r, i):
            return (bi, jnp.maximum(i - 1, 0) if prev else i, r * cb + section * kb + group)
        return pl.BlockSpec(blk, index)

    out_spec = pl.BlockSpec(blk, lambda bi, r, i: (bi, i, r))
    out_sds = jax.ShapeDtypeStruct((b, sub_len, dilation * DIL_OUT), F32)
    o, lse = pl.pallas_call(
        functools.partial(_dilated_body, dilation=dilation, group=group),
        grid=(b, dilation, nb),
        in_specs=[pl.BlockSpec(memory_space=pltpu.SMEM),
                  spec(0, False), spec(1, True), spec(1, False), spec(2, True), spec(2, False)],
        out_specs=[out_spec, out_spec],
        out_shape=[out_sds, out_sds],
        compiler_params=_params("parallel", "parallel", "arbitrary"),
        name=f"dilated_attn_g{group}",
    )(slopes, view, view, view, view, view)
    return o.reshape(b * s, DIL_OUT), lse.reshape(b * s, DIL_OUT)


def _moba_body(slope_ref, q_ref, k_ref, v_ref, o_ref, kmean_ref, *, nblk):
    blk = MOBA_BLOCK
    head = pl.program_id(1)
    own = pl.program_id(2)

    @pl.when(own == 0)
    def _():
        def fill(jb, carry):
            rows = k_ref[pl.ds(pl.multiple_of(jb * blk, blk), blk), :].astype(F32)
            kmean_ref[pl.ds(jb, 1), :] = jnp.mean(rows, axis=0, keepdims=True)
            return carry
        lax.fori_loop(0, nblk, fill, 0)

    q = q_ref[...]
    slope = slope_ref[head]

    gate = lax.dot_general(q.astype(F32), kmean_ref[...], _NT,
                           precision=lax.Precision.HIGHEST, preferred_element_type=F32)
    blk_id = lax.broadcasted_iota(jnp.int32, (blk, nblk), 1).astype(F32)
    gate = jnp.where(blk_id < own.astype(F32), gate, NEG_INF)
    sel = jnp.zeros((blk, nblk), F32)
    for _ in range(MOBA_TOPK):
        best = jnp.max(gate, axis=-1, keepdims=True)
        is_best = (gate == best) & (gate > NEG_INF)
        pick = jnp.min(jnp.where(is_best, blk_id, float(nblk)), axis=-1, keepdims=True)
        picked = blk_id == pick
        sel = jnp.where(picked, 1.0, sel)
        gate = jnp.where(picked, NEG_INF, gate)

    row = lax.broadcasted_iota(jnp.int32, (blk, blk), 0)
    col = lax.broadcasted_iota(jnp.int32, (blk, blk), 1)
    rel = (row - col).astype(F32)

    def block_scores(j):
        start = pl.multiple_of(j * blk, blk)
        s = lax.dot_general(q, k_ref[pl.ds(start, blk), :], _NT, preferred_element_type=F32) * SCALE
        return s, v_ref[pl.ds(start, blk), :]

    s, v_own = block_scores(own)
    s = jnp.where(rel >= 0, s - slope * rel, NEG_INF)
    m0 = jnp.max(s, axis=-1, keepdims=True)
    p = jnp.exp(s - m0)
    l0 = jnp.sum(p, axis=-1, keepdims=True)
    acc0 = jnp.dot(p.astype(BF16), v_own, preferred_element_type=F32)

    def past_block(j, carry):
        m, l, acc = carry
        s, v_j = block_scores(j)
        dist = rel + ((own - j) * blk).astype(F32)
        chosen = jnp.sum(jnp.where(blk_id == j.astype(F32), sel, 0.0), axis=-1, keepdims=True) > 0.5
        s = jnp.where(chosen, s - slope * dist, NEG_INF)
        m_new = jnp.maximum(m, jnp.max(s, axis=-1, keepdims=True))
        alpha = jnp.exp(m - m_new)
        p = jnp.exp(s - m_new)
        l = alpha * l + jnp.sum(p, axis=-1, keepdims=True)
        acc = alpha * acc + jnp.dot(p.astype(BF16), v_j, preferred_element_type=F32)
        return m_new, l, acc

    _, l, acc = lax.fori_loop(0, own, past_block, (m0, l0, acc0))
    o_ref[...] = (acc / l).astype(o_ref.dtype)


def _moba(proj, slopes):
    b, s, _ = proj.shape
    assert s % MOBA_BLOCK == 0
    nblk = s // MOBA_BLOCK
    q0 = 3 * DIL_WIDTH // HEAD_DIM
    k0 = q0 + MOBA_HEADS
    v0 = k0 + MOBA_HEADS
    y = pl.pallas_call(
        functools.partial(_moba_body, nblk=nblk),
        grid=(b, MOBA_HEADS, nblk),
        in_specs=[
            pl.BlockSpec(memory_space=pltpu.SMEM),
            pl.BlockSpec((None, MOBA_BLOCK, HEAD_DIM), lambda bi, h, i: (bi, i, q0 + h)),
            pl.BlockSpec((None, s, HEAD_DIM), lambda bi, h, i: (bi, 0, k0 + h)),
            pl.BlockSpec((None, s, HEAD_DIM), lambda bi, h, i: (bi, 0, v0 + h)),
        ],
        out_specs=pl.BlockSpec((None, MOBA_BLOCK, HEAD_DIM), lambda bi, h, i: (bi, i, h)),
        out_shape=jax.ShapeDtypeStruct((b, s, MOBA_WIDTH), BF16),
        scratch_shapes=[pltpu.VMEM((nblk, HEAD_DIM), F32)],
        compiler_params=_params("parallel", "parallel", "arbitrary"),
        name="moba_attn",
    )(slopes, proj, proj, proj)
    return y.reshape(b * s, MOBA_WIDTH)


def _merge_body(x_ref, o0_ref, o1_ref, o2_ref, l0_ref, l1_ref, l2_ref, ym_ref, gd_ref, gm_ref,
                wud_ref, wum_ref, wo_ref, out_ref):
    l0, l1, l2 = l0_ref[...], l1_ref[...], l2_ref[...]
    m = jnp.maximum(jnp.maximum(l0, l1), l2)
    e0, e1, e2 = jnp.exp(l0 - m), jnp.exp(l1 - m), jnp.exp(l2 - m)
    den = e0 + e1 + e2
    y_dil = (e0 / den) * o0_ref[...] + (e1 / den) * o1_ref[...] + (e2 / den) * o2_ref[...]
    lift_dil = jnp.dot(y_dil.astype(BF16), wud_ref[...], preferred_element_type=F32)
    lift_moba = jnp.dot(ym_ref[...], wum_ref[...], preferred_element_type=F32)
    merged = gd_ref[...].astype(F32) * lift_dil + gm_ref[...].astype(F32) * lift_moba
    out_ref[...] = x_ref[...] + jnp.dot(merged.astype(BF16), wo_ref[...], preferred_element_type=F32)


def _merge(x, dil_outs, dil_lses, y_moba, gates, w_up_dil, w_up_moba, w_out, *, tm):
    t, d = x.shape
    assert t % tm == 0
    row = lambda w: pl.BlockSpec((tm, w), lambda i: (i, 0))
    return pl.pallas_call(
        _merge_body,
        grid=(t // tm,),
        in_specs=[row(d)] + [row(DIL_OUT)] * 6 + [
            row(MOBA_WIDTH),
            pl.BlockSpec((tm, d), lambda i: (i, 0)),
            pl.BlockSpec((tm, d), lambda i: (i, 1)),
            _resident((DIL_OUT, d)), _resident((MOBA_WIDTH, d)), _resident((d, d)),
        ],
        out_specs=row(d),
        out_shape=jax.ShapeDtypeStruct((t, d), F32),
        compiler_params=_params("parallel"),
        name="merge_out_proj",
    )(x, *dil_outs, *dil_lses, y_moba, gates, gates, w_up_dil, w_up_moba, w_out)


def _cross_body(x_ref, g_ref, wq_ref, kv_ref, wo_ref, out_ref):
    x = x_ref[...]
    h = _rms(x, g_ref[...]).astype(BF16)
    q = jnp.dot(h, wq_ref[...], preferred_element_type=F32).astype(BF16)
    heads = []
    for hd in range(MEM_HEADS):
        k = kv_ref[:, hd * HEAD_DIM:(hd + 1) * HEAD_DIM]
        v = kv_ref[:, MEM_WIDTH + hd * HEAD_DIM:MEM_WIDTH + (hd + 1) * HEAD_DIM]
        s = lax.dot_general(q[:, hd * HEAD_DIM:(hd + 1) * HEAD_DIM], k, _NT,
                            preferred_element_type=F32) * SCALE
        p = jnp.exp(s - jnp.max(s, axis=-1, keepdims=True))
        den = jnp.sum(p, axis=-1, keepdims=True)
        heads.append((jnp.dot(p.astype(BF16), v, preferred_element_type=F32) / den).astype(BF16))
    o = jnp.concatenate(heads, axis=-1)
    out_ref[...] = x + jnp.dot(o, wo_ref[...], preferred_element_type=F32)


def _cross(x, g, w_q, kv, w_o, *, batch, tm):
    t, d = x.shape
    s = t // batch
    mem_len = kv.shape[1]
    assert s % tm == 0
    per_batch = s // tm
    return pl.pallas_call(
        _cross_body,
        grid=(t // tm,),
        in_specs=[
            pl.BlockSpec((tm, d), lambda i: (i, 0)),
            _resident((1, d)),
            _resident((d, MEM_WIDTH)),
            pl.BlockSpec((None, mem_len, 2 * MEM_WIDTH), lambda i: (i // per_batch, 0, 0)),
            _resident((MEM_WIDTH, d)),
        ],
        out_specs=pl.BlockSpec((tm, d), lambda i: (i, 0)),
        out_shape=jax.ShapeDtypeStruct((t, d), F32),
        compiler_params=_params("parallel"),
        name="memory_cross_attn",
    )(x, g.reshape(1, d), w_q, kv, w_o)


def _moe_body(x_ref, g_ref, wr_ref, br_ref, wg_ref, wu_ref, wd_ref, gf_ref, out_ref,
              t_ref, comb_ref, y_ref):
    e = pl.program_id(1)
    tm = x_ref.shape[0]
    n_route = N_GROUPS + N_EXPERTS
    lane = lax.broadcasted_iota(jnp.int32, (tm, n_route), 1).astype(F32)

    @pl.when(e == 0)
    def _():
        t = _rms(x_ref[...], g_ref[...])
        t_ref[...] = t.astype(BF16)
        logits = jnp.dot(t, wr_ref[...], precision=lax.Precision.HIGHEST,
                         preferred_element_type=F32) + br_ref[...]
        is_group = lane < N_GROUPS
        glog = jnp.where(is_group, logits, NEG_INF)
        gmax = jnp.max(glog, axis=-1, keepdims=True)
        none = float(n_route)
        gsel = jnp.min(jnp.where(glog == gmax, lane, none), axis=-1, keepdims=True)
        pg = 1.0 / jnp.sum(jnp.exp(glog - gmax), axis=-1, keepdims=True)
        first = N_GROUPS + gsel * EXPERTS_PER_GROUP
        in_group = (lane >= first) & (lane < first + EXPERTS_PER_GROUP)
        elog = jnp.where(in_group, logits, NEG_INF)
        top1 = jnp.max(elog, axis=-1, keepdims=True)
        i1 = jnp.min(jnp.where(elog == top1, lane, none), axis=-1, keepdims=True)
        rest = jnp.where(lane == i1, NEG_INF, elog)
        top2 = jnp.max(rest, axis=-1, keepdims=True)
        i2 = jnp.min(jnp.where(rest == top2, lane, none), axis=-1, keepdims=True)
        e2 = jnp.exp(top2 - top1)
        w1 = pg / (1.0 + e2)
        w2 = pg * e2 / (1.0 + e2)
        comb_ref[...] = jnp.where(lane == i1, w1, jnp.where(lane == i2, w2, 0.0))
        y_ref[...] = jnp.zeros_like(y_ref)

    t = t_ref[...]
    gate = jnp.dot(t, wg_ref[...], preferred_element_type=F32)
    up = jnp.dot(t, wu_ref[...], preferred_element_type=F32)
    w_e = jnp.sum(jnp.where(lane == (N_GROUPS + e).astype(F32), comb_ref[...], 0.0),
                  axis=-1, keepdims=True)
    a = (jax.nn.silu(gate) * up) * w_e
    y_ref[...] += jnp.dot(a.astype(BF16), wd_ref[...], preferred_element_type=F32)

    @pl.when(e == N_EXPERTS - 1)
    def _():
        out_ref[...] = _rms(x_ref[...] + y_ref[...], gf_ref[...])


def _moe(x, g, w_route, b_route, w_gate, w_up, w_down, g_final, *, tm):
    t, d = x.shape
    ff = w_gate.shape[-1]
    n_route = N_GROUPS + N_EXPERTS
    assert t % tm == 0
    return pl.pallas_call(
        _moe_body,
        grid=(t // tm, N_EXPERTS),
        in_specs=[
            pl.BlockSpec((tm, d), lambda i, e: (i, 0)),
            pl.BlockSpec((1, d), lambda i, e: (0, 0)),
            pl.BlockSpec((d, n_route), lambda i, e: (0, 0)),
            pl.BlockSpec((1, n_route), lambda i, e: (0, 0)),
            pl.BlockSpec((None, d, ff), lambda i, e: (e, 0, 0)),
            pl.BlockSpec((None, d, ff), lambda i, e: (e, 0, 0)),
            pl.BlockSpec((None, ff, d), lambda i, e: (e, 0, 0)),
            pl.BlockSpec((1, d), lambda i, e: (0, 0)),
        ],
        out_specs=pl.BlockSpec((tm, d), lambda i, e: (i, 0)),
        out_shape=jax.ShapeDtypeStruct((t, d), F32),
        scratch_shapes=[pltpu.VMEM((tm, d), BF16), pltpu.VMEM((tm, n_route), F32),
                        pltpu.VMEM((tm, d), F32)],
        compiler_params=_params("parallel", "arbitrary"),
        name="hier_moe_final_norm",
    )(x, g.reshape(1, d), w_route, b_route.reshape(1, n_route), w_gate, w_up, w_down,
      g_final.reshape(1, d))


def _layer(x, mem, attn_norm, w_in, w_up_dil, w_up_moba, w_branch_gate, w_out, cross_norm,
           mem_norm, w_q_mem, w_kv_mem, w_o_mem):
    b, s, d = x.shape
    t = b * s
    xt = x.reshape(t, d)
    proj = _norm_matmul(xt, attn_norm, w_in.astype(BF16), tm=min(1024, t), tn=512,
                        name="norm_in_proj").reshape(b, s, IN_WIDTH)
    gates = _norm_matmul(xt, attn_norm, w_branch_gate.astype(BF16), tm=min(1024, t), tn=512,
                         sigmoid=True, name="norm_branch_gates")
    dil_slopes = _alibi_slopes(DIL_HEADS)
    dil = [_dilated_group(proj, dil_slopes, g, dilation)
           for g, (_, dilation) in enumerate(DIL_CONFIGS)]
    y_moba = _moba(proj, _alibi_slopes(MOBA_HEADS))
    x1 = _merge(xt, [o for o, _ in dil], [l for _, l in dil], y_moba, gates,
                w_up_dil.astype(BF16), w_up_moba.astype(BF16), w_out.astype(BF16), tm=min(256, t))
    mem_len = mem.shape[1]
    kv = _norm_matmul(mem.reshape(b * mem_len, d), mem_norm, w_kv_mem.astype(BF16),
                      tm=b * mem_len, tn=512, name="norm_mem_kv").reshape(b, mem_len, 2 * MEM_WIDTH)
    return _cross(x1, cross_norm, w_q_mem.astype(BF16), kv, w_o_mem.astype(BF16),
                  batch=b, tm=min(512, s))


def kernel(x, mem, attn_norm, w_in, w_up_dil, w_up_moba, w_branch_gate, w_out, cross_norm, mem_norm,
           w_q_mem, w_kv_mem, w_o_mem, ffn_norm, w_router_group, b_router_group, w_router_expert,
           b_router_expert, w_expert_gate, w_expert_up, w_expert_down, final_norm):
    b, s, d = x.shape
    depth = attn_norm.shape[0]
    assert depth == 1, "the final norm is fused into the last layer's MoE call"
    l = 0
    x2 = _layer(x, mem, attn_norm[l], w_in[l], w_up_dil[l], w_up_moba[l], w_branch_gate[l], w_out[l],
                cross_norm[l], mem_norm[l], w_q_mem[l], w_kv_mem[l], w_o_mem[l])
    w_route = jnp.concatenate([w_router_group[l], w_router_expert[l]], axis=1)
    b_route = jnp.concatenate([b_router_group[l], b_router_expert[l]], axis=0)
    out = _moe(x2, ffn_norm[l], w_route, b_route, w_expert_gate[l].astype(BF16),
               w_expert_up[l].astype(BF16), w_expert_down[l].astype(BF16), final_norm,
               tm=min(512, b * s))
    return out.reshape(b, s, d)
```

```python
import functools

import numpy as np
import jax
import jax.numpy as jnp
from jax import lax
from jax.experimental import pallas as pl
from jax.experimental.pallas import tpu as pltpu

F32 = jnp.float32
BF16 = jnp.bfloat16

HEAD_DIM = 128
DIL_CONFIGS = ((128, 1), (512, 4), (2048, 16))
DIL_HEADS_PER_GROUP = 4
DIL_HEADS = DIL_HEADS_PER_GROUP * len(DIL_CONFIGS)
DIL_WIDTH = DIL_HEADS * HEAD_DIM
DIL_OUT = DIL_HEADS_PER_GROUP * HEAD_DIM
DIL_STEPS = 128
MOBA_HEADS = 8
MOBA_WIDTH = MOBA_HEADS * HEAD_DIM
MOBA_BLOCK = 256
MOBA_TOPK = 3
IN_WIDTH = 3 * (DIL_WIDTH + MOBA_WIDTH)
MEM_HEADS = 4
MEM_WIDTH = MEM_HEADS * HEAD_DIM
N_GROUPS = 4
EXPERTS_PER_GROUP = 8
N_EXPERTS = N_GROUPS * EXPERTS_PER_GROUP
MOE_TOPK = 2
RMS_EPS = 1e-6
SCALE = HEAD_DIM ** -0.5
NEG_INF = float("-inf")
LOG2E = 1.4426950408889634
MOBA_GROUP = 4
MOBA_HEADS_PER_STEP = 2

VMEM_LIMIT_BYTES = 56 * 1024 * 1024
_NT = (((1,), (1,)), ((), ()))


def _alibi_slopes(n):
    return jnp.asarray(2.0 ** (-8.0 * np.arange(1, n + 1) / n), dtype=F32)


def _rms(x, g):
    return x * lax.rsqrt(jnp.mean(x * x, axis=-1, keepdims=True) + RMS_EPS) * g


def _params(*sem, flags=None):
    return pltpu.CompilerParams(dimension_semantics=sem, vmem_limit_bytes=VMEM_LIMIT_BYTES,
                                flags=flags)


def _resident(shape):
    nd = len(shape)
    return pl.BlockSpec(shape, lambda *_: (0,) * nd, pipeline_mode=pl.Buffered(1))


def _norm_matmul_body(x_ref, g_ref, w_ref, o_ref, h_ref, *, sigmoid):
    @pl.when(pl.program_id(1) == 0)
    def _():
        h_ref[...] = _rms(x_ref[...], g_ref[...]).astype(BF16)

    acc = jnp.dot(h_ref[...], w_ref[...], preferred_element_type=F32)
    if sigmoid:
        acc = jax.nn.sigmoid(acc)
    o_ref[...] = acc.astype(o_ref.dtype)


def _norm_matmul(x, g, w, *, tm, tn, sigmoid=False, name):
    m, d = x.shape
    n = w.shape[1]
    assert m % tm == 0 and n % tn == 0
    return pl.pallas_call(
        functools.partial(_norm_matmul_body, sigmoid=sigmoid),
        grid=(m // tm, n // tn),
        in_specs=[
            pl.BlockSpec((tm, d), lambda i, j: (i, 0)),
            pl.BlockSpec((1, d), lambda i, j: (0, 0)),
            pl.BlockSpec((d, tn), lambda i, j: (0, j)),
        ],
        out_specs=pl.BlockSpec((tm, tn), lambda i, j: (i, j)),
        out_shape=jax.ShapeDtypeStruct((m, n), BF16),
        scratch_shapes=[pltpu.VMEM((tm, d), BF16)],
        compiler_params=_params("parallel", "arbitrary"),
        name=name,
    )(x, g.reshape(1, d), w)


def _norm_body(x_ref, g_ref, o_ref):
    o_ref[...] = _rms(x_ref[...], g_ref[...]).astype(o_ref.dtype)


def _norm(x, g, *, tm, name):
    m, d = x.shape
    assert m % tm == 0
    return pl.pallas_call(
        _norm_body,
        grid=(m // tm,),
        in_specs=[pl.BlockSpec((tm, d), lambda i: (i, 0)), pl.BlockSpec((1, d), lambda i: (0, 0))],
        out_specs=pl.BlockSpec((tm, d), lambda i: (i, 0)),
        out_shape=jax.ShapeDtypeStruct((m, d), BF16),
        compiler_params=_params("parallel"),
        name=name,
    )(x, g.reshape(1, d))


def _in_proj_body(h_ref, win_ref, wbg_ref, proj_ref, gates_ref, perm_ref, *, n_proj_tiles):
    j = pl.program_id(1)
    tm = h_ref.shape[0]
    n = DIL_STEPS
    groups = len(DIL_CONFIGS)

    def store_regrouped(acc, dilation):
        tile = n * dilation
        for c in range(acc.shape[1] // HEAD_DIM):
            cols = slice(c * HEAD_DIM, (c + 1) * HEAD_DIM)
            perm_ref[c] = acc[:, cols]
            for t0 in range(0, tm, tile):
                for r in range(dilation):
                    rows = perm_ref[c, pl.ds(t0 + r, n, stride=dilation), :]
                    proj_ref[t0 + r * n:t0 + (r + 1) * n, cols] = rows.astype(BF16)

    @pl.when(j < n_proj_tiles)
    def _():
        acc = jnp.dot(h_ref[...], win_ref[...], preferred_element_type=F32)
        group = jnp.where(j < 3 * groups, lax.rem(j, groups), 0)
        def store_natural():
            proj_ref[...] = acc.astype(BF16)

        for g, (_, dilation) in enumerate(DIL_CONFIGS):
            store = store_natural if dilation == 1 else functools.partial(store_regrouped, acc, dilation)
            pl.when(group == g)(store)

    @pl.when(j >= n_proj_tiles)
    def _():
        acc = jnp.dot(h_ref[...], wbg_ref[...], preferred_element_type=F32)
        gates_ref[...] = jax.nn.sigmoid(acc).astype(BF16)


def _in_proj(h, w_in, w_bg, *, tm):
    t, d = h.shape
    tn = DIL_OUT
    assert t % tm == 0 and tm % (DIL_STEPS * max(dl for _, dl in DIL_CONFIGS)) == 0
    assert DIL_WIDTH == len(DIL_CONFIGS) * tn and w_in.shape[1] % tn == 0 and w_bg.shape[1] % tn == 0
    n_proj = w_in.shape[1] // tn
    n_gate = w_bg.shape[1] // tn
    return pl.pallas_call(
        functools.partial(_in_proj_body, n_proj_tiles=n_proj),
        grid=(t // tm, n_proj + n_gate),
        in_specs=[
            pl.BlockSpec((tm, d), lambda i, j: (i, 0)),
            pl.BlockSpec((d, tn), lambda i, j: (0, jnp.minimum(j, n_proj - 1))),
            pl.BlockSpec((d, tn), lambda i, j: (0, jnp.maximum(j - n_proj, 0))),
        ],
        out_specs=[
            pl.BlockSpec((tm, tn), lambda i, j: (i, jnp.minimum(j, n_proj - 1))),
            pl.BlockSpec((tm, tn), lambda i, j: (i, jnp.maximum(j - n_proj, 0))),
        ],
        out_shape=[jax.ShapeDtypeStruct((t, w_in.shape[1]), BF16),
                   jax.ShapeDtypeStruct((t, w_bg.shape[1]), BF16)],
        scratch_shapes=[pltpu.VMEM((tn // HEAD_DIM, tm, HEAD_DIM), F32)],
        compiler_params=_params("parallel", "arbitrary"),
        name="in_proj_gates",
    )(h, w_in, w_bg)


def _dilated_body(slope_ref, q_ref, kp_ref, kc_ref, vp_ref, vc_ref, o_ref, lse_ref, *scratch,
                  dilation, group):
    n = DIL_STEPS
    sub = pl.program_id(2)
    if dilation > 1:
        o_tile, lse_tile = scratch
        sub_rows = pl.ds(pl.multiple_of(sub * n, n), n)
    qi = lax.broadcasted_iota(jnp.int32, (n, n), 0)
    kj = lax.broadcasted_iota(jnp.int32, (n, n), 1)
    steps_cur = qi - kj
    valid_cur = steps_cur >= 0
    prev_limit = jnp.where(pl.program_id(1) > 0, 0, -n)
    valid_prev = steps_cur <= prev_limit
    dist_cur = (steps_cur * dilation).astype(F32)
    dist_prev = ((steps_cur + n) * dilation).astype(F32)
    for h in range(DIL_HEADS_PER_GROUP):
        cols = slice(h * HEAD_DIM, (h + 1) * HEAD_DIM)
        slope = slope_ref[group * DIL_HEADS_PER_GROUP + h]
        q = q_ref[:, cols]
        s_cur = lax.dot_general(q, kc_ref[:, cols], _NT, preferred_element_type=F32) * SCALE
        s_prev = lax.dot_general(q, kp_ref[:, cols], _NT, preferred_element_type=F32) * SCALE
        s_cur = jnp.where(valid_cur, s_cur - slope * dist_cur, NEG_INF)
        s_prev = jnp.where(valid_prev, s_prev - slope * dist_prev, NEG_INF)
        m = jnp.maximum(jnp.max(s_cur, axis=-1, keepdims=True),
                        jnp.max(s_prev, axis=-1, keepdims=True))
        p_cur = jnp.exp(s_cur - m)
        p_prev = jnp.exp(s_prev - m)
        den = jnp.sum(p_cur, axis=-1, keepdims=True) + jnp.sum(p_prev, axis=-1, keepdims=True)
        o = (jnp.dot(p_cur.astype(BF16), vc_ref[:, cols], preferred_element_type=F32)
             + jnp.dot(p_prev.astype(BF16), vp_ref[:, cols], preferred_element_type=F32))
        lse = jnp.broadcast_to(m + jnp.log(den), (n, HEAD_DIM))
        if dilation == 1:
            o_ref[h] = o / den
            lse_ref[h] = lse
        else:
            o_tile[h, sub_rows, :] = o / den
            lse_tile[h, sub_rows, :] = lse

    if dilation > 1:
        @pl.when(sub == dilation - 1)
        def _():
            for h in range(DIL_HEADS_PER_GROUP):
                for r in range(dilation):
                    natural = pl.ds(r, n, stride=dilation)
                    o_ref[h, natural, :] = o_tile[h, r * n:(r + 1) * n, :]
                    lse_ref[h, natural, :] = lse_tile[h, r * n:(r + 1) * n, :]


def _dilated_group(proj, slopes, group, dilation):
    b, s, _ = proj.shape
    n = DIL_STEPS
    tile = n * dilation
    assert s % tile == 0
    groups = len(DIL_CONFIGS)

    def spec(section, prev):
        def index(bi, i, r):
            return (bi, (jnp.maximum(i - 1, 0) if prev else i) * dilation + r,
                    section * groups + group)
        return pl.BlockSpec((None, n, DIL_OUT), index)

    out_block = (DIL_HEADS_PER_GROUP, tile, HEAD_DIM)
    out_spec = pl.BlockSpec((None,) + out_block, lambda bi, i, r: (bi, 0, i, 0))
    out_sds = jax.ShapeDtypeStruct((b, DIL_HEADS_PER_GROUP, s, HEAD_DIM), F32)
    scratch = [pltpu.VMEM(out_block, F32)] * 2 if dilation > 1 else []
    o, lse = pl.pallas_call(
        functools.partial(_dilated_body, dilation=dilation, group=group),
        grid=(b, s // tile, dilation),
        in_specs=[pl.BlockSpec(memory_space=pltpu.SMEM),
                  spec(0, False), spec(1, True), spec(1, False), spec(2, True), spec(2, False)],
        out_specs=[out_spec, out_spec],
        out_shape=[out_sds, out_sds],
        scratch_shapes=scratch,
        compiler_params=_params("parallel", "arbitrary", "arbitrary"),
        name=f"dilated_attn_g{group}",
    )(slopes, proj, proj, proj, proj, proj)
    return o, lse


def _moba_body(slope_ref, q_ref, k_ref, v_ref, o_ref, kmean_ref, vt_ref, bias_ref, sel_ref,
               *, nblk, group):
    blk = MOBA_BLOCK
    hd = HEAD_DIM
    heads = q_ref.shape[1] // hd
    own = pl.program_id(2)
    key_off = lax.broadcasted_iota(jnp.int32, (blk, blk), 0)
    qry_off = lax.broadcasted_iota(jnp.int32, (blk, blk), 1)
    slope2 = [slope_ref[pl.program_id(1) * heads + h] * LOG2E for h in range(heads)]

    @pl.when(own == 0)
    def _():
        def fill(jb, carry):
            start = pl.multiple_of(jb * blk, blk)
            rows = k_ref[pl.ds(start, blk), :].astype(F32)
            kmean_ref[pl.ds(jb, 1), :] = jnp.mean(rows, axis=0, keepdims=True)
            vrows = v_ref[pl.ds(start, blk), :].astype(F32)
            for h in range(heads):
                vt_ref[h, :, pl.ds(start, blk)] = vrows[:, h * hd:(h + 1) * hd].T.astype(BF16)
            return carry
        lax.fori_loop(0, nblk, fill, 0)
        for h in range(heads):
            bias_ref[h] = -slope2[h] * (qry_off - key_off).astype(F32)

    q = [q_ref[:, h * hd:(h + 1) * hd] for h in range(heads)]

    blk_id = lax.broadcasted_iota(jnp.int32, (nblk, blk), 0).astype(F32)
    for h in range(heads):
        gate = lax.dot_general(kmean_ref[:, h * hd:(h + 1) * hd], q[h].astype(F32), _NT,
                               precision=lax.Precision.HIGHEST, preferred_element_type=F32)
        gate = jnp.where(blk_id < own.astype(F32), gate, NEG_INF)
        sel = jnp.zeros((nblk, blk), F32)
        for _ in range(MOBA_TOPK):
            best = jnp.max(gate, axis=0, keepdims=True)
            is_best = (gate == best) & (gate > NEG_INF)
            pick = jnp.min(jnp.where(is_best, blk_id, float(nblk)), axis=0, keepdims=True)
            picked = blk_id == pick
            sel = jnp.where(picked, 1.0, sel)
            gate = jnp.where(picked, NEG_INF, gate)
        sel_ref[h] = sel

    def scores(h, start, rows):
        k = k_ref[pl.ds(start, rows), h * hd:(h + 1) * hd]
        return lax.dot_general(k, q[h], _NT, preferred_element_type=F32) * (SCALE * LOG2E)

    own_start = pl.multiple_of(own * blk, blk)
    init = []
    for h in range(heads):
        x = jnp.where(qry_off >= key_off, scores(h, own_start, blk) + bias_ref[h], NEG_INF)
        m0 = jnp.max(x, axis=0, keepdims=True)
        p = jnp.exp2(x - m0)
        l0 = jnp.sum(p, axis=0, keepdims=True)
        acc0 = jnp.dot(vt_ref[h, :, pl.ds(own_start, blk)], p.astype(BF16),
                       preferred_element_type=F32)
        init += [m0, l0, acc0]

    def past_blocks(i, carry):
        start = pl.multiple_of(i * (group * blk), group * blk)
        out = []
        for h in range(heads):
            m, l, acc = carry[3 * h:3 * h + 3]
            x = scores(h, start, group * blk)
            xs, chosen, shift = [], [], []
            m_new = m
            for g in range(group):
                j = i * group + g
                xs.append(x[g * blk:(g + 1) * blk] + bias_ref[h])
                chosen.append(sel_ref[h, pl.ds(j, 1), :] > 0.5)
                shift.append(-slope2[h] * ((own - j) * blk).astype(F32))
                top = jnp.max(xs[g], axis=0, keepdims=True) + shift[g]
                m_new = jnp.maximum(m_new, jnp.where(chosen[g], top, NEG_INF))
            l = jnp.exp2(m - m_new) * l
            acc = jnp.exp2(m - m_new) * acc
            ps = []
            for g in range(group):
                ref_g = jnp.where(chosen[g], m_new - shift[g], jnp.inf)
                ps.append(jnp.exp2(xs[g] - ref_g))
                l = l + jnp.sum(ps[g], axis=0, keepdims=True)
            p = jnp.concatenate(ps, axis=0).astype(BF16)
            acc = acc + jnp.dot(vt_ref[h, :, pl.ds(start, group * blk)], p,
                                preferred_element_type=F32)
            out += [m_new, l, acc]
        return tuple(out)

    n_groups = lax.div(own + (group - 1), group)
    final = lax.fori_loop(0, n_groups, past_blocks, tuple(init))
    for h in range(heads):
        _, l, acc = final[3 * h:3 * h + 3]
        o_ref[:, h * hd:(h + 1) * hd] = (acc / l).T.astype(o_ref.dtype)


def _moba(proj, slopes):
    b, s, _ = proj.shape
    assert s % (MOBA_BLOCK * MOBA_GROUP) == 0
    nblk = s // MOBA_BLOCK
    hp = MOBA_HEADS_PER_STEP
    width = hp * HEAD_DIM
    q0 = 3 * DIL_WIDTH // width
    k0 = q0 + MOBA_WIDTH // width
    v0 = k0 + MOBA_WIDTH // width
    y = pl.pallas_call(
        functools.partial(_moba_body, nblk=nblk, group=MOBA_GROUP),
        grid=(b, MOBA_HEADS // hp, nblk),
        in_specs=[
            pl.BlockSpec(memory_space=pltpu.SMEM),
            pl.BlockSpec((None, MOBA_BLOCK, width), lambda bi, h, i: (bi, i, q0 + h)),
            pl.BlockSpec((None, s, width), lambda bi, h, i: (bi, 0, k0 + h)),
            pl.BlockSpec((None, s, width), lambda bi, h, i: (bi, 0, v0 + h)),
        ],
        out_specs=pl.BlockSpec((None, MOBA_BLOCK, width), lambda bi, h, i: (bi, i, h)),
        out_shape=jax.ShapeDtypeStruct((b, s, MOBA_WIDTH), BF16),
        scratch_shapes=[pltpu.VMEM((nblk, width), F32),
                        pltpu.VMEM((hp, HEAD_DIM, s), BF16),
                        pltpu.VMEM((hp, MOBA_BLOCK, MOBA_BLOCK), F32),
                        pltpu.VMEM((hp, nblk, MOBA_BLOCK), F32)],
        compiler_params=_params("parallel", "parallel", "arbitrary"),
        name="moba_attn",
    )(slopes, proj, proj, proj)
    return y.reshape(b * s, MOBA_WIDTH)


def _merge_body(x_ref, o0_ref, o1_ref, o2_ref, l0_ref, l1_ref, l2_ref, ym_ref, gd_ref, gm_ref,
                wud_ref, wum_ref, wo_ref, out_ref):
    heads = []
    for h in range(DIL_HEADS_PER_GROUP):
        l0, l1, l2 = l0_ref[h], l1_ref[h], l2_ref[h]
        m = jnp.maximum(jnp.maximum(l0, l1), l2)
        e0, e1, e2 = jnp.exp(l0 - m), jnp.exp(l1 - m), jnp.exp(l2 - m)
        den = e0 + e1 + e2
        mixed = (e0 / den) * o0_ref[h] + (e1 / den) * o1_ref[h] + (e2 / den) * o2_ref[h]
        heads.append(mixed.astype(BF16))
    y_dil = jnp.concatenate(heads, axis=-1)
    lift_dil = jnp.dot(y_dil, wud_ref[...], preferred_element_type=F32)
    lift_moba = jnp.dot(ym_ref[...], wum_ref[...], preferred_element_type=F32)
    merged = gd_ref[...].astype(F32) * lift_dil + gm_ref[...].astype(F32) * lift_moba
    out_ref[...] = x_ref[...] + jnp.dot(merged.astype(BF16), wo_ref[...], preferred_element_type=F32)


def _merge(x, dil_outs, dil_lses, y_moba, gates, w_up_dil, w_up_moba, w_out, *, tm):
    t, d = x.shape
    s = dil_outs[0].shape[2]
    assert t % tm == 0 and s % tm == 0
    per_batch = s // tm
    row = lambda w: pl.BlockSpec((tm, w), lambda i: (i, 0))
    dil = pl.BlockSpec((None, DIL_HEADS_PER_GROUP, tm, HEAD_DIM),
                       lambda i: (i // per_batch, 0, i % per_batch, 0))
    return pl.pallas_call(
        _merge_body,
        grid=(t // tm,),
        in_specs=[row(d)] + [dil] * 6 + [
            row(MOBA_WIDTH),
            pl.BlockSpec((tm, d), lambda i: (i, 0)),
            pl.BlockSpec((tm, d), lambda i: (i, 1)),
            _resident((DIL_OUT, d)), _resident((MOBA_WIDTH, d)), _resident((d, d)),
        ],
        out_specs=row(d),
        out_shape=jax.ShapeDtypeStruct((t, d), F32),
        compiler_params=_params("parallel"),
        name="merge_out_proj",
    )(x, *dil_outs, *dil_lses, y_moba, gates, gates, w_up_dil, w_up_moba, w_out)


def _cross_body(x_ref, g_ref, wq_ref, kv_ref, wo_ref, out_ref):
    x = x_ref[...]
    h = _rms(x, g_ref[...]).astype(BF16)
    q = jnp.dot(h, wq_ref[...], preferred_element_type=F32).astype(BF16)
    heads = []
    for hd in range(MEM_HEADS):
        k = kv_ref[:, hd * HEAD_DIM:(hd + 1) * HEAD_DIM]
        v = kv_ref[:, MEM_WIDTH + hd * HEAD_DIM:MEM_WIDTH + (hd + 1) * HEAD_DIM]
        s = lax.dot_general(q[:, hd * HEAD_DIM:(hd + 1) * HEAD_DIM], k, _NT,
                            preferred_element_type=F32) * SCALE
        p = jnp.exp(s - jnp.max(s, axis=-1, keepdims=True))
        den = jnp.sum(p, axis=-1, keepdims=True)
        heads.append((jnp.dot(p.astype(BF16), v, preferred_element_type=F32) / den).astype(BF16))
    o = jnp.concatenate(heads, axis=-1)
    out_ref[...] = x + jnp.dot(o, wo_ref[...], preferred_element_type=F32)


def _cross(x, g, w_q, kv, w_o, *, batch, tm):
    t, d = x.shape
    s = t // batch
    mem_len = kv.shape[1]
    assert s % tm == 0
    per_batch = s // tm
    return pl.pallas_call(
        _cross_body,
        grid=(t // tm,),
        in_specs=[
            pl.BlockSpec((tm, d), lambda i: (i, 0)),
            _resident((1, d)),
            _resident((d, MEM_WIDTH)),
            pl.BlockSpec((None, mem_len, 2 * MEM_WIDTH), lambda i: (i // per_batch, 0, 0)),
            _resident((MEM_WIDTH, d)),
        ],
        out_specs=pl.BlockSpec((tm, d), lambda i: (i, 0)),
        out_shape=jax.ShapeDtypeStruct((t, d), F32),
        compiler_params=_params("parallel"),
        name="memory_cross_attn",
    )(x, g.reshape(1, d), w_q, kv, w_o)


def _moe_body(x_ref, g_ref, wr_ref, br_ref, wg_ref, wu_ref, wd_ref, gf_ref, out_ref,
              t_ref, comb_ref, y_ref):
    e = pl.program_id(1)
    tm = x_ref.shape[0]
    n_route = N_GROUPS + N_EXPERTS
    lane = lax.broadcasted_iota(jnp.int32, (tm, n_route), 1).astype(F32)

    @pl.when(e == 0)
    def _():
        t = _rms(x_ref[...], g_ref[...])
        t_ref[...] = t.astype(BF16)
        logits = jnp.dot(t, wr_ref[...], precision=lax.Precision.HIGHEST,
                         preferred_element_type=F32) + br_ref[...]
        is_group = lane < N_GROUPS
        glog = jnp.where(is_group, logits, NEG_INF)
        gmax = jnp.max(glog, axis=-1, keepdims=True)
        none = float(n_route)
        gsel = jnp.min(jnp.where(glog == gmax, lane, none), axis=-1, keepdims=True)
        pg = 1.0 / jnp.sum(jnp.exp(glog - gmax), axis=-1, keepdims=True)
        first = N_GROUPS + gsel * EXPERTS_PER_GROUP
        in_group = (lane >= first) & (lane < first + EXPERTS_PER_GROUP)
        elog = jnp.where(in_group, logits, NEG_INF)
        top1 = jnp.max(elog, axis=-1, keepdims=True)
        i1 = jnp.min(jnp.where(elog == top1, lane, none), axis=-1, keepdims=True)
        rest = jnp.where(lane == i1, NEG_INF, elog)
        top2 = jnp.max(rest, axis=-1, keepdims=True)
        i2 = jnp.min(jnp.where(rest == top2, lane, none), axis=-1, keepdims=True)
        e2 = jnp.exp(top2 - top1)
        w1 = pg / (1.0 + e2)
        w2 = pg * e2 / (1.0 + e2)
        comb_ref[...] = jnp.where(lane == i1, w1, jnp.where(lane == i2, w2, 0.0))
        y_ref[...] = jnp.zeros_like(y_ref)

    t = t_ref[...]
    gate = jnp.dot(t, wg_ref[...], preferred_element_type=F32)
    up = jnp.dot(t, wu_ref[...], preferred_element_type=F32)
    w_e = jnp.sum(jnp.where(lane == (N_GROUPS + e).astype(F32), comb_ref[...], 0.0),
                  axis=-1, keepdims=True)
    a = (jax.nn.silu(gate) * up) * w_e
    y_ref[...] += jnp.dot(a.astype(BF16), wd_ref[...], preferred_element_type=F32)

    @pl.when(e == N_EXPERTS - 1)
    def _():
        out_ref[...] = _rms(x_ref[...] + y_ref[...], gf_ref[...])


def _moe(x, g, w_route, b_route, w_gate, w_up, w_down, g_final, *, tm):
    t, d = x.shape
    ff = w_gate.shape[-1]
    n_route = N_GROUPS + N_EXPERTS
    assert t % tm == 0
    return pl.pallas_call(
        _moe_body,
        grid=(t // tm, N_EXPERTS),
        in_specs=[
            pl.BlockSpec((tm, d), lambda i, e: (i, 0)),
            pl.BlockSpec((1, d), lambda i, e: (0, 0)),
            pl.BlockSpec((d, n_route), lambda i, e: (0, 0)),
            pl.BlockSpec((1, n_route), lambda i, e: (0, 0)),
            pl.BlockSpec((None, d, ff), lambda i, e: (e, 0, 0)),
            pl.BlockSpec((None, d, ff), lambda i, e: (e, 0, 0)),
            pl.BlockSpec((None, ff, d), lambda i, e: (e, 0, 0)),
            pl.BlockSpec((1, d), lambda i, e: (0, 0)),
        ],
        out_specs=pl.BlockSpec((tm, d), lambda i, e: (i, 0)),
        out_shape=jax.ShapeDtypeStruct((t, d), F32),
        scratch_shapes=[pltpu.VMEM((tm, d), BF16), pltpu.VMEM((tm, n_route), F32),
                        pltpu.VMEM((tm, d), F32)],
        compiler_params=_params("parallel", "arbitrary"),
        name="hier_moe_final_norm",
    )(x, g.reshape(1, d), w_route, b_route.reshape(1, n_route), w_gate, w_up, w_down,
      g_final.reshape(1, d))


def _layer(x, mem, attn_norm, w_in, w_up_dil, w_up_moba, w_branch_gate, w_out, cross_norm,
           mem_norm, w_q_mem, w_kv_mem, w_o_mem):
    b, s, d = x.shape
    t = b * s
    xt = x.reshape(t, d)
    h = _norm(xt, attn_norm, tm=min(512, t), name="attn_norm")
    proj, gates = _in_proj(h, w_in.astype(BF16), w_branch_gate.astype(BF16), tm=min(2048, t))
    proj = proj.reshape(b, s, IN_WIDTH)
    dil_slopes = _alibi_slopes(DIL_HEADS)
    dil = [_dilated_group(proj, dil_slopes, g, dilation)
           for g, (_, dilation) in enumerate(DIL_CONFIGS)]
    y_moba = _moba(proj, _alibi_slopes(MOBA_HEADS))
    x1 = _merge(xt, [o for o, _ in dil], [l for _, l in dil], y_moba, gates,
                w_up_dil.astype(BF16), w_up_moba.astype(BF16), w_out.astype(BF16), tm=min(256, t))
    mem_len = mem.shape[1]
    kv = _norm_matmul(mem.reshape(b * mem_len, d), mem_norm, w_kv_mem.astype(BF16),
                      tm=b * mem_len, tn=512, name="norm_mem_kv").reshape(b, mem_len, 2 * MEM_WIDTH)
    return _cross(x1, cross_norm, w_q_mem.astype(BF16), kv, w_o_mem.astype(BF16),
                  batch=b, tm=min(512, s))


def kernel(x, mem, attn_norm, w_in, w_up_dil, w_up_moba, w_branch_gate, w_out, cross_norm, mem_norm,
           w_q_mem, w_kv_mem, w_o_mem, ffn_norm, w_router_group, b_router_group, w_router_expert,
           b_router_expert, w_expert_gate, w_expert_up, w_expert_down, final_norm):
    b, s, d = x.shape
    depth = attn_norm.shape[0]
    assert depth == 1, "the final norm is fused into the last layer's MoE call"
    l = 0
    x2 = _layer(x, mem, attn_norm[l], w_in[l], w_up_dil[l], w_up_moba[l], w_branch_gate[l], w_out[l],
                cross_norm[l], mem_norm[l], w_q_mem[l], w_kv_mem[l], w_o_mem[l])
    w_route = jnp.concatenate([w_router_group[l], w_router_expert[l]], axis=1)
    b_route = jnp.concatenate([b_router_group[l], b_router_expert[l]], axis=0)
    out = _moe(x2, ffn_norm[l], w_route, b_route, w_expert_gate[l].astype(BF16),
               w_expert_up[l].astype(BF16), w_expert_down[l].astype(BF16), final_norm,
               tm=min(512, b * s))
    return out.reshape(b, s, d)
```

```python
import functools

import numpy as np
import jax
import jax.numpy as jnp
from jax import lax
from jax.experimental import pallas as pl
from jax.experimental.pallas import tpu as pltpu

F32 = jnp.float32
BF16 = jnp.bfloat16

HEAD_DIM = 128
DIL_CONFIGS = ((128, 1), (512, 4), (2048, 16))
DIL_HEADS_PER_GROUP = 4
DIL_HEADS = DIL_HEADS_PER_GROUP * len(DIL_CONFIGS)
DIL_WIDTH = DIL_HEADS * HEAD_DIM
DIL_OUT = DIL_HEADS_PER_GROUP * HEAD_DIM
DIL_STEPS = 128
MOBA_HEADS = 8
MOBA_WIDTH = MOBA_HEADS * HEAD_DIM
MOBA_BLOCK = 256
MOBA_TOPK = 3
IN_WIDTH = 3 * (DIL_WIDTH + MOBA_WIDTH)
MEM_HEADS = 4
MEM_WIDTH = MEM_HEADS * HEAD_DIM
N_GROUPS = 4
EXPERTS_PER_GROUP = 8
N_EXPERTS = N_GROUPS * EXPERTS_PER_GROUP
MOE_TOPK = 2
RMS_EPS = 1e-6
SCALE = HEAD_DIM ** -0.5
NEG_INF = float("-inf")
LOG2E = 1.4426950408889634
MOBA_GROUP = 4
MOBA_HEADS_PER_STEP = 2

VMEM_LIMIT_BYTES = 56 * 1024 * 1024
_NT = (((1,), (1,)), ((), ()))


def _alibi_slopes(n):
    return jnp.asarray(2.0 ** (-8.0 * np.arange(1, n + 1) / n), dtype=F32)


def _rms(x, g):
    return x * lax.rsqrt(jnp.mean(x * x, axis=-1, keepdims=True) + RMS_EPS) * g


def _params(*sem, flags=None):
    return pltpu.CompilerParams(dimension_semantics=sem, vmem_limit_bytes=VMEM_LIMIT_BYTES,
                                flags=flags)


def _resident(shape):
    nd = len(shape)
    return pl.BlockSpec(shape, lambda *_: (0,) * nd, pipeline_mode=pl.Buffered(1))


def _norm_matmul_body(x_ref, g_ref, w_ref, o_ref, h_ref, *, sigmoid):
    @pl.when(pl.program_id(1) == 0)
    def _():
        h_ref[...] = _rms(x_ref[...], g_ref[...]).astype(BF16)

    acc = jnp.dot(h_ref[...], w_ref[...], preferred_element_type=F32)
    if sigmoid:
        acc = jax.nn.sigmoid(acc)
    o_ref[...] = acc.astype(o_ref.dtype)


def _norm_matmul(x, g, w, *, tm, tn, sigmoid=False, name):
    m, d = x.shape
    n = w.shape[1]
    assert m % tm == 0 and n % tn == 0
    return pl.pallas_call(
        functools.partial(_norm_matmul_body, sigmoid=sigmoid),
        grid=(m // tm, n // tn),
        in_specs=[
            pl.BlockSpec((tm, d), lambda i, j: (i, 0)),
            pl.BlockSpec((1, d), lambda i, j: (0, 0)),
            pl.BlockSpec((d, tn), lambda i, j: (0, j)),
        ],
        out_specs=pl.BlockSpec((tm, tn), lambda i, j: (i, j)),
        out_shape=jax.ShapeDtypeStruct((m, n), BF16),
        scratch_shapes=[pltpu.VMEM((tm, d), BF16)],
        compiler_params=_params("parallel", "arbitrary"),
        name=name,
    )(x, g.reshape(1, d), w)


def _norm_body(x_ref, g_ref, o_ref):
    o_ref[...] = _rms(x_ref[...], g_ref[...]).astype(o_ref.dtype)


def _norm(x, g, *, tm, name):
    m, d = x.shape
    assert m % tm == 0
    return pl.pallas_call(
        _norm_body,
        grid=(m // tm,),
        in_specs=[pl.BlockSpec((tm, d), lambda i: (i, 0)), pl.BlockSpec((1, d), lambda i: (0, 0))],
        out_specs=pl.BlockSpec((tm, d), lambda i: (i, 0)),
        out_shape=jax.ShapeDtypeStruct((m, d), BF16),
        compiler_params=_params("parallel"),
        name=name,
    )(x, g.reshape(1, d))


def _in_proj_body(h_ref, win_ref, wbg_ref, proj_ref, gates_ref, perm_ref, *, n_proj_tiles):
    j = pl.program_id(1)
    tm = h_ref.shape[0]
    n = DIL_STEPS
    groups = len(DIL_CONFIGS)

    def store_regrouped(acc, dilation):
        tile = n * dilation
        for c in range(acc.shape[1] // HEAD_DIM):
            cols = slice(c * HEAD_DIM, (c + 1) * HEAD_DIM)
            perm_ref[c] = acc[:, cols]
            for t0 in range(0, tm, tile):
                for r in range(dilation):
                    rows = perm_ref[c, pl.ds(t0 + r, n, stride=dilation), :]
                    proj_ref[t0 + r * n:t0 + (r + 1) * n, cols] = rows.astype(BF16)

    @pl.when(j < n_proj_tiles)
    def _():
        acc = jnp.dot(h_ref[...], win_ref[...], preferred_element_type=F32)
        group = jnp.where(j < 3 * groups, lax.rem(j, groups), 0)
        def store_natural():
            proj_ref[...] = acc.astype(BF16)

        for g, (_, dilation) in enumerate(DIL_CONFIGS):
            store = store_natural if dilation == 1 else functools.partial(store_regrouped, acc, dilation)
            pl.when(group == g)(store)

    @pl.when(j >= n_proj_tiles)
    def _():
        acc = jnp.dot(h_ref[...], wbg_ref[...], preferred_element_type=F32)
        gates_ref[...] = jax.nn.sigmoid(acc).astype(BF16)


def _in_proj(h, w_in, w_bg, *, tm):
    t, d = h.shape
    tn = DIL_OUT
    assert t % tm == 0 and tm % (DIL_STEPS * max(dl for _, dl in DIL_CONFIGS)) == 0
    assert DIL_WIDTH == len(DIL_CONFIGS) * tn and w_in.shape[1] % tn == 0 and w_bg.shape[1] % tn == 0
    n_proj = w_in.shape[1] // tn
    n_gate = w_bg.shape[1] // tn
    return pl.pallas_call(
        functools.partial(_in_proj_body, n_proj_tiles=n_proj),
        grid=(t // tm, n_proj + n_gate),
        in_specs=[
            pl.BlockSpec((tm, d), lambda i, j: (i, 0)),
            pl.BlockSpec((d, tn), lambda i, j: (0, jnp.minimum(j, n_proj - 1))),
            pl.BlockSpec((d, tn), lambda i, j: (0, jnp.maximum(j - n_proj, 0))),
        ],
        out_specs=[
            pl.BlockSpec((tm, tn), lambda i, j: (i, jnp.minimum(j, n_proj - 1))),
            pl.BlockSpec((tm, tn), lambda i, j: (i, jnp.maximum(j - n_proj, 0))),
        ],
        out_shape=[jax.ShapeDtypeStruct((t, w_in.shape[1]), BF16),
                   jax.ShapeDtypeStruct((t, w_bg.shape[1]), BF16)],
        scratch_shapes=[pltpu.VMEM((tn // HEAD_DIM, tm, HEAD_DIM), F32)],
        compiler_params=_params("parallel", "arbitrary"),
        name="in_proj_gates",
    )(h, w_in, w_bg)


def _dilated_body(slope_ref, q_ref, kp_ref, kc_ref, vp_ref, vc_ref, o_ref, lse_ref, *scratch,
                  dilation, group):
    n = DIL_STEPS
    sub = pl.program_id(2)
    if dilation > 1:
        o_tile, lse_tile = scratch
        sub_rows = pl.ds(pl.multiple_of(sub * n, n), n)
    qi = lax.broadcasted_iota(jnp.int32, (n, n), 0)
    kj = lax.broadcasted_iota(jnp.int32, (n, n), 1)
    steps_cur = qi - kj
    valid_cur = steps_cur >= 0
    prev_limit = jnp.where(pl.program_id(1) > 0, 0, -n)
    valid_prev = steps_cur <= prev_limit
    dist_cur = (steps_cur * dilation).astype(F32)
    dist_prev = ((steps_cur + n) * dilation).astype(F32)
    for h in range(DIL_HEADS_PER_GROUP):
        cols = slice(h * HEAD_DIM, (h + 1) * HEAD_DIM)
        slope = slope_ref[group * DIL_HEADS_PER_GROUP + h]
        q = q_ref[:, cols]
        s_cur = lax.dot_general(q, kc_ref[:, cols], _NT, preferred_element_type=F32) * SCALE
        s_prev = lax.dot_general(q, kp_ref[:, cols], _NT, preferred_element_type=F32) * SCALE
        s_cur = jnp.where(valid_cur, s_cur - slope * dist_cur, NEG_INF)
        s_prev = jnp.where(valid_prev, s_prev - slope * dist_prev, NEG_INF)
        m = jnp.maximum(jnp.max(s_cur, axis=-1, keepdims=True),
                        jnp.max(s_prev, axis=-1, keepdims=True))
        p_cur = jnp.exp(s_cur - m)
        p_prev = jnp.exp(s_prev - m)
        den = jnp.sum(p_cur, axis=-1, keepdims=True) + jnp.sum(p_prev, axis=-1, keepdims=True)
        o = (jnp.dot(p_cur.astype(BF16), vc_ref[:, cols], preferred_element_type=F32)
             + jnp.dot(p_prev.astype(BF16), vp_ref[:, cols], preferred_element_type=F32))
        lse = jnp.broadcast_to(m + jnp.log(den), (n, HEAD_DIM))
        if dilation == 1:
            o_ref[h] = o / den
            lse_ref[h] = lse
        else:
            o_tile[h, sub_rows, :] = o / den
            lse_tile[h, sub_rows, :] = lse

    if dilation > 1:
        @pl.when(sub == dilation - 1)
        def _():
            for h in range(DIL_HEADS_PER_GROUP):
                for r in range(dilation):
                    natural = pl.ds(r, n, stride=dilation)
                    o_ref[h, natural, :] = o_tile[h, r * n:(r + 1) * n, :]
                    lse_ref[h, natural, :] = lse_tile[h, r * n:(r + 1) * n, :]


def _dilated_group(proj, slopes, group, dilation):
    b, s, _ = proj.shape
    n = DIL_STEPS
    tile = n * dilation
    assert s % tile == 0
    groups = len(DIL_CONFIGS)

    def spec(section, prev):
        def index(bi, i, r):
            return (bi, (jnp.maximum(i - 1, 0) if prev else i) * dilation + r,
                    section * groups + group)
        return pl.BlockSpec((None, n, DIL_OUT), index)

    out_block = (DIL_HEADS_PER_GROUP, tile, HEAD_DIM)
    out_spec = pl.BlockSpec((None,) + out_block, lambda bi, i, r: (bi, 0, i, 0))
    out_sds = jax.ShapeDtypeStruct((b, DIL_HEADS_PER_GROUP, s, HEAD_DIM), F32)
    scratch = [pltpu.VMEM(out_block, F32)] * 2 if dilation > 1 else []
    o, lse = pl.pallas_call(
        functools.partial(_dilated_body, dilation=dilation, group=group),
        grid=(b, s // tile, dilation),
        in_specs=[pl.BlockSpec(memory_space=pltpu.SMEM),
                  spec(0, False), spec(1, True), spec(1, False), spec(2, True), spec(2, False)],
        out_specs=[out_spec, out_spec],
        out_shape=[out_sds, out_sds],
        scratch_shapes=scratch,
        compiler_params=_params("parallel", "arbitrary", "arbitrary"),
        name=f"dilated_attn_g{group}",
    )(slopes, proj, proj, proj, proj, proj)
    return o, lse


def _moba_body(slope_ref, q_ref, k_ref, v_ref, o_ref, kmean_ref, vt_ref, bias_ref, sel_ref,
               *, nblk, group):
    blk = MOBA_BLOCK
    hd = HEAD_DIM
    heads = q_ref.shape[1] // hd
    own = pl.program_id(2)
    key_off = lax.broadcasted_iota(jnp.int32, (blk, blk), 0)
    qry_off = lax.broadcasted_iota(jnp.int32, (blk, blk), 1)
    slope2 = [slope_ref[pl.program_id(1) * heads + h] * LOG2E for h in range(heads)]

    @pl.when(own == 0)
    def _():
        def fill(jb, carry):
            start = pl.multiple_of(jb * blk, blk)
            rows = k_ref[pl.ds(start, blk), :].astype(F32)
            kmean_ref[pl.ds(jb, 1), :] = jnp.mean(rows, axis=0, keepdims=True)
            vrows = v_ref[pl.ds(start, blk), :].astype(F32)
            for h in range(heads):
                vt_ref[h, :, pl.ds(start, blk)] = vrows[:, h * hd:(h + 1) * hd].T.astype(BF16)
            return carry
        lax.fori_loop(0, nblk, fill, 0)
        for h in range(heads):
            bias_ref[h] = -slope2[h] * (qry_off - key_off).astype(F32)

    q = [q_ref[:, h * hd:(h + 1) * hd] for h in range(heads)]

    blk_id = lax.broadcasted_iota(jnp.int32, (nblk, blk), 0).astype(F32)
    for h in range(heads):
        gate = lax.dot_general(kmean_ref[:, h * hd:(h + 1) * hd], q[h].astype(F32), _NT,
                               precision=lax.Precision.HIGHEST, preferred_element_type=F32)
        gate = jnp.where(blk_id < own.astype(F32), gate, NEG_INF)
        sel = jnp.zeros((nblk, blk), F32)
        for _ in range(MOBA_TOPK):
            best = jnp.max(gate, axis=0, keepdims=True)
            is_best = (gate == best) & (gate > NEG_INF)
            pick = jnp.min(jnp.where(is_best, blk_id, float(nblk)), axis=0, keepdims=True)
            picked = blk_id == pick
            sel = jnp.where(picked, 1.0, sel)
            gate = jnp.where(picked, NEG_INF, gate)
        sel_ref[h] = sel

    def scores(h, start, rows):
        k = k_ref[pl.ds(start, rows), h * hd:(h + 1) * hd]
        return lax.dot_general(k, q[h], _NT, preferred_element_type=F32) * (SCALE * LOG2E)

    own_start = pl.multiple_of(own * blk, blk)
    init = []
    for h in range(heads):
        x = jnp.where(qry_off >= key_off, scores(h, own_start, blk) + bias_ref[h], NEG_INF)
        m0 = jnp.max(x, axis=0, keepdims=True)
        p = jnp.exp2(x - m0)
        l0 = jnp.sum(p, axis=0, keepdims=True)
        acc0 = jnp.dot(vt_ref[h, :, pl.ds(own_start, blk)], p.astype(BF16),
                       preferred_element_type=F32)
        init += [m0, l0, acc0]

    def past_blocks(i, carry):
        start = pl.multiple_of(i * (group * blk), group * blk)
        out = []
        for h in range(heads):
            m, l, acc = carry[3 * h:3 * h + 3]
            x = scores(h, start, group * blk)
            xs, chosen, shift = [], [], []
            m_new = m
            for g in range(group):
                j = i * group + g
                xs.append(x[g * blk:(g + 1) * blk] + bias_ref[h])
                chosen.append(sel_ref[h, pl.ds(j, 1), :] > 0.5)
                shift.append(-slope2[h] * ((own - j) * blk).astype(F32))
                top = jnp.max(xs[g], axis=0, keepdims=True) + shift[g]
                m_new = jnp.maximum(m_new, jnp.where(chosen[g], top, NEG_INF))
            l = jnp.exp2(m - m_new) * l
            acc = jnp.exp2(m - m_new) * acc
            ps = []
            for g in range(group):
                ref_g = jnp.where(chosen[g], m_new - shift[g], jnp.inf)
                ps.append(jnp.exp2(xs[g] - ref_g))
                l = l + jnp.sum(ps[g], axis=0, keepdims=True)
            p = jnp.concatenate(ps, axis=0).astype(BF16)
            acc = acc + jnp.dot(vt_ref[h, :, pl.ds(start, group * blk)], p,
                                preferred_element_type=F32)
            out += [m_new, l, acc]
        return tuple(out)

    n_groups = lax.div(own + (group - 1), group)
    final = lax.fori_loop(0, n_groups, past_blocks, tuple(init))
    for h in range(heads):
        _, l, acc = final[3 * h:3 * h + 3]
        o_ref[:, h * hd:(h + 1) * hd] = (acc / l).T.astype(o_ref.dtype)


def _moba(proj, slopes):
    b, s, _ = proj.shape
    assert s % (MOBA_BLOCK * MOBA_GROUP) == 0
    nblk = s // MOBA_BLOCK
    hp = MOBA_HEADS_PER_STEP
    width = hp * HEAD_DIM
    q0 = 3 * DIL_WIDTH // width
    k0 = q0 + MOBA_WIDTH // width
    v0 = k0 + MOBA_WIDTH // width
    y = pl.pallas_call(
        functools.partial(_moba_body, nblk=nblk, group=MOBA_GROUP),
        grid=(b, MOBA_HEADS // hp, nblk),
        in_specs=[
            pl.BlockSpec(memory_space=pltpu.SMEM),
            pl.BlockSpec((None, MOBA_BLOCK, width), lambda bi, h, i: (bi, i, q0 + h)),
            pl.BlockSpec((None, s, width), lambda bi, h, i: (bi, 0, k0 + h)),
            pl.BlockSpec((None, s, width), lambda bi, h, i: (bi, 0, v0 + h)),
        ],
        out_specs=pl.BlockSpec((None, MOBA_BLOCK, width), lambda bi, h, i: (bi, i, h)),
        out_shape=jax.ShapeDtypeStruct((b, s, MOBA_WIDTH), BF16),
        scratch_shapes=[pltpu.VMEM((nblk, width), F32),
                        pltpu.VMEM((hp, HEAD_DIM, s), BF16),
                        pltpu.VMEM((hp, MOBA_BLOCK, MOBA_BLOCK), F32),
                        pltpu.VMEM((hp, nblk, MOBA_BLOCK), F32)],
        compiler_params=_params("parallel", "parallel", "arbitrary"),
        name="moba_attn",
    )(slopes, proj, proj, proj)
    return y.reshape(b * s, MOBA_WIDTH)


def _merge_body(x_ref, o0_ref, o1_ref, o2_ref, l0_ref, l1_ref, l2_ref, ym_ref, gd_ref, gm_ref,
                wud_ref, wum_ref, wo_ref, out_ref):
    heads = []
    for h in range(DIL_HEADS_PER_GROUP):
        l0, l1, l2 = l0_ref[h], l1_ref[h], l2_ref[h]
        m = jnp.maximum(jnp.maximum(l0, l1), l2)
        e0, e1, e2 = jnp.exp(l0 - m), jnp.exp(l1 - m), jnp.exp(l2 - m)
        den = e0 + e1 + e2
        mixed = (e0 / den) * o0_ref[h] + (e1 / den) * o1_ref[h] + (e2 / den) * o2_ref[h]
        heads.append(mixed.astype(BF16))
    y_dil = jnp.concatenate(heads, axis=-1)
    lift_dil = jnp.dot(y_dil, wud_ref[...], preferred_element_type=F32)
    lift_moba = jnp.dot(ym_ref[...], wum_ref[...], preferred_element_type=F32)
    merged = gd_ref[...].astype(F32) * lift_dil + gm_ref[...].astype(F32) * lift_moba
    out_ref[...] = x_ref[...] + jnp.dot(merged.astype(BF16), wo_ref[...], preferred_element_type=F32)


def _merge(x, dil_outs, dil_lses, y_moba, gates, w_up_dil, w_up_moba, w_out, *, tm):
    t, d = x.shape
    s = dil_outs[0].shape[2]
    assert t % tm == 0 and s % tm == 0
    per_batch = s // tm
    row = lambda w: pl.BlockSpec((tm, w), lambda i: (i, 0))
    dil = pl.BlockSpec((None, DIL_HEADS_PER_GROUP, tm, HEAD_DIM),
                       lambda i: (i // per_batch, 0, i % per_batch, 0))
    return pl.pallas_call(
        _merge_body,
        grid=(t // tm,),
        in_specs=[row(d)] + [dil] * 6 + [
            row(MOBA_WIDTH),
            pl.BlockSpec((tm, d), lambda i: (i, 0)),
            pl.BlockSpec((tm, d), lambda i: (i, 1)),
            _resident((DIL_OUT, d)), _resident((MOBA_WIDTH, d)), _resident((d, d)),
        ],
        out_specs=row(d),
        out_shape=jax.ShapeDtypeStruct((t, d), F32),
        compiler_params=_params("parallel"),
        name="merge_out_proj",
    )(x, *dil_outs, *dil_lses, y_moba, gates, gates, w_up_dil, w_up_moba, w_out)


def _cross_body(x_ref, g_ref, wq_ref, kv_ref, wo_ref, out_ref):
    x = x_ref[...]
    h = _rms(x, g_ref[...]).astype(BF16)
    q = jnp.dot(h, wq_ref[...], preferred_element_type=F32).astype(BF16)
    heads = []
    for hd in range(MEM_HEADS):
        k = kv_ref[:, hd * HEAD_DIM:(hd + 1) * HEAD_DIM]
        v = kv_ref[:, MEM_WIDTH + hd * HEAD_DIM:MEM_WIDTH + (hd + 1) * HEAD_DIM]
        s = lax.dot_general(q[:, hd * HEAD_DIM:(hd + 1) * HEAD_DIM], k, _NT,
                            preferred_element_type=F32) * SCALE
        p = jnp.exp(s - jnp.max(s, axis=-1, keepdims=True))
        den = jnp.sum(p, axis=-1, keepdims=True)
        heads.append((jnp.dot(p.astype(BF16), v, preferred_element_type=F32) / den).astype(BF16))
    o = jnp.concatenate(heads, axis=-1)
    out_ref[...] = x + jnp.dot(o, wo_ref[...], preferred_element_type=F32)


def _cross(x, g, w_q, kv, w_o, *, batch, tm):
    t, d = x.shape
    s = t // batch
    mem_len = kv.shape[1]
    assert s % tm == 0
    per_batch = s // tm
    return pl.pallas_call(
        _cross_body,
        grid=(t // tm,),
        in_specs=[
            pl.BlockSpec((tm, d), lambda i: (i, 0)),
            _resident((1, d)),
            _resident((d, MEM_WIDTH)),
            pl.BlockSpec((None, mem_len, 2 * MEM_WIDTH), lambda i: (i // per_batch, 0, 0)),
            _resident((MEM_WIDTH, d)),
        ],
        out_specs=pl.BlockSpec((tm, d), lambda i: (i, 0)),
        out_shape=jax.ShapeDtypeStruct((t, d), F32),
        compiler_params=_params("parallel"),
        name="memory_cross_attn",
    )(x, g.reshape(1, d), w_q, kv, w_o)


ROUTE_COLS = 8
MOE_ROW_TILE = 256


def _route_body(x_ref, g_ref, wr_ref, br_ref, info_ref, counts_ref, run_ref, tri_ref):
    tm = x_ref.shape[0]
    n_route = N_GROUPS + N_EXPERTS
    lane = lax.broadcasted_iota(jnp.int32, (tm, n_route), 1).astype(F32)

    @pl.when(pl.program_id(0) == 0)
    def _():
        run_ref[...] = jnp.zeros_like(run_ref)
        earlier = (lax.broadcasted_iota(jnp.int32, (tm, tm), 0)
                   > lax.broadcasted_iota(jnp.int32, (tm, tm), 1))
        tri_ref[...] = jnp.where(earlier, 1.0, 0.0).astype(BF16)

    t = _rms(x_ref[...], g_ref[...])
    logits = jnp.dot(t, wr_ref[...], precision=lax.Precision.HIGHEST,
                     preferred_element_type=F32) + br_ref[...]
    none = float(n_route)
    glog = jnp.where(lane < N_GROUPS, logits, NEG_INF)
    gmax = jnp.max(glog, axis=-1, keepdims=True)
    gsel = jnp.min(jnp.where(glog == gmax, lane, none), axis=-1, keepdims=True)
    pg = 1.0 / jnp.sum(jnp.exp(glog - gmax), axis=-1, keepdims=True)
    first = N_GROUPS + gsel * EXPERTS_PER_GROUP
    in_group = (lane >= first) & (lane < first + EXPERTS_PER_GROUP)
    elog = jnp.where(in_group, logits, NEG_INF)
    top1 = jnp.max(elog, axis=-1, keepdims=True)
    i1 = jnp.min(jnp.where(elog == top1, lane, none), axis=-1, keepdims=True)
    rest = jnp.where(lane == i1, NEG_INF, elog)
    top2 = jnp.max(rest, axis=-1, keepdims=True)
    i2 = jnp.min(jnp.where(rest == top2, lane, none), axis=-1, keepdims=True)
    e2 = jnp.exp(top2 - top1)
    w1 = pg / (1.0 + e2)
    w2 = pg * e2 / (1.0 + e2)

    hit1 = lane == i1
    hit2 = lane == i2
    assigned = jnp.where(hit1 | hit2, 1.0, 0.0)
    before = jnp.dot(tri_ref[...], assigned.astype(BF16), preferred_element_type=F32) + run_ref[...]
    rank1 = jnp.sum(jnp.where(hit1, before, 0.0), axis=-1, keepdims=True)
    rank2 = jnp.sum(jnp.where(hit2, before, 0.0), axis=-1, keepdims=True)
    run_ref[...] += jnp.sum(assigned, axis=0, keepdims=True)
    counts_ref[...] = run_ref[...]

    col = lax.broadcasted_iota(jnp.int32, (tm, ROUTE_COLS), 1)
    fields = (i1 - N_GROUPS, i2 - N_GROUPS, rank1, rank2, w1, w2)
    info = jnp.zeros((tm, ROUTE_COLS), F32)
    for c, field in enumerate(fields):
        info = jnp.where(col == c, field, info)
    info_ref[...] = info


def _route(x, g, w_route, b_route, *, tm):
    t, d = x.shape
    n_route = N_GROUPS + N_EXPERTS
    assert t % tm == 0
    return pl.pallas_call(
        _route_body,
        grid=(t // tm,),
        in_specs=[
            pl.BlockSpec((tm, d), lambda i: (i, 0)),
            pl.BlockSpec((1, d), lambda i: (0, 0)),
            pl.BlockSpec((d, n_route), lambda i: (0, 0)),
            pl.BlockSpec((1, n_route), lambda i: (0, 0)),
        ],
        out_specs=[pl.BlockSpec((tm, ROUTE_COLS), lambda i: (i, 0)),
                   pl.BlockSpec((1, n_route), lambda i: (0, 0))],
        out_shape=[jax.ShapeDtypeStruct((t, ROUTE_COLS), F32),
                   jax.ShapeDtypeStruct((1, n_route), F32)],
        scratch_shapes=[pltpu.VMEM((1, n_route), F32), pltpu.VMEM((tm, tm), BF16)],
        compiler_params=_params("arbitrary"),
        name="moe_route",
    )(x, g.reshape(1, d), w_route, b_route.reshape(1, n_route))


def _row_copies_wait(src_rows, dst_rows, sem):
    pltpu.make_async_copy(src_rows, dst_rows, sem).wait()


def _dispatch_body(pos_ref, x_ref, g_ref, sorted_in_ref, sorted_ref, t_ref, sem):
    del sorted_in_ref
    tm = x_ref.shape[0]
    base = pl.program_id(0) * tm
    t_ref[...] = _rms(x_ref[...], g_ref[...])

    def issue(r, carry):
        for k in range(MOE_TOPK):
            dst = pos_ref[k, base + r]
            pltpu.make_async_copy(t_ref.at[pl.ds(r, 1)], sorted_ref.at[pl.ds(dst, 1)], sem).start()
        return carry

    lax.fori_loop(0, tm, issue, 0, unroll=8)
    for _ in range(MOE_TOPK):
        _row_copies_wait(t_ref, sorted_ref.at[pl.ds(0, tm)], sem)


def _dispatch(x, g, pos, sorted_zero, *, tm):
    t, d = x.shape
    assert t % tm == 0
    return pl.pallas_call(
        _dispatch_body,
        grid_spec=pltpu.PrefetchScalarGridSpec(
            num_scalar_prefetch=1,
            grid=(t // tm,),
            in_specs=[pl.BlockSpec((tm, d), lambda i, pos: (i, 0)),
                      pl.BlockSpec((1, d), lambda i, pos: (0, 0)),
                      pl.BlockSpec(memory_space=pl.ANY)],
            out_specs=pl.BlockSpec(memory_space=pl.ANY),
            scratch_shapes=[pltpu.VMEM((tm, d), F32), pltpu.SemaphoreType.DMA(())],
        ),
        out_shape=jax.ShapeDtypeStruct(sorted_zero.shape, F32),
        input_output_aliases={3: 0},
        compiler_params=pltpu.CompilerParams(dimension_semantics=("arbitrary",),
                                             vmem_limit_bytes=VMEM_LIMIT_BYTES,
                                             disable_bounds_checks=True),
        name="moe_dispatch",
    )(pos, x, g.reshape(1, d), sorted_zero)


def _experts_body(tile_expert_ref, n_tiles_ref, x_ref, wg_ref, wu_ref, wd_ref, y_ref):
    del tile_expert_ref
    in_use = pl.program_id(0) < n_tiles_ref[0]

    @pl.when(in_use)
    def _():
        t = x_ref[...].astype(BF16)
        gate = jnp.dot(t, wg_ref[...], preferred_element_type=F32)
        up = jnp.dot(t, wu_ref[...], preferred_element_type=F32)
        a = jax.nn.silu(gate) * up
        y_ref[...] = jnp.dot(a.astype(BF16), wd_ref[...], preferred_element_type=F32)

    @pl.when(jnp.logical_not(in_use))
    def _():
        y_ref[...] = jnp.zeros_like(y_ref)


def _experts(sorted_rows, tile_expert, n_tiles, w_gate, w_up, w_down):
    p, d = sorted_rows.shape
    ff = w_gate.shape[-1]
    tr = MOE_ROW_TILE
    assert p % tr == 0
    used = lambda i, nt: jnp.minimum(i, nt[0] - 1)
    return pl.pallas_call(
        _experts_body,
        grid_spec=pltpu.PrefetchScalarGridSpec(
            num_scalar_prefetch=2,
            grid=(p // tr,),
            in_specs=[pl.BlockSpec((tr, d), lambda i, te, nt: (used(i, nt), 0)),
                      pl.BlockSpec((None, d, ff), lambda i, te, nt: (te[used(i, nt)], 0, 0)),
                      pl.BlockSpec((None, d, ff), lambda i, te, nt: (te[used(i, nt)], 0, 0)),
                      pl.BlockSpec((None, ff, d), lambda i, te, nt: (te[used(i, nt)], 0, 0))],
            out_specs=pl.BlockSpec((tr, d), lambda i, te, nt: (i, 0)),
        ),
        out_shape=jax.ShapeDtypeStruct((p, d), F32),
        compiler_params=_params("arbitrary"),
        name="moe_experts",
    )(tile_expert, n_tiles, sorted_rows, w_gate, w_up, w_down)


def _combine_body(pos_ref, x_ref, info_ref, gf_ref, y_sorted_ref, out_ref, rows_ref, sem):
    tm = x_ref.shape[0]
    base = pl.program_id(0) * tm

    def issue(r, carry):
        for k in range(MOE_TOPK):
            src = pos_ref[k, base + r]
            pltpu.make_async_copy(y_sorted_ref.at[pl.ds(src, 1)], rows_ref.at[k, pl.ds(r, 1)],
                                  sem).start()
        return carry

    lax.fori_loop(0, tm, issue, 0, unroll=8)
    for k in range(MOE_TOPK):
        _row_copies_wait(y_sorted_ref.at[pl.ds(0, tm)], rows_ref.at[k], sem)

    info = info_ref[...]
    y = info[:, 4:5] * rows_ref[0] + info[:, 5:6] * rows_ref[1]
    out_ref[...] = _rms(x_ref[...] + y, gf_ref[...])


def _combine(x, info, pos, y_sorted, g_final, *, tm):
    t, d = x.shape
    assert t % tm == 0
    return pl.pallas_call(
        _combine_body,
        grid_spec=pltpu.PrefetchScalarGridSpec(
            num_scalar_prefetch=1,
            grid=(t // tm,),
            in_specs=[pl.BlockSpec((tm, d), lambda i, pos: (i, 0)),
                      pl.BlockSpec((tm, ROUTE_COLS), lambda i, pos: (i, 0)),
                      pl.BlockSpec((1, d), lambda i, pos: (0, 0)),
                      pl.BlockSpec(memory_space=pl.ANY)],
            out_specs=pl.BlockSpec((tm, d), lambda i, pos: (i, 0)),
            scratch_shapes=[pltpu.VMEM((MOE_TOPK, tm, d), F32), pltpu.SemaphoreType.DMA(())],
        ),
        out_shape=jax.ShapeDtypeStruct((t, d), F32),
        compiler_params=pltpu.CompilerParams(dimension_semantics=("arbitrary",),
                                             vmem_limit_bytes=VMEM_LIMIT_BYTES,
                                             disable_bounds_checks=True),
        name="moe_combine_final_norm",
    )(pos, x, info, g_final.reshape(1, d), y_sorted)


def _moe(x, g, w_route, b_route, w_gate, w_up, w_down, g_final):
    t, d = x.shape
    tr = MOE_ROW_TILE
    assert (MOE_TOPK * t) % tr == 0
    info, counts = _route(x, g, w_route, b_route, tm=min(512, t))

    expert = info[:, 0:MOE_TOPK].astype(jnp.int32)
    rank = info[:, MOE_TOPK:2 * MOE_TOPK].astype(jnp.int32)
    count = counts[0, N_GROUPS:].astype(jnp.int32)
    seg_tiles = (count + (tr - 1)) // tr
    seg_end = jnp.cumsum(seg_tiles)
    seg_start_row = (seg_end - seg_tiles) * tr
    pos = (seg_start_row[expert] + rank).T
    max_tiles = (MOE_TOPK * t) // tr + N_EXPERTS
    tile_id = jnp.arange(max_tiles, dtype=jnp.int32)
    tile_expert = jnp.minimum(
        jnp.sum((seg_end[None, :] <= tile_id[:, None]).astype(jnp.int32), axis=1), N_EXPERTS - 1)
    n_tiles = seg_end[-1:].astype(jnp.int32)

    sorted_rows = _dispatch(x, g, pos, jnp.zeros((max_tiles * tr, d), F32), tm=min(256, t))
    y_sorted = _experts(sorted_rows, tile_expert, n_tiles, w_gate, w_up, w_down)
    return _combine(x, info, pos, y_sorted, g_final, tm=min(256, t))


def _layer(x, mem, attn_norm, w_in, w_up_dil, w_up_moba, w_branch_gate, w_out, cross_norm,
           mem_norm, w_q_mem, w_kv_mem, w_o_mem):
    b, s, d = x.shape
    t = b * s
    xt = x.reshape(t, d)
    h = _norm(xt, attn_norm, tm=min(512, t), name="attn_norm")
    proj, gates = _in_proj(h, w_in.astype(BF16), w_branch_gate.astype(BF16), tm=min(2048, t))
    proj = proj.reshape(b, s, IN_WIDTH)
    dil_slopes = _alibi_slopes(DIL_HEADS)
    dil = [_dilated_group(proj, dil_slopes, g, dilation)
           for g, (_, dilation) in enumerate(DIL_CONFIGS)]
    y_moba = _moba(proj, _alibi_slopes(MOBA_HEADS))
    x1 = _merge(xt, [o for o, _ in dil], [l for _, l in dil], y_moba, gates,
                w_up_dil.astype(BF16), w_up_moba.astype(BF16), w_out.astype(BF16), tm=min(256, t))
    mem_len = mem.shape[1]
    kv = _norm_matmul(mem.reshape(b * mem_len, d), mem_norm, w_kv_mem.astype(BF16),
                      tm=b * mem_len, tn=512, name="norm_mem_kv").reshape(b, mem_len, 2 * MEM_WIDTH)
    return _cross(x1, cross_norm, w_q_mem.astype(BF16), kv, w_o_mem.astype(BF16),
                  batch=b, tm=min(512, s))


def kernel(x, mem, attn_norm, w_in, w_up_dil, w_up_moba, w_branch_gate, w_out, cross_norm, mem_norm,
           w_q_mem, w_kv_mem, w_o_mem, ffn_norm, w_router_group, b_router_group, w_router_expert,
           b_router_expert, w_expert_gate, w_expert_up, w_expert_down, final_norm):
    b, s, d = x.shape
    depth = attn_norm.shape[0]
    assert depth == 1, "the final norm is fused into the last layer's MoE call"
    l = 0
    x2 = _layer(x, mem, attn_norm[l], w_in[l], w_up_dil[l], w_up_moba[l], w_branch_gate[l], w_out[l],
                cross_norm[l], mem_norm[l], w_q_mem[l], w_kv_mem[l], w_o_mem[l])
    w_route = jnp.concatenate([w_router_group[l], w_router_expert[l]], axis=1)
    b_route = jnp.concatenate([b_router_group[l], b_router_expert[l]], axis=0)
    out = _moe(x2, ffn_norm[l], w_route, b_route, w_expert_gate[l].astype(BF16),
               w_expert_up[l].astype(BF16), w_expert_down[l].astype(BF16), final_norm)
    return out.reshape(b, s, d)
```

```python
import functools

import numpy as np
import jax
import jax.numpy as jnp
from jax import lax
from jax.experimental import pallas as pl
from jax.experimental.pallas import tpu as pltpu

F32 = jnp.float32
BF16 = jnp.bfloat16

HEAD_DIM = 128
DIL_CONFIGS = ((128, 1), (512, 4), (2048, 16))
DIL_HEADS_PER_GROUP = 4
DIL_HEADS = DIL_HEADS_PER_GROUP * len(DIL_CONFIGS)
DIL_WIDTH = DIL_HEADS * HEAD_DIM
DIL_OUT = DIL_HEADS_PER_GROUP * HEAD_DIM
DIL_STEPS = 128
MOBA_HEADS = 8
MOBA_WIDTH = MOBA_HEADS * HEAD_DIM
MOBA_BLOCK = 256
MOBA_TOPK = 3
IN_WIDTH = 3 * (DIL_WIDTH + MOBA_WIDTH)
MEM_HEADS = 4
MEM_WIDTH = MEM_HEADS * HEAD_DIM
N_GROUPS = 4
EXPERTS_PER_GROUP = 8
N_EXPERTS = N_GROUPS * EXPERTS_PER_GROUP
MOE_TOPK = 2
RMS_EPS = 1e-6
SCALE = HEAD_DIM ** -0.5
NEG_INF = float("-inf")
LOG2E = 1.4426950408889634
MOBA_GROUP = 4
MOBA_HEADS_PER_STEP = 4

VMEM_LIMIT_BYTES = 56 * 1024 * 1024
_NT = (((1,), (1,)), ((), ()))


def _alibi_slopes(n):
    return jnp.asarray(2.0 ** (-8.0 * np.arange(1, n + 1) / n), dtype=F32)


def _rms(x, g):
    return x * lax.rsqrt(jnp.mean(x * x, axis=-1, keepdims=True) + RMS_EPS) * g


def _params(*sem, flags=None):
    return pltpu.CompilerParams(dimension_semantics=sem, vmem_limit_bytes=VMEM_LIMIT_BYTES,
                                flags=flags)


def _resident(shape):
    nd = len(shape)
    return pl.BlockSpec(shape, lambda *_: (0,) * nd, pipeline_mode=pl.Buffered(1))


def _norm_matmul_body(x_ref, g_ref, w_ref, o_ref, h_ref, *, sigmoid):
    @pl.when(pl.program_id(1) == 0)
    def _():
        h_ref[...] = _rms(x_ref[...], g_ref[...]).astype(BF16)

    acc = jnp.dot(h_ref[...], w_ref[...], preferred_element_type=F32)
    if sigmoid:
        acc = jax.nn.sigmoid(acc)
    o_ref[...] = acc.astype(o_ref.dtype)


def _norm_matmul(x, g, w, *, tm, tn, sigmoid=False, name):
    m, d = x.shape
    n = w.shape[1]
    assert m % tm == 0 and n % tn == 0
    return pl.pallas_call(
        functools.partial(_norm_matmul_body, sigmoid=sigmoid),
        grid=(m // tm, n // tn),
        in_specs=[
            pl.BlockSpec((tm, d), lambda i, j: (i, 0)),
            pl.BlockSpec((1, d), lambda i, j: (0, 0)),
            pl.BlockSpec((d, tn), lambda i, j: (0, j)),
        ],
        out_specs=pl.BlockSpec((tm, tn), lambda i, j: (i, j)),
        out_shape=jax.ShapeDtypeStruct((m, n), BF16),
        scratch_shapes=[pltpu.VMEM((tm, d), BF16)],
        compiler_params=_params("parallel", "arbitrary"),
        name=name,
    )(x, g.reshape(1, d), w)


def _norm_body(x_ref, g_ref, o_ref):
    o_ref[...] = _rms(x_ref[...], g_ref[...]).astype(o_ref.dtype)


def _norm(x, g, *, tm, name):
    m, d = x.shape
    assert m % tm == 0
    return pl.pallas_call(
        _norm_body,
        grid=(m // tm,),
        in_specs=[pl.BlockSpec((tm, d), lambda i: (i, 0)), pl.BlockSpec((1, d), lambda i: (0, 0))],
        out_specs=pl.BlockSpec((tm, d), lambda i: (i, 0)),
        out_shape=jax.ShapeDtypeStruct((m, d), BF16),
        compiler_params=_params("parallel"),
        name=name,
    )(x, g.reshape(1, d))


def _in_proj_body(h_ref, win_ref, wbg_ref, proj_ref, gates_ref, perm_ref, *, n_proj_tiles):
    j = pl.program_id(1)
    tm = h_ref.shape[0]
    n = DIL_STEPS
    groups = len(DIL_CONFIGS)

    def store_regrouped(acc, dilation):
        tile = n * dilation
        for c in range(acc.shape[1] // HEAD_DIM):
            cols = slice(c * HEAD_DIM, (c + 1) * HEAD_DIM)
            perm_ref[c] = acc[:, cols]
            for t0 in range(0, tm, tile):
                for r in range(dilation):
                    rows = perm_ref[c, pl.ds(t0 + r, n, stride=dilation), :]
                    proj_ref[t0 + r * n:t0 + (r + 1) * n, cols] = rows.astype(BF16)

    @pl.when(j < n_proj_tiles)
    def _():
        acc = jnp.dot(h_ref[...], win_ref[...], preferred_element_type=F32)
        group = jnp.where(j < 3 * groups, lax.rem(j, groups), 0)
        def store_natural():
            proj_ref[...] = acc.astype(BF16)

        for g, (_, dilation) in enumerate(DIL_CONFIGS):
            store = store_natural if dilation == 1 else functools.partial(store_regrouped, acc, dilation)
            pl.when(group == g)(store)

    @pl.when(j >= n_proj_tiles)
    def _():
        acc = jnp.dot(h_ref[...], wbg_ref[...], preferred_element_type=F32)
        gates_ref[...] = jax.nn.sigmoid(acc).astype(BF16)


def _in_proj(h, w_in, w_bg, *, tm):
    t, d = h.shape
    tn = DIL_OUT
    assert t % tm == 0 and tm % (DIL_STEPS * max(dl for _, dl in DIL_CONFIGS)) == 0
    assert DIL_WIDTH == len(DIL_CONFIGS) * tn and w_in.shape[1] % tn == 0 and w_bg.shape[1] % tn == 0
    n_proj = w_in.shape[1] // tn
    n_gate = w_bg.shape[1] // tn
    return pl.pallas_call(
        functools.partial(_in_proj_body, n_proj_tiles=n_proj),
        grid=(t // tm, n_proj + n_gate),
        in_specs=[
            pl.BlockSpec((tm, d), lambda i, j: (i, 0)),
            pl.BlockSpec((d, tn), lambda i, j: (0, jnp.minimum(j, n_proj - 1))),
            pl.BlockSpec((d, tn), lambda i, j: (0, jnp.maximum(j - n_proj, 0))),
        ],
        out_specs=[
            pl.BlockSpec((tm, tn), lambda i, j: (i, jnp.minimum(j, n_proj - 1))),
            pl.BlockSpec((tm, tn), lambda i, j: (i, jnp.maximum(j - n_proj, 0))),
        ],
        out_shape=[jax.ShapeDtypeStruct((t, w_in.shape[1]), BF16),
                   jax.ShapeDtypeStruct((t, w_bg.shape[1]), BF16)],
        scratch_shapes=[pltpu.VMEM((tn // HEAD_DIM, tm, HEAD_DIM), F32)],
        compiler_params=_params("parallel", "arbitrary"),
        name="in_proj_gates",
    )(h, w_in, w_bg)


def _dilated_body(slope_ref, q_ref, kp_ref, kc_ref, vp_ref, vc_ref, o_ref, lse_ref, *scratch,
                  dilation, group):
    n = DIL_STEPS
    sub = pl.program_id(2)
    if dilation > 1:
        o_tile, lse_tile = scratch
        sub_rows = pl.ds(pl.multiple_of(sub * n, n), n)
    qi = lax.broadcasted_iota(jnp.int32, (n, n), 0)
    kj = lax.broadcasted_iota(jnp.int32, (n, n), 1)
    steps_cur = qi - kj
    valid_cur = steps_cur >= 0
    prev_limit = jnp.where(pl.program_id(1) > 0, 0, -n)
    valid_prev = steps_cur <= prev_limit
    dist_cur = (steps_cur * dilation).astype(F32)
    dist_prev = ((steps_cur + n) * dilation).astype(F32)
    for h in range(DIL_HEADS_PER_GROUP):
        cols = slice(h * HEAD_DIM, (h + 1) * HEAD_DIM)
        slope = slope_ref[group * DIL_HEADS_PER_GROUP + h]
        q = q_ref[:, cols]
        s_cur = lax.dot_general(q, kc_ref[:, cols], _NT, preferred_element_type=F32) * SCALE
        s_prev = lax.dot_general(q, kp_ref[:, cols], _NT, preferred_element_type=F32) * SCALE
        s_cur = jnp.where(valid_cur, s_cur - slope * dist_cur, NEG_INF)
        s_prev = jnp.where(valid_prev, s_prev - slope * dist_prev, NEG_INF)
        m = jnp.maximum(jnp.max(s_cur, axis=-1, keepdims=True),
                        jnp.max(s_prev, axis=-1, keepdims=True))
        p_cur = jnp.exp(s_cur - m)
        p_prev = jnp.exp(s_prev - m)
        den = jnp.sum(p_cur, axis=-1, keepdims=True) + jnp.sum(p_prev, axis=-1, keepdims=True)
        o = (jnp.dot(p_cur.astype(BF16), vc_ref[:, cols], preferred_element_type=F32)
             + jnp.dot(p_prev.astype(BF16), vp_ref[:, cols], preferred_element_type=F32))
        lse = jnp.broadcast_to(m + jnp.log(den), (n, HEAD_DIM))
        if dilation == 1:
            o_ref[h] = o / den
            lse_ref[h] = lse
        else:
            o_tile[h, sub_rows, :] = o / den
            lse_tile[h, sub_rows, :] = lse

    if dilation > 1:
        @pl.when(sub == dilation - 1)
        def _():
            for h in range(DIL_HEADS_PER_GROUP):
                for r in range(dilation):
                    natural = pl.ds(r, n, stride=dilation)
                    o_ref[h, natural, :] = o_tile[h, r * n:(r + 1) * n, :]
                    lse_ref[h, natural, :] = lse_tile[h, r * n:(r + 1) * n, :]


def _dilated_group(proj, slopes, group, dilation):
    b, s, _ = proj.shape
    n = DIL_STEPS
    tile = n * dilation
    assert s % tile == 0
    groups = len(DIL_CONFIGS)

    def spec(section, prev):
        def index(bi, i, r):
            return (bi, (jnp.maximum(i - 1, 0) if prev else i) * dilation + r,
                    section * groups + group)
        return pl.BlockSpec((None, n, DIL_OUT), index)

    out_block = (DIL_HEADS_PER_GROUP, tile, HEAD_DIM)
    out_spec = pl.BlockSpec((None,) + out_block, lambda bi, i, r: (bi, 0, i, 0))
    out_sds = jax.ShapeDtypeStruct((b, DIL_HEADS_PER_GROUP, s, HEAD_DIM), F32)
    scratch = [pltpu.VMEM(out_block, F32)] * 2 if dilation > 1 else []
    o, lse = pl.pallas_call(
        functools.partial(_dilated_body, dilation=dilation, group=group),
        grid=(b, s // tile, dilation),
        in_specs=[pl.BlockSpec(memory_space=pltpu.SMEM),
                  spec(0, False), spec(1, True), spec(1, False), spec(2, True), spec(2, False)],
        out_specs=[out_spec, out_spec],
        out_shape=[out_sds, out_sds],
        scratch_shapes=scratch,
        compiler_params=_params("parallel", "arbitrary", "arbitrary"),
        name=f"dilated_attn_g{group}",
    )(slopes, proj, proj, proj, proj, proj)
    return o, lse


def _moba_body(slope_ref, q_ref, k_ref, v_ref, o_ref, kmean_ref, vt_ref, bias_ref, sel_ref,
               *, nblk, group):
    blk = MOBA_BLOCK
    hd = HEAD_DIM
    heads = q_ref.shape[1] // hd
    own = pl.program_id(2)
    key_off = lax.broadcasted_iota(jnp.int32, (blk, blk), 0)
    qry_off = lax.broadcasted_iota(jnp.int32, (blk, blk), 1)
    slope2 = [slope_ref[pl.program_id(1) * heads + h] * LOG2E for h in range(heads)]

    @pl.when(own == 0)
    def _():
        def fill(jb, carry):
            start = pl.multiple_of(jb * blk, blk)
            rows = k_ref[pl.ds(start, blk), :].astype(F32)
            kmean_ref[pl.ds(jb, 1), :] = jnp.mean(rows, axis=0, keepdims=True)
            vrows = v_ref[pl.ds(start, blk), :].astype(F32)
            for h in range(heads):
                vt_ref[h, :, pl.ds(start, blk)] = vrows[:, h * hd:(h + 1) * hd].T.astype(BF16)
            return carry
        lax.fori_loop(0, nblk, fill, 0)
        for h in range(heads):
            bias_ref[h] = -slope2[h] * (qry_off - key_off).astype(F32)

    q = [q_ref[:, h * hd:(h + 1) * hd] for h in range(heads)]

    blk_id = lax.broadcasted_iota(jnp.int32, (nblk, blk), 0).astype(F32)
    for h in range(heads):
        gate = lax.dot_general(kmean_ref[:, h * hd:(h + 1) * hd], q[h].astype(F32), _NT,
                               precision=lax.Precision.HIGHEST, preferred_element_type=F32)
        gate = jnp.where(blk_id < own.astype(F32), gate, NEG_INF)
        sel = jnp.zeros((nblk, blk), F32)
        for _ in range(MOBA_TOPK):
            best = jnp.max(gate, axis=0, keepdims=True)
            is_best = (gate == best) & (gate > NEG_INF)
            pick = jnp.min(jnp.where(is_best, blk_id, float(nblk)), axis=0, keepdims=True)
            picked = blk_id == pick
            sel = jnp.where(picked, 1.0, sel)
            gate = jnp.where(picked, NEG_INF, gate)
        sel_ref[h] = sel

    def scores(h, start, rows):
        k = k_ref[pl.ds(start, rows), h * hd:(h + 1) * hd]
        return lax.dot_general(k, q[h], _NT, preferred_element_type=F32) * (SCALE * LOG2E)

    own_start = pl.multiple_of(own * blk, blk)
    init = []
    for h in range(heads):
        x = jnp.where(qry_off >= key_off, scores(h, own_start, blk) + bias_ref[h], NEG_INF)
        m0 = jnp.max(x, axis=0, keepdims=True)
        p = jnp.exp2(x - m0)
        l0 = jnp.sum(p, axis=0, keepdims=True)
        acc0 = jnp.dot(vt_ref[h, :, pl.ds(own_start, blk)], p.astype(BF16),
                       preferred_element_type=F32)
        init += [m0, l0, acc0]

    def softmax_update(h, i, x, m, l):
        xs, chosen, shift = [], [], []
        m_new = m
        for g in range(group):
            j = i * group + g
            xs.append(x[g * blk:(g + 1) * blk] + bias_ref[h])
            chosen.append(sel_ref[h, pl.ds(j, 1), :] > 0.5)
            shift.append(-slope2[h] * ((own - j) * blk).astype(F32))
            top = jnp.max(xs[g], axis=0, keepdims=True) + shift[g]
            m_new = jnp.maximum(m_new, jnp.where(chosen[g], top, NEG_INF))
        alpha = jnp.exp2(m - m_new)
        l = alpha * l
        ps = []
        for g in range(group):
            ref_g = jnp.where(chosen[g], m_new - shift[g], jnp.inf)
            ps.append(jnp.exp2(xs[g] - ref_g))
            l = l + jnp.sum(ps[g], axis=0, keepdims=True)
        return m_new, l, alpha, jnp.concatenate(ps, axis=0).astype(BF16)

    def past_blocks(i, carry):
        start = pl.multiple_of(i * (group * blk), group * blk)
        x, sm, out = {}, {}, [None] * (3 * heads)
        x[0] = scores(0, start, group * blk)
        for h in range(heads + 1):
            if h + 1 < heads:
                x[h + 1] = scores(h + 1, start, group * blk)
            if h < heads:
                sm[h] = softmax_update(h, i, x[h], carry[3 * h], carry[3 * h + 1])
            if h >= 1:
                m_new, l, alpha, p = sm[h - 1]
                acc = alpha * carry[3 * h - 1] + jnp.dot(
                    vt_ref[h - 1, :, pl.ds(start, group * blk)], p, preferred_element_type=F32)
                out[3 * h - 3:3 * h] = [m_new, l, acc]
        return tuple(out)

    n_groups = lax.div(own + (group - 1), group)
    final = lax.fori_loop(0, n_groups, past_blocks, tuple(init))
    for h in range(heads):
        _, l, acc = final[3 * h:3 * h + 3]
        o_ref[:, h * hd:(h + 1) * hd] = (acc / l).T.astype(o_ref.dtype)


def _moba(proj, slopes):
    b, s, _ = proj.shape
    assert s % (MOBA_BLOCK * MOBA_GROUP) == 0
    nblk = s // MOBA_BLOCK
    hp = MOBA_HEADS_PER_STEP
    width = hp * HEAD_DIM
    q0 = 3 * DIL_WIDTH // width
    k0 = q0 + MOBA_WIDTH // width
    v0 = k0 + MOBA_WIDTH // width
    y = pl.pallas_call(
        functools.partial(_moba_body, nblk=nblk, group=MOBA_GROUP),
        grid=(b, MOBA_HEADS // hp, nblk),
        in_specs=[
            pl.BlockSpec(memory_space=pltpu.SMEM),
            pl.BlockSpec((None, MOBA_BLOCK, width), lambda bi, h, i: (bi, i, q0 + h)),
            pl.BlockSpec((None, s, width), lambda bi, h, i: (bi, 0, k0 + h)),
            pl.BlockSpec((None, s, width), lambda bi, h, i: (bi, 0, v0 + h)),
        ],
        out_specs=pl.BlockSpec((None, MOBA_BLOCK, width), lambda bi, h, i: (bi, i, h)),
        out_shape=jax.ShapeDtypeStruct((b, s, MOBA_WIDTH), BF16),
        scratch_shapes=[pltpu.VMEM((nblk, width), F32),
                        pltpu.VMEM((hp, HEAD_DIM, s), BF16),
                        pltpu.VMEM((hp, MOBA_BLOCK, MOBA_BLOCK), F32),
                        pltpu.VMEM((hp, nblk, MOBA_BLOCK), F32)],
        compiler_params=_params("parallel", "parallel", "arbitrary"),
        name="moba_attn",
    )(slopes, proj, proj, proj)
    return y.reshape(b * s, MOBA_WIDTH)


def _merge_body(x_ref, o0_ref, o1_ref, o2_ref, l0_ref, l1_ref, l2_ref, ym_ref, gd_ref, gm_ref,
                wud_ref, wum_ref, wo_ref, out_ref):
    heads = []
    for h in range(DIL_HEADS_PER_GROUP):
        l0, l1, l2 = l0_ref[h], l1_ref[h], l2_ref[h]
        m = jnp.maximum(jnp.maximum(l0, l1), l2)
        e0, e1, e2 = jnp.exp(l0 - m), jnp.exp(l1 - m), jnp.exp(l2 - m)
        den = e0 + e1 + e2
        mixed = (e0 / den) * o0_ref[h] + (e1 / den) * o1_ref[h] + (e2 / den) * o2_ref[h]
        heads.append(mixed.astype(BF16))
    y_dil = jnp.concatenate(heads, axis=-1)
    lift_dil = jnp.dot(y_dil, wud_ref[...], preferred_element_type=F32)
    lift_moba = jnp.dot(ym_ref[...], wum_ref[...], preferred_element_type=F32)
    merged = gd_ref[...].astype(F32) * lift_dil + gm_ref[...].astype(F32) * lift_moba
    out_ref[...] = x_ref[...] + jnp.dot(merged.astype(BF16), wo_ref[...], preferred_element_type=F32)


def _merge(x, dil_outs, dil_lses, y_moba, gates, w_up_dil, w_up_moba, w_out, *, tm):
    t, d = x.shape
    s = dil_outs[0].shape[2]
    assert t % tm == 0 and s % tm == 0
    per_batch = s // tm
    row = lambda w: pl.BlockSpec((tm, w), lambda i: (i, 0))
    dil = pl.BlockSpec((None, DIL_HEADS_PER_GROUP, tm, HEAD_DIM),
                       lambda i: (i // per_batch, 0, i % per_batch, 0))
    return pl.pallas_call(
        _merge_body,
        grid=(t // tm,),
        in_specs=[row(d)] + [dil] * 6 + [
            row(MOBA_WIDTH),
            pl.BlockSpec((tm, d), lambda i: (i, 0)),
            pl.BlockSpec((tm, d), lambda i: (i, 1)),
            _resident((DIL_OUT, d)), _resident((MOBA_WIDTH, d)), _resident((d, d)),
        ],
        out_specs=row(d),
        out_shape=jax.ShapeDtypeStruct((t, d), F32),
        compiler_params=_params("parallel"),
        name="merge_out_proj",
    )(x, *dil_outs, *dil_lses, y_moba, gates, gates, w_up_dil, w_up_moba, w_out)


def _cross_body(x_ref, g_ref, wq_ref, kv_ref, wo_ref, out_ref):
    x = x_ref[...]
    h = _rms(x, g_ref[...]).astype(BF16)
    q = jnp.dot(h, wq_ref[...], preferred_element_type=F32).astype(BF16)
    heads = []
    for hd in range(MEM_HEADS):
        k = kv_ref[:, hd * HEAD_DIM:(hd + 1) * HEAD_DIM]
        v = kv_ref[:, MEM_WIDTH + hd * HEAD_DIM:MEM_WIDTH + (hd + 1) * HEAD_DIM]
        s = lax.dot_general(q[:, hd * HEAD_DIM:(hd + 1) * HEAD_DIM], k, _NT,
                            preferred_element_type=F32) * SCALE
        p = jnp.exp(s - jnp.max(s, axis=-1, keepdims=True))
        den = jnp.sum(p, axis=-1, keepdims=True)
        heads.append((jnp.dot(p.astype(BF16), v, preferred_element_type=F32) / den).astype(BF16))
    o = jnp.concatenate(heads, axis=-1)
    out_ref[...] = x + jnp.dot(o, wo_ref[...], preferred_element_type=F32)


def _cross(x, g, w_q, kv, w_o, *, batch, tm):
    t, d = x.shape
    s = t // batch
    mem_len = kv.shape[1]
    assert s % tm == 0
    per_batch = s // tm
    return pl.pallas_call(
        _cross_body,
        grid=(t // tm,),
        in_specs=[
            pl.BlockSpec((tm, d), lambda i: (i, 0)),
            _resident((1, d)),
            _resident((d, MEM_WIDTH)),
            pl.BlockSpec((None, mem_len, 2 * MEM_WIDTH), lambda i: (i // per_batch, 0, 0)),
            _resident((MEM_WIDTH, d)),
        ],
        out_specs=pl.BlockSpec((tm, d), lambda i: (i, 0)),
        out_shape=jax.ShapeDtypeStruct((t, d), F32),
        compiler_params=_params("parallel"),
        name="memory_cross_attn",
    )(x, g.reshape(1, d), w_q, kv, w_o)


ROUTE_COLS = 8
MOE_ROW_TILE = 256


def _route_body(x_ref, g_ref, wr_ref, br_ref, info_ref, counts_ref, run_ref, tri_ref):
    tm = x_ref.shape[0]
    n_route = N_GROUPS + N_EXPERTS
    lane = lax.broadcasted_iota(jnp.int32, (tm, n_route), 1).astype(F32)

    @pl.when(pl.program_id(0) == 0)
    def _():
        run_ref[...] = jnp.zeros_like(run_ref)
        earlier = (lax.broadcasted_iota(jnp.int32, (tm, tm), 0)
                   > lax.broadcasted_iota(jnp.int32, (tm, tm), 1))
        tri_ref[...] = jnp.where(earlier, 1.0, 0.0).astype(BF16)

    t = _rms(x_ref[...], g_ref[...])
    logits = jnp.dot(t, wr_ref[...], precision=lax.Precision.HIGHEST,
                     preferred_element_type=F32) + br_ref[...]
    none = float(n_route)
    glog = jnp.where(lane < N_GROUPS, logits, NEG_INF)
    gmax = jnp.max(glog, axis=-1, keepdims=True)
    gsel = jnp.min(jnp.where(glog == gmax, lane, none), axis=-1, keepdims=True)
    pg = 1.0 / jnp.sum(jnp.exp(glog - gmax), axis=-1, keepdims=True)
    first = N_GROUPS + gsel * EXPERTS_PER_GROUP
    in_group = (lane >= first) & (lane < first + EXPERTS_PER_GROUP)
    elog = jnp.where(in_group, logits, NEG_INF)
    top1 = jnp.max(elog, axis=-1, keepdims=True)
    i1 = jnp.min(jnp.where(elog == top1, lane, none), axis=-1, keepdims=True)
    rest = jnp.where(lane == i1, NEG_INF, elog)
    top2 = jnp.max(rest, axis=-1, keepdims=True)
    i2 = jnp.min(jnp.where(rest == top2, lane, none), axis=-1, keepdims=True)
    e2 = jnp.exp(top2 - top1)
    w1 = pg / (1.0 + e2)
    w2 = pg * e2 / (1.0 + e2)

    hit1 = lane == i1
    hit2 = lane == i2
    assigned = jnp.where(hit1 | hit2, 1.0, 0.0)
    before = jnp.dot(tri_ref[...], assigned.astype(BF16), preferred_element_type=F32) + run_ref[...]
    rank1 = jnp.sum(jnp.where(hit1, before, 0.0), axis=-1, keepdims=True)
    rank2 = jnp.sum(jnp.where(hit2, before, 0.0), axis=-1, keepdims=True)
    run_ref[...] += jnp.sum(assigned, axis=0, keepdims=True)
    counts_ref[...] = run_ref[...]

    col = lax.broadcasted_iota(jnp.int32, (tm, ROUTE_COLS), 1)
    fields = (i1 - N_GROUPS, i2 - N_GROUPS, rank1, rank2, w1, w2)
    info = jnp.zeros((tm, ROUTE_COLS), F32)
    for c, field in enumerate(fields):
        info = jnp.where(col == c, field, info)
    info_ref[...] = info


def _route(x, g, w_route, b_route, *, tm):
    t, d = x.shape
    n_route = N_GROUPS + N_EXPERTS
    assert t % tm == 0
    return pl.pallas_call(
        _route_body,
        grid=(t // tm,),
        in_specs=[
            pl.BlockSpec((tm, d), lambda i: (i, 0)),
            pl.BlockSpec((1, d), lambda i: (0, 0)),
            pl.BlockSpec((d, n_route), lambda i: (0, 0)),
            pl.BlockSpec((1, n_route), lambda i: (0, 0)),
        ],
        out_specs=[pl.BlockSpec((tm, ROUTE_COLS), lambda i: (i, 0)),
                   pl.BlockSpec((1, n_route), lambda i: (0, 0))],
        out_shape=[jax.ShapeDtypeStruct((t, ROUTE_COLS), F32),
                   jax.ShapeDtypeStruct((1, n_route), F32)],
        scratch_shapes=[pltpu.VMEM((1, n_route), F32), pltpu.VMEM((tm, tm), BF16)],
        compiler_params=_params("arbitrary"),
        name="moe_route",
    )(x, g.reshape(1, d), w_route, b_route.reshape(1, n_route))


def _row_copies_wait(src_rows, dst_rows, sem):
    pltpu.make_async_copy(src_rows, dst_rows, sem).wait()


def _dispatch_body(pos_ref, x_ref, g_ref, sorted_in_ref, sorted_ref, t_ref, sem):
    del sorted_in_ref
    tm = x_ref.shape[0]
    base = pl.program_id(0) * tm
    t_ref[...] = _rms(x_ref[...], g_ref[...])

    def issue(r, carry):
        for k in range(MOE_TOPK):
            dst = pos_ref[k, base + r]
            pltpu.make_async_copy(t_ref.at[pl.ds(r, 1)], sorted_ref.at[pl.ds(dst, 1)], sem).start()
        return carry

    lax.fori_loop(0, tm, issue, 0, unroll=8)
    for _ in range(MOE_TOPK):
        _row_copies_wait(t_ref, sorted_ref.at[pl.ds(0, tm)], sem)


def _dispatch(x, g, pos, sorted_zero, *, tm):
    t, d = x.shape
    assert t % tm == 0
    return pl.pallas_call(
        _dispatch_body,
        grid_spec=pltpu.PrefetchScalarGridSpec(
            num_scalar_prefetch=1,
            grid=(t // tm,),
            in_specs=[pl.BlockSpec((tm, d), lambda i, pos: (i, 0)),
                      pl.BlockSpec((1, d), lambda i, pos: (0, 0)),
                      pl.BlockSpec(memory_space=pl.ANY)],
            out_specs=pl.BlockSpec(memory_space=pl.ANY),
            scratch_shapes=[pltpu.VMEM((tm, d), F32), pltpu.SemaphoreType.DMA(())],
        ),
        out_shape=jax.ShapeDtypeStruct(sorted_zero.shape, F32),
        input_output_aliases={3: 0},
        compiler_params=pltpu.CompilerParams(dimension_semantics=("arbitrary",),
                                             vmem_limit_bytes=VMEM_LIMIT_BYTES,
                                             disable_bounds_checks=True),
        name="moe_dispatch",
    )(pos, x, g.reshape(1, d), sorted_zero)


def _experts_body(tile_expert_ref, n_tiles_ref, x_ref, wg_ref, wu_ref, wd_ref, y_ref):
    del tile_expert_ref
    in_use = pl.program_id(0) < n_tiles_ref[0]

    @pl.when(in_use)
    def _():
        t = x_ref[...].astype(BF16)
        gate = jnp.dot(t, wg_ref[...], preferred_element_type=F32)
        up = jnp.dot(t, wu_ref[...], preferred_element_type=F32)
        a = jax.nn.silu(gate) * up
        y_ref[...] = jnp.dot(a.astype(BF16), wd_ref[...], preferred_element_type=F32)

    @pl.when(jnp.logical_not(in_use))
    def _():
        y_ref[...] = jnp.zeros_like(y_ref)


def _experts(sorted_rows, tile_expert, n_tiles, w_gate, w_up, w_down):
    p, d = sorted_rows.shape
    ff = w_gate.shape[-1]
    tr = MOE_ROW_TILE
    assert p % tr == 0
    used = lambda i, nt: jnp.minimum(i, nt[0] - 1)
    return pl.pallas_call(
        _experts_body,
        grid_spec=pltpu.PrefetchScalarGridSpec(
            num_scalar_prefetch=2,
            grid=(p // tr,),
            in_specs=[pl.BlockSpec((tr, d), lambda i, te, nt: (used(i, nt), 0)),
                      pl.BlockSpec((None, d, ff), lambda i, te, nt: (te[used(i, nt)], 0, 0)),
                      pl.BlockSpec((None, d, ff), lambda i, te, nt: (te[used(i, nt)], 0, 0)),
                      pl.BlockSpec((None, ff, d), lambda i, te, nt: (te[used(i, nt)], 0, 0))],
            out_specs=pl.BlockSpec((tr, d), lambda i, te, nt: (i, 0)),
        ),
        out_shape=jax.ShapeDtypeStruct((p, d), F32),
        compiler_params=_params("arbitrary"),
        name="moe_experts",
    )(tile_expert, n_tiles, sorted_rows, w_gate, w_up, w_down)


def _combine_body(pos_ref, x_ref, info_ref, gf_ref, y_sorted_ref, out_ref, rows_ref, sem):
    tm = x_ref.shape[0]
    base = pl.program_id(0) * tm

    def issue(r, carry):
        for k in range(MOE_TOPK):
            src = pos_ref[k, base + r]
            pltpu.make_async_copy(y_sorted_ref.at[pl.ds(src, 1)], rows_ref.at[k, pl.ds(r, 1)],
                                  sem).start()
        return carry

    lax.fori_loop(0, tm, issue, 0, unroll=8)
    for k in range(MOE_TOPK):
        _row_copies_wait(y_sorted_ref.at[pl.ds(0, tm)], rows_ref.at[k], sem)

    info = info_ref[...]
    y = info[:, 4:5] * rows_ref[0] + info[:, 5:6] * rows_ref[1]
    out_ref[...] = _rms(x_ref[...] + y, gf_ref[...])


def _combine(x, info, pos, y_sorted, g_final, *, tm):
    t, d = x.shape
    assert t % tm == 0
    return pl.pallas_call(
        _combine_body,
        grid_spec=pltpu.PrefetchScalarGridSpec(
            num_scalar_prefetch=1,
            grid=(t // tm,),
            in_specs=[pl.BlockSpec((tm, d), lambda i, pos: (i, 0)),
                      pl.BlockSpec((tm, ROUTE_COLS), lambda i, pos: (i, 0)),
                      pl.BlockSpec((1, d), lambda i, pos: (0, 0)),
                      pl.BlockSpec(memory_space=pl.ANY)],
            out_specs=pl.BlockSpec((tm, d), lambda i, pos: (i, 0)),
            scratch_shapes=[pltpu.VMEM((MOE_TOPK, tm, d), F32), pltpu.SemaphoreType.DMA(())],
        ),
        out_shape=jax.ShapeDtypeStruct((t, d), F32),
        compiler_params=pltpu.CompilerParams(dimension_semantics=("arbitrary",),
                                             vmem_limit_bytes=VMEM_LIMIT_BYTES,
                                             disable_bounds_checks=True),
        name="moe_combine_final_norm",
    )(pos, x, info, g_final.reshape(1, d), y_sorted)


def _moe(x, g, w_route, b_route, w_gate, w_up, w_down, g_final):
    t, d = x.shape
    tr = MOE_ROW_TILE
    assert (MOE_TOPK * t) % tr == 0
    info, counts = _route(x, g, w_route, b_route, tm=min(512, t))

    expert = info[:, 0:MOE_TOPK].astype(jnp.int32)
    rank = info[:, MOE_TOPK:2 * MOE_TOPK].astype(jnp.int32)
    count = counts[0, N_GROUPS:].astype(jnp.int32)
    seg_tiles = (count + (tr - 1)) // tr
    seg_end = jnp.cumsum(seg_tiles)
    seg_start_row = (seg_end - seg_tiles) * tr
    pos = (seg_start_row[expert] + rank).T
    max_tiles = (MOE_TOPK * t) // tr + N_EXPERTS
    tile_id = jnp.arange(max_tiles, dtype=jnp.int32)
    tile_expert = jnp.minimum(
        jnp.sum((seg_end[None, :] <= tile_id[:, None]).astype(jnp.int32), axis=1), N_EXPERTS - 1)
    n_tiles = seg_end[-1:].astype(jnp.int32)

    sorted_rows = _dispatch(x, g, pos, jnp.zeros((max_tiles * tr, d), F32), tm=min(256, t))
    y_sorted = _experts(sorted_rows, tile_expert, n_tiles, w_gate, w_up, w_down)
    return _combine(x, info, pos, y_sorted, g_final, tm=min(256, t))


def _layer(x, mem, attn_norm, w_in, w_up_dil, w_up_moba, w_branch_gate, w_out, cross_norm,
           mem_norm, w_q_mem, w_kv_mem, w_o_mem):
    b, s, d = x.shape
    t = b * s
    xt = x.reshape(t, d)
    h = _norm(xt, attn_norm, tm=min(512, t), name="attn_norm")
    proj, gates = _in_proj(h, w_in.astype(BF16), w_branch_gate.astype(BF16), tm=min(2048, t))
    proj = proj.reshape(b, s, IN_WIDTH)
    dil_slopes = _alibi_slopes(DIL_HEADS)
    dil = [_dilated_group(proj, dil_slopes, g, dilation)
           for g, (_, dilation) in enumerate(DIL_CONFIGS)]
    y_moba = _moba(proj, _alibi_slopes(MOBA_HEADS))
    x1 = _merge(xt, [o for o, _ in dil], [l for _, l in dil], y_moba, gates,
                w_up_dil.astype(BF16), w_up_moba.astype(BF16), w_out.astype(BF16), tm=min(256, t))
    mem_len = mem.shape[1]
    kv = _norm_matmul(mem.reshape(b * mem_len, d), mem_norm, w_kv_mem.astype(BF16),
                      tm=b * mem_len, tn=512, name="norm_mem_kv").reshape(b, mem_len, 2 * MEM_WIDTH)
    return _cross(x1, cross_norm, w_q_mem.astype(BF16), kv, w_o_mem.astype(BF16),
                  batch=b, tm=min(512, s))


def kernel(x, mem, attn_norm, w_in, w_up_dil, w_up_moba, w_branch_gate, w_out, cross_norm, mem_norm,
           w_q_mem, w_kv_mem, w_o_mem, ffn_norm, w_router_group, b_router_group, w_router_expert,
           b_router_expert, w_expert_gate, w_expert_up, w_expert_down, final_norm):
    b, s, d = x.shape
    depth = attn_norm.shape[0]
    assert depth == 1, "the final norm is fused into the last layer's MoE call"
    l = 0
    x2 = _layer(x, mem, attn_norm[l], w_in[l], w_up_dil[l], w_up_moba[l], w_branch_gate[l], w_out[l],
                cross_norm[l], mem_norm[l], w_q_mem[l], w_kv_mem[l], w_o_mem[l])
    w_route = jnp.concatenate([w_router_group[l], w_router_expert[l]], axis=1)
    b_route = jnp.concatenate([b_router_group[l], b_router_expert[l]], axis=0)
    out = _moe(x2, ffn_norm[l], w_route, b_route, w_expert_gate[l].astype(BF16),
               w_expert_up[l].astype(BF16), w_expert_down[l].astype(BF16), final_norm)
    return out.reshape(b, s, d)
```

```python
import functools

import numpy as np
import jax
import jax.numpy as jnp
from jax import lax
from jax.experimental import pallas as pl
from jax.experimental.pallas import tpu as pltpu

F32 = jnp.float32
BF16 = jnp.bfloat16

HEAD_DIM = 128
DIL_CONFIGS = ((128, 1), (512, 4), (2048, 16))
DIL_HEADS_PER_GROUP = 4
DIL_HEADS = DIL_HEADS_PER_GROUP * len(DIL_CONFIGS)
DIL_WIDTH = DIL_HEADS * HEAD_DIM
DIL_OUT = DIL_HEADS_PER_GROUP * HEAD_DIM
DIL_STEPS = 128
MOBA_HEADS = 8
MOBA_WIDTH = MOBA_HEADS * HEAD_DIM
MOBA_BLOCK = 256
MOBA_TOPK = 3
IN_WIDTH = 3 * (DIL_WIDTH + MOBA_WIDTH)
MEM_HEADS = 4
MEM_WIDTH = MEM_HEADS * HEAD_DIM
N_GROUPS = 4
EXPERTS_PER_GROUP = 8
N_EXPERTS = N_GROUPS * EXPERTS_PER_GROUP
MOE_TOPK = 2
RMS_EPS = 1e-6
SCALE = HEAD_DIM ** -0.5
NEG_INF = float("-inf")
LOG2E = 1.4426950408889634
MOBA_GROUP = 4
MOBA_HEADS_PER_STEP = 4

VMEM_LIMIT_BYTES = 56 * 1024 * 1024
_NT = (((1,), (1,)), ((), ()))


def _alibi_slopes(n):
    return jnp.asarray(2.0 ** (-8.0 * np.arange(1, n + 1) / n), dtype=F32)


def _rms(x, g):
    return x * lax.rsqrt(jnp.mean(x * x, axis=-1, keepdims=True) + RMS_EPS) * g


def _dot_split3(a, b):
    a_hi = a.astype(BF16)
    b_hi = b.astype(BF16)
    a_lo = (a - a_hi.astype(F32)).astype(BF16)
    b_lo = (b - b_hi.astype(F32)).astype(BF16)
    dot = functools.partial(jnp.dot, preferred_element_type=F32)
    return dot(a_hi, b_hi) + (dot(a_lo, b_hi) + dot(a_hi, b_lo))


def _params(*sem, flags=None):
    return pltpu.CompilerParams(dimension_semantics=sem, vmem_limit_bytes=VMEM_LIMIT_BYTES,
                                flags=flags)


def _resident(shape):
    nd = len(shape)
    return pl.BlockSpec(shape, lambda *_: (0,) * nd, pipeline_mode=pl.Buffered(1))


def _norm_matmul_body(x_ref, g_ref, w_ref, o_ref, h_ref, *, sigmoid):
    @pl.when(pl.program_id(1) == 0)
    def _():
        h_ref[...] = _rms(x_ref[...], g_ref[...]).astype(BF16)

    acc = jnp.dot(h_ref[...], w_ref[...], preferred_element_type=F32)
    if sigmoid:
        acc = jax.nn.sigmoid(acc)
    o_ref[...] = acc.astype(o_ref.dtype)


def _norm_matmul(x, g, w, *, tm, tn, sigmoid=False, name):
    m, d = x.shape
    n = w.shape[1]
    assert m % tm == 0 and n % tn == 0
    return pl.pallas_call(
        functools.partial(_norm_matmul_body, sigmoid=sigmoid),
        grid=(m // tm, n // tn),
        in_specs=[
            pl.BlockSpec((tm, d), lambda i, j: (i, 0)),
            pl.BlockSpec((1, d), lambda i, j: (0, 0)),
            pl.BlockSpec((d, tn), lambda i, j: (0, j)),
        ],
        out_specs=pl.BlockSpec((tm, tn), lambda i, j: (i, j)),
        out_shape=jax.ShapeDtypeStruct((m, n), BF16),
        scratch_shapes=[pltpu.VMEM((tm, d), BF16)],
        compiler_params=_params("parallel", "arbitrary"),
        name=name,
    )(x, g.reshape(1, d), w)


def _norm_body(x_ref, g_ref, o_ref):
    o_ref[...] = _rms(x_ref[...], g_ref[...]).astype(o_ref.dtype)


def _norm(x, g, *, tm, name):
    m, d = x.shape
    assert m % tm == 0
    return pl.pallas_call(
        _norm_body,
        grid=(m // tm,),
        in_specs=[pl.BlockSpec((tm, d), lambda i: (i, 0)), pl.BlockSpec((1, d), lambda i: (0, 0))],
        out_specs=pl.BlockSpec((tm, d), lambda i: (i, 0)),
        out_shape=jax.ShapeDtypeStruct((m, d), BF16),
        compiler_params=_params("parallel"),
        name=name,
    )(x, g.reshape(1, d))


def _in_proj_body(h_ref, win_ref, wbg_ref, proj_ref, gates_ref, perm_ref, *, n_proj_tiles):
    j = pl.program_id(1)
    tm = h_ref.shape[0]
    n = DIL_STEPS
    groups = len(DIL_CONFIGS)

    def store_regrouped(acc, dilation):
        tile = n * dilation
        for c in range(acc.shape[1] // HEAD_DIM):
            cols = slice(c * HEAD_DIM, (c + 1) * HEAD_DIM)
            perm_ref[c] = acc[:, cols]
            for t0 in range(0, tm, tile):
                for r in range(dilation):
                    rows = perm_ref[c, pl.ds(t0 + r, n, stride=dilation), :]
                    proj_ref[t0 + r * n:t0 + (r + 1) * n, cols] = rows.astype(BF16)

    @pl.when(j < n_proj_tiles)
    def _():
        acc = jnp.dot(h_ref[...], win_ref[...], preferred_element_type=F32)
        group = jnp.where(j < 3 * groups, lax.rem(j, groups), 0)
        def store_natural():
            proj_ref[...] = acc.astype(BF16)

        for g, (_, dilation) in enumerate(DIL_CONFIGS):
            store = store_natural if dilation == 1 else functools.partial(store_regrouped, acc, dilation)
            pl.when(group == g)(store)

    @pl.when(j >= n_proj_tiles)
    def _():
        acc = jnp.dot(h_ref[...], wbg_ref[...], preferred_element_type=F32)
        gates_ref[...] = jax.nn.sigmoid(acc).astype(BF16)


def _in_proj(h, w_in, w_bg, *, tm):
    t, d = h.shape
    tn = DIL_OUT
    assert t % tm == 0 and tm % (DIL_STEPS * max(dl for _, dl in DIL_CONFIGS)) == 0
    assert DIL_WIDTH == len(DIL_CONFIGS) * tn and w_in.shape[1] % tn == 0 and w_bg.shape[1] % tn == 0
    n_proj = w_in.shape[1] // tn
    n_gate = w_bg.shape[1] // tn
    return pl.pallas_call(
        functools.partial(_in_proj_body, n_proj_tiles=n_proj),
        grid=(t // tm, n_proj + n_gate),
        in_specs=[
            pl.BlockSpec((tm, d), lambda i, j: (i, 0)),
            pl.BlockSpec((d, tn), lambda i, j: (0, jnp.minimum(j, n_proj - 1))),
            pl.BlockSpec((d, tn), lambda i, j: (0, jnp.maximum(j - n_proj, 0))),
        ],
        out_specs=[
            pl.BlockSpec((tm, tn), lambda i, j: (i, jnp.minimum(j, n_proj - 1))),
            pl.BlockSpec((tm, tn), lambda i, j: (i, jnp.maximum(j - n_proj, 0))),
        ],
        out_shape=[jax.ShapeDtypeStruct((t, w_in.shape[1]), BF16),
                   jax.ShapeDtypeStruct((t, w_bg.shape[1]), BF16)],
        scratch_shapes=[pltpu.VMEM((tn // HEAD_DIM, tm, HEAD_DIM), F32)],
        compiler_params=_params("parallel", "arbitrary"),
        name="in_proj_gates",
    )(h, w_in, w_bg)


def _dilated_body(slope_ref, q_ref, kp_ref, kc_ref, vp_ref, vc_ref, o_ref, lse_ref, *scratch,
                  dilation, group):
    n = DIL_STEPS
    sub = pl.program_id(2)
    if dilation > 1:
        o_tile, lse_tile = scratch
        sub_rows = pl.ds(pl.multiple_of(sub * n, n), n)
    qi = lax.broadcasted_iota(jnp.int32, (n, n), 0)
    kj = lax.broadcasted_iota(jnp.int32, (n, n), 1)
    steps_cur = qi - kj
    valid_cur = steps_cur >= 0
    prev_limit = jnp.where(pl.program_id(1) > 0, 0, -n)
    valid_prev = steps_cur <= prev_limit
    dist_cur = (steps_cur * dilation).astype(F32)
    dist_prev = ((steps_cur + n) * dilation).astype(F32)
    heads = range(DIL_HEADS_PER_GROUP)
    cols_of = [slice(h * HEAD_DIM, (h + 1) * HEAD_DIM) for h in heads]
    raw = []
    for h in heads:
        q = q_ref[:, cols_of[h]]
        raw.append((lax.dot_general(q, kc_ref[:, cols_of[h]], _NT, preferred_element_type=F32),
                    lax.dot_general(q, kp_ref[:, cols_of[h]], _NT, preferred_element_type=F32)))
    probs = []
    for h in heads:
        slope = slope_ref[group * DIL_HEADS_PER_GROUP + h]
        s_cur = jnp.where(valid_cur, raw[h][0] * SCALE - slope * dist_cur, NEG_INF)
        s_prev = jnp.where(valid_prev, raw[h][1] * SCALE - slope * dist_prev, NEG_INF)
        m = jnp.maximum(jnp.max(s_cur, axis=-1, keepdims=True),
                        jnp.max(s_prev, axis=-1, keepdims=True))
        p_cur = jnp.exp(s_cur - m)
        p_prev = jnp.exp(s_prev - m)
        den = jnp.sum(p_cur, axis=-1, keepdims=True) + jnp.sum(p_prev, axis=-1, keepdims=True)
        probs.append((p_cur.astype(BF16), p_prev.astype(BF16), m, den))
    for h in heads:
        cols = cols_of[h]
        p_cur, p_prev, m, den = probs[h]
        o = (jnp.dot(p_cur, vc_ref[:, cols], preferred_element_type=F32)
             + jnp.dot(p_prev, vp_ref[:, cols], preferred_element_type=F32))
        lse = jnp.broadcast_to(m + jnp.log(den), (n, HEAD_DIM))
        if dilation == 1:
            o_ref[h] = o / den
            lse_ref[h] = lse
        else:
            o_tile[h, sub_rows, :] = o / den
            lse_tile[h, sub_rows, :] = lse

    if dilation > 1:
        @pl.when(sub == dilation - 1)
        def _():
            for h in range(DIL_HEADS_PER_GROUP):
                for r in range(dilation):
                    natural = pl.ds(r, n, stride=dilation)
                    o_ref[h, natural, :] = o_tile[h, r * n:(r + 1) * n, :]
                    lse_ref[h, natural, :] = lse_tile[h, r * n:(r + 1) * n, :]


def _dilated_group(proj, slopes, group, dilation):
    b, s, _ = proj.shape
    n = DIL_STEPS
    tile = n * dilation
    assert s % tile == 0
    groups = len(DIL_CONFIGS)

    def spec(section, prev):
        def index(bi, i, r):
            return (bi, (jnp.maximum(i - 1, 0) if prev else i) * dilation + r,
                    section * groups + group)
        return pl.BlockSpec((None, n, DIL_OUT), index)

    out_block = (DIL_HEADS_PER_GROUP, tile, HEAD_DIM)
    out_spec = pl.BlockSpec((None,) + out_block, lambda bi, i, r: (bi, 0, i, 0))
    out_sds = jax.ShapeDtypeStruct((b, DIL_HEADS_PER_GROUP, s, HEAD_DIM), F32)
    scratch = [pltpu.VMEM(out_block, F32)] * 2 if dilation > 1 else []
    o, lse = pl.pallas_call(
        functools.partial(_dilated_body, dilation=dilation, group=group),
        grid=(b, s // tile, dilation),
        in_specs=[pl.BlockSpec(memory_space=pltpu.SMEM),
                  spec(0, False), spec(1, True), spec(1, False), spec(2, True), spec(2, False)],
        out_specs=[out_spec, out_spec],
        out_shape=[out_sds, out_sds],
        scratch_shapes=scratch,
        compiler_params=_params("parallel", "arbitrary", "arbitrary"),
        name=f"dilated_attn_g{group}",
    )(slopes, proj, proj, proj, proj, proj)
    return o, lse


def _moba_body(slope_ref, q_ref, k_ref, v_ref, o_ref, kmean_ref, vt_ref, bias_ref, sel_ref,
               *, nblk, group):
    blk = MOBA_BLOCK
    hd = HEAD_DIM
    heads = q_ref.shape[1] // hd
    own = pl.program_id(2)
    key_off = lax.broadcasted_iota(jnp.int32, (blk, blk), 0)
    qry_off = lax.broadcasted_iota(jnp.int32, (blk, blk), 1)
    slope2 = [slope_ref[pl.program_id(1) * heads + h] * LOG2E for h in range(heads)]

    @pl.when(own == 0)
    def _():
        def fill(jb, carry):
            start = pl.multiple_of(jb * blk, blk)
            rows = k_ref[pl.ds(start, blk), :].astype(F32)
            kmean_ref[pl.ds(jb, 1), :] = jnp.mean(rows, axis=0, keepdims=True)
            vrows = v_ref[pl.ds(start, blk), :].astype(F32)
            for h in range(heads):
                vt_ref[h, :, pl.ds(start, blk)] = vrows[:, h * hd:(h + 1) * hd].T.astype(BF16)
            return carry
        lax.fori_loop(0, nblk, fill, 0)
        for h in range(heads):
            bias_ref[h] = -slope2[h] * (qry_off - key_off).astype(F32)

    q = [q_ref[:, h * hd:(h + 1) * hd] for h in range(heads)]

    blk_id = lax.broadcasted_iota(jnp.int32, (nblk, blk), 0).astype(F32)
    gates = []
    for h in range(heads):
        gate = lax.dot_general(kmean_ref[:, h * hd:(h + 1) * hd], q[h].astype(F32), _NT,
                               precision=lax.Precision.HIGHEST, preferred_element_type=F32)
        gates.append(jnp.where(blk_id < own.astype(F32), gate, NEG_INF))

    def scores(h, start, rows):
        k = k_ref[pl.ds(start, rows), h * hd:(h + 1) * hd]
        return lax.dot_general(k, q[h], _NT, preferred_element_type=F32) * (SCALE * LOG2E)

    own_start = pl.multiple_of(own * blk, blk)
    own_x = [scores(h, own_start, blk) for h in range(heads)]

    sels = [jnp.zeros((nblk, blk), F32) for _ in range(heads)]
    for _ in range(MOBA_TOPK):
        for h in range(heads):
            best = jnp.max(gates[h], axis=0, keepdims=True)
            is_best = (gates[h] == best) & (gates[h] > NEG_INF)
            pick = jnp.min(jnp.where(is_best, blk_id, float(nblk)), axis=0, keepdims=True)
            picked = blk_id == pick
            sels[h] = jnp.where(picked, 1.0, sels[h])
            gates[h] = jnp.where(picked, NEG_INF, gates[h])
    for h in range(heads):
        sel_ref[h] = sels[h]

    own_p = []
    for h in range(heads):
        x = jnp.where(qry_off >= key_off, own_x[h] + bias_ref[h], NEG_INF)
        m0 = jnp.max(x, axis=0, keepdims=True)
        p = jnp.exp2(x - m0)
        own_p.append((m0, jnp.sum(p, axis=0, keepdims=True), p.astype(BF16)))
    init = []
    for h in range(heads):
        m0, l0, p = own_p[h]
        acc0 = jnp.dot(vt_ref[h, :, pl.ds(own_start, blk)], p, preferred_element_type=F32)
        init += [m0, l0, acc0]

    def softmax_update(h, i, x, m, l):
        xs, chosen, shift = [], [], []
        m_new = m
        for g in range(group):
            j = i * group + g
            xs.append(x[g * blk:(g + 1) * blk] + bias_ref[h])
            chosen.append(sel_ref[h, pl.ds(j, 1), :] > 0.5)
            shift.append(-slope2[h] * ((own - j) * blk).astype(F32))
            top = jnp.max(xs[g], axis=0, keepdims=True) + shift[g]
            m_new = jnp.maximum(m_new, jnp.where(chosen[g], top, NEG_INF))
        alpha = jnp.exp2(m - m_new)
        l = alpha * l
        ps = []
        for g in range(group):
            ref_g = jnp.where(chosen[g], m_new - shift[g], jnp.inf)
            ps.append(jnp.exp2(xs[g] - ref_g))
            l = l + jnp.sum(ps[g], axis=0, keepdims=True)
        return m_new, l, alpha, jnp.concatenate(ps, axis=0).astype(BF16)

    def past_blocks(i, carry):
        start = pl.multiple_of(i * (group * blk), group * blk)
        x, sm, out = {}, {}, [None] * (3 * heads)
        x[0] = scores(0, start, group * blk)
        for h in range(heads + 1):
            if h + 1 < heads:
                x[h + 1] = scores(h + 1, start, group * blk)
            if h < heads:
                sm[h] = softmax_update(h, i, x[h], carry[3 * h], carry[3 * h + 1])
            if h >= 1:
                m_new, l, alpha, p = sm[h - 1]
                acc = alpha * carry[3 * h - 1] + jnp.dot(
                    vt_ref[h - 1, :, pl.ds(start, group * blk)], p, preferred_element_type=F32)
                out[3 * h - 3:3 * h] = [m_new, l, acc]
        return tuple(out)

    n_groups = lax.div(own + (group - 1), group)
    final = lax.fori_loop(0, n_groups, past_blocks, tuple(init))
    for h in range(heads):
        _, l, acc = final[3 * h:3 * h + 3]
        o_ref[:, h * hd:(h + 1) * hd] = (acc / l).T.astype(o_ref.dtype)


def _moba(proj, slopes):
    b, s, _ = proj.shape
    assert s % (MOBA_BLOCK * MOBA_GROUP) == 0
    nblk = s // MOBA_BLOCK
    hp = MOBA_HEADS_PER_STEP
    width = hp * HEAD_DIM
    q0 = 3 * DIL_WIDTH // width
    k0 = q0 + MOBA_WIDTH // width
    v0 = k0 + MOBA_WIDTH // width
    y = pl.pallas_call(
        functools.partial(_moba_body, nblk=nblk, group=MOBA_GROUP),
        grid=(b, MOBA_HEADS // hp, nblk),
        in_specs=[
            pl.BlockSpec(memory_space=pltpu.SMEM),
            pl.BlockSpec((None, MOBA_BLOCK, width), lambda bi, h, i: (bi, i, q0 + h)),
            pl.BlockSpec((None, s, width), lambda bi, h, i: (bi, 0, k0 + h)),
            pl.BlockSpec((None, s, width), lambda bi, h, i: (bi, 0, v0 + h)),
        ],
        out_specs=pl.BlockSpec((None, MOBA_BLOCK, width), lambda bi, h, i: (bi, i, h)),
        out_shape=jax.ShapeDtypeStruct((b, s, MOBA_WIDTH), BF16),
        scratch_shapes=[pltpu.VMEM((nblk, width), F32),
                        pltpu.VMEM((hp, HEAD_DIM, s), BF16),
                        pltpu.VMEM((hp, MOBA_BLOCK, MOBA_BLOCK), F32),
                        pltpu.VMEM((hp, nblk, MOBA_BLOCK), F32)],
        compiler_params=_params("parallel", "parallel", "arbitrary"),
        name="moba_attn",
    )(slopes, proj, proj, proj)
    return y.reshape(b * s, MOBA_WIDTH)


def _merge_body(x_ref, o0_ref, o1_ref, o2_ref, l0_ref, l1_ref, l2_ref, ym_ref, gd_ref, gm_ref,
                wud_ref, wum_ref, wo_ref, out_ref):
    heads = []
    for h in range(DIL_HEADS_PER_GROUP):
        l0, l1, l2 = l0_ref[h], l1_ref[h], l2_ref[h]
        m = jnp.maximum(jnp.maximum(l0, l1), l2)
        e0, e1, e2 = jnp.exp(l0 - m), jnp.exp(l1 - m), jnp.exp(l2 - m)
        den = e0 + e1 + e2
        mixed = (e0 / den) * o0_ref[h] + (e1 / den) * o1_ref[h] + (e2 / den) * o2_ref[h]
        heads.append(mixed.astype(BF16))
    y_dil = jnp.concatenate(heads, axis=-1)
    lift_dil = jnp.dot(y_dil, wud_ref[...], preferred_element_type=F32)
    lift_moba = jnp.dot(ym_ref[...], wum_ref[...], preferred_element_type=F32)
    merged = gd_ref[...].astype(F32) * lift_dil + gm_ref[...].astype(F32) * lift_moba
    out_ref[...] = x_ref[...] + jnp.dot(merged.astype(BF16), wo_ref[...], preferred_element_type=F32)


def _merge(x, dil_outs, dil_lses, y_moba, gates, w_up_dil, w_up_moba, w_out, *, tm):
    t, d = x.shape
    s = dil_outs[0].shape[2]
    assert t % tm == 0 and s % tm == 0
    per_batch = s // tm
    row = lambda w: pl.BlockSpec((tm, w), lambda i: (i, 0))
    dil = pl.BlockSpec((None, DIL_HEADS_PER_GROUP, tm, HEAD_DIM),
                       lambda i: (i // per_batch, 0, i % per_batch, 0))
    return pl.pallas_call(
        _merge_body,
        grid=(t // tm,),
        in_specs=[row(d)] + [dil] * 6 + [
            row(MOBA_WIDTH),
            pl.BlockSpec((tm, d), lambda i: (i, 0)),
            pl.BlockSpec((tm, d), lambda i: (i, 1)),
            _resident((DIL_OUT, d)), _resident((MOBA_WIDTH, d)), _resident((d, d)),
        ],
        out_specs=row(d),
        out_shape=jax.ShapeDtypeStruct((t, d), F32),
        compiler_params=_params("parallel"),
        name="merge_out_proj",
    )(x, *dil_outs, *dil_lses, y_moba, gates, gates, w_up_dil, w_up_moba, w_out)


def _cross_body(x_ref, g_ref, wq_ref, kv_ref, wo_ref, out_ref):
    x = x_ref[...]
    h = _rms(x, g_ref[...]).astype(BF16)
    q = jnp.dot(h, wq_ref[...], preferred_element_type=F32).astype(BF16)
    heads = []
    for hd in range(MEM_HEADS):
        k = kv_ref[:, hd * HEAD_DIM:(hd + 1) * HEAD_DIM]
        v = kv_ref[:, MEM_WIDTH + hd * HEAD_DIM:MEM_WIDTH + (hd + 1) * HEAD_DIM]
        s = lax.dot_general(q[:, hd * HEAD_DIM:(hd + 1) * HEAD_DIM], k, _NT,
                            preferred_element_type=F32) * SCALE
        p = jnp.exp(s - jnp.max(s, axis=-1, keepdims=True))
        den = jnp.sum(p, axis=-1, keepdims=True)
        heads.append((jnp.dot(p.astype(BF16), v, preferred_element_type=F32) / den).astype(BF16))
    o = jnp.concatenate(heads, axis=-1)
    out_ref[...] = x + jnp.dot(o, wo_ref[...], preferred_element_type=F32)


def _cross(x, g, w_q, kv, w_o, *, batch, tm):
    t, d = x.shape
    s = t // batch
    mem_len = kv.shape[1]
    assert s % tm == 0
    per_batch = s // tm
    return pl.pallas_call(
        _cross_body,
        grid=(t // tm,),
        in_specs=[
            pl.BlockSpec((tm, d), lambda i: (i, 0)),
            _resident((1, d)),
            _resident((d, MEM_WIDTH)),
            pl.BlockSpec((None, mem_len, 2 * MEM_WIDTH), lambda i: (i // per_batch, 0, 0)),
            _resident((MEM_WIDTH, d)),
        ],
        out_specs=pl.BlockSpec((tm, d), lambda i: (i, 0)),
        out_shape=jax.ShapeDtypeStruct((t, d), F32),
        compiler_params=_params("parallel"),
        name="memory_cross_attn",
    )(x, g.reshape(1, d), w_q, kv, w_o)


ROUTE_COLS = 8
MOE_ROW_TILE = 256


def _route_body(x_ref, g_ref, wr_ref, br_ref, info_ref, counts_ref, run_ref, tri_ref):
    tm = x_ref.shape[0]
    n_route = N_GROUPS + N_EXPERTS
    lane = lax.broadcasted_iota(jnp.int32, (tm, n_route), 1).astype(F32)

    @pl.when(pl.program_id(0) == 0)
    def _():
        run_ref[...] = jnp.zeros_like(run_ref)
        earlier = (lax.broadcasted_iota(jnp.int32, (tm, tm), 0)
                   > lax.broadcasted_iota(jnp.int32, (tm, tm), 1))
        tri_ref[...] = jnp.where(earlier, 1.0, 0.0).astype(BF16)

    t = _rms(x_ref[...], g_ref[...])
    logits = _dot_split3(t, wr_ref[...]) + br_ref[...]
    none = float(n_route)
    glog = jnp.where(lane < N_GROUPS, logits, NEG_INF)
    gmax = jnp.max(glog, axis=-1, keepdims=True)
    gsel = jnp.min(jnp.where(glog == gmax, lane, none), axis=-1, keepdims=True)
    pg = 1.0 / jnp.sum(jnp.exp(glog - gmax), axis=-1, keepdims=True)
    first = N_GROUPS + gsel * EXPERTS_PER_GROUP
    in_group = (lane >= first) & (lane < first + EXPERTS_PER_GROUP)
    elog = jnp.where(in_group, logits, NEG_INF)
    top1 = jnp.max(elog, axis=-1, keepdims=True)
    i1 = jnp.min(jnp.where(elog == top1, lane, none), axis=-1, keepdims=True)
    rest = jnp.where(lane == i1, NEG_INF, elog)
    top2 = jnp.max(rest, axis=-1, keepdims=True)
    i2 = jnp.min(jnp.where(rest == top2, lane, none), axis=-1, keepdims=True)
    e2 = jnp.exp(top2 - top1)
    w1 = pg / (1.0 + e2)
    w2 = pg * e2 / (1.0 + e2)

    hit1 = lane == i1
    hit2 = lane == i2
    assigned = jnp.where(hit1 | hit2, 1.0, 0.0)
    before = jnp.dot(tri_ref[...], assigned.astype(BF16), preferred_element_type=F32) + run_ref[...]
    rank1 = jnp.sum(jnp.where(hit1, before, 0.0), axis=-1, keepdims=True)
    rank2 = jnp.sum(jnp.where(hit2, before, 0.0), axis=-1, keepdims=True)
    run_ref[...] += jnp.sum(assigned, axis=0, keepdims=True)
    counts_ref[...] = run_ref[...]

    col = lax.broadcasted_iota(jnp.int32, (tm, ROUTE_COLS), 1)
    fields = (i1 - N_GROUPS, i2 - N_GROUPS, rank1, rank2, w1, w2)
    info = jnp.zeros((tm, ROUTE_COLS), F32)
    for c, field in enumerate(fields):
        info = jnp.where(col == c, field, info)
    info_ref[...] = info


def _route(x, g, w_route, b_route, *, tm):
    t, d = x.shape
    n_route = N_GROUPS + N_EXPERTS
    assert t % tm == 0
    return pl.pallas_call(
        _route_body,
        grid=(t // tm,),
        in_specs=[
            pl.BlockSpec((tm, d), lambda i: (i, 0)),
            pl.BlockSpec((1, d), lambda i: (0, 0)),
            pl.BlockSpec((d, n_route), lambda i: (0, 0)),
            pl.BlockSpec((1, n_route), lambda i: (0, 0)),
        ],
        out_specs=[pl.BlockSpec((tm, ROUTE_COLS), lambda i: (i, 0)),
                   pl.BlockSpec((1, n_route), lambda i: (0, 0))],
        out_shape=[jax.ShapeDtypeStruct((t, ROUTE_COLS), F32),
                   jax.ShapeDtypeStruct((1, n_route), F32)],
        scratch_shapes=[pltpu.VMEM((1, n_route), F32), pltpu.VMEM((tm, tm), BF16)],
        compiler_params=_params("arbitrary"),
        name="moe_route",
    )(x, g.reshape(1, d), w_route, b_route.reshape(1, n_route))


def _row_copies_wait(src_rows, dst_rows, sem):
    pltpu.make_async_copy(src_rows, dst_rows, sem).wait()


def _dispatch_body(pos_ref, fill_ref, x_ref, g_ref, sorted_ref, t_ref, zero_ref, sem, fill_sem):
    tm = x_ref.shape[0]
    tr = zero_ref.shape[0]
    base = pl.program_id(0) * tm

    @pl.when(pl.program_id(0) == 0)
    def _():
        zero_ref[...] = jnp.zeros_like(zero_ref)

        def fill_copy(tile):
            return pltpu.make_async_copy(zero_ref, sorted_ref.at[pl.ds(tile * tr, tr)], fill_sem)

        def start(idx, carry):
            pl.when(fill_ref[idx] >= 0)(lambda: fill_copy(fill_ref[idx]).start())
            return carry

        def finish(idx, carry):
            pl.when(fill_ref[idx] >= 0)(lambda: fill_copy(fill_ref[idx]).wait())
            return carry

        lax.fori_loop(0, fill_ref.shape[0], start, 0)
        lax.fori_loop(0, fill_ref.shape[0], finish, 0)

    t_ref[...] = _rms(x_ref[...], g_ref[...])

    def issue(r, carry):
        for k in range(MOE_TOPK):
            dst = pos_ref[k, base + r]
            pltpu.make_async_copy(t_ref.at[pl.ds(r, 1)], sorted_ref.at[pl.ds(dst, 1)], sem).start()
        return carry

    lax.fori_loop(0, tm, issue, 0, unroll=8)
    for _ in range(MOE_TOPK):
        _row_copies_wait(t_ref, sorted_ref.at[pl.ds(0, tm)], sem)


def _dispatch(x, g, pos, fill_tiles, n_rows, *, tm):
    t, d = x.shape
    assert t % tm == 0
    return pl.pallas_call(
        _dispatch_body,
        grid_spec=pltpu.PrefetchScalarGridSpec(
            num_scalar_prefetch=2,
            grid=(t // tm,),
            in_specs=[pl.BlockSpec((tm, d), lambda i, pos, fill: (i, 0)),
                      pl.BlockSpec((1, d), lambda i, pos, fill: (0, 0))],
            out_specs=pl.BlockSpec(memory_space=pl.ANY),
            scratch_shapes=[pltpu.VMEM((tm, d), F32), pltpu.VMEM((MOE_ROW_TILE, d), F32),
                            pltpu.SemaphoreType.DMA(()), pltpu.SemaphoreType.DMA(())],
        ),
        out_shape=jax.ShapeDtypeStruct((n_rows, d), F32),
        compiler_params=pltpu.CompilerParams(dimension_semantics=("arbitrary",),
                                             vmem_limit_bytes=VMEM_LIMIT_BYTES,
                                             disable_bounds_checks=True),
        name="moe_dispatch",
    )(pos, fill_tiles, x, g.reshape(1, d))


def _experts_body(tile_expert_ref, n_tiles_ref, x_ref, wg_ref, wu_ref, wd_ref, y_ref):
    del tile_expert_ref
    in_use = pl.program_id(0) < n_tiles_ref[0]

    @pl.when(in_use)
    def _():
        t = x_ref[...].astype(BF16)
        gate = jnp.dot(t, wg_ref[...], preferred_element_type=F32)
        up = jnp.dot(t, wu_ref[...], preferred_element_type=F32)
        a = jax.nn.silu(gate) * up
        y_ref[...] = jnp.dot(a.astype(BF16), wd_ref[...], preferred_element_type=F32)

    @pl.when(jnp.logical_not(in_use))
    def _():
        y_ref[...] = jnp.zeros_like(y_ref)


def _experts(sorted_rows, tile_expert, n_tiles, w_gate, w_up, w_down):
    p, d = sorted_rows.shape
    ff = w_gate.shape[-1]
    tr = MOE_ROW_TILE
    assert p % tr == 0
    used = lambda i, nt: jnp.minimum(i, nt[0] - 1)
    return pl.pallas_call(
        _experts_body,
        grid_spec=pltpu.PrefetchScalarGridSpec(
            num_scalar_prefetch=2,
            grid=(p // tr,),
            in_specs=[pl.BlockSpec((tr, d), lambda i, te, nt: (used(i, nt), 0)),
                      pl.BlockSpec((None, d, ff), lambda i, te, nt: (te[used(i, nt)], 0, 0)),
                      pl.BlockSpec((None, d, ff), lambda i, te, nt: (te[used(i, nt)], 0, 0)),
                      pl.BlockSpec((None, ff, d), lambda i, te, nt: (te[used(i, nt)], 0, 0))],
            out_specs=pl.BlockSpec((tr, d), lambda i, te, nt: (i, 0)),
        ),
        out_shape=jax.ShapeDtypeStruct((p, d), F32),
        compiler_params=_params("arbitrary"),
        name="moe_experts",
    )(tile_expert, n_tiles, sorted_rows, w_gate, w_up, w_down)


def _combine_body(pos_ref, x_ref, info_ref, gf_ref, y_sorted_ref, out_ref, rows_ref, sem):
    tm = x_ref.shape[0]
    base = pl.program_id(0) * tm

    def issue(r, carry):
        for k in range(MOE_TOPK):
            src = pos_ref[k, base + r]
            pltpu.make_async_copy(y_sorted_ref.at[pl.ds(src, 1)], rows_ref.at[k, pl.ds(r, 1)],
                                  sem).start()
        return carry

    lax.fori_loop(0, tm, issue, 0, unroll=8)
    for k in range(MOE_TOPK):
        _row_copies_wait(y_sorted_ref.at[pl.ds(0, tm)], rows_ref.at[k], sem)

    info = info_ref[...]
    y = info[:, 4:5] * rows_ref[0] + info[:, 5:6] * rows_ref[1]
    out_ref[...] = _rms(x_ref[...] + y, gf_ref[...])


def _combine(x, info, pos, y_sorted, g_final, *, tm):
    t, d = x.shape
    assert t % tm == 0
    return pl.pallas_call(
        _combine_body,
        grid_spec=pltpu.PrefetchScalarGridSpec(
            num_scalar_prefetch=1,
            grid=(t // tm,),
            in_specs=[pl.BlockSpec((tm, d), lambda i, pos: (i, 0)),
                      pl.BlockSpec((tm, ROUTE_COLS), lambda i, pos: (i, 0)),
                      pl.BlockSpec((1, d), lambda i, pos: (0, 0)),
                      pl.BlockSpec(memory_space=pl.ANY)],
            out_specs=pl.BlockSpec((tm, d), lambda i, pos: (i, 0)),
            scratch_shapes=[pltpu.VMEM((MOE_TOPK, tm, d), F32), pltpu.SemaphoreType.DMA(())],
        ),
        out_shape=jax.ShapeDtypeStruct((t, d), F32),
        compiler_params=pltpu.CompilerParams(dimension_semantics=("arbitrary",),
                                             vmem_limit_bytes=VMEM_LIMIT_BYTES,
                                             disable_bounds_checks=True),
        name="moe_combine_final_norm",
    )(pos, x, info, g_final.reshape(1, d), y_sorted)


def _moe(x, g, w_route, b_route, w_gate, w_up, w_down, g_final):
    t, d = x.shape
    tr = MOE_ROW_TILE
    assert (MOE_TOPK * t) % tr == 0
    info, counts = _route(x, g, w_route, b_route, tm=min(512, t))

    expert = info[:, 0:MOE_TOPK].astype(jnp.int32)
    rank = info[:, MOE_TOPK:2 * MOE_TOPK].astype(jnp.int32)
    count = counts[0, N_GROUPS:].astype(jnp.int32)
    seg_tiles = (count + (tr - 1)) // tr
    seg_end = jnp.cumsum(seg_tiles)
    seg_start_row = (seg_end - seg_tiles) * tr
    pos = (seg_start_row[expert] + rank).T
    max_tiles = (MOE_TOPK * t) // tr + N_EXPERTS
    tile_id = jnp.arange(max_tiles, dtype=jnp.int32)
    tile_expert = jnp.minimum(
        jnp.sum((seg_end[None, :] <= tile_id[:, None]).astype(jnp.int32), axis=1), N_EXPERTS - 1)
    n_tiles = seg_end[-1:].astype(jnp.int32)
    tail = n_tiles + jnp.arange(N_EXPERTS, dtype=jnp.int32)
    fill_tiles = jnp.concatenate([jnp.where(seg_tiles > 0, seg_end - 1, -1),
                                  jnp.where(tail < max_tiles, tail, -1)]).astype(jnp.int32)

    sorted_rows = _dispatch(x, g, pos, fill_tiles, max_tiles * tr, tm=min(256, t))
    y_sorted = _experts(sorted_rows, tile_expert, n_tiles, w_gate, w_up, w_down)
    return _combine(x, info, pos, y_sorted, g_final, tm=min(256, t))


def _layer(x, mem, attn_norm, w_in, w_up_dil, w_up_moba, w_branch_gate, w_out, cross_norm,
           mem_norm, w_q_mem, w_kv_mem, w_o_mem):
    b, s, d = x.shape
    t = b * s
    xt = x.reshape(t, d)
    h = _norm(xt, attn_norm, tm=min(512, t), name="attn_norm")
    proj, gates = _in_proj(h, w_in.astype(BF16), w_branch_gate.astype(BF16), tm=min(2048, t))
    proj = proj.reshape(b, s, IN_WIDTH)
    dil_slopes = _alibi_slopes(DIL_HEADS)
    dil = [_dilated_group(proj, dil_slopes, g, dilation)
           for g, (_, dilation) in enumerate(DIL_CONFIGS)]
    y_moba = _moba(proj, _alibi_slopes(MOBA_HEADS))
    x1 = _merge(xt, [o for o, _ in dil], [l for _, l in dil], y_moba, gates,
                w_up_dil.astype(BF16), w_up_moba.astype(BF16), w_out.astype(BF16), tm=min(256, t))
    mem_len = mem.shape[1]
    kv = _norm_matmul(mem.reshape(b * mem_len, d), mem_norm, w_kv_mem.astype(BF16),
                      tm=b * mem_len, tn=512, name="norm_mem_kv").reshape(b, mem_len, 2 * MEM_WIDTH)
    return _cross(x1, cross_norm, w_q_mem.astype(BF16), kv, w_o_mem.astype(BF16),
                  batch=b, tm=min(512, s))


def kernel(x, mem, attn_norm, w_in, w_up_dil, w_up_moba, w_branch_gate, w_out, cross_norm, mem_norm,
           w_q_mem, w_kv_mem, w_o_mem, ffn_norm, w_router_group, b_router_group, w_router_expert,
           b_router_expert, w_expert_gate, w_expert_up, w_expert_down, final_norm):
    b, s, d = x.shape
    depth = attn_norm.shape[0]
    assert depth == 1, "the final norm is fused into the last layer's MoE call"
    l = 0
    x2 = _layer(x, mem, attn_norm[l], w_in[l], w_up_dil[l], w_up_moba[l], w_branch_gate[l], w_out[l],
                cross_norm[l], mem_norm[l], w_q_mem[l], w_kv_mem[l], w_o_mem[l])
    w_route = jnp.concatenate([w_router_group[l], w_router_expert[l]], axis=1)
    b_route = jnp.concatenate([b_router_group[l], b_router_expert[l]], axis=0)
    out = _moe(x2, ffn_norm[l], w_route, b_route, w_expert_gate[l].astype(BF16),
               w_expert_up[l].astype(BF16), w_expert_down[l].astype(BF16), final_norm)
    return out.reshape(b, s, d)
```

```python
import functools

import numpy as np
import jax
import jax.numpy as jnp
from jax import lax
from jax.experimental import pallas as pl
from jax.experimental.pallas import tpu as pltpu

F32 = jnp.float32
BF16 = jnp.bfloat16

HEAD_DIM = 128
DIL_CONFIGS = ((128, 1), (512, 4), (2048, 16))
DIL_HEADS_PER_GROUP = 4
DIL_HEADS = DIL_HEADS_PER_GROUP * len(DIL_CONFIGS)
DIL_WIDTH = DIL_HEADS * HEAD_DIM
DIL_OUT = DIL_HEADS_PER_GROUP * HEAD_DIM
DIL_STEPS = 128
MOBA_HEADS = 8
MOBA_WIDTH = MOBA_HEADS * HEAD_DIM
MOBA_BLOCK = 256
MOBA_TOPK = 3
IN_WIDTH = 3 * (DIL_WIDTH + MOBA_WIDTH)
MEM_HEADS = 4
MEM_WIDTH = MEM_HEADS * HEAD_DIM
N_GROUPS = 4
EXPERTS_PER_GROUP = 8
N_EXPERTS = N_GROUPS * EXPERTS_PER_GROUP
MOE_TOPK = 2
RMS_EPS = 1e-6
SCALE = HEAD_DIM ** -0.5
NEG_INF = float("-inf")
LOG2E = 1.4426950408889634
MOBA_GROUP = 4
MOBA_HEADS_PER_STEP = 4
MOBA_Q_TILES = (9, 11)

VMEM_LIMIT_BYTES = 56 * 1024 * 1024
BF16_ROWS = 16
_NT = (((1,), (1,)), ((), ()))


def _alibi_slopes(n):
    return jnp.asarray(2.0 ** (-8.0 * np.arange(1, n + 1) / n), dtype=F32)


def _rms(x, g):
    return x * lax.rsqrt(jnp.mean(x * x, axis=-1, keepdims=True) + RMS_EPS) * g


def _dot_split3(a, b):
    a_hi = a.astype(BF16)
    b_hi = b.astype(BF16)
    a_lo = (a - a_hi.astype(F32)).astype(BF16)
    b_lo = (b - b_hi.astype(F32)).astype(BF16)
    dot = functools.partial(jnp.dot, preferred_element_type=F32)
    return dot(a_hi, b_hi) + (dot(a_lo, b_hi) + dot(a_hi, b_lo))


def _params(*sem, flags=None):
    return pltpu.CompilerParams(dimension_semantics=sem, vmem_limit_bytes=VMEM_LIMIT_BYTES,
                                flags=flags)


def _resident(shape):
    nd = len(shape)
    return pl.BlockSpec(shape, lambda *_: (0,) * nd, pipeline_mode=pl.Buffered(1))


def _norm_matmul_body(x_ref, g_ref, w_ref, o_ref, h_ref, *, sigmoid):
    @pl.when(pl.program_id(1) == 0)
    def _():
        h_ref[...] = _rms(x_ref[...], g_ref[...]).astype(BF16)

    acc = jnp.dot(h_ref[...], w_ref[...], preferred_element_type=F32)
    if sigmoid:
        acc = jax.nn.sigmoid(acc)
    o_ref[...] = acc.astype(o_ref.dtype)


def _norm_matmul(x, g, w, *, tm, tn, sigmoid=False, name):
    m, d = x.shape
    n = w.shape[1]
    assert m % tm == 0 and n % tn == 0
    return pl.pallas_call(
        functools.partial(_norm_matmul_body, sigmoid=sigmoid),
        grid=(m // tm, n // tn),
        in_specs=[
            pl.BlockSpec((tm, d), lambda i, j: (i, 0)),
            pl.BlockSpec((1, d), lambda i, j: (0, 0)),
            pl.BlockSpec((d, tn), lambda i, j: (0, j)),
        ],
        out_specs=pl.BlockSpec((tm, tn), lambda i, j: (i, j)),
        out_shape=jax.ShapeDtypeStruct((m, n), BF16),
        scratch_shapes=[pltpu.VMEM((tm, d), BF16)],
        compiler_params=_params("parallel", "arbitrary"),
        name=name,
    )(x, g.reshape(1, d), w)


def _norm_body(x_ref, g_ref, o_ref):
    o_ref[...] = _rms(x_ref[...], g_ref[...]).astype(o_ref.dtype)


def _norm(x, g, *, tm, name):
    m, d = x.shape
    assert m % tm == 0
    return pl.pallas_call(
        _norm_body,
        grid=(m // tm,),
        in_specs=[pl.BlockSpec((tm, d), lambda i: (i, 0)), pl.BlockSpec((1, d), lambda i: (0, 0))],
        out_specs=pl.BlockSpec((tm, d), lambda i: (i, 0)),
        out_shape=jax.ShapeDtypeStruct((m, d), BF16),
        compiler_params=_params("parallel"),
        name=name,
    )(x, g.reshape(1, d))


def _in_proj_body(h_ref, win_ref, wbg_ref, proj_ref, gates_ref, perm_ref, *, n_proj_tiles):
    j = pl.program_id(1)
    tm = h_ref.shape[0]
    n = DIL_STEPS
    groups = len(DIL_CONFIGS)

    def store_regrouped(acc, dilation):
        tile = n * dilation
        for c in range(acc.shape[1] // HEAD_DIM):
            cols = slice(c * HEAD_DIM, (c + 1) * HEAD_DIM)
            perm_ref[c] = acc[:, cols]
            for t0 in range(0, tm, tile):
                for r in range(dilation):
                    rows = perm_ref[c, pl.ds(t0 + r, n, stride=dilation), :]
                    proj_ref[t0 + r * n:t0 + (r + 1) * n, cols] = rows.astype(BF16)

    @pl.when(j < n_proj_tiles)
    def _():
        acc = jnp.dot(h_ref[...], win_ref[...].astype(BF16), preferred_element_type=F32)
        group = jnp.where(j < 3 * groups, lax.rem(j, groups), 0)
        moba_q = (j >= MOBA_Q_TILES[0]) & (j < MOBA_Q_TILES[1])
        scale = jnp.where(moba_q, SCALE * LOG2E, 1.0)

        def store_natural():
            proj_ref[...] = (acc * scale).astype(BF16)

        for g, (_, dilation) in enumerate(DIL_CONFIGS):
            store = store_natural if dilation == 1 else functools.partial(store_regrouped, acc, dilation)
            pl.when(group == g)(store)

    @pl.when(j >= n_proj_tiles)
    def _():
        acc = jnp.dot(h_ref[...], wbg_ref[...].astype(BF16), preferred_element_type=F32)
        gates_ref[...] = jax.nn.sigmoid(acc).astype(BF16)


def _in_proj(h, w_in, w_bg, *, tm):
    t, d = h.shape
    tn = DIL_OUT
    assert MOBA_Q_TILES == (3 * DIL_WIDTH // tn, (3 * DIL_WIDTH + MOBA_WIDTH) // tn)
    assert t % tm == 0 and tm % (DIL_STEPS * max(dl for _, dl in DIL_CONFIGS)) == 0
    assert DIL_WIDTH == len(DIL_CONFIGS) * tn and w_in.shape[1] % tn == 0 and w_bg.shape[1] % tn == 0
    n_proj = w_in.shape[1] // tn
    n_gate = w_bg.shape[1] // tn
    return pl.pallas_call(
        functools.partial(_in_proj_body, n_proj_tiles=n_proj),
        grid=(t // tm, n_proj + n_gate),
        in_specs=[
            pl.BlockSpec((tm, d), lambda i, j: (i, 0)),
            pl.BlockSpec((d, tn), lambda i, j: (0, jnp.minimum(j, n_proj - 1))),
            pl.BlockSpec((d, tn), lambda i, j: (0, jnp.maximum(j - n_proj, 0))),
        ],
        out_specs=[
            pl.BlockSpec((tm, tn), lambda i, j: (i, jnp.minimum(j, n_proj - 1))),
            pl.BlockSpec((tm, tn), lambda i, j: (i, jnp.maximum(j - n_proj, 0))),
        ],
        out_shape=[jax.ShapeDtypeStruct((t, w_in.shape[1]), BF16),
                   jax.ShapeDtypeStruct((t, w_bg.shape[1]), BF16)],
        scratch_shapes=[pltpu.VMEM((tn // HEAD_DIM, tm, HEAD_DIM), F32)],
        compiler_params=_params("parallel", "arbitrary"),
        name="in_proj_gates",
    )(h, w_in, w_bg)


def _dilated_body(slope_ref, q_ref, kp_ref, kc_ref, vp_ref, vc_ref, o_ref, lse_ref, *scratch,
                  dilation, group):
    n = DIL_STEPS
    sub = pl.program_id(2)
    if dilation > 1:
        o_tile, lse_tile = scratch
        sub_rows = pl.ds(pl.multiple_of(sub * n, n), n)
    qi = lax.broadcasted_iota(jnp.int32, (n, n), 0)
    kj = lax.broadcasted_iota(jnp.int32, (n, n), 1)
    steps_cur = qi - kj
    valid_cur = steps_cur >= 0
    prev_limit = jnp.where(pl.program_id(1) > 0, 0, -n)
    valid_prev = steps_cur <= prev_limit
    dist_cur = (steps_cur * dilation).astype(F32)
    dist_prev = ((steps_cur + n) * dilation).astype(F32)
    heads = range(DIL_HEADS_PER_GROUP)
    cols_of = [slice(h * HEAD_DIM, (h + 1) * HEAD_DIM) for h in heads]
    raw = []
    for h in heads:
        q = q_ref[:, cols_of[h]]
        raw.append((lax.dot_general(q, kc_ref[:, cols_of[h]], _NT, preferred_element_type=F32),
                    lax.dot_general(q, kp_ref[:, cols_of[h]], _NT, preferred_element_type=F32)))
    probs = []
    for h in heads:
        slope = slope_ref[group * DIL_HEADS_PER_GROUP + h]
        s_cur = jnp.where(valid_cur, raw[h][0] * SCALE - slope * dist_cur, NEG_INF)
        s_prev = jnp.where(valid_prev, raw[h][1] * SCALE - slope * dist_prev, NEG_INF)
        m = jnp.maximum(jnp.max(s_cur, axis=-1, keepdims=True),
                        jnp.max(s_prev, axis=-1, keepdims=True))
        p_cur = jnp.exp(s_cur - m)
        p_prev = jnp.exp(s_prev - m)
        den = jnp.sum(p_cur, axis=-1, keepdims=True) + jnp.sum(p_prev, axis=-1, keepdims=True)
        probs.append((p_cur.astype(BF16), p_prev.astype(BF16), m, den))
    for h in heads:
        cols = cols_of[h]
        p_cur, p_prev, m, den = probs[h]
        o = (jnp.dot(p_cur, vc_ref[:, cols], preferred_element_type=F32)
             + jnp.dot(p_prev, vp_ref[:, cols], preferred_element_type=F32))
        lse = jnp.broadcast_to(m + jnp.log(den), (n, HEAD_DIM))
        if dilation == 1:
            o_ref[h] = o / den
            lse_ref[h] = lse
        else:
            o_tile[h, sub_rows, :] = o / den
            lse_tile[h, sub_rows, :] = lse

    if dilation > 1:
        @pl.when(sub == dilation - 1)
        def _():
            for h in range(DIL_HEADS_PER_GROUP):
                for r in range(dilation):
                    natural = pl.ds(r, n, stride=dilation)
                    o_ref[h, natural, :] = o_tile[h, r * n:(r + 1) * n, :]
                    lse_ref[h, natural, :] = lse_tile[h, r * n:(r + 1) * n, :]


def _dilated_group(proj, slopes, group, dilation):
    b, s, _ = proj.shape
    n = DIL_STEPS
    tile = n * dilation
    assert s % tile == 0
    groups = len(DIL_CONFIGS)

    def spec(section, prev):
        def index(bi, i, r):
            return (bi, (jnp.maximum(i - 1, 0) if prev else i) * dilation + r,
                    section * groups + group)
        return pl.BlockSpec((None, n, DIL_OUT), index)

    out_block = (DIL_HEADS_PER_GROUP, tile, HEAD_DIM)
    out_spec = pl.BlockSpec((None,) + out_block, lambda bi, i, r: (bi, 0, i, 0))
    out_sds = jax.ShapeDtypeStruct((b, DIL_HEADS_PER_GROUP, s, HEAD_DIM), F32)
    scratch = [pltpu.VMEM(out_block, F32)] * 2 if dilation > 1 else []
    o, lse = pl.pallas_call(
        functools.partial(_dilated_body, dilation=dilation, group=group),
        grid=(b, s // tile, dilation),
        in_specs=[pl.BlockSpec(memory_space=pltpu.SMEM),
                  spec(0, False), spec(1, True), spec(1, False), spec(2, True), spec(2, False)],
        out_specs=[out_spec, out_spec],
        out_shape=[out_sds, out_sds],
        scratch_shapes=scratch,
        compiler_params=_params("parallel", "arbitrary", "arbitrary"),
        name=f"dilated_attn_g{group}",
    )(slopes, proj, proj, proj, proj, proj)
    return o, lse


def _moba_body(slope_ref, q_ref, k_ref, v_ref, o_ref, kmean_ref, vt_ref, bias_ref, sel_ref,
               *, nblk, group):
    blk = MOBA_BLOCK
    hd = HEAD_DIM
    heads = q_ref.shape[1] // hd
    own = pl.program_id(2)
    key_off = lax.broadcasted_iota(jnp.int32, (blk, blk), 0)
    qry_off = lax.broadcasted_iota(jnp.int32, (blk, blk), 1)
    slope2 = [slope_ref[pl.program_id(1) * heads + h] * LOG2E for h in range(heads)]

    @pl.when(own == 0)
    def _():
        def fill(jb, carry):
            start = pl.multiple_of(jb * blk, blk)
            rows = k_ref[pl.ds(start, blk), :].astype(F32)
            kmean_ref[pl.ds(jb, 1), :] = jnp.mean(rows, axis=0, keepdims=True)
            vrows = v_ref[pl.ds(start, blk), :].astype(F32)
            for h in range(heads):
                vt_ref[h, :hd, pl.ds(start, blk)] = vrows[:, h * hd:(h + 1) * hd].T.astype(BF16)
            return carry
        lax.fori_loop(0, nblk, fill, 0)
        for h in range(heads):
            vt_ref[h, hd:, :] = jnp.ones((vt_ref.shape[1] - hd, vt_ref.shape[2]), BF16)
        for h in range(heads):
            bias_ref[h] = -slope2[h] * (qry_off - key_off).astype(F32)

    q = [q_ref[:, h * hd:(h + 1) * hd] for h in range(heads)]

    blk_id = lax.broadcasted_iota(jnp.int32, (nblk, blk), 0).astype(F32)
    gates = []
    for h in range(heads):
        gate = lax.dot_general(kmean_ref[:, h * hd:(h + 1) * hd], q[h].astype(F32), _NT,
                               precision=lax.Precision.HIGHEST, preferred_element_type=F32)
        gates.append(jnp.where(blk_id < own.astype(F32), gate, NEG_INF))

    def scores(h, start, rows):
        k = k_ref[pl.ds(start, rows), h * hd:(h + 1) * hd]
        return lax.dot_general(k, q[h], _NT, preferred_element_type=F32)

    own_start = pl.multiple_of(own * blk, blk)
    own_x = [scores(h, own_start, blk) for h in range(heads)]

    sels = [jnp.zeros((nblk, blk), F32) for _ in range(heads)]
    for _ in range(MOBA_TOPK):
        for h in range(heads):
            best = jnp.max(gates[h], axis=0, keepdims=True)
            is_best = (gates[h] == best) & (gates[h] > NEG_INF)
            pick = jnp.min(jnp.where(is_best, blk_id, float(nblk)), axis=0, keepdims=True)
            picked = blk_id == pick
            sels[h] = jnp.where(picked, 1.0, sels[h])
            gates[h] = jnp.where(picked, NEG_INF, gates[h])
    for h in range(heads):
        sel_ref[h] = sels[h]

    own_p = []
    for h in range(heads):
        x = jnp.where(qry_off >= key_off, own_x[h] + bias_ref[h], NEG_INF)
        m0 = jnp.max(x, axis=0, keepdims=True)
        own_p.append((m0, jnp.exp2(x - m0).astype(BF16)))
    init = []
    for h in range(heads):
        m0, p = own_p[h]
        acc0 = jnp.dot(vt_ref[h, :, pl.ds(own_start, blk)], p, preferred_element_type=F32)
        init += [m0, acc0]

    def softmax_update(h, i, x, m):
        xs, chosen, shift = [], [], []
        m_new = m
        for g in range(group):
            j = i * group + g
            xs.append(x[g * blk:(g + 1) * blk] + bias_ref[h])
            chosen.append(sel_ref[h, pl.ds(j, 1), :] > 0.5)
            shift.append(-slope2[h] * ((own - j) * blk).astype(F32))
            top = jnp.max(xs[g], axis=0, keepdims=True) + shift[g]
            m_new = jnp.maximum(m_new, jnp.where(chosen[g], top, NEG_INF))
        ps = []
        for g in range(group):
            ref_g = jnp.where(chosen[g], m_new - shift[g], jnp.inf)
            ps.append(jnp.exp2(xs[g] - ref_g).astype(BF16))
        return m_new, jnp.exp2(m - m_new), jnp.concatenate(ps, axis=0)

    def past_blocks(i, carry):
        start = pl.multiple_of(i * (group * blk), group * blk)
        x, sm, out = {}, {}, [None] * (2 * heads)
        x[0] = scores(0, start, group * blk)
        for h in range(heads + 1):
            if h + 1 < heads:
                x[h + 1] = scores(h + 1, start, group * blk)
            if h < heads:
                sm[h] = softmax_update(h, i, x[h], carry[2 * h])
            if h >= 1:
                m_new, alpha, p = sm[h - 1]
                acc = alpha * carry[2 * h - 1] + jnp.dot(
                    vt_ref[h - 1, :, pl.ds(start, group * blk)], p, preferred_element_type=F32)
                out[2 * h - 2:2 * h] = [m_new, acc]
        return tuple(out)

    n_groups = lax.div(own + (group - 1), group)
    final = lax.fori_loop(0, n_groups, past_blocks, tuple(init))
    for h in range(heads):
        acc = final[2 * h + 1]
        o_ref[:, h * hd:(h + 1) * hd] = (acc[:hd] / acc[hd:hd + 1]).T.astype(o_ref.dtype)


def _moba(proj, slopes):
    b, s, _ = proj.shape
    assert s % (MOBA_BLOCK * MOBA_GROUP) == 0
    nblk = s // MOBA_BLOCK
    hp = MOBA_HEADS_PER_STEP
    width = hp * HEAD_DIM
    q0 = 3 * DIL_WIDTH // width
    k0 = q0 + MOBA_WIDTH // width
    v0 = k0 + MOBA_WIDTH // width
    y = pl.pallas_call(
        functools.partial(_moba_body, nblk=nblk, group=MOBA_GROUP),
        grid=(b, MOBA_HEADS // hp, nblk),
        in_specs=[
            pl.BlockSpec(memory_space=pltpu.SMEM),
            pl.BlockSpec((None, MOBA_BLOCK, width), lambda bi, h, i: (bi, i, q0 + h)),
            pl.BlockSpec((None, s, width), lambda bi, h, i: (bi, 0, k0 + h)),
            pl.BlockSpec((None, s, width), lambda bi, h, i: (bi, 0, v0 + h)),
        ],
        out_specs=pl.BlockSpec((None, MOBA_BLOCK, width), lambda bi, h, i: (bi, i, h)),
        out_shape=jax.ShapeDtypeStruct((b, s, MOBA_WIDTH), BF16),
        scratch_shapes=[pltpu.VMEM((nblk, width), F32),
                        pltpu.VMEM((hp, HEAD_DIM + BF16_ROWS, s), BF16),
                        pltpu.VMEM((hp, MOBA_BLOCK, MOBA_BLOCK), F32),
                        pltpu.VMEM((hp, nblk, MOBA_BLOCK), F32)],
        compiler_params=_params("parallel", "parallel", "arbitrary"),
        name="moba_attn",
    )(slopes, proj, proj, proj)
    return y.reshape(b * s, MOBA_WIDTH)


def _merge_body(x_ref, o0_ref, o1_ref, o2_ref, l0_ref, l1_ref, l2_ref, ym_ref, gd_ref, gm_ref,
                wud_ref, wum_ref, wo_ref, out_ref):
    heads = []
    for h in range(DIL_HEADS_PER_GROUP):
        l0, l1, l2 = l0_ref[h], l1_ref[h], l2_ref[h]
        m = jnp.maximum(jnp.maximum(l0, l1), l2)
        e0, e1, e2 = jnp.exp(l0 - m), jnp.exp(l1 - m), jnp.exp(l2 - m)
        den = e0 + e1 + e2
        mixed = (e0 / den) * o0_ref[h] + (e1 / den) * o1_ref[h] + (e2 / den) * o2_ref[h]
        heads.append(mixed.astype(BF16))
    y_dil = jnp.concatenate(heads, axis=-1)
    lift_dil = jnp.dot(y_dil, wud_ref[...], preferred_element_type=F32)
    lift_moba = jnp.dot(ym_ref[...], wum_ref[...], preferred_element_type=F32)
    merged = gd_ref[...].astype(F32) * lift_dil + gm_ref[...].astype(F32) * lift_moba
    out_ref[...] = x_ref[...] + jnp.dot(merged.astype(BF16), wo_ref[...], preferred_element_type=F32)


def _merge(x, dil_outs, dil_lses, y_moba, gates, w_up_dil, w_up_moba, w_out, *, tm):
    t, d = x.shape
    s = dil_outs[0].shape[2]
    assert t % tm == 0 and s % tm == 0
    per_batch = s // tm
    row = lambda w: pl.BlockSpec((tm, w), lambda i: (i, 0))
    dil = pl.BlockSpec((None, DIL_HEADS_PER_GROUP, tm, HEAD_DIM),
                       lambda i: (i // per_batch, 0, i % per_batch, 0))
    return pl.pallas_call(
        _merge_body,
        grid=(t // tm,),
        in_specs=[row(d)] + [dil] * 6 + [
            row(MOBA_WIDTH),
            pl.BlockSpec((tm, d), lambda i: (i, 0)),
            pl.BlockSpec((tm, d), lambda i: (i, 1)),
            _resident((DIL_OUT, d)), _resident((MOBA_WIDTH, d)), _resident((d, d)),
        ],
        out_specs=row(d),
        out_shape=jax.ShapeDtypeStruct((t, d), F32),
        compiler_params=_params("parallel"),
        name="merge_out_proj",
    )(x, *dil_outs, *dil_lses, y_moba, gates, gates, w_up_dil, w_up_moba, w_out)


def _cross_body(x_ref, g_ref, wq_ref, kv_ref, wo_ref, out_ref):
    x = x_ref[...]
    h = _rms(x, g_ref[...]).astype(BF16)
    q = jnp.dot(h, wq_ref[...], preferred_element_type=F32).astype(BF16)
    heads = []
    for hd in range(MEM_HEADS):
        k = kv_ref[:, hd * HEAD_DIM:(hd + 1) * HEAD_DIM]
        v = kv_ref[:, MEM_WIDTH + hd * HEAD_DIM:MEM_WIDTH + (hd + 1) * HEAD_DIM]
        s = lax.dot_general(q[:, hd * HEAD_DIM:(hd + 1) * HEAD_DIM], k, _NT,
                            preferred_element_type=F32) * SCALE
        p = jnp.exp(s - jnp.max(s, axis=-1, keepdims=True))
        den = jnp.sum(p, axis=-1, keepdims=True)
        heads.append((jnp.dot(p.astype(BF16), v, preferred_element_type=F32) / den).astype(BF16))
    o = jnp.concatenate(heads, axis=-1)
    out_ref[...] = x + jnp.dot(o, wo_ref[...], preferred_element_type=F32)


def _cross(x, g, w_q, kv, w_o, *, batch, tm):
    t, d = x.shape
    s = t // batch
    mem_len = kv.shape[1]
    assert s % tm == 0
    per_batch = s // tm
    return pl.pallas_call(
        _cross_body,
        grid=(t // tm,),
        in_specs=[
            pl.BlockSpec((tm, d), lambda i: (i, 0)),
            _resident((1, d)),
            _resident((d, MEM_WIDTH)),
            pl.BlockSpec((None, mem_len, 2 * MEM_WIDTH), lambda i: (i // per_batch, 0, 0)),
            _resident((MEM_WIDTH, d)),
        ],
        out_specs=pl.BlockSpec((tm, d), lambda i: (i, 0)),
        out_shape=jax.ShapeDtypeStruct((t, d), F32),
        compiler_params=_params("parallel"),
        name="memory_cross_attn",
    )(x, g.reshape(1, d), w_q, kv, w_o)


ROUTE_COLS = 8
MOE_ROW_TILE = 256


def _route_body(x_ref, g_ref, wr_ref, br_ref, info_ref, counts_ref, run_ref, tri_ref):
    tm = x_ref.shape[0]
    n_route = N_GROUPS + N_EXPERTS
    lane = lax.broadcasted_iota(jnp.int32, (tm, n_route), 1).astype(F32)

    @pl.when(pl.program_id(0) == 0)
    def _():
        run_ref[...] = jnp.zeros_like(run_ref)
        earlier = (lax.broadcasted_iota(jnp.int32, (tm, tm), 0)
                   > lax.broadcasted_iota(jnp.int32, (tm, tm), 1))
        tri_ref[...] = jnp.where(earlier, 1.0, 0.0).astype(BF16)

    t = _rms(x_ref[...], g_ref[...])
    logits = _dot_split3(t, wr_ref[...]) + br_ref[...]
    none = float(n_route)
    glog = jnp.where(lane < N_GROUPS, logits, NEG_INF)
    gmax = jnp.max(glog, axis=-1, keepdims=True)
    gsel = jnp.min(jnp.where(glog == gmax, lane, none), axis=-1, keepdims=True)
    pg = 1.0 / jnp.sum(jnp.exp(glog - gmax), axis=-1, keepdims=True)
    first = N_GROUPS + gsel * EXPERTS_PER_GROUP
    in_group = (lane >= first) & (lane < first + EXPERTS_PER_GROUP)
    elog = jnp.where(in_group, logits, NEG_INF)
    top1 = jnp.max(elog, axis=-1, keepdims=True)
    i1 = jnp.min(jnp.where(elog == top1, lane, none), axis=-1, keepdims=True)
    rest = jnp.where(lane == i1, NEG_INF, elog)
    top2 = jnp.max(rest, axis=-1, keepdims=True)
    i2 = jnp.min(jnp.where(rest == top2, lane, none), axis=-1, keepdims=True)
    e2 = jnp.exp(top2 - top1)
    w1 = pg / (1.0 + e2)
    w2 = pg * e2 / (1.0 + e2)

    hit1 = lane == i1
    hit2 = lane == i2
    assigned = jnp.where(hit1 | hit2, 1.0, 0.0)
    before = jnp.dot(tri_ref[...], assigned.astype(BF16), preferred_element_type=F32) + run_ref[...]
    rank1 = jnp.sum(jnp.where(hit1, before, 0.0), axis=-1, keepdims=True)
    rank2 = jnp.sum(jnp.where(hit2, before, 0.0), axis=-1, keepdims=True)
    run_ref[...] += jnp.sum(assigned, axis=0, keepdims=True)
    counts_ref[...] = run_ref[...]

    col = lax.broadcasted_iota(jnp.int32, (tm, ROUTE_COLS), 1)
    fields = (i1 - N_GROUPS, i2 - N_GROUPS, rank1, rank2, w1, w2)
    info = jnp.zeros((tm, ROUTE_COLS), F32)
    for c, field in enumerate(fields):
        info = jnp.where(col == c, field, info)
    info_ref[...] = info


def _route(x, g, w_route, b_route, *, tm):
    t, d = x.shape
    n_route = N_GROUPS + N_EXPERTS
    assert t % tm == 0
    return pl.pallas_call(
        _route_body,
        grid=(t // tm,),
        in_specs=[
            pl.BlockSpec((tm, d), lambda i: (i, 0)),
            pl.BlockSpec((1, d), lambda i: (0, 0)),
            pl.BlockSpec((d, n_route), lambda i: (0, 0)),
            pl.BlockSpec((1, n_route), lambda i: (0, 0)),
        ],
        out_specs=[pl.BlockSpec((tm, ROUTE_COLS), lambda i: (i, 0)),
                   pl.BlockSpec((1, n_route), lambda i: (0, 0))],
        out_shape=[jax.ShapeDtypeStruct((t, ROUTE_COLS), F32),
                   jax.ShapeDtypeStruct((1, n_route), F32)],
        scratch_shapes=[pltpu.VMEM((1, n_route), F32), pltpu.VMEM((tm, tm), BF16)],
        compiler_params=_params("arbitrary"),
        name="moe_route",
    )(x, g.reshape(1, d), w_route, b_route.reshape(1, n_route))


def _row_copies_wait(src_rows, dst_rows, sem):
    pltpu.make_async_copy(src_rows, dst_rows, sem).wait()


def _dispatch_body(pos_ref, fill_ref, x_ref, g_ref, sorted_ref, t_ref, zero_ref, sem, fill_sem):
    tm = x_ref.shape[0]
    tr = zero_ref.shape[0]
    base = pl.program_id(0) * tm

    @pl.when(pl.program_id(0) == 0)
    def _():
        zero_ref[...] = jnp.zeros_like(zero_ref)

        def fill_copy(tile):
            return pltpu.make_async_copy(zero_ref, sorted_ref.at[pl.ds(tile * tr, tr)], fill_sem)

        def start(idx, carry):
            pl.when(fill_ref[idx] >= 0)(lambda: fill_copy(fill_ref[idx]).start())
            return carry

        def finish(idx, carry):
            pl.when(fill_ref[idx] >= 0)(lambda: fill_copy(fill_ref[idx]).wait())
            return carry

        lax.fori_loop(0, fill_ref.shape[0], start, 0)
        lax.fori_loop(0, fill_ref.shape[0], finish, 0)

    t_ref[...] = _rms(x_ref[...], g_ref[...])

    def issue(r, carry):
        for k in range(MOE_TOPK):
            dst = pos_ref[k, base + r]
            pltpu.make_async_copy(t_ref.at[pl.ds(r, 1)], sorted_ref.at[pl.ds(dst, 1)], sem).start()
        return carry

    lax.fori_loop(0, tm, issue, 0, unroll=8)
    for _ in range(MOE_TOPK):
        _row_copies_wait(t_ref, sorted_ref.at[pl.ds(0, tm)], sem)


def _dispatch(x, g, pos, fill_tiles, n_rows, *, tm):
    t, d = x.shape
    assert t % tm == 0
    return pl.pallas_call(
        _dispatch_body,
        grid_spec=pltpu.PrefetchScalarGridSpec(
            num_scalar_prefetch=2,
            grid=(t // tm,),
            in_specs=[pl.BlockSpec((tm, d), lambda i, pos, fill: (i, 0)),
                      pl.BlockSpec((1, d), lambda i, pos, fill: (0, 0))],
            out_specs=pl.BlockSpec(memory_space=pl.ANY),
            scratch_shapes=[pltpu.VMEM((tm, d), F32), pltpu.VMEM((MOE_ROW_TILE, d), F32),
                            pltpu.SemaphoreType.DMA(()), pltpu.SemaphoreType.DMA(())],
        ),
        out_shape=jax.ShapeDtypeStruct((n_rows, d), F32),
        compiler_params=pltpu.CompilerParams(dimension_semantics=("arbitrary",),
                                             vmem_limit_bytes=VMEM_LIMIT_BYTES,
                                             disable_bounds_checks=True),
        name="moe_dispatch",
    )(pos, fill_tiles, x, g.reshape(1, d))


def _experts_body(tile_expert_ref, n_tiles_ref, x_ref, wg_ref, wu_ref, wd_ref, y_ref):
    del tile_expert_ref
    in_use = pl.program_id(0) < n_tiles_ref[0]

    @pl.when(in_use)
    def _():
        t = x_ref[...].astype(BF16)
        gate = jnp.dot(t, wg_ref[...], preferred_element_type=F32)
        up = jnp.dot(t, wu_ref[...], preferred_element_type=F32)
        a = jax.nn.silu(gate) * up
        y_ref[...] = jnp.dot(a.astype(BF16), wd_ref[...], preferred_element_type=F32)

    @pl.when(jnp.logical_not(in_use))
    def _():
        y_ref[...] = jnp.zeros_like(y_ref)


def _experts(sorted_rows, tile_expert, n_tiles, w_gate, w_up, w_down):
    p, d = sorted_rows.shape
    ff = w_gate.shape[-1]
    tr = MOE_ROW_TILE
    assert p % tr == 0
    used = lambda i, nt: jnp.minimum(i, nt[0] - 1)
    return pl.pallas_call(
        _experts_body,
        grid_spec=pltpu.PrefetchScalarGridSpec(
            num_scalar_prefetch=2,
            grid=(p // tr,),
            in_specs=[pl.BlockSpec((tr, d), lambda i, te, nt: (used(i, nt), 0)),
                      pl.BlockSpec((None, d, ff), lambda i, te, nt: (te[used(i, nt)], 0, 0)),
                      pl.BlockSpec((None, d, ff), lambda i, te, nt: (te[used(i, nt)], 0, 0)),
                      pl.BlockSpec((None, ff, d), lambda i, te, nt: (te[used(i, nt)], 0, 0))],
            out_specs=pl.BlockSpec((tr, d), lambda i, te, nt: (i, 0)),
        ),
        out_shape=jax.ShapeDtypeStruct((p, d), F32),
        compiler_params=_params("arbitrary"),
        name="moe_experts",
    )(tile_expert, n_tiles, sorted_rows, w_gate, w_up, w_down)


def _combine_body(pos_ref, x_ref, info_ref, gf_ref, y_sorted_ref, out_ref, rows_ref, sem):
    tm = x_ref.shape[0]
    base = pl.program_id(0) * tm

    def issue(r, carry):
        for k in range(MOE_TOPK):
            src = pos_ref[k, base + r]
            pltpu.make_async_copy(y_sorted_ref.at[pl.ds(src, 1)], rows_ref.at[k, pl.ds(r, 1)],
                                  sem).start()
        return carry

    lax.fori_loop(0, tm, issue, 0, unroll=8)
    for k in range(MOE_TOPK):
        _row_copies_wait(y_sorted_ref.at[pl.ds(0, tm)], rows_ref.at[k], sem)

    info = info_ref[...]
    y = info[:, 4:5] * rows_ref[0] + info[:, 5:6] * rows_ref[1]
    out_ref[...] = _rms(x_ref[...] + y, gf_ref[...])


def _combine(x, info, pos, y_sorted, g_final, *, tm):
    t, d = x.shape
    assert t % tm == 0
    return pl.pallas_call(
        _combine_body,
        grid_spec=pltpu.PrefetchScalarGridSpec(
            num_scalar_prefetch=1,
            grid=(t // tm,),
            in_specs=[pl.BlockSpec((tm, d), lambda i, pos: (i, 0)),
                      pl.BlockSpec((tm, ROUTE_COLS), lambda i, pos: (i, 0)),
                      pl.BlockSpec((1, d), lambda i, pos: (0, 0)),
                      pl.BlockSpec(memory_space=pl.ANY)],
            out_specs=pl.BlockSpec((tm, d), lambda i, pos: (i, 0)),
            scratch_shapes=[pltpu.VMEM((MOE_TOPK, tm, d), F32), pltpu.SemaphoreType.DMA(())],
        ),
        out_shape=jax.ShapeDtypeStruct((t, d), F32),
        compiler_params=pltpu.CompilerParams(dimension_semantics=("arbitrary",),
                                             vmem_limit_bytes=VMEM_LIMIT_BYTES,
                                             disable_bounds_checks=True),
        name="moe_combine_final_norm",
    )(pos, x, info, g_final.reshape(1, d), y_sorted)


def _moe(x, g, w_route, b_route, w_gate, w_up, w_down, g_final):
    t, d = x.shape
    tr = MOE_ROW_TILE
    assert (MOE_TOPK * t) % tr == 0
    info, counts = _route(x, g, w_route, b_route, tm=min(512, t))

    expert = info[:, 0:MOE_TOPK].astype(jnp.int32)
    rank = info[:, MOE_TOPK:2 * MOE_TOPK].astype(jnp.int32)
    count = counts[0, N_GROUPS:].astype(jnp.int32)
    seg_tiles = (count + (tr - 1)) // tr
    seg_end = jnp.cumsum(seg_tiles)
    seg_start_row = (seg_end - seg_tiles) * tr
    pos = (seg_start_row[expert] + rank).T
    max_tiles = (MOE_TOPK * t) // tr + N_EXPERTS
    tile_id = jnp.arange(max_tiles, dtype=jnp.int32)
    tile_expert = jnp.minimum(
        jnp.sum((seg_end[None, :] <= tile_id[:, None]).astype(jnp.int32), axis=1), N_EXPERTS - 1)
    n_tiles = seg_end[-1:].astype(jnp.int32)
    tail = n_tiles + jnp.arange(N_EXPERTS, dtype=jnp.int32)
    fill_tiles = jnp.concatenate([jnp.where(seg_tiles > 0, seg_end - 1, -1),
                                  jnp.where(tail < max_tiles, tail, -1)]).astype(jnp.int32)

    sorted_rows = _dispatch(x, g, pos, fill_tiles, max_tiles * tr, tm=min(512, t))
    y_sorted = _experts(sorted_rows, tile_expert, n_tiles, w_gate, w_up, w_down)
    return _combine(x, info, pos, y_sorted, g_final, tm=min(512, t))


def _layer(x, mem, attn_norm, w_in, w_up_dil, w_up_moba, w_branch_gate, w_out, cross_norm,
           mem_norm, w_q_mem, w_kv_mem, w_o_mem):
    b, s, d = x.shape
    t = b * s
    xt = x.reshape(t, d)
    h = _norm(xt, attn_norm, tm=min(512, t), name="attn_norm")
    proj, gates = _in_proj(h, w_in, w_branch_gate, tm=min(2048, t))
    proj = proj.reshape(b, s, IN_WIDTH)
    dil_slopes = _alibi_slopes(DIL_HEADS)
    dil = [_dilated_group(proj, dil_slopes, g, dilation)
           for g, (_, dilation) in enumerate(DIL_CONFIGS)]
    y_moba = _moba(proj, _alibi_slopes(MOBA_HEADS))
    x1 = _merge(xt, [o for o, _ in dil], [l for _, l in dil], y_moba, gates,
                w_up_dil.astype(BF16), w_up_moba.astype(BF16), w_out.astype(BF16), tm=min(256, t))
    mem_len = mem.shape[1]
    kv = _norm_matmul(mem.reshape(b * mem_len, d), mem_norm, w_kv_mem.astype(BF16),
                      tm=b * mem_len, tn=512, name="norm_mem_kv").reshape(b, mem_len, 2 * MEM_WIDTH)
    return _cross(x1, cross_norm, w_q_mem.astype(BF16), kv, w_o_mem.astype(BF16),
                  batch=b, tm=min(512, s))


def kernel(x, mem, attn_norm, w_in, w_up_dil, w_up_moba, w_branch_gate, w_out, cross_norm, mem_norm,
           w_q_mem, w_kv_mem, w_o_mem, ffn_norm, w_router_group, b_router_group, w_router_expert,
           b_router_expert, w_expert_gate, w_expert_up, w_expert_down, final_norm):
    b, s, d = x.shape
    depth = attn_norm.shape[0]
    assert depth == 1, "the final norm is fused into the last layer's MoE call"
    l = 0
    x2 = _layer(x, mem, attn_norm[l], w_in[l], w_up_dil[l], w_up_moba[l], w_branch_gate[l], w_out[l],
                cross_norm[l], mem_norm[l], w_q_mem[l], w_kv_mem[l], w_o_mem[l])
    w_route = jnp.concatenate([w_router_group[l], w_router_expert[l]], axis=1)
    b_route = jnp.concatenate([b_router_group[l], b_router_expert[l]], axis=0)
    out = _moe(x2, ffn_norm[l], w_route, b_route, w_expert_gate[l].astype(BF16),
               w_expert_up[l].astype(BF16), w_expert_down[l].astype(BF16), final_norm)
    return out.reshape(b, s, d)
```

```python
import functools

import numpy as np
import jax
import jax.numpy as jnp
from jax import lax
from jax.experimental import pallas as pl
from jax.experimental.pallas import tpu as pltpu

F32 = jnp.float32
BF16 = jnp.bfloat16

HEAD_DIM = 128
DIL_CONFIGS = ((128, 1), (512, 4), (2048, 16))
DIL_HEADS_PER_GROUP = 4
DIL_HEADS = DIL_HEADS_PER_GROUP * len(DIL_CONFIGS)
DIL_WIDTH = DIL_HEADS * HEAD_DIM
DIL_OUT = DIL_HEADS_PER_GROUP * HEAD_DIM
DIL_STEPS = 128
MOBA_HEADS = 8
MOBA_WIDTH = MOBA_HEADS * HEAD_DIM
MOBA_BLOCK = 256
MOBA_TOPK = 3
IN_WIDTH = 3 * (DIL_WIDTH + MOBA_WIDTH)
MEM_HEADS = 4
MEM_WIDTH = MEM_HEADS * HEAD_DIM
N_GROUPS = 4
EXPERTS_PER_GROUP = 8
N_EXPERTS = N_GROUPS * EXPERTS_PER_GROUP
MOE_TOPK = 2
RMS_EPS = 1e-6
SCALE = HEAD_DIM ** -0.5
NEG_INF = float("-inf")
LOG2E = 1.4426950408889634
MOBA_GROUP = 4
MOBA_HEADS_PER_STEP = 4
MOBA_Q_TILES = (9, 11)

VMEM_LIMIT_BYTES = 56 * 1024 * 1024
BF16_ROWS = 16
_NT = (((1,), (1,)), ((), ()))


def _alibi_slopes(n):
    return jnp.asarray(2.0 ** (-8.0 * np.arange(1, n + 1) / n), dtype=F32)


def _rms(x, g):
    return x * lax.rsqrt(jnp.mean(x * x, axis=-1, keepdims=True) + RMS_EPS) * g


def _dot_split3(a, b):
    a_hi = a.astype(BF16)
    b_hi = b.astype(BF16)
    a_lo = (a - a_hi.astype(F32)).astype(BF16)
    b_lo = (b - b_hi.astype(F32)).astype(BF16)
    dot = functools.partial(jnp.dot, preferred_element_type=F32)
    return dot(a_hi, b_hi) + (dot(a_lo, b_hi) + dot(a_hi, b_lo))


def _params(*sem, flags=None):
    return pltpu.CompilerParams(dimension_semantics=sem, vmem_limit_bytes=VMEM_LIMIT_BYTES,
                                flags=flags)


def _resident(shape):
    nd = len(shape)
    return pl.BlockSpec(shape, lambda *_: (0,) * nd, pipeline_mode=pl.Buffered(1))


def _norm_matmul_body(x_ref, g_ref, w_ref, o_ref, h_ref, *, sigmoid):
    @pl.when(pl.program_id(1) == 0)
    def _():
        h_ref[...] = _rms(x_ref[...], g_ref[...]).astype(BF16)

    acc = jnp.dot(h_ref[...], w_ref[...], preferred_element_type=F32)
    if sigmoid:
        acc = jax.nn.sigmoid(acc)
    o_ref[...] = acc.astype(o_ref.dtype)


def _norm_matmul(x, g, w, *, tm, tn, sigmoid=False, name):
    m, d = x.shape
    n = w.shape[1]
    assert m % tm == 0 and n % tn == 0
    return pl.pallas_call(
        functools.partial(_norm_matmul_body, sigmoid=sigmoid),
        grid=(m // tm, n // tn),
        in_specs=[
            pl.BlockSpec((tm, d), lambda i, j: (i, 0)),
            pl.BlockSpec((1, d), lambda i, j: (0, 0)),
            pl.BlockSpec((d, tn), lambda i, j: (0, j)),
        ],
        out_specs=pl.BlockSpec((tm, tn), lambda i, j: (i, j)),
        out_shape=jax.ShapeDtypeStruct((m, n), BF16),
        scratch_shapes=[pltpu.VMEM((tm, d), BF16)],
        compiler_params=_params("parallel", "arbitrary"),
        name=name,
    )(x, g.reshape(1, d), w)


def _norm_body(x_ref, g_ref, o_ref):
    o_ref[...] = _rms(x_ref[...], g_ref[...]).astype(o_ref.dtype)


def _norm(x, g, *, tm, name):
    m, d = x.shape
    assert m % tm == 0
    return pl.pallas_call(
        _norm_body,
        grid=(m // tm,),
        in_specs=[pl.BlockSpec((tm, d), lambda i: (i, 0)), pl.BlockSpec((1, d), lambda i: (0, 0))],
        out_specs=pl.BlockSpec((tm, d), lambda i: (i, 0)),
        out_shape=jax.ShapeDtypeStruct((m, d), BF16),
        compiler_params=_params("parallel"),
        name=name,
    )(x, g.reshape(1, d))


def _in_proj_body(h_ref, win_ref, wbg_ref, proj_ref, gates_ref, perm_ref, *, n_proj_tiles):
    j = pl.program_id(1)
    tm = h_ref.shape[0]
    n = DIL_STEPS
    groups = len(DIL_CONFIGS)

    def store_regrouped(acc, dilation):
        tile = n * dilation
        for c in range(acc.shape[1] // HEAD_DIM):
            cols = slice(c * HEAD_DIM, (c + 1) * HEAD_DIM)
            perm_ref[c] = acc[:, cols]
            for t0 in range(0, tm, tile):
                for r in range(dilation):
                    rows = perm_ref[c, pl.ds(t0 + r, n, stride=dilation), :]
                    proj_ref[t0 + r * n:t0 + (r + 1) * n, cols] = rows.astype(BF16)

    @pl.when(j < n_proj_tiles)
    def _():
        acc = jnp.dot(h_ref[...], win_ref[...].astype(BF16), preferred_element_type=F32)
        group = jnp.where(j < 3 * groups, lax.rem(j, groups), 0)
        moba_q = (j >= MOBA_Q_TILES[0]) & (j < MOBA_Q_TILES[1])
        scale = jnp.where(moba_q, SCALE * LOG2E, 1.0)

        def store_natural():
            proj_ref[...] = (acc * scale).astype(BF16)

        for g, (_, dilation) in enumerate(DIL_CONFIGS):
            store = store_natural if dilation == 1 else functools.partial(store_regrouped, acc, dilation)
            pl.when(group == g)(store)

    @pl.when(j >= n_proj_tiles)
    def _():
        acc = jnp.dot(h_ref[...], wbg_ref[...].astype(BF16), preferred_element_type=F32)
        gates_ref[...] = jax.nn.sigmoid(acc).astype(BF16)


def _in_proj(h, w_in, w_bg, *, tm):
    t, d = h.shape
    tn = DIL_OUT
    assert MOBA_Q_TILES == (3 * DIL_WIDTH // tn, (3 * DIL_WIDTH + MOBA_WIDTH) // tn)
    assert t % tm == 0 and tm % (DIL_STEPS * max(dl for _, dl in DIL_CONFIGS)) == 0
    assert DIL_WIDTH == len(DIL_CONFIGS) * tn and w_in.shape[1] % tn == 0 and w_bg.shape[1] % tn == 0
    n_proj = w_in.shape[1] // tn
    n_gate = w_bg.shape[1] // tn
    return pl.pallas_call(
        functools.partial(_in_proj_body, n_proj_tiles=n_proj),
        grid=(t // tm, n_proj + n_gate),
        in_specs=[
            pl.BlockSpec((tm, d), lambda i, j: (i, 0)),
            pl.BlockSpec((d, tn), lambda i, j: (0, jnp.minimum(j, n_proj - 1))),
            pl.BlockSpec((d, tn), lambda i, j: (0, jnp.maximum(j - n_proj, 0))),
        ],
        out_specs=[
            pl.BlockSpec((tm, tn), lambda i, j: (i, jnp.minimum(j, n_proj - 1))),
            pl.BlockSpec((tm, tn), lambda i, j: (i, jnp.maximum(j - n_proj, 0))),
        ],
        out_shape=[jax.ShapeDtypeStruct((t, w_in.shape[1]), BF16),
                   jax.ShapeDtypeStruct((t, w_bg.shape[1]), BF16)],
        scratch_shapes=[pltpu.VMEM((tn // HEAD_DIM, tm, HEAD_DIM), F32)],
        compiler_params=_params("parallel", "arbitrary"),
        name="in_proj_gates",
    )(h, w_in, w_bg)


DIL_SUBBLOCKS = 4


def _dilated_body(slope_ref, q_ref, kp_ref, kc_ref, vp_ref, vc_ref, o_ref, lse_ref, *scratch,
                  dilation, group):
    n = DIL_STEPS
    first_tile = pl.program_id(1) == 0
    chunk = pl.program_id(2)
    chunks = max(dilation // DIL_SUBBLOCKS, 1)
    qi = lax.broadcasted_iota(jnp.int32, (n, n), 0)
    kj = lax.broadcasted_iota(jnp.int32, (n, n), 1)
    steps_cur = qi - kj
    valid_cur = steps_cur >= 0
    dist_cur = (steps_cur * dilation).astype(F32)
    dist_prev = ((steps_cur + n) * dilation).astype(F32)
    limit_across_tiles = jnp.where(first_tile, -n, 0)
    heads = range(DIL_HEADS_PER_GROUP)
    cols_of = [slice(h * HEAD_DIM, (h + 1) * HEAD_DIM) for h in heads]
    rows_of = [slice(s * n, (s + 1) * n) for s in range(DIL_SUBBLOCKS)]

    def prev_rows(s):
        if dilation == 1:
            return (slice(0, n), False, limit_across_tiles) if s == 0 else (rows_of[s - 1], True, 0)
        return rows_of[s], False, limit_across_tiles

    def scores(s):
        rows, from_cur, _ = prev_rows(s)
        out = []
        for h in heads:
            q = q_ref[rows_of[s], cols_of[h]]
            k_prev = (kc_ref if from_cur else kp_ref)[rows, cols_of[h]]
            out.append((lax.dot_general(q, kc_ref[rows_of[s], cols_of[h]], _NT,
                                        preferred_element_type=F32),
                        lax.dot_general(q, k_prev, _NT, preferred_element_type=F32)))
        return out

    def softmax(s, raw):
        valid_prev = steps_cur <= prev_rows(s)[2]
        out = []
        for h in heads:
            slope = slope_ref[group * DIL_HEADS_PER_GROUP + h]
            s_cur = jnp.where(valid_cur, raw[h][0] * SCALE - slope * dist_cur, NEG_INF)
            s_prev = jnp.where(valid_prev, raw[h][1] * SCALE - slope * dist_prev, NEG_INF)
            m = jnp.maximum(jnp.max(s_cur, axis=-1, keepdims=True),
                            jnp.max(s_prev, axis=-1, keepdims=True))
            p_cur = jnp.exp(s_cur - m)
            p_prev = jnp.exp(s_prev - m)
            den = jnp.sum(p_cur, axis=-1, keepdims=True) + jnp.sum(p_prev, axis=-1, keepdims=True)
            out.append((p_cur.astype(BF16), p_prev.astype(BF16), m, den))
        return out

    def values(s, probs):
        rows, from_cur, _ = prev_rows(s)
        for h in heads:
            p_cur, p_prev, m, den = probs[h]
            v_prev = (vc_ref if from_cur else vp_ref)[rows, cols_of[h]]
            o = (jnp.dot(p_cur, vc_ref[rows_of[s], cols_of[h]], preferred_element_type=F32)
                 + jnp.dot(p_prev, v_prev, preferred_element_type=F32)) / den
            lse = jnp.broadcast_to(m + jnp.log(den), (n, HEAD_DIM))
            if dilation == 1:
                dst_o, dst_lse, rows_out = o_ref, lse_ref, rows_of[s]
            elif chunks == 1:
                dst_o, dst_lse, rows_out = o_ref, lse_ref, pl.ds(s, n, stride=dilation)
            else:
                dst_o, dst_lse = scratch
                rows_out = pl.ds(pl.multiple_of((chunk * DIL_SUBBLOCKS + s) * n, n), n)
            dst_o[h, rows_out, :] = o
            dst_lse[h, rows_out, :] = lse

    raw, probs = {0: scores(0)}, {}
    for s in range(DIL_SUBBLOCKS + 1):
        if s + 1 < DIL_SUBBLOCKS:
            raw[s + 1] = scores(s + 1)
        if s < DIL_SUBBLOCKS:
            probs[s] = softmax(s, raw.pop(s))
        if s >= 1:
            values(s - 1, probs.pop(s - 1))

    if chunks > 1:
        o_tile, lse_tile = scratch

        @pl.when(chunk == chunks - 1)
        def _():
            for h in heads:
                for r in range(dilation):
                    natural = pl.ds(r, n, stride=dilation)
                    o_ref[h, natural, :] = o_tile[h, r * n:(r + 1) * n, :]
                    lse_ref[h, natural, :] = lse_tile[h, r * n:(r + 1) * n, :]


def _dilated_group(proj, slopes, group, dilation):
    b, s, _ = proj.shape
    n = DIL_STEPS
    step_rows = DIL_SUBBLOCKS * n
    assert dilation == 1 or dilation % DIL_SUBBLOCKS == 0
    chunks = max(dilation // DIL_SUBBLOCKS, 1)
    tile = step_rows * chunks
    assert s % tile == 0
    groups = len(DIL_CONFIGS)
    prev_rows = n if dilation == 1 else step_rows
    blocks_per_step = step_rows // prev_rows

    def cur(section):
        return pl.BlockSpec((None, step_rows, DIL_OUT),
                            lambda bi, i, c: (bi, i * chunks + c, section * groups + group))

    def prev(section):
        def index(bi, i, c):
            if dilation == 1:
                return (bi, jnp.maximum(i * blocks_per_step - 1, 0), section * groups + group)
            return (bi, jnp.maximum(i - 1, 0) * chunks + c, section * groups + group)
        return pl.BlockSpec((None, prev_rows, DIL_OUT), index)

    out_block = (DIL_HEADS_PER_GROUP, tile, HEAD_DIM)
    out_spec = pl.BlockSpec((None,) + out_block, lambda bi, i, c: (bi, 0, i, 0))
    out_sds = jax.ShapeDtypeStruct((b, DIL_HEADS_PER_GROUP, s, HEAD_DIM), F32)
    scratch = [pltpu.VMEM(out_block, F32)] * 2 if chunks > 1 else []
    o, lse = pl.pallas_call(
        functools.partial(_dilated_body, dilation=dilation, group=group),
        grid=(b, s // tile, chunks),
        in_specs=[pl.BlockSpec(memory_space=pltpu.SMEM),
                  cur(0), prev(1), cur(1), prev(2), cur(2)],
        out_specs=[out_spec, out_spec],
        out_shape=[out_sds, out_sds],
        scratch_shapes=scratch,
        compiler_params=_params("parallel", "arbitrary", "arbitrary"),
        name=f"dilated_attn_g{group}",
    )(slopes, proj, proj, proj, proj, proj)
    return o, lse


def _moba_body(slope_ref, q_ref, k_ref, v_ref, o_ref, kmean_ref, vt_ref, bias_ref, sel_ref,
               *, nblk, group):
    blk = MOBA_BLOCK
    hd = HEAD_DIM
    heads = q_ref.shape[1] // hd
    own = pl.program_id(2)
    key_off = lax.broadcasted_iota(jnp.int32, (blk, blk), 0)
    qry_off = lax.broadcasted_iota(jnp.int32, (blk, blk), 1)
    slope2 = [slope_ref[pl.program_id(1) * heads + h] * LOG2E for h in range(heads)]

    @pl.when(own == 0)
    def _():
        def fill(jb, carry):
            start = pl.multiple_of(jb * blk, blk)
            rows = k_ref[pl.ds(start, blk), :].astype(F32)
            kmean_ref[pl.ds(jb, 1), :] = jnp.mean(rows, axis=0, keepdims=True)
            vrows = v_ref[pl.ds(start, blk), :].astype(F32)
            for h in range(heads):
                vt_ref[h, :hd, pl.ds(start, blk)] = vrows[:, h * hd:(h + 1) * hd].T.astype(BF16)
            return carry
        lax.fori_loop(0, nblk, fill, 0)
        for h in range(heads):
            vt_ref[h, hd:, :] = jnp.ones((vt_ref.shape[1] - hd, vt_ref.shape[2]), BF16)
        for h in range(heads):
            bias_ref[h] = -slope2[h] * (qry_off - key_off).astype(F32)

    q = [q_ref[:, h * hd:(h + 1) * hd] for h in range(heads)]

    blk_id = lax.broadcasted_iota(jnp.int32, (nblk, blk), 0).astype(F32)
    gates = []
    for h in range(heads):
        gate = lax.dot_general(kmean_ref[:, h * hd:(h + 1) * hd], q[h].astype(F32), _NT,
                               precision=lax.Precision.HIGHEST, preferred_element_type=F32)
        gates.append(jnp.where(blk_id < own.astype(F32), gate, NEG_INF))

    def scores(h, start, rows):
        k = k_ref[pl.ds(start, rows), h * hd:(h + 1) * hd]
        return lax.dot_general(k, q[h], _NT, preferred_element_type=F32)

    own_start = pl.multiple_of(own * blk, blk)
    own_x = [scores(h, own_start, blk) for h in range(heads)]

    sels = [jnp.zeros((nblk, blk), F32) for _ in range(heads)]
    for _ in range(MOBA_TOPK):
        for h in range(heads):
            best = jnp.max(gates[h], axis=0, keepdims=True)
            is_best = (gates[h] == best) & (gates[h] > NEG_INF)
            pick = jnp.min(jnp.where(is_best, blk_id, float(nblk)), axis=0, keepdims=True)
            picked = blk_id == pick
            sels[h] = jnp.where(picked, 1.0, sels[h])
            gates[h] = jnp.where(picked, NEG_INF, gates[h])
    for h in range(heads):
        sel_ref[h] = sels[h]

    own_p = []
    for h in range(heads):
        x = jnp.where(qry_off >= key_off, own_x[h] + bias_ref[h], NEG_INF)
        m0 = jnp.max(x, axis=0, keepdims=True)
        own_p.append((m0, jnp.exp2(x - m0).astype(BF16)))
    init = []
    for h in range(heads):
        m0, p = own_p[h]
        acc0 = jnp.dot(vt_ref[h, :, pl.ds(own_start, blk)], p, preferred_element_type=F32)
        init += [m0, acc0]

    def softmax_update(h, i, x, m):
        xs, chosen, shift = [], [], []
        m_new = m
        for g in range(group):
            j = i * group + g
            xs.append(x[g * blk:(g + 1) * blk] + bias_ref[h])
            chosen.append(sel_ref[h, pl.ds(j, 1), :] > 0.5)
            shift.append(-slope2[h] * ((own - j) * blk).astype(F32))
            top = jnp.max(xs[g], axis=0, keepdims=True) + shift[g]
            m_new = jnp.maximum(m_new, jnp.where(chosen[g], top, NEG_INF))
        ps = []
        for g in range(group):
            ref_g = jnp.where(chosen[g], m_new - shift[g], jnp.inf)
            ps.append(jnp.exp2(xs[g] - ref_g).astype(BF16))
        return m_new, jnp.exp2(m - m_new), jnp.concatenate(ps, axis=0)

    def past_blocks(i, carry):
        start = pl.multiple_of(i * (group * blk), group * blk)
        x, sm, out = {0: carry[-1]}, {}, [None] * (2 * heads + 1)
        for h in range(heads + 1):
            if h + 1 < heads:
                x[h + 1] = scores(h + 1, start, group * blk)
            elif h + 1 == heads:
                out[-1] = scores(0, group_start(i + 1), group * blk)
            if h < heads:
                sm[h] = softmax_update(h, i, x[h], carry[2 * h])
            if h >= 1:
                m_new, alpha, p = sm[h - 1]
                acc = alpha * carry[2 * h - 1] + jnp.dot(
                    vt_ref[h - 1, :, pl.ds(start, group * blk)], p, preferred_element_type=F32)
                out[2 * h - 2:2 * h] = [m_new, acc]
        return tuple(out)

    def group_start(i):
        return pl.multiple_of(jnp.minimum(i, nblk // group - 1) * (group * blk), group * blk)

    n_groups = lax.div(own + (group - 1), group)
    init.append(scores(0, group_start(0), group * blk))
    final = lax.fori_loop(0, n_groups, past_blocks, tuple(init))
    for h in range(heads):
        acc = final[2 * h + 1]
        o_ref[:, h * hd:(h + 1) * hd] = (acc[:hd] / acc[hd:hd + 1]).T.astype(o_ref.dtype)


def _moba(proj, slopes):
    b, s, _ = proj.shape
    assert s % (MOBA_BLOCK * MOBA_GROUP) == 0
    nblk = s // MOBA_BLOCK
    hp = MOBA_HEADS_PER_STEP
    width = hp * HEAD_DIM
    q0 = 3 * DIL_WIDTH // width
    k0 = q0 + MOBA_WIDTH // width
    v0 = k0 + MOBA_WIDTH // width
    y = pl.pallas_call(
        functools.partial(_moba_body, nblk=nblk, group=MOBA_GROUP),
        grid=(b, MOBA_HEADS // hp, nblk),
        in_specs=[
            pl.BlockSpec(memory_space=pltpu.SMEM),
            pl.BlockSpec((None, MOBA_BLOCK, width), lambda bi, h, i: (bi, i, q0 + h)),
            pl.BlockSpec((None, s, width), lambda bi, h, i: (bi, 0, k0 + h)),
            pl.BlockSpec((None, s, width), lambda bi, h, i: (bi, 0, v0 + h)),
        ],
        out_specs=pl.BlockSpec((None, MOBA_BLOCK, width), lambda bi, h, i: (bi, i, h)),
        out_shape=jax.ShapeDtypeStruct((b, s, MOBA_WIDTH), BF16),
        scratch_shapes=[pltpu.VMEM((nblk, width), F32),
                        pltpu.VMEM((hp, HEAD_DIM + BF16_ROWS, s), BF16),
                        pltpu.VMEM((hp, MOBA_BLOCK, MOBA_BLOCK), F32),
                        pltpu.VMEM((hp, nblk, MOBA_BLOCK), F32)],
        compiler_params=_params("parallel", "parallel", "arbitrary"),
        name="moba_attn",
    )(slopes, proj, proj, proj)
    return y.reshape(b * s, MOBA_WIDTH)


def _merged_branches(x, o_refs, l_refs, ym_ref, gd_ref, gm_ref, wud_ref, wum_ref, wo_ref):
    heads = []
    for h in range(DIL_HEADS_PER_GROUP):
        l0, l1, l2 = (l_ref[h] for l_ref in l_refs)
        m = jnp.maximum(jnp.maximum(l0, l1), l2)
        e0, e1, e2 = jnp.exp(l0 - m), jnp.exp(l1 - m), jnp.exp(l2 - m)
        den = e0 + e1 + e2
        mixed = ((e0 / den) * o_refs[0][h] + (e1 / den) * o_refs[1][h] + (e2 / den) * o_refs[2][h])
        heads.append(mixed.astype(BF16))
    y_dil = jnp.concatenate(heads, axis=-1)
    lift_dil = jnp.dot(y_dil, wud_ref[...], preferred_element_type=F32)
    lift_moba = jnp.dot(ym_ref[...], wum_ref[...], preferred_element_type=F32)
    merged = gd_ref[...].astype(F32) * lift_dil + gm_ref[...].astype(F32) * lift_moba
    return x + jnp.dot(merged.astype(BF16), wo_ref[...], preferred_element_type=F32)


def _cross_attended(x, g_ref, wq_ref, kv_ref, wo_ref):
    h = _rms(x, g_ref[...]).astype(BF16)
    q = jnp.dot(h, wq_ref[...], preferred_element_type=F32).astype(BF16)
    heads = []
    for hd in range(MEM_HEADS):
        k = kv_ref[:, hd * HEAD_DIM:(hd + 1) * HEAD_DIM]
        v = kv_ref[:, MEM_WIDTH + hd * HEAD_DIM:MEM_WIDTH + (hd + 1) * HEAD_DIM]
        s = lax.dot_general(q[:, hd * HEAD_DIM:(hd + 1) * HEAD_DIM], k, _NT,
                            preferred_element_type=F32) * SCALE
        p = jnp.exp(s - jnp.max(s, axis=-1, keepdims=True))
        den = jnp.sum(p, axis=-1, keepdims=True)
        heads.append((jnp.dot(p.astype(BF16), v, preferred_element_type=F32) / den).astype(BF16))
    o = jnp.concatenate(heads, axis=-1)
    return x + jnp.dot(o, wo_ref[...], preferred_element_type=F32)


def _merge_cross_body(x_ref, o0_ref, o1_ref, o2_ref, l0_ref, l1_ref, l2_ref, ym_ref, gd_ref, gm_ref,
                      wud_ref, wum_ref, wo_ref, gc_ref, wq_ref, kv_ref, wom_ref, out_ref):
    x1 = _merged_branches(x_ref[...], (o0_ref, o1_ref, o2_ref), (l0_ref, l1_ref, l2_ref),
                          ym_ref, gd_ref, gm_ref, wud_ref, wum_ref, wo_ref)
    out_ref[...] = _cross_attended(x1, gc_ref, wq_ref, kv_ref, wom_ref)


def _merge_cross(x, dil_outs, dil_lses, y_moba, gates, w_up_dil, w_up_moba, w_out,
                 g_cross, w_q, kv, w_o, *, tm):
    t, d = x.shape
    s = dil_outs[0].shape[2]
    mem_len = kv.shape[1]
    assert t % tm == 0 and s % tm == 0
    per_batch = s // tm
    row = lambda w: pl.BlockSpec((tm, w), lambda i: (i, 0))
    dil = pl.BlockSpec((None, DIL_HEADS_PER_GROUP, tm, HEAD_DIM),
                       lambda i: (i // per_batch, 0, i % per_batch, 0))
    return pl.pallas_call(
        _merge_cross_body,
        grid=(t // tm,),
        in_specs=[row(d)] + [dil] * 6 + [
            row(MOBA_WIDTH),
            pl.BlockSpec((tm, d), lambda i: (i, 0)),
            pl.BlockSpec((tm, d), lambda i: (i, 1)),
            _resident((DIL_OUT, d)), _resident((MOBA_WIDTH, d)), _resident((d, d)),
            _resident((1, d)), _resident((d, MEM_WIDTH)),
            pl.BlockSpec((None, mem_len, 2 * MEM_WIDTH), lambda i: (i // per_batch, 0, 0)),
            _resident((MEM_WIDTH, d)),
        ],
        out_specs=row(d),
        out_shape=jax.ShapeDtypeStruct((t, d), F32),
        compiler_params=_params("parallel"),
        name="merge_out_proj_cross_attn",
    )(x, *dil_outs, *dil_lses, y_moba, gates, gates, w_up_dil, w_up_moba, w_out,
      g_cross.reshape(1, d), w_q, kv, w_o)


ROUTE_COLS = 8
MOE_ROW_TILE = 256


def _route_body(x_ref, g_ref, wr_ref, br_ref, info_ref, counts_ref, run_ref, tri_ref):
    tm = x_ref.shape[0]
    n_route = N_GROUPS + N_EXPERTS
    lane = lax.broadcasted_iota(jnp.int32, (tm, n_route), 1).astype(F32)

    @pl.when(pl.program_id(0) == 0)
    def _():
        run_ref[...] = jnp.zeros_like(run_ref)
        earlier = (lax.broadcasted_iota(jnp.int32, (tm, tm), 0)
                   > lax.broadcasted_iota(jnp.int32, (tm, tm), 1))
        tri_ref[...] = jnp.where(earlier, 1.0, 0.0).astype(BF16)

    t = _rms(x_ref[...], g_ref[...])
    logits = _dot_split3(t, wr_ref[...]) + br_ref[...]
    none = float(n_route)
    glog = jnp.where(lane < N_GROUPS, logits, NEG_INF)
    gmax = jnp.max(glog, axis=-1, keepdims=True)
    gsel = jnp.min(jnp.where(glog == gmax, lane, none), axis=-1, keepdims=True)
    pg = 1.0 / jnp.sum(jnp.exp(glog - gmax), axis=-1, keepdims=True)
    first = N_GROUPS + gsel * EXPERTS_PER_GROUP
    in_group = (lane >= first) & (lane < first + EXPERTS_PER_GROUP)
    elog = jnp.where(in_group, logits, NEG_INF)
    top1 = jnp.max(elog, axis=-1, keepdims=True)
    i1 = jnp.min(jnp.where(elog == top1, lane, none), axis=-1, keepdims=True)
    rest = jnp.where(lane == i1, NEG_INF, elog)
    top2 = jnp.max(rest, axis=-1, keepdims=True)
    i2 = jnp.min(jnp.where(rest == top2, lane, none), axis=-1, keepdims=True)
    e2 = jnp.exp(top2 - top1)
    w1 = pg / (1.0 + e2)
    w2 = pg * e2 / (1.0 + e2)

    hit1 = lane == i1
    hit2 = lane == i2
    assigned = jnp.where(hit1 | hit2, 1.0, 0.0)
    before = jnp.dot(tri_ref[...], assigned.astype(BF16), preferred_element_type=F32) + run_ref[...]
    rank1 = jnp.sum(jnp.where(hit1, before, 0.0), axis=-1, keepdims=True)
    rank2 = jnp.sum(jnp.where(hit2, before, 0.0), axis=-1, keepdims=True)
    run_ref[...] += jnp.sum(assigned, axis=0, keepdims=True)
    counts_ref[...] = run_ref[...]

    col = lax.broadcasted_iota(jnp.int32, (tm, ROUTE_COLS), 1)
    fields = (i1 - N_GROUPS, i2 - N_GROUPS, rank1, rank2, w1, w2)
    info = jnp.zeros((tm, ROUTE_COLS), F32)
    for c, field in enumerate(fields):
        info = jnp.where(col == c, field, info)
    info_ref[...] = info


def _route(x, g, w_route, b_route, *, tm):
    t, d = x.shape
    n_route = N_GROUPS + N_EXPERTS
    assert t % tm == 0
    return pl.pallas_call(
        _route_body,
        grid=(t // tm,),
        in_specs=[
            pl.BlockSpec((tm, d), lambda i: (i, 0)),
            pl.BlockSpec((1, d), lambda i: (0, 0)),
            pl.BlockSpec((d, n_route), lambda i: (0, 0)),
            pl.BlockSpec((1, n_route), lambda i: (0, 0)),
        ],
        out_specs=[pl.BlockSpec((tm, ROUTE_COLS), lambda i: (i, 0)),
                   pl.BlockSpec((1, n_route), lambda i: (0, 0))],
        out_shape=[jax.ShapeDtypeStruct((t, ROUTE_COLS), F32),
                   jax.ShapeDtypeStruct((1, n_route), F32)],
        scratch_shapes=[pltpu.VMEM((1, n_route), F32), pltpu.VMEM((tm, tm), BF16)],
        compiler_params=_params("arbitrary"),
        name="moe_route",
    )(x, g.reshape(1, d), w_route, b_route.reshape(1, n_route))


def _row_copies_wait(src_rows, dst_rows, sem):
    pltpu.make_async_copy(src_rows, dst_rows, sem).wait()


def _dispatch_body(pos_ref, fill_ref, x_ref, g_ref, sorted_ref, t_ref, zero_ref, sem, fill_sem):
    tm = x_ref.shape[0]
    tr = zero_ref.shape[0]
    base = pl.program_id(0) * tm

    @pl.when(pl.program_id(0) == 0)
    def _():
        zero_ref[...] = jnp.zeros_like(zero_ref)

        def fill_copy(tile):
            return pltpu.make_async_copy(zero_ref, sorted_ref.at[pl.ds(tile * tr, tr)], fill_sem)

        def start(idx, carry):
            pl.when(fill_ref[idx] >= 0)(lambda: fill_copy(fill_ref[idx]).start())
            return carry

        def finish(idx, carry):
            pl.when(fill_ref[idx] >= 0)(lambda: fill_copy(fill_ref[idx]).wait())
            return carry

        lax.fori_loop(0, fill_ref.shape[0], start, 0)
        lax.fori_loop(0, fill_ref.shape[0], finish, 0)

    t_ref[...] = _rms(x_ref[...], g_ref[...])

    def issue(r, carry):
        for k in range(MOE_TOPK):
            dst = pos_ref[k, base + r]
            pltpu.make_async_copy(t_ref.at[pl.ds(r, 1)], sorted_ref.at[pl.ds(dst, 1)], sem).start()
        return carry

    lax.fori_loop(0, tm, issue, 0, unroll=8)
    for _ in range(MOE_TOPK):
        _row_copies_wait(t_ref, sorted_ref.at[pl.ds(0, tm)], sem)


def _dispatch(x, g, pos, fill_tiles, n_rows, *, tm):
    t, d = x.shape
    assert t % tm == 0
    return pl.pallas_call(
        _dispatch_body,
        grid_spec=pltpu.PrefetchScalarGridSpec(
            num_scalar_prefetch=2,
            grid=(t // tm,),
            in_specs=[pl.BlockSpec((tm, d), lambda i, pos, fill: (i, 0)),
                      pl.BlockSpec((1, d), lambda i, pos, fill: (0, 0))],
            out_specs=pl.BlockSpec(memory_space=pl.ANY),
            scratch_shapes=[pltpu.VMEM((tm, d), F32), pltpu.VMEM((MOE_ROW_TILE, d), F32),
                            pltpu.SemaphoreType.DMA(()), pltpu.SemaphoreType.DMA(())],
        ),
        out_shape=jax.ShapeDtypeStruct((n_rows, d), F32),
        compiler_params=pltpu.CompilerParams(dimension_semantics=("arbitrary",),
                                             vmem_limit_bytes=VMEM_LIMIT_BYTES,
                                             disable_bounds_checks=True),
        name="moe_dispatch",
    )(pos, fill_tiles, x, g.reshape(1, d))


def _experts_body(tile_expert_ref, n_tiles_ref, x_ref, wg_ref, wu_ref, wd_ref, y_ref):
    del tile_expert_ref
    in_use = pl.program_id(0) < n_tiles_ref[0]

    @pl.when(in_use)
    def _():
        t = x_ref[...].astype(BF16)
        gate = jnp.dot(t, wg_ref[...].astype(BF16), preferred_element_type=F32)
        up = jnp.dot(t, wu_ref[...].astype(BF16), preferred_element_type=F32)
        a = jax.nn.silu(gate) * up
        y_ref[...] = jnp.dot(a.astype(BF16), wd_ref[...].astype(BF16), preferred_element_type=F32)

    @pl.when(jnp.logical_not(in_use))
    def _():
        y_ref[...] = jnp.zeros_like(y_ref)


def _experts(sorted_rows, tile_expert, n_tiles, w_gate, w_up, w_down):
    p, d = sorted_rows.shape
    ff = w_gate.shape[-1]
    tr = MOE_ROW_TILE
    assert p % tr == 0
    used = lambda i, nt: jnp.minimum(i, nt[0] - 1)
    return pl.pallas_call(
        _experts_body,
        grid_spec=pltpu.PrefetchScalarGridSpec(
            num_scalar_prefetch=2,
            grid=(p // tr,),
            in_specs=[pl.BlockSpec((tr, d), lambda i, te, nt: (used(i, nt), 0)),
                      pl.BlockSpec((None, d, ff), lambda i, te, nt: (te[used(i, nt)], 0, 0)),
                      pl.BlockSpec((None, d, ff), lambda i, te, nt: (te[used(i, nt)], 0, 0)),
                      pl.BlockSpec((None, ff, d), lambda i, te, nt: (te[used(i, nt)], 0, 0))],
            out_specs=pl.BlockSpec((tr, d), lambda i, te, nt: (i, 0)),
        ),
        out_shape=jax.ShapeDtypeStruct((p, d), F32),
        compiler_params=_params("arbitrary"),
        name="moe_experts",
    )(tile_expert, n_tiles, sorted_rows, w_gate, w_up, w_down)


def _combine_body(pos_ref, x_ref, info_ref, gf_ref, y_sorted_ref, out_ref, rows_ref, sem):
    tm = x_ref.shape[0]
    base = pl.program_id(0) * tm

    def issue(r, carry):
        for k in range(MOE_TOPK):
            src = pos_ref[k, base + r]
            pltpu.make_async_copy(y_sorted_ref.at[pl.ds(src, 1)], rows_ref.at[k, pl.ds(r, 1)],
                                  sem).start()
        return carry

    lax.fori_loop(0, tm, issue, 0, unroll=8)
    for k in range(MOE_TOPK):
        _row_copies_wait(y_sorted_ref.at[pl.ds(0, tm)], rows_ref.at[k], sem)

    info = info_ref[...]
    y = info[:, 4:5] * rows_ref[0] + info[:, 5:6] * rows_ref[1]
    out_ref[...] = _rms(x_ref[...] + y, gf_ref[...])


def _combine(x, info, pos, y_sorted, g_final, *, tm):
    t, d = x.shape
    assert t % tm == 0
    return pl.pallas_call(
        _combine_body,
        grid_spec=pltpu.PrefetchScalarGridSpec(
            num_scalar_prefetch=1,
            grid=(t // tm,),
            in_specs=[pl.BlockSpec((tm, d), lambda i, pos: (i, 0)),
                      pl.BlockSpec((tm, ROUTE_COLS), lambda i, pos: (i, 0)),
                      pl.BlockSpec((1, d), lambda i, pos: (0, 0)),
                      pl.BlockSpec(memory_space=pl.ANY)],
            out_specs=pl.BlockSpec((tm, d), lambda i, pos: (i, 0)),
            scratch_shapes=[pltpu.VMEM((MOE_TOPK, tm, d), F32), pltpu.SemaphoreType.DMA(())],
        ),
        out_shape=jax.ShapeDtypeStruct((t, d), F32),
        compiler_params=pltpu.CompilerParams(dimension_semantics=("arbitrary",),
                                             vmem_limit_bytes=VMEM_LIMIT_BYTES,
                                             disable_bounds_checks=True),
        name="moe_combine_final_norm",
    )(pos, x, info, g_final.reshape(1, d), y_sorted)


def _moe(x, g, w_route, b_route, w_gate, w_up, w_down, g_final):
    t, d = x.shape
    tr = MOE_ROW_TILE
    assert (MOE_TOPK * t) % tr == 0
    info, counts = _route(x, g, w_route, b_route, tm=min(512, t))

    expert = info[:, 0:MOE_TOPK].astype(jnp.int32)
    rank = info[:, MOE_TOPK:2 * MOE_TOPK].astype(jnp.int32)
    count = counts[0, N_GROUPS:].astype(jnp.int32)
    seg_tiles = (count + (tr - 1)) // tr
    seg_end = jnp.cumsum(seg_tiles)
    seg_start_row = (seg_end - seg_tiles) * tr
    pos = (seg_start_row[expert] + rank).T
    max_tiles = (MOE_TOPK * t) // tr + N_EXPERTS
    tile_id = jnp.arange(max_tiles, dtype=jnp.int32)
    tile_expert = jnp.minimum(
        jnp.sum((seg_end[None, :] <= tile_id[:, None]).astype(jnp.int32), axis=1), N_EXPERTS - 1)
    n_tiles = seg_end[-1:].astype(jnp.int32)
    tail = n_tiles + jnp.arange(N_EXPERTS, dtype=jnp.int32)
    fill_tiles = jnp.concatenate([jnp.where(seg_tiles > 0, seg_end - 1, -1),
                                  jnp.where(tail < max_tiles, tail, -1)]).astype(jnp.int32)

    sorted_rows = _dispatch(x, g, pos, fill_tiles, max_tiles * tr, tm=min(512, t))
    y_sorted = _experts(sorted_rows, tile_expert, n_tiles, w_gate, w_up, w_down)
    return _combine(x, info, pos, y_sorted, g_final, tm=min(512, t))


def _layer(x, mem, attn_norm, w_in, w_up_dil, w_up_moba, w_branch_gate, w_out, cross_norm,
           mem_norm, w_q_mem, w_kv_mem, w_o_mem):
    b, s, d = x.shape
    t = b * s
    xt = x.reshape(t, d)
    h = _norm(xt, attn_norm, tm=min(512, t), name="attn_norm")
    proj, gates = _in_proj(h, w_in, w_branch_gate, tm=min(2048, t))
    proj = proj.reshape(b, s, IN_WIDTH)
    dil_slopes = _alibi_slopes(DIL_HEADS)
    dil = [_dilated_group(proj, dil_slopes, g, dilation)
           for g, (_, dilation) in enumerate(DIL_CONFIGS)]
    y_moba = _moba(proj, _alibi_slopes(MOBA_HEADS))
    mem_len = mem.shape[1]
    kv = _norm_matmul(mem.reshape(b * mem_len, d), mem_norm, w_kv_mem.astype(BF16),
                      tm=b * mem_len, tn=512, name="norm_mem_kv").reshape(b, mem_len, 2 * MEM_WIDTH)
    return _merge_cross(xt, [o for o, _ in dil], [l for _, l in dil], y_moba, gates,
                        w_up_dil.astype(BF16), w_up_moba.astype(BF16), w_out.astype(BF16),
                        cross_norm, w_q_mem.astype(BF16), kv, w_o_mem.astype(BF16), tm=min(256, s))


def kernel(x, mem, attn_norm, w_in, w_up_dil, w_up_moba, w_branch_gate, w_out, cross_norm, mem_norm,
           w_q_mem, w_kv_mem, w_o_mem, ffn_norm, w_router_group, b_router_group, w_router_expert,
           b_router_expert, w_expert_gate, w_expert_up, w_expert_down, final_norm):
    b, s, d = x.shape
    depth = attn_norm.shape[0]
    assert depth == 1, "the final norm is fused into the last layer's MoE call"
    l = 0
    x2 = _layer(x, mem, attn_norm[l], w_in[l], w_up_dil[l], w_up_moba[l], w_branch_gate[l], w_out[l],
                cross_norm[l], mem_norm[l], w_q_mem[l], w_kv_mem[l], w_o_mem[l])
    w_route = jnp.concatenate([w_router_group[l], w_router_expert[l]], axis=1)
    b_route = jnp.concatenate([b_router_group[l], b_router_expert[l]], axis=0)
    out = _moe(x2, ffn_norm[l], w_route, b_route, w_expert_gate[l], w_expert_up[l],
               w_expert_down[l], final_norm)
    return out.reshape(b, s, d)
```

```python
import functools

import numpy as np
import jax
import jax.numpy as jnp
from jax import lax
from jax.experimental import pallas as pl
from jax.experimental.pallas import tpu as pltpu

F32 = jnp.float32
BF16 = jnp.bfloat16

HEAD_DIM = 128
DIL_CONFIGS = ((128, 1), (512, 4), (2048, 16))
DIL_HEADS_PER_GROUP = 4
DIL_HEADS = DIL_HEADS_PER_GROUP * len(DIL_CONFIGS)
DIL_WIDTH = DIL_HEADS * HEAD_DIM
DIL_OUT = DIL_HEADS_PER_GROUP * HEAD_DIM
DIL_STEPS = 128
MOBA_HEADS = 8
MOBA_WIDTH = MOBA_HEADS * HEAD_DIM
MOBA_BLOCK = 256
MOBA_TOPK = 3
IN_WIDTH = 3 * (DIL_WIDTH + MOBA_WIDTH)
MEM_HEADS = 4
MEM_WIDTH = MEM_HEADS * HEAD_DIM
N_GROUPS = 4
EXPERTS_PER_GROUP = 8
N_EXPERTS = N_GROUPS * EXPERTS_PER_GROUP
MOE_TOPK = 2
RMS_EPS = 1e-6
SCALE = HEAD_DIM ** -0.5
NEG_INF = float("-inf")
LOG2E = 1.4426950408889634
MOBA_GROUP = 4
MOBA_HEADS_PER_STEP = 4
MOBA_Q_TILES = (9, 11)

VMEM_LIMIT_BYTES = 56 * 1024 * 1024
BF16_ROWS = 16
MXU_COLUMNS = 256
_NT = (((1,), (1,)), ((), ()))


def _alibi_slopes(n):
    return jnp.asarray(2.0 ** (-8.0 * np.arange(1, n + 1) / n), dtype=F32)


def _rms(x, g):
    return x * lax.rsqrt(jnp.mean(x * x, axis=-1, keepdims=True) + RMS_EPS) * g


def _dot_split3(a, b):
    a_hi = a.astype(BF16)
    b_hi = b.astype(BF16)
    a_lo = (a - a_hi.astype(F32)).astype(BF16)
    b_lo = (b - b_hi.astype(F32)).astype(BF16)
    dot = functools.partial(jnp.dot, preferred_element_type=F32)
    n = b.shape[1]
    if 2 * n <= MXU_COLUMNS:
        both = dot(a_hi, jnp.concatenate([b_hi, b_lo], axis=1))
        return both[:, :n] + (dot(a_lo, b_hi) + both[:, n:])
    return dot(a_hi, b_hi) + (dot(a_lo, b_hi) + dot(a_hi, b_lo))


def _params(*sem, flags=None):
    return pltpu.CompilerParams(dimension_semantics=sem, vmem_limit_bytes=VMEM_LIMIT_BYTES,
                                flags=flags)


def _resident(shape):
    nd = len(shape)
    return pl.BlockSpec(shape, lambda *_: (0,) * nd, pipeline_mode=pl.Buffered(1))


def _norm_matmul_body(x_ref, g_ref, w_ref, o_ref, h_ref, *, sigmoid):
    @pl.when(pl.program_id(1) == 0)
    def _():
        h_ref[...] = _rms(x_ref[...], g_ref[...]).astype(BF16)

    acc = jnp.dot(h_ref[...], w_ref[...], preferred_element_type=F32)
    if sigmoid:
        acc = jax.nn.sigmoid(acc)
    o_ref[...] = acc.astype(o_ref.dtype)


def _norm_matmul(x, g, w, *, tm, tn, sigmoid=False, name):
    m, d = x.shape
    n = w.shape[1]
    assert m % tm == 0 and n % tn == 0
    return pl.pallas_call(
        functools.partial(_norm_matmul_body, sigmoid=sigmoid),
        grid=(m // tm, n // tn),
        in_specs=[
            pl.BlockSpec((tm, d), lambda i, j: (i, 0)),
            pl.BlockSpec((1, d), lambda i, j: (0, 0)),
            pl.BlockSpec((d, tn), lambda i, j: (0, j)),
        ],
        out_specs=pl.BlockSpec((tm, tn), lambda i, j: (i, j)),
        out_shape=jax.ShapeDtypeStruct((m, n), BF16),
        scratch_shapes=[pltpu.VMEM((tm, d), BF16)],
        compiler_params=_params("parallel", "arbitrary"),
        name=name,
    )(x, g.reshape(1, d), w)


def _norm_body(x_ref, g_ref, o_ref):
    o_ref[...] = _rms(x_ref[...], g_ref[...]).astype(o_ref.dtype)


def _norm(x, g, *, tm, name):
    m, d = x.shape
    assert m % tm == 0
    return pl.pallas_call(
        _norm_body,
        grid=(m // tm,),
        in_specs=[pl.BlockSpec((tm, d), lambda i: (i, 0)), pl.BlockSpec((1, d), lambda i: (0, 0))],
        out_specs=pl.BlockSpec((tm, d), lambda i: (i, 0)),
        out_shape=jax.ShapeDtypeStruct((m, d), BF16),
        compiler_params=_params("parallel"),
        name=name,
    )(x, g.reshape(1, d))


def _in_proj_body(h_ref, win_ref, wbg_ref, proj_ref, gates_ref, perm_ref, *, n_proj_tiles):
    j = pl.program_id(1)
    tm = h_ref.shape[0]
    n = DIL_STEPS
    groups = len(DIL_CONFIGS)

    def store_regrouped(acc, dilation):
        tile = n * dilation
        for c in range(acc.shape[1] // HEAD_DIM):
            cols = slice(c * HEAD_DIM, (c + 1) * HEAD_DIM)
            perm_ref[c] = acc[:, cols]
            for t0 in range(0, tm, tile):
                for r in range(dilation):
                    rows = perm_ref[c, pl.ds(t0 + r, n, stride=dilation), :]
                    proj_ref[t0 + r * n:t0 + (r + 1) * n, cols] = rows.astype(BF16)

    @pl.when(j < n_proj_tiles)
    def _():
        acc = jnp.dot(h_ref[...], win_ref[...].astype(BF16), preferred_element_type=F32)
        group = jnp.where(j < 3 * groups, lax.rem(j, groups), 0)
        moba_q = (j >= MOBA_Q_TILES[0]) & (j < MOBA_Q_TILES[1])
        scale = jnp.where(moba_q, SCALE * LOG2E, 1.0)

        def store_natural():
            proj_ref[...] = (acc * scale).astype(BF16)

        for g, (_, dilation) in enumerate(DIL_CONFIGS):
            store = store_natural if dilation == 1 else functools.partial(store_regrouped, acc, dilation)
            pl.when(group == g)(store)

    @pl.when(j >= n_proj_tiles)
    def _():
        acc = jnp.dot(h_ref[...], wbg_ref[...].astype(BF16), preferred_element_type=F32)
        gates_ref[...] = jax.nn.sigmoid(acc).astype(BF16)


def _in_proj(h, w_in, w_bg, *, tm):
    t, d = h.shape
    tn = DIL_OUT
    assert MOBA_Q_TILES == (3 * DIL_WIDTH // tn, (3 * DIL_WIDTH + MOBA_WIDTH) // tn)
    assert t % tm == 0 and tm % (DIL_STEPS * max(dl for _, dl in DIL_CONFIGS)) == 0
    assert DIL_WIDTH == len(DIL_CONFIGS) * tn and w_in.shape[1] % tn == 0 and w_bg.shape[1] % tn == 0
    n_proj = w_in.shape[1] // tn
    n_gate = w_bg.shape[1] // tn
    return pl.pallas_call(
        functools.partial(_in_proj_body, n_proj_tiles=n_proj),
        grid=(t // tm, n_proj + n_gate),
        in_specs=[
            pl.BlockSpec((tm, d), lambda i, j: (i, 0)),
            pl.BlockSpec((d, tn), lambda i, j: (0, jnp.minimum(j, n_proj - 1))),
            pl.BlockSpec((d, tn), lambda i, j: (0, jnp.maximum(j - n_proj, 0))),
        ],
        out_specs=[
            pl.BlockSpec((tm, tn), lambda i, j: (i, jnp.minimum(j, n_proj - 1))),
            pl.BlockSpec((tm, tn), lambda i, j: (i, jnp.maximum(j - n_proj, 0))),
        ],
        out_shape=[jax.ShapeDtypeStruct((t, w_in.shape[1]), BF16),
                   jax.ShapeDtypeStruct((t, w_bg.shape[1]), BF16)],
        scratch_shapes=[pltpu.VMEM((tn // HEAD_DIM, tm, HEAD_DIM), F32)],
        compiler_params=_params("parallel", "arbitrary"),
        name="in_proj_gates",
    )(h, w_in, w_bg)


DIL_SUBBLOCKS = 4


def _dilated_body(slope_ref, q_ref, kp_ref, kc_ref, vp_ref, vc_ref, o_ref, lse_ref, *scratch,
                  dilation, group):
    n = DIL_STEPS
    first_tile = pl.program_id(1) == 0
    chunk = pl.program_id(2)
    chunks = max(dilation // DIL_SUBBLOCKS, 1)
    qi = lax.broadcasted_iota(jnp.int32, (n, n), 0)
    kj = lax.broadcasted_iota(jnp.int32, (n, n), 1)
    steps_cur = qi - kj
    valid_cur = steps_cur >= 0
    dist_cur = (steps_cur * dilation).astype(F32)
    dist_prev = ((steps_cur + n) * dilation).astype(F32)
    limit_across_tiles = jnp.where(first_tile, -n, 0)
    heads = range(DIL_HEADS_PER_GROUP)
    cols_of = [slice(h * HEAD_DIM, (h + 1) * HEAD_DIM) for h in heads]
    rows_of = [slice(s * n, (s + 1) * n) for s in range(DIL_SUBBLOCKS)]

    def prev_rows(s):
        if dilation == 1:
            return (slice(0, n), False, limit_across_tiles) if s == 0 else (rows_of[s - 1], True, 0)
        return rows_of[s], False, limit_across_tiles

    def scores(s):
        rows, from_cur, _ = prev_rows(s)
        out = []
        for h in heads:
            q = q_ref[rows_of[s], cols_of[h]]
            k_prev = (kc_ref if from_cur else kp_ref)[rows, cols_of[h]]
            out.append((lax.dot_general(q, kc_ref[rows_of[s], cols_of[h]], _NT,
                                        preferred_element_type=F32),
                        lax.dot_general(q, k_prev, _NT, preferred_element_type=F32)))
        return out

    def softmax(s, raw):
        valid_prev = steps_cur <= prev_rows(s)[2]
        out = []
        for h in heads:
            slope = slope_ref[group * DIL_HEADS_PER_GROUP + h]
            s_cur = jnp.where(valid_cur, raw[h][0] * SCALE - slope * dist_cur, NEG_INF)
            s_prev = jnp.where(valid_prev, raw[h][1] * SCALE - slope * dist_prev, NEG_INF)
            m = jnp.maximum(jnp.max(s_cur, axis=-1, keepdims=True),
                            jnp.max(s_prev, axis=-1, keepdims=True))
            p_cur = jnp.exp(s_cur - m)
            p_prev = jnp.exp(s_prev - m)
            den = jnp.sum(p_cur, axis=-1, keepdims=True) + jnp.sum(p_prev, axis=-1, keepdims=True)
            out.append((p_cur.astype(BF16), p_prev.astype(BF16), m, den))
        return out

    def values(s, probs):
        rows, from_cur, _ = prev_rows(s)
        for h in heads:
            p_cur, p_prev, m, den = probs[h]
            v_prev = (vc_ref if from_cur else vp_ref)[rows, cols_of[h]]
            o = (jnp.dot(p_cur, vc_ref[rows_of[s], cols_of[h]], preferred_element_type=F32)
                 + jnp.dot(p_prev, v_prev, preferred_element_type=F32)) / den
            lse = jnp.broadcast_to(m + jnp.log(den), (n, HEAD_DIM))
            if dilation == 1:
                dst_o, dst_lse, rows_out = o_ref, lse_ref, rows_of[s]
            elif chunks == 1:
                dst_o, dst_lse, rows_out = o_ref, lse_ref, pl.ds(s, n, stride=dilation)
            else:
                dst_o, dst_lse = scratch
                rows_out = pl.ds(pl.multiple_of((chunk * DIL_SUBBLOCKS + s) * n, n), n)
            dst_o[h, rows_out, :] = o
            dst_lse[h, rows_out, :] = lse

    raw, probs = {0: scores(0)}, {}
    for s in range(DIL_SUBBLOCKS + 1):
        if s + 1 < DIL_SUBBLOCKS:
            raw[s + 1] = scores(s + 1)
        if s < DIL_SUBBLOCKS:
            probs[s] = softmax(s, raw.pop(s))
        if s >= 1:
            values(s - 1, probs.pop(s - 1))

    if chunks > 1:
        o_tile, lse_tile = scratch

        @pl.when(chunk == chunks - 1)
        def _():
            for h in heads:
                for r in range(dilation):
                    natural = pl.ds(r, n, stride=dilation)
                    o_ref[h, natural, :] = o_tile[h, r * n:(r + 1) * n, :]
                    lse_ref[h, natural, :] = lse_tile[h, r * n:(r + 1) * n, :]


def _dilated_group(proj, slopes, group, dilation):
    b, s, _ = proj.shape
    n = DIL_STEPS
    step_rows = DIL_SUBBLOCKS * n
    assert dilation == 1 or dilation % DIL_SUBBLOCKS == 0
    chunks = max(dilation // DIL_SUBBLOCKS, 1)
    tile = step_rows * chunks
    assert s % tile == 0
    groups = len(DIL_CONFIGS)
    prev_rows = n if dilation == 1 else step_rows
    blocks_per_step = step_rows // prev_rows

    def cur(section):
        return pl.BlockSpec((None, step_rows, DIL_OUT),
                            lambda bi, i, c: (bi, i * chunks + c, section * groups + group))

    def prev(section):
        def index(bi, i, c):
            if dilation == 1:
                return (bi, jnp.maximum(i * blocks_per_step - 1, 0), section * groups + group)
            return (bi, jnp.maximum(i - 1, 0) * chunks + c, section * groups + group)
        return pl.BlockSpec((None, prev_rows, DIL_OUT), index)

    out_block = (DIL_HEADS_PER_GROUP, tile, HEAD_DIM)
    out_spec = pl.BlockSpec((None,) + out_block, lambda bi, i, c: (bi, 0, i, 0))
    out_sds = jax.ShapeDtypeStruct((b, DIL_HEADS_PER_GROUP, s, HEAD_DIM), F32)
    scratch = [pltpu.VMEM(out_block, F32)] * 2 if chunks > 1 else []
    o, lse = pl.pallas_call(
        functools.partial(_dilated_body, dilation=dilation, group=group),
        grid=(b, s // tile, chunks),
        in_specs=[pl.BlockSpec(memory_space=pltpu.SMEM),
                  cur(0), prev(1), cur(1), prev(2), cur(2)],
        out_specs=[out_spec, out_spec],
        out_shape=[out_sds, out_sds],
        scratch_shapes=scratch,
        compiler_params=_params("parallel", "arbitrary", "arbitrary"),
        name=f"dilated_attn_g{group}",
    )(slopes, proj, proj, proj, proj, proj)
    return o, lse


def _moba_body(slope_ref, q_ref, k_ref, v_ref, o_ref, kmean_ref, vt_ref, bias_ref, sel_ref,
               m_ref, acc_ref, ahead_ref, *, nblk, group):
    blk = MOBA_BLOCK
    hd = HEAD_DIM
    heads = q_ref.shape[1] // hd
    own = pl.program_id(2)
    key_off = lax.broadcasted_iota(jnp.int32, (blk, blk), 0)
    qry_off = lax.broadcasted_iota(jnp.int32, (blk, blk), 1)
    slope2 = [slope_ref[pl.program_id(1) * heads + h] * LOG2E for h in range(heads)]

    @pl.when(own == 0)
    def _():
        def fill(jb, carry):
            start = pl.multiple_of(jb * blk, blk)
            rows = k_ref[pl.ds(start, blk), :].astype(F32)
            kmean_ref[pl.ds(jb, 1), :] = jnp.mean(rows, axis=0, keepdims=True)
            vrows = v_ref[pl.ds(start, blk), :].astype(F32)
            for h in range(heads):
                vt_ref[h, :hd, pl.ds(start, blk)] = vrows[:, h * hd:(h + 1) * hd].T.astype(BF16)
            return carry
        lax.fori_loop(0, nblk, fill, 0)
        for h in range(heads):
            vt_ref[h, hd:, :] = jnp.ones((vt_ref.shape[1] - hd, vt_ref.shape[2]), BF16)
        for h in range(heads):
            bias_ref[h] = -slope2[h] * (qry_off - key_off).astype(F32)

    q = [q_ref[:, h * hd:(h + 1) * hd] for h in range(heads)]

    blk_id = lax.broadcasted_iota(jnp.int32, (nblk, blk), 0).astype(F32)
    gates = []
    for h in range(heads):
        gate = lax.dot_general(kmean_ref[:, h * hd:(h + 1) * hd], q[h].astype(F32), _NT,
                               precision=lax.Precision.HIGHEST, preferred_element_type=F32)
        gates.append(jnp.where(blk_id < own.astype(F32), gate, NEG_INF))

    def scores(h, start, rows):
        k = k_ref[pl.ds(start, rows), h * hd:(h + 1) * hd]
        return lax.dot_general(k, q[h], _NT, preferred_element_type=F32)

    own_start = pl.multiple_of(own * blk, blk)
    own_x = [scores(h, own_start, blk) for h in range(heads)]

    sels = [jnp.zeros((nblk, blk), F32) for _ in range(heads)]
    for _ in range(MOBA_TOPK):
        for h in range(heads):
            best = jnp.max(gates[h], axis=0, keepdims=True)
            is_best = (gates[h] == best) & (gates[h] > NEG_INF)
            pick = jnp.min(jnp.where(is_best, blk_id, float(nblk)), axis=0, keepdims=True)
            picked = blk_id == pick
            sels[h] = jnp.where(picked, 1.0, sels[h])
            gates[h] = jnp.where(picked, NEG_INF, gates[h])
    for h in range(heads):
        sel_ref[h] = sels[h]

    own_p = []
    for h in range(heads):
        x = jnp.where(qry_off >= key_off, own_x[h] + bias_ref[h], NEG_INF)
        m0 = jnp.max(x, axis=0, keepdims=True)
        own_p.append((m0, jnp.exp2(x - m0).astype(BF16)))
    for h in range(heads):
        m0, p = own_p[h]
        m_ref[h] = m0
        acc_ref[h] = jnp.dot(vt_ref[h, :, pl.ds(own_start, blk)], p, preferred_element_type=F32)

    def softmax_update(h, i, x, m):
        xs, chosen, shift = [], [], []
        m_new = m
        for g in range(group):
            j = i * group + g
            xs.append(x[g * blk:(g + 1) * blk] + bias_ref[h])
            chosen.append(sel_ref[h, pl.ds(j, 1), :] > 0.5)
            shift.append(-slope2[h] * ((own - j) * blk).astype(F32))
            top = jnp.max(xs[g], axis=0, keepdims=True) + shift[g]
            m_new = jnp.maximum(m_new, jnp.where(chosen[g], top, NEG_INF))
        ps = []
        for g in range(group):
            ref_g = jnp.where(chosen[g], m_new - shift[g], jnp.inf)
            ps.append(jnp.exp2(xs[g] - ref_g).astype(BF16))
        return m_new, jnp.exp2(m - m_new), jnp.concatenate(ps, axis=0)

    def past_blocks(i, carry):
        start = pl.multiple_of(i * (group * blk), group * blk)
        x, sm = {0: ahead_ref[...]}, {}
        for h in range(heads + 1):
            if h + 1 < heads:
                x[h + 1] = scores(h + 1, start, group * blk)
            elif h + 1 == heads:
                ahead_ref[...] = scores(0, group_start(i + 1), group * blk)
            if h < heads:
                sm[h] = softmax_update(h, i, x[h], m_ref[h])
            if h >= 1:
                m_new, alpha, p = sm[h - 1]
                m_ref[h - 1] = m_new
                acc_ref[h - 1] = alpha * acc_ref[h - 1] + jnp.dot(
                    vt_ref[h - 1, :, pl.ds(start, group * blk)], p, preferred_element_type=F32)
        return carry

    def group_start(i):
        return pl.multiple_of(jnp.minimum(i, nblk // group - 1) * (group * blk), group * blk)

    n_groups = lax.div(own + (group - 1), group)
    ahead_ref[...] = scores(0, group_start(0), group * blk)
    lax.fori_loop(0, n_groups, past_blocks, 0)
    for h in range(heads):
        acc = acc_ref[h]
        o_ref[:, h * hd:(h + 1) * hd] = (acc[:hd] / acc[hd:hd + 1]).T.astype(o_ref.dtype)


def _moba(proj, slopes):
    b, s, _ = proj.shape
    assert s % (MOBA_BLOCK * MOBA_GROUP) == 0
    nblk = s // MOBA_BLOCK
    hp = MOBA_HEADS_PER_STEP
    width = hp * HEAD_DIM
    q0 = 3 * DIL_WIDTH // width
    k0 = q0 + MOBA_WIDTH // width
    v0 = k0 + MOBA_WIDTH // width
    y = pl.pallas_call(
        functools.partial(_moba_body, nblk=nblk, group=MOBA_GROUP),
        grid=(b, MOBA_HEADS // hp, nblk),
        in_specs=[
            pl.BlockSpec(memory_space=pltpu.SMEM),
            pl.BlockSpec((None, MOBA_BLOCK, width), lambda bi, h, i: (bi, i, q0 + h)),
            pl.BlockSpec((None, s, width), lambda bi, h, i: (bi, 0, k0 + h)),
            pl.BlockSpec((None, s, width), lambda bi, h, i: (bi, 0, v0 + h)),
        ],
        out_specs=pl.BlockSpec((None, MOBA_BLOCK, width), lambda bi, h, i: (bi, i, h)),
        out_shape=jax.ShapeDtypeStruct((b, s, MOBA_WIDTH), BF16),
        scratch_shapes=[pltpu.VMEM((nblk, width), F32),
                        pltpu.VMEM((hp, HEAD_DIM + BF16_ROWS, s), BF16),
                        pltpu.VMEM((hp, MOBA_BLOCK, MOBA_BLOCK), F32),
                        pltpu.VMEM((hp, nblk, MOBA_BLOCK), F32),
                        pltpu.VMEM((hp, 1, MOBA_BLOCK), F32),
                        pltpu.VMEM((hp, HEAD_DIM + BF16_ROWS, MOBA_BLOCK), F32),
                        pltpu.VMEM((MOBA_GROUP * MOBA_BLOCK, MOBA_BLOCK), F32)],
        compiler_params=_params("parallel", "parallel", "arbitrary"),
        name="moba_attn",
    )(slopes, proj, proj, proj)
    return y.reshape(b * s, MOBA_WIDTH)


def _merged_branches(x, o_refs, l_refs, ym_ref, gd_ref, gm_ref, wud_ref, wum_ref, wo_ref):
    heads = []
    for h in range(DIL_HEADS_PER_GROUP):
        l0, l1, l2 = (l_ref[h] for l_ref in l_refs)
        m = jnp.maximum(jnp.maximum(l0, l1), l2)
        e0, e1, e2 = jnp.exp(l0 - m), jnp.exp(l1 - m), jnp.exp(l2 - m)
        den = e0 + e1 + e2
        mixed = ((e0 / den) * o_refs[0][h] + (e1 / den) * o_refs[1][h] + (e2 / den) * o_refs[2][h])
        heads.append(mixed.astype(BF16))
    y_dil = jnp.concatenate(heads, axis=-1)
    lift_dil = jnp.dot(y_dil, wud_ref[...], preferred_element_type=F32)
    lift_moba = jnp.dot(ym_ref[...], wum_ref[...], preferred_element_type=F32)
    merged = gd_ref[...].astype(F32) * lift_dil + gm_ref[...].astype(F32) * lift_moba
    return x + jnp.dot(merged.astype(BF16), wo_ref[...], preferred_element_type=F32)


def _cross_attended(x, g_ref, wq_ref, kv_ref, wo_ref):
    h = _rms(x, g_ref[...]).astype(BF16)
    q = jnp.dot(h, wq_ref[...], preferred_element_type=F32).astype(BF16)
    heads = []
    for hd in range(MEM_HEADS):
        k = kv_ref[:, hd * HEAD_DIM:(hd + 1) * HEAD_DIM]
        v = kv_ref[:, MEM_WIDTH + hd * HEAD_DIM:MEM_WIDTH + (hd + 1) * HEAD_DIM]
        s = lax.dot_general(q[:, hd * HEAD_DIM:(hd + 1) * HEAD_DIM], k, _NT,
                            preferred_element_type=F32) * SCALE
        p = jnp.exp(s - jnp.max(s, axis=-1, keepdims=True))
        den = jnp.sum(p, axis=-1, keepdims=True)
        heads.append((jnp.dot(p.astype(BF16), v, preferred_element_type=F32) / den).astype(BF16))
    o = jnp.concatenate(heads, axis=-1)
    return x + jnp.dot(o, wo_ref[...], preferred_element_type=F32)


def _merge_cross_body(x_ref, o0_ref, o1_ref, o2_ref, l0_ref, l1_ref, l2_ref, ym_ref, gd_ref, gm_ref,
                      wud_ref, wum_ref, wo_ref, gc_ref, wq_ref, kv_ref, wom_ref, out_ref):
    x1 = _merged_branches(x_ref[...], (o0_ref, o1_ref, o2_ref), (l0_ref, l1_ref, l2_ref),
                          ym_ref, gd_ref, gm_ref, wud_ref, wum_ref, wo_ref)
    out_ref[...] = _cross_attended(x1, gc_ref, wq_ref, kv_ref, wom_ref)


def _merge_cross(x, dil_outs, dil_lses, y_moba, gates, w_up_dil, w_up_moba, w_out,
                 g_cross, w_q, kv, w_o, *, tm):
    t, d = x.shape
    s = dil_outs[0].shape[2]
    mem_len = kv.shape[1]
    assert t % tm == 0 and s % tm == 0
    per_batch = s // tm
    row = lambda w: pl.BlockSpec((tm, w), lambda i: (i, 0))
    dil = pl.BlockSpec((None, DIL_HEADS_PER_GROUP, tm, HEAD_DIM),
                       lambda i: (i // per_batch, 0, i % per_batch, 0))
    return pl.pallas_call(
        _merge_cross_body,
        grid=(t // tm,),
        in_specs=[row(d)] + [dil] * 6 + [
            row(MOBA_WIDTH),
            pl.BlockSpec((tm, d), lambda i: (i, 0)),
            pl.BlockSpec((tm, d), lambda i: (i, 1)),
            _resident((DIL_OUT, d)), _resident((MOBA_WIDTH, d)), _resident((d, d)),
            _resident((1, d)), _resident((d, MEM_WIDTH)),
            pl.BlockSpec((None, mem_len, 2 * MEM_WIDTH), lambda i: (i // per_batch, 0, 0)),
            _resident((MEM_WIDTH, d)),
        ],
        out_specs=row(d),
        out_shape=jax.ShapeDtypeStruct((t, d), F32),
        compiler_params=_params("parallel"),
        name="merge_out_proj_cross_attn",
    )(x, *dil_outs, *dil_lses, y_moba, gates, gates, w_up_dil, w_up_moba, w_out,
      g_cross.reshape(1, d), w_q, kv, w_o)


ROUTE_COLS = 8
MOE_ROW_TILE = 256


def _route_body(x_ref, g_ref, wr_ref, br_ref, info_ref, counts_ref, run_ref, tri_ref):
    tm = x_ref.shape[0]
    n_route = N_GROUPS + N_EXPERTS
    lane = lax.broadcasted_iota(jnp.int32, (tm, n_route), 1).astype(F32)

    @pl.when(pl.program_id(0) == 0)
    def _():
        run_ref[...] = jnp.zeros_like(run_ref)
        earlier = (lax.broadcasted_iota(jnp.int32, (tm, tm), 0)
                   > lax.broadcasted_iota(jnp.int32, (tm, tm), 1))
        tri_ref[...] = jnp.where(earlier, 1.0, 0.0).astype(BF16)

    t = _rms(x_ref[...], g_ref[...])
    logits = _dot_split3(t, wr_ref[...]) + br_ref[...]
    none = float(n_route)
    glog = jnp.where(lane < N_GROUPS, logits, NEG_INF)
    gmax = jnp.max(glog, axis=-1, keepdims=True)
    gsel = jnp.min(jnp.where(glog == gmax, lane, none), axis=-1, keepdims=True)
    pg = 1.0 / jnp.sum(jnp.exp(glog - gmax), axis=-1, keepdims=True)
    first = N_GROUPS + gsel * EXPERTS_PER_GROUP
    in_group = (lane >= first) & (lane < first + EXPERTS_PER_GROUP)
    elog = jnp.where(in_group, logits, NEG_INF)
    top1 = jnp.max(elog, axis=-1, keepdims=True)
    i1 = jnp.min(jnp.where(elog == top1, lane, none), axis=-1, keepdims=True)
    rest = jnp.where(lane == i1, NEG_INF, elog)
    top2 = jnp.max(rest, axis=-1, keepdims=True)
    i2 = jnp.min(jnp.where(rest == top2, lane, none), axis=-1, keepdims=True)
    e2 = jnp.exp(top2 - top1)
    w1 = pg / (1.0 + e2)
    w2 = pg * e2 / (1.0 + e2)

    hit1 = lane == i1
    hit2 = lane == i2
    assigned = jnp.where(hit1 | hit2, 1.0, 0.0)
    before = jnp.dot(tri_ref[...], assigned.astype(BF16), preferred_element_type=F32) + run_ref[...]
    rank1 = jnp.sum(jnp.where(hit1, before, 0.0), axis=-1, keepdims=True)
    rank2 = jnp.sum(jnp.where(hit2, before, 0.0), axis=-1, keepdims=True)
    run_ref[...] += jnp.sum(assigned, axis=0, keepdims=True)
    counts_ref[...] = run_ref[...]

    col = lax.broadcasted_iota(jnp.int32, (tm, ROUTE_COLS), 1)
    fields = (i1 - N_GROUPS, i2 - N_GROUPS, rank1, rank2, w1, w2)
    info = jnp.zeros((tm, ROUTE_COLS), F32)
    for c, field in enumerate(fields):
        info = jnp.where(col == c, field, info)
    info_ref[...] = info


def _route(x, g, w_route, b_route, *, tm):
    t, d = x.shape
    n_route = N_GROUPS + N_EXPERTS
    assert t % tm == 0
    return pl.pallas_call(
        _route_body,
        grid=(t // tm,),
        in_specs=[
            pl.BlockSpec((tm, d), lambda i: (i, 0)),
            pl.BlockSpec((1, d), lambda i: (0, 0)),
            pl.BlockSpec((d, n_route), lambda i: (0, 0)),
            pl.BlockSpec((1, n_route), lambda i: (0, 0)),
        ],
        out_specs=[pl.BlockSpec((tm, ROUTE_COLS), lambda i: (i, 0)),
                   pl.BlockSpec((1, n_route), lambda i: (0, 0))],
        out_shape=[jax.ShapeDtypeStruct((t, ROUTE_COLS), F32),
                   jax.ShapeDtypeStruct((1, n_route), F32)],
        scratch_shapes=[pltpu.VMEM((1, n_route), F32), pltpu.VMEM((tm, tm), BF16)],
        compiler_params=_params("arbitrary"),
        name="moe_route",
    )(x, g.reshape(1, d), w_route, b_route.reshape(1, n_route))


def _row_copies_wait(src_rows, dst_rows, sem):
    pltpu.make_async_copy(src_rows, dst_rows, sem).wait()


def _dispatch_body(pos_ref, fill_ref, x_ref, g_ref, sorted_ref, t_ref, zero_ref, sem, fill_sem):
    tm = x_ref.shape[0]
    tr = zero_ref.shape[0]
    base = pl.program_id(0) * tm

    @pl.when(pl.program_id(0) == 0)
    def _():
        zero_ref[...] = jnp.zeros_like(zero_ref)

        def fill_copy(tile):
            return pltpu.make_async_copy(zero_ref, sorted_ref.at[pl.ds(tile * tr, tr)], fill_sem)

        def start(idx, carry):
            pl.when(fill_ref[idx] >= 0)(lambda: fill_copy(fill_ref[idx]).start())
            return carry

        def finish(idx, carry):
            pl.when(fill_ref[idx] >= 0)(lambda: fill_copy(fill_ref[idx]).wait())
            return carry

        lax.fori_loop(0, fill_ref.shape[0], start, 0)
        lax.fori_loop(0, fill_ref.shape[0], finish, 0)

    t_ref[...] = _rms(x_ref[...], g_ref[...])

    def issue(r, carry):
        for k in range(MOE_TOPK):
            dst = pos_ref[k, base + r]
            pltpu.make_async_copy(t_ref.at[pl.ds(r, 1)], sorted_ref.at[pl.ds(dst, 1)], sem).start()
        return carry

    lax.fori_loop(0, tm, issue, 0, unroll=8)
    for _ in range(MOE_TOPK):
        _row_copies_wait(t_ref, sorted_ref.at[pl.ds(0, tm)], sem)


def _dispatch(x, g, pos, fill_tiles, n_rows, *, tm):
    t, d = x.shape
    assert t % tm == 0
    return pl.pallas_call(
        _dispatch_body,
        grid_spec=pltpu.PrefetchScalarGridSpec(
            num_scalar_prefetch=2,
            grid=(t // tm,),
            in_specs=[pl.BlockSpec((tm, d), lambda i, pos, fill: (i, 0)),
                      pl.BlockSpec((1, d), lambda i, pos, fill: (0, 0))],
            out_specs=pl.BlockSpec(memory_space=pl.ANY),
            scratch_shapes=[pltpu.VMEM((tm, d), F32), pltpu.VMEM((MOE_ROW_TILE, d), F32),
                            pltpu.SemaphoreType.DMA(()), pltpu.SemaphoreType.DMA(())],
        ),
        out_shape=jax.ShapeDtypeStruct((n_rows, d), F32),
        compiler_params=pltpu.CompilerParams(dimension_semantics=("arbitrary",),
                                             vmem_limit_bytes=VMEM_LIMIT_BYTES,
                                             disable_bounds_checks=True),
        name="moe_dispatch",
    )(pos, fill_tiles, x, g.reshape(1, d))


def _experts_body(tile_expert_ref, n_tiles_ref, x_ref, wg_ref, wu_ref, wd_ref, y_ref):
    del tile_expert_ref
    in_use = pl.program_id(0) < n_tiles_ref[0]

    @pl.when(in_use)
    def _():
        t = x_ref[...].astype(BF16)
        gate = jnp.dot(t, wg_ref[...].astype(BF16), preferred_element_type=F32)
        up = jnp.dot(t, wu_ref[...].astype(BF16), preferred_element_type=F32)
        a = jax.nn.silu(gate) * up
        y_ref[...] = jnp.dot(a.astype(BF16), wd_ref[...].astype(BF16), preferred_element_type=F32)

    @pl.when(jnp.logical_not(in_use))
    def _():
        y_ref[...] = jnp.zeros_like(y_ref)


def _experts(sorted_rows, tile_expert, n_tiles, w_gate, w_up, w_down):
    p, d = sorted_rows.shape
    ff = w_gate.shape[-1]
    tr = MOE_ROW_TILE
    assert p % tr == 0
    used = lambda i, nt: jnp.minimum(i, nt[0] - 1)
    return pl.pallas_call(
        _experts_body,
        grid_spec=pltpu.PrefetchScalarGridSpec(
            num_scalar_prefetch=2,
            grid=(p // tr,),
            in_specs=[pl.BlockSpec((tr, d), lambda i, te, nt: (used(i, nt), 0)),
                      pl.BlockSpec((None, d, ff), lambda i, te, nt: (te[used(i, nt)], 0, 0)),
                      pl.BlockSpec((None, d, ff), lambda i, te, nt: (te[used(i, nt)], 0, 0)),
                      pl.BlockSpec((None, ff, d), lambda i, te, nt: (te[used(i, nt)], 0, 0))],
            out_specs=pl.BlockSpec((tr, d), lambda i, te, nt: (i, 0)),
        ),
        out_shape=jax.ShapeDtypeStruct((p, d), F32),
        compiler_params=_params("arbitrary"),
        name="moe_experts",
    )(tile_expert, n_tiles, sorted_rows, w_gate, w_up, w_down)


def _combine_body(pos_ref, x_ref, info_ref, gf_ref, y_sorted_ref, out_ref, rows_ref, sem):
    tm = x_ref.shape[0]
    base = pl.program_id(0) * tm

    def issue(r, carry):
        for k in range(MOE_TOPK):
            src = pos_ref[k, base + r]
            pltpu.make_async_copy(y_sorted_ref.at[pl.ds(src, 1)], rows_ref.at[k, pl.ds(r, 1)],
                                  sem).start()
        return carry

    lax.fori_loop(0, tm, issue, 0, unroll=8)
    for k in range(MOE_TOPK):
        _row_copies_wait(y_sorted_ref.at[pl.ds(0, tm)], rows_ref.at[k], sem)

    info = info_ref[...]
    y = info[:, 4:5] * rows_ref[0] + info[:, 5:6] * rows_ref[1]
    out_ref[...] = _rms(x_ref[...] + y, gf_ref[...])


def _combine(x, info, pos, y_sorted, g_final, *, tm):
    t, d = x.shape
    assert t % tm == 0
    return pl.pallas_call(
        _combine_body,
        grid_spec=pltpu.PrefetchScalarGridSpec(
            num_scalar_prefetch=1,
            grid=(t // tm,),
            in_specs=[pl.BlockSpec((tm, d), lambda i, pos: (i, 0)),
                      pl.BlockSpec((tm, ROUTE_COLS), lambda i, pos: (i, 0)),
                      pl.BlockSpec((1, d), lambda i, pos: (0, 0)),
                      pl.BlockSpec(memory_space=pl.ANY)],
            out_specs=pl.BlockSpec((tm, d), lambda i, pos: (i, 0)),
            scratch_shapes=[pltpu.VMEM((MOE_TOPK, tm, d), F32), pltpu.SemaphoreType.DMA(())],
        ),
        out_shape=jax.ShapeDtypeStruct((t, d), F32),
        compiler_params=pltpu.CompilerParams(dimension_semantics=("arbitrary",),
                                             vmem_limit_bytes=VMEM_LIMIT_BYTES,
                                             disable_bounds_checks=True),
        name="moe_combine_final_norm",
    )(pos, x, info, g_final.reshape(1, d), y_sorted)


def _moe(x, g, w_route, b_route, w_gate, w_up, w_down, g_final):
    t, d = x.shape
    tr = MOE_ROW_TILE
    assert (MOE_TOPK * t) % tr == 0
    info, counts = _route(x, g, w_route, b_route, tm=min(512, t))

    expert = info[:, 0:MOE_TOPK].astype(jnp.int32)
    rank = info[:, MOE_TOPK:2 * MOE_TOPK].astype(jnp.int32)
    count = counts[0, N_GROUPS:].astype(jnp.int32)
    seg_tiles = (count + (tr - 1)) // tr
    seg_end = jnp.cumsum(seg_tiles)
    seg_start_row = (seg_end - seg_tiles) * tr
    pos = (seg_start_row[expert] + rank).T
    max_tiles = (MOE_TOPK * t) // tr + N_EXPERTS
    tile_id = jnp.arange(max_tiles, dtype=jnp.int32)
    tile_expert = jnp.minimum(
        jnp.sum((seg_end[None, :] <= tile_id[:, None]).astype(jnp.int32), axis=1), N_EXPERTS - 1)
    n_tiles = seg_end[-1:].astype(jnp.int32)
    tail = n_tiles + jnp.arange(N_EXPERTS, dtype=jnp.int32)
    fill_tiles = jnp.concatenate([jnp.where(seg_tiles > 0, seg_end - 1, -1),
                                  jnp.where(tail < max_tiles, tail, -1)]).astype(jnp.int32)

    sorted_rows = _dispatch(x, g, pos, fill_tiles, max_tiles * tr, tm=min(512, t))
    y_sorted = _experts(sorted_rows, tile_expert, n_tiles, w_gate, w_up, w_down)
    return _combine(x, info, pos, y_sorted, g_final, tm=min(512, t))


def _layer(x, mem, attn_norm, w_in, w_up_dil, w_up_moba, w_branch_gate, w_out, cross_norm,
           mem_norm, w_q_mem, w_kv_mem, w_o_mem):
    b, s, d = x.shape
    t = b * s
    xt = x.reshape(t, d)
    h = _norm(xt, attn_norm, tm=min(512, t), name="attn_norm")
    proj, gates = _in_proj(h, w_in, w_branch_gate, tm=min(2048, t))
    proj = proj.reshape(b, s, IN_WIDTH)
    dil_slopes = _alibi_slopes(DIL_HEADS)
    dil = [_dilated_group(proj, dil_slopes, g, dilation)
           for g, (_, dilation) in enumerate(DIL_CONFIGS)]
    y_moba = _moba(proj, _alibi_slopes(MOBA_HEADS))
    mem_len = mem.shape[1]
    kv = _norm_matmul(mem.reshape(b * mem_len, d), mem_norm, w_kv_mem.astype(BF16),
                      tm=b * mem_len, tn=512, name="norm_mem_kv").reshape(b, mem_len, 2 * MEM_WIDTH)
    return _merge_cross(xt, [o for o, _ in dil], [l for _, l in dil], y_moba, gates,
                        w_up_dil.astype(BF16), w_up_moba.astype(BF16), w_out.astype(BF16),
                        cross_norm, w_q_mem.astype(BF16), kv, w_o_mem.astype(BF16), tm=min(256, s))


def kernel(x, mem, attn_norm, w_in, w_up_dil, w_up_moba, w_branch_gate, w_out, cross_norm, mem_norm,
           w_q_mem, w_kv_mem, w_o_mem, ffn_norm, w_router_group, b_router_group, w_router_expert,
           b_router_expert, w_expert_gate, w_expert_up, w_expert_down, final_norm):
    b, s, d = x.shape
    depth = attn_norm.shape[0]
    assert depth == 1, "the final norm is fused into the last layer's MoE call"
    l = 0
    x2 = _layer(x, mem, attn_norm[l], w_in[l], w_up_dil[l], w_up_moba[l], w_branch_gate[l], w_out[l],
                cross_norm[l], mem_norm[l], w_q_mem[l], w_kv_mem[l], w_o_mem[l])
    w_route = jnp.concatenate([w_router_group[l], w_router_expert[l]], axis=1)
    b_route = jnp.concatenate([b_router_group[l], b_router_expert[l]], axis=0)
    out = _moe(x2, ffn_norm[l], w_route, b_route, w_expert_gate[l], w_expert_up[l],
               w_expert_down[l], final_norm)
    return out.reshape(b, s, d)
```

```python
import functools

import numpy as np
import jax
import jax.numpy as jnp
from jax import lax
from jax.experimental import pallas as pl
from jax.experimental.pallas import tpu as pltpu

F32 = jnp.float32
BF16 = jnp.bfloat16

HEAD_DIM = 128
DIL_CONFIGS = ((128, 1), (512, 4), (2048, 16))
DIL_HEADS_PER_GROUP = 4
DIL_HEADS = DIL_HEADS_PER_GROUP * len(DIL_CONFIGS)
DIL_WIDTH = DIL_HEADS * HEAD_DIM
DIL_OUT = DIL_HEADS_PER_GROUP * HEAD_DIM
DIL_STEPS = 128
MOBA_HEADS = 8
MOBA_WIDTH = MOBA_HEADS * HEAD_DIM
MOBA_BLOCK = 256
MOBA_TOPK = 3
IN_WIDTH = 3 * (DIL_WIDTH + MOBA_WIDTH)
MEM_HEADS = 4
MEM_WIDTH = MEM_HEADS * HEAD_DIM
N_GROUPS = 4
EXPERTS_PER_GROUP = 8
N_EXPERTS = N_GROUPS * EXPERTS_PER_GROUP
MOE_TOPK = 2
RMS_EPS = 1e-6
SCALE = HEAD_DIM ** -0.5
NEG_INF = float("-inf")
LOG2E = 1.4426950408889634
MOBA_GROUP = 4
MOBA_HEADS_PER_STEP = 4
MOBA_Q_TILES = (9, 11)

VMEM_LIMIT_BYTES = 56 * 1024 * 1024
BF16_ROWS = 16
MXU_COLUMNS = 256
_NT = (((1,), (1,)), ((), ()))


def _alibi_slopes(n):
    return jnp.asarray(2.0 ** (-8.0 * np.arange(1, n + 1) / n), dtype=F32)


def _rms(x, g):
    return x * lax.rsqrt(jnp.mean(x * x, axis=-1, keepdims=True) + RMS_EPS) * g


def _dot_split3(a, b):
    a_hi = a.astype(BF16)
    b_hi = b.astype(BF16)
    a_lo = (a - a_hi.astype(F32)).astype(BF16)
    b_lo = (b - b_hi.astype(F32)).astype(BF16)
    dot = functools.partial(jnp.dot, preferred_element_type=F32)
    n = b.shape[1]
    if 2 * n <= MXU_COLUMNS:
        both = dot(a_hi, jnp.concatenate([b_hi, b_lo], axis=1))
        return both[:, :n] + (dot(a_lo, b_hi) + both[:, n:])
    return dot(a_hi, b_hi) + (dot(a_lo, b_hi) + dot(a_hi, b_lo))


def _params(*sem, flags=None):
    return pltpu.CompilerParams(dimension_semantics=sem, vmem_limit_bytes=VMEM_LIMIT_BYTES,
                                flags=flags)


def _resident(shape):
    nd = len(shape)
    return pl.BlockSpec(shape, lambda *_: (0,) * nd, pipeline_mode=pl.Buffered(1))


def _norm_matmul_body(x_ref, g_ref, w_ref, o_ref, h_ref, *, sigmoid):
    @pl.when(pl.program_id(1) == 0)
    def _():
        h_ref[...] = _rms(x_ref[...], g_ref[...]).astype(BF16)

    acc = jnp.dot(h_ref[...], w_ref[...], preferred_element_type=F32)
    if sigmoid:
        acc = jax.nn.sigmoid(acc)
    o_ref[...] = acc.astype(o_ref.dtype)


def _norm_matmul(x, g, w, *, tm, tn, sigmoid=False, name):
    m, d = x.shape
    n = w.shape[1]
    assert m % tm == 0 and n % tn == 0
    return pl.pallas_call(
        functools.partial(_norm_matmul_body, sigmoid=sigmoid),
        grid=(m // tm, n // tn),
        in_specs=[
            pl.BlockSpec((tm, d), lambda i, j: (i, 0)),
            pl.BlockSpec((1, d), lambda i, j: (0, 0)),
            pl.BlockSpec((d, tn), lambda i, j: (0, j)),
        ],
        out_specs=pl.BlockSpec((tm, tn), lambda i, j: (i, j)),
        out_shape=jax.ShapeDtypeStruct((m, n), BF16),
        scratch_shapes=[pltpu.VMEM((tm, d), BF16)],
        compiler_params=_params("parallel", "arbitrary"),
        name=name,
    )(x, g.reshape(1, d), w)


def _norm_body(x_ref, g_ref, o_ref):
    o_ref[...] = _rms(x_ref[...], g_ref[...]).astype(o_ref.dtype)


def _norm(x, g, *, tm, name):
    m, d = x.shape
    assert m % tm == 0
    return pl.pallas_call(
        _norm_body,
        grid=(m // tm,),
        in_specs=[pl.BlockSpec((tm, d), lambda i: (i, 0)), pl.BlockSpec((1, d), lambda i: (0, 0))],
        out_specs=pl.BlockSpec((tm, d), lambda i: (i, 0)),
        out_shape=jax.ShapeDtypeStruct((m, d), BF16),
        compiler_params=_params("parallel"),
        name=name,
    )(x, g.reshape(1, d))


def _in_proj_body(h_ref, win_ref, wbg_ref, proj_ref, gates_ref, perm_ref, *, n_proj_tiles):
    j = pl.program_id(1)
    tm = h_ref.shape[0]
    n = DIL_STEPS
    groups = len(DIL_CONFIGS)

    def store_regrouped(acc, dilation):
        tile = n * dilation
        for c in range(acc.shape[1] // HEAD_DIM):
            cols = slice(c * HEAD_DIM, (c + 1) * HEAD_DIM)
            perm_ref[c] = acc[:, cols]
            for t0 in range(0, tm, tile):
                for r in range(dilation):
                    rows = perm_ref[c, pl.ds(t0 + r, n, stride=dilation), :]
                    proj_ref[t0 + r * n:t0 + (r + 1) * n, cols] = rows.astype(BF16)

    @pl.when(j < n_proj_tiles)
    def _():
        acc = jnp.dot(h_ref[...], win_ref[...].astype(BF16), preferred_element_type=F32)
        group = jnp.where(j < 3 * groups, lax.rem(j, groups), 0)
        moba_q = (j >= MOBA_Q_TILES[0]) & (j < MOBA_Q_TILES[1])
        scale = jnp.where(moba_q, SCALE * LOG2E, 1.0)

        def store_natural():
            proj_ref[...] = (acc * scale).astype(BF16)

        for g, (_, dilation) in enumerate(DIL_CONFIGS):
            store = store_natural if dilation == 1 else functools.partial(store_regrouped, acc, dilation)
            pl.when(group == g)(store)

    @pl.when(j >= n_proj_tiles)
    def _():
        acc = jnp.dot(h_ref[...], wbg_ref[...].astype(BF16), preferred_element_type=F32)
        gates_ref[...] = jax.nn.sigmoid(acc).astype(BF16)


def _in_proj(h, w_in, w_bg, *, tm):
    t, d = h.shape
    tn = DIL_OUT
    assert MOBA_Q_TILES == (3 * DIL_WIDTH // tn, (3 * DIL_WIDTH + MOBA_WIDTH) // tn)
    assert t % tm == 0 and tm % (DIL_STEPS * max(dl for _, dl in DIL_CONFIGS)) == 0
    assert DIL_WIDTH == len(DIL_CONFIGS) * tn and w_in.shape[1] % tn == 0 and w_bg.shape[1] % tn == 0
    n_proj = w_in.shape[1] // tn
    n_gate = w_bg.shape[1] // tn
    return pl.pallas_call(
        functools.partial(_in_proj_body, n_proj_tiles=n_proj),
        grid=(t // tm, n_proj + n_gate),
        in_specs=[
            pl.BlockSpec((tm, d), lambda i, j: (i, 0)),
            pl.BlockSpec((d, tn), lambda i, j: (0, jnp.minimum(j, n_proj - 1))),
            pl.BlockSpec((d, tn), lambda i, j: (0, jnp.maximum(j - n_proj, 0))),
        ],
        out_specs=[
            pl.BlockSpec((tm, tn), lambda i, j: (i, jnp.minimum(j, n_proj - 1))),
            pl.BlockSpec((tm, tn), lambda i, j: (i, jnp.maximum(j - n_proj, 0))),
        ],
        out_shape=[jax.ShapeDtypeStruct((t, w_in.shape[1]), BF16),
                   jax.ShapeDtypeStruct((t, w_bg.shape[1]), BF16)],
        scratch_shapes=[pltpu.VMEM((tn // HEAD_DIM, tm, HEAD_DIM), F32)],
        compiler_params=_params("parallel", "arbitrary"),
        name="in_proj_gates",
    )(h, w_in, w_bg)


DIL_SUBBLOCKS = 4


def _dilated_body(slope_ref, q_ref, kp_ref, kc_ref, vp_ref, vc_ref, o_ref, lse_ref, *scratch,
                  dilation, group):
    n = DIL_STEPS
    first_tile = pl.program_id(1) == 0
    chunk = pl.program_id(2)
    chunks = max(dilation // DIL_SUBBLOCKS, 1)
    qi = lax.broadcasted_iota(jnp.int32, (n, n), 0)
    kj = lax.broadcasted_iota(jnp.int32, (n, n), 1)
    steps_cur = qi - kj
    valid_cur = steps_cur >= 0
    dist_cur = (steps_cur * dilation).astype(F32)
    dist_prev = ((steps_cur + n) * dilation).astype(F32)
    limit_across_tiles = jnp.where(first_tile, -n, 0)
    heads = range(DIL_HEADS_PER_GROUP)
    cols_of = [slice(h * HEAD_DIM, (h + 1) * HEAD_DIM) for h in heads]
    rows_of = [slice(s * n, (s + 1) * n) for s in range(DIL_SUBBLOCKS)]

    def prev_rows(s):
        if dilation == 1:
            return (slice(0, n), False, limit_across_tiles) if s == 0 else (rows_of[s - 1], True, 0)
        return rows_of[s], False, limit_across_tiles

    def scores(s):
        rows, from_cur, _ = prev_rows(s)
        out = []
        for h in heads:
            q = q_ref[rows_of[s], cols_of[h]]
            k_prev = (kc_ref if from_cur else kp_ref)[rows, cols_of[h]]
            out.append((lax.dot_general(q, kc_ref[rows_of[s], cols_of[h]], _NT,
                                        preferred_element_type=F32),
                        lax.dot_general(q, k_prev, _NT, preferred_element_type=F32)))
        return out

    def softmax(s, raw):
        valid_prev = steps_cur <= prev_rows(s)[2]
        out = []
        for h in heads:
            slope = slope_ref[group * DIL_HEADS_PER_GROUP + h]
            s_cur = jnp.where(valid_cur, raw[h][0] * SCALE - slope * dist_cur, NEG_INF)
            s_prev = jnp.where(valid_prev, raw[h][1] * SCALE - slope * dist_prev, NEG_INF)
            m = jnp.maximum(jnp.max(s_cur, axis=-1, keepdims=True),
                            jnp.max(s_prev, axis=-1, keepdims=True))
            p_cur = jnp.exp(s_cur - m)
            p_prev = jnp.exp(s_prev - m)
            den = jnp.sum(p_cur, axis=-1, keepdims=True) + jnp.sum(p_prev, axis=-1, keepdims=True)
            out.append((p_cur.astype(BF16), p_prev.astype(BF16), m, den))
        return out

    def values(s, probs):
        rows, from_cur, _ = prev_rows(s)
        for h in heads:
            p_cur, p_prev, m, den = probs[h]
            v_prev = (vc_ref if from_cur else vp_ref)[rows, cols_of[h]]
            o = (jnp.dot(p_cur, vc_ref[rows_of[s], cols_of[h]], preferred_element_type=F32)
                 + jnp.dot(p_prev, v_prev, preferred_element_type=F32)) / den
            lse = jnp.broadcast_to(m + jnp.log(den), (n, HEAD_DIM))
            if dilation == 1:
                dst_o, dst_lse, rows_out = o_ref, lse_ref, rows_of[s]
            elif chunks == 1:
                dst_o, dst_lse, rows_out = o_ref, lse_ref, pl.ds(s, n, stride=dilation)
            else:
                dst_o, dst_lse = scratch
                rows_out = pl.ds(pl.multiple_of((chunk * DIL_SUBBLOCKS + s) * n, n), n)
            dst_o[h, rows_out, :] = o
            dst_lse[h, rows_out, :] = lse

    raw, probs = {0: scores(0)}, {}
    for s in range(DIL_SUBBLOCKS + 1):
        if s + 1 < DIL_SUBBLOCKS:
            raw[s + 1] = scores(s + 1)
        if s < DIL_SUBBLOCKS:
            probs[s] = softmax(s, raw.pop(s))
        if s >= 1:
            values(s - 1, probs.pop(s - 1))

    if chunks > 1:
        o_tile, lse_tile = scratch

        @pl.when(chunk == chunks - 1)
        def _():
            for h in heads:
                for r in range(dilation):
                    natural = pl.ds(r, n, stride=dilation)
                    o_ref[h, natural, :] = o_tile[h, r * n:(r + 1) * n, :]
                    lse_ref[h, natural, :] = lse_tile[h, r * n:(r + 1) * n, :]


def _dilated_group(proj, slopes, group, dilation):
    b, s, _ = proj.shape
    n = DIL_STEPS
    step_rows = DIL_SUBBLOCKS * n
    assert dilation == 1 or dilation % DIL_SUBBLOCKS == 0
    chunks = max(dilation // DIL_SUBBLOCKS, 1)
    tile = step_rows * chunks
    assert s % tile == 0
    groups = len(DIL_CONFIGS)
    prev_rows = n if dilation == 1 else step_rows
    blocks_per_step = step_rows // prev_rows

    def cur(section):
        return pl.BlockSpec((None, step_rows, DIL_OUT),
                            lambda bi, i, c: (bi, i * chunks + c, section * groups + group))

    def prev(section):
        def index(bi, i, c):
            if dilation == 1:
                return (bi, jnp.maximum(i * blocks_per_step - 1, 0), section * groups + group)
            return (bi, jnp.maximum(i - 1, 0) * chunks + c, section * groups + group)
        return pl.BlockSpec((None, prev_rows, DIL_OUT), index)

    out_block = (DIL_HEADS_PER_GROUP, tile, HEAD_DIM)
    out_spec = pl.BlockSpec((None,) + out_block, lambda bi, i, c: (bi, 0, i, 0))
    out_sds = jax.ShapeDtypeStruct((b, DIL_HEADS_PER_GROUP, s, HEAD_DIM), F32)
    scratch = [pltpu.VMEM(out_block, F32)] * 2 if chunks > 1 else []
    o, lse = pl.pallas_call(
        functools.partial(_dilated_body, dilation=dilation, group=group),
        grid=(b, s // tile, chunks),
        in_specs=[pl.BlockSpec(memory_space=pltpu.SMEM),
                  cur(0), prev(1), cur(1), prev(2), cur(2)],
        out_specs=[out_spec, out_spec],
        out_shape=[out_sds, out_sds],
        scratch_shapes=scratch,
        compiler_params=_params("parallel", "arbitrary", "arbitrary"),
        name=f"dilated_attn_g{group}",
    )(slopes, proj, proj, proj, proj, proj)
    return o, lse


def _moba_body(slope_ref, q_ref, k_ref, v_ref, o_ref, kmean_ref, vt_ref, bias_ref, sel_ref,
               m_ref, acc_ref, ahead_ref, *, nblk, group):
    blk = MOBA_BLOCK
    hd = HEAD_DIM
    heads = q_ref.shape[1] // hd
    own = pl.program_id(2)
    key_off = lax.broadcasted_iota(jnp.int32, (blk, blk), 0)
    qry_off = lax.broadcasted_iota(jnp.int32, (blk, blk), 1)
    slope2 = [slope_ref[pl.program_id(1) * heads + h] * LOG2E for h in range(heads)]

    @pl.when(own == 0)
    def _():
        def fill(jb, carry):
            start = pl.multiple_of(jb * blk, blk)
            rows = k_ref[pl.ds(start, blk), :].astype(F32)
            kmean_ref[pl.ds(jb, 1), :] = jnp.mean(rows, axis=0, keepdims=True)
            vrows = v_ref[pl.ds(start, blk), :].astype(F32)
            for h in range(heads):
                vt_ref[h, :hd, pl.ds(start, blk)] = vrows[:, h * hd:(h + 1) * hd].T.astype(BF16)
            return carry
        lax.fori_loop(0, nblk, fill, 0)
        for h in range(heads):
            vt_ref[h, hd:, :] = jnp.ones((vt_ref.shape[1] - hd, vt_ref.shape[2]), BF16)
        for h in range(heads):
            bias_ref[h] = -slope2[h] * (qry_off - key_off).astype(F32)

    q = [q_ref[:, h * hd:(h + 1) * hd] for h in range(heads)]

    blk_id = lax.broadcasted_iota(jnp.int32, (nblk, blk), 0).astype(F32)
    gates = []
    for h in range(heads):
        gate = lax.dot_general(kmean_ref[:, h * hd:(h + 1) * hd], q[h].astype(F32), _NT,
                               precision=lax.Precision.HIGHEST, preferred_element_type=F32)
        gates.append(jnp.where(blk_id < own.astype(F32), gate, NEG_INF))

    def scores(h, start, rows):
        k = k_ref[pl.ds(start, rows), h * hd:(h + 1) * hd]
        return lax.dot_general(k, q[h], _NT, preferred_element_type=F32)

    own_start = pl.multiple_of(own * blk, blk)
    own_x = [scores(h, own_start, blk) for h in range(heads)]

    sels = [jnp.zeros((nblk, blk), F32) for _ in range(heads)]
    for _ in range(MOBA_TOPK):
        for h in range(heads):
            best = jnp.max(gates[h], axis=0, keepdims=True)
            is_best = (gates[h] == best) & (gates[h] > NEG_INF)
            pick = jnp.min(jnp.where(is_best, blk_id, float(nblk)), axis=0, keepdims=True)
            picked = blk_id == pick
            sels[h] = jnp.where(picked, 1.0, sels[h])
            gates[h] = jnp.where(picked, NEG_INF, gates[h])
    for h in range(heads):
        sel_ref[h] = sels[h]

    own_p = []
    for h in range(heads):
        x = jnp.where(qry_off >= key_off, own_x[h] + bias_ref[h], NEG_INF)
        m0 = jnp.max(x, axis=0, keepdims=True)
        own_p.append((m0, jnp.exp2(x - m0).astype(BF16)))
    for h in range(heads):
        m0, p = own_p[h]
        m_ref[h] = m0
        acc_ref[h] = jnp.dot(vt_ref[h, :, pl.ds(own_start, blk)], p, preferred_element_type=F32)

    def softmax_update(h, i, x, m):
        xs, chosen, shift = [], [], []
        m_new = m
        for g in range(group):
            j = i * group + g
            xs.append(x[g * blk:(g + 1) * blk] + bias_ref[h])
            chosen.append(sel_ref[h, pl.ds(j, 1), :] > 0.5)
            shift.append(-slope2[h] * ((own - j) * blk).astype(F32))
            top = jnp.max(xs[g], axis=0, keepdims=True) + shift[g]
            m_new = jnp.maximum(m_new, jnp.where(chosen[g], top, NEG_INF))
        ps = []
        for g in range(group):
            ref_g = jnp.where(chosen[g], m_new - shift[g], jnp.inf)
            ps.append(jnp.exp2(xs[g] - ref_g).astype(BF16))
        return m_new, jnp.exp2(m - m_new), jnp.concatenate(ps, axis=0)

    def past_blocks(i, carry):
        start = pl.multiple_of(i * (group * blk), group * blk)
        x, sm = {0: ahead_ref[...]}, {}
        for h in range(heads + 1):
            if h + 1 < heads:
                x[h + 1] = scores(h + 1, start, group * blk)
            elif h + 1 == heads:
                ahead_ref[...] = scores(0, group_start(i + 1), group * blk)
            if h < heads:
                sm[h] = softmax_update(h, i, x[h], m_ref[h])
            if h >= 1:
                m_new, alpha, p = sm[h - 1]
                m_ref[h - 1] = m_new
                acc_ref[h - 1] = alpha * acc_ref[h - 1] + jnp.dot(
                    vt_ref[h - 1, :, pl.ds(start, group * blk)], p, preferred_element_type=F32)
        return carry

    def group_start(i):
        return pl.multiple_of(jnp.minimum(i, nblk // group - 1) * (group * blk), group * blk)

    n_groups = lax.div(own + (group - 1), group)
    ahead_ref[...] = scores(0, group_start(0), group * blk)
    lax.fori_loop(0, n_groups, past_blocks, 0)
    for h in range(heads):
        acc = acc_ref[h]
        o_ref[:, h * hd:(h + 1) * hd] = (acc[:hd] / acc[hd:hd + 1]).T.astype(o_ref.dtype)


def _moba(proj, slopes):
    b, s, _ = proj.shape
    assert s % (MOBA_BLOCK * MOBA_GROUP) == 0
    nblk = s // MOBA_BLOCK
    hp = MOBA_HEADS_PER_STEP
    width = hp * HEAD_DIM
    q0 = 3 * DIL_WIDTH // width
    k0 = q0 + MOBA_WIDTH // width
    v0 = k0 + MOBA_WIDTH // width
    y = pl.pallas_call(
        functools.partial(_moba_body, nblk=nblk, group=MOBA_GROUP),
        grid=(b, MOBA_HEADS // hp, nblk),
        in_specs=[
            pl.BlockSpec(memory_space=pltpu.SMEM),
            pl.BlockSpec((None, MOBA_BLOCK, width), lambda bi, h, i: (bi, i, q0 + h)),
            pl.BlockSpec((None, s, width), lambda bi, h, i: (bi, 0, k0 + h)),
            pl.BlockSpec((None, s, width), lambda bi, h, i: (bi, 0, v0 + h)),
        ],
        out_specs=pl.BlockSpec((None, MOBA_BLOCK, width), lambda bi, h, i: (bi, i, h)),
        out_shape=jax.ShapeDtypeStruct((b, s, MOBA_WIDTH), BF16),
        scratch_shapes=[pltpu.VMEM((nblk, width), F32),
                        pltpu.VMEM((hp, HEAD_DIM + BF16_ROWS, s), BF16),
                        pltpu.VMEM((hp, MOBA_BLOCK, MOBA_BLOCK), F32),
                        pltpu.VMEM((hp, nblk, MOBA_BLOCK), F32),
                        pltpu.VMEM((hp, 1, MOBA_BLOCK), F32),
                        pltpu.VMEM((hp, HEAD_DIM + BF16_ROWS, MOBA_BLOCK), F32),
                        pltpu.VMEM((MOBA_GROUP * MOBA_BLOCK, MOBA_BLOCK), F32)],
        compiler_params=_params("parallel", "parallel", "arbitrary"),
        name="moba_attn",
    )(slopes, proj, proj, proj)
    return y.reshape(b * s, MOBA_WIDTH)


def _merged_branches(x, o_refs, l_refs, ym_ref, gd_ref, gm_ref, wud_ref, wum_ref, wo_ref):
    heads = []
    for h in range(DIL_HEADS_PER_GROUP):
        l0, l1, l2 = (l_ref[h] for l_ref in l_refs)
        m = jnp.maximum(jnp.maximum(l0, l1), l2)
        e0, e1, e2 = jnp.exp(l0 - m), jnp.exp(l1 - m), jnp.exp(l2 - m)
        den = e0 + e1 + e2
        mixed = ((e0 / den) * o_refs[0][h] + (e1 / den) * o_refs[1][h] + (e2 / den) * o_refs[2][h])
        heads.append(mixed.astype(BF16))
    y_dil = jnp.concatenate(heads, axis=-1)
    lift_dil = jnp.dot(y_dil, wud_ref[...], preferred_element_type=F32)
    lift_moba = jnp.dot(ym_ref[...], wum_ref[...], preferred_element_type=F32)
    merged = gd_ref[...].astype(F32) * lift_dil + gm_ref[...].astype(F32) * lift_moba
    return x + jnp.dot(merged.astype(BF16), wo_ref[...], preferred_element_type=F32)


def _cross_attended(x, g_ref, wq_ref, kv_ref, wo_ref):
    h = _rms(x, g_ref[...]).astype(BF16)
    q = jnp.dot(h, wq_ref[...], preferred_element_type=F32).astype(BF16)
    heads = []
    for hd in range(MEM_HEADS):
        k = kv_ref[:, hd * HEAD_DIM:(hd + 1) * HEAD_DIM]
        v = kv_ref[:, MEM_WIDTH + hd * HEAD_DIM:MEM_WIDTH + (hd + 1) * HEAD_DIM]
        s = lax.dot_general(q[:, hd * HEAD_DIM:(hd + 1) * HEAD_DIM], k, _NT,
                            preferred_element_type=F32) * SCALE
        p = jnp.exp(s - jnp.max(s, axis=-1, keepdims=True))
        den = jnp.sum(p, axis=-1, keepdims=True)
        heads.append((jnp.dot(p.astype(BF16), v, preferred_element_type=F32) / den).astype(BF16))
    o = jnp.concatenate(heads, axis=-1)
    return x + jnp.dot(o, wo_ref[...], preferred_element_type=F32)


def _merge_cross_body(x_ref, o0_ref, o1_ref, o2_ref, l0_ref, l1_ref, l2_ref, ym_ref, gd_ref, gm_ref,
                      wud_ref, wum_ref, wo_ref, gc_ref, wq_ref, kv_ref, wom_ref, out_ref):
    x1 = _merged_branches(x_ref[...], (o0_ref, o1_ref, o2_ref), (l0_ref, l1_ref, l2_ref),
                          ym_ref, gd_ref, gm_ref, wud_ref, wum_ref, wo_ref)
    out_ref[...] = _cross_attended(x1, gc_ref, wq_ref, kv_ref, wom_ref)


def _merge_cross(x, dil_outs, dil_lses, y_moba, gates, w_up_dil, w_up_moba, w_out,
                 g_cross, w_q, kv, w_o, *, tm):
    t, d = x.shape
    s = dil_outs[0].shape[2]
    mem_len = kv.shape[1]
    assert t % tm == 0 and s % tm == 0
    per_batch = s // tm
    row = lambda w: pl.BlockSpec((tm, w), lambda i: (i, 0))
    dil = pl.BlockSpec((None, DIL_HEADS_PER_GROUP, tm, HEAD_DIM),
                       lambda i: (i // per_batch, 0, i % per_batch, 0))
    return pl.pallas_call(
        _merge_cross_body,
        grid=(t // tm,),
        in_specs=[row(d)] + [dil] * 6 + [
            row(MOBA_WIDTH),
            pl.BlockSpec((tm, d), lambda i: (i, 0)),
            pl.BlockSpec((tm, d), lambda i: (i, 1)),
            _resident((DIL_OUT, d)), _resident((MOBA_WIDTH, d)), _resident((d, d)),
            _resident((1, d)), _resident((d, MEM_WIDTH)),
            pl.BlockSpec((None, mem_len, 2 * MEM_WIDTH), lambda i: (i // per_batch, 0, 0)),
            _resident((MEM_WIDTH, d)),
        ],
        out_specs=row(d),
        out_shape=jax.ShapeDtypeStruct((t, d), F32),
        compiler_params=_params("parallel"),
        name="merge_out_proj_cross_attn",
    )(x, *dil_outs, *dil_lses, y_moba, gates, gates, w_up_dil, w_up_moba, w_out,
      g_cross.reshape(1, d), w_q, kv, w_o)


ROUTE_COLS = 8
MOE_ROW_TILE = 256


def _route_body(x_ref, g_ref, wr_ref, br_ref, t_ref, info_ref, counts_ref, run_ref, tri_ref):
    tm = x_ref.shape[0]
    n_route = N_GROUPS + N_EXPERTS
    lane = lax.broadcasted_iota(jnp.int32, (tm, n_route), 1).astype(F32)

    @pl.when(pl.program_id(0) == 0)
    def _():
        run_ref[...] = jnp.zeros_like(run_ref)
        earlier = (lax.broadcasted_iota(jnp.int32, (tm, tm), 0)
                   > lax.broadcasted_iota(jnp.int32, (tm, tm), 1))
        tri_ref[...] = jnp.where(earlier, 1.0, 0.0).astype(BF16)

    t = _rms(x_ref[...], g_ref[...])
    t_ref[...] = t
    logits = _dot_split3(t, wr_ref[...]) + br_ref[...]
    none = float(n_route)
    glog = jnp.where(lane < N_GROUPS, logits, NEG_INF)
    gmax = jnp.max(glog, axis=-1, keepdims=True)
    gsel = jnp.min(jnp.where(glog == gmax, lane, none), axis=-1, keepdims=True)
    pg = 1.0 / jnp.sum(jnp.exp(glog - gmax), axis=-1, keepdims=True)
    first = N_GROUPS + gsel * EXPERTS_PER_GROUP
    in_group = (lane >= first) & (lane < first + EXPERTS_PER_GROUP)
    elog = jnp.where(in_group, logits, NEG_INF)
    top1 = jnp.max(elog, axis=-1, keepdims=True)
    i1 = jnp.min(jnp.where(elog == top1, lane, none), axis=-1, keepdims=True)
    rest = jnp.where(lane == i1, NEG_INF, elog)
    top2 = jnp.max(rest, axis=-1, keepdims=True)
    i2 = jnp.min(jnp.where(rest == top2, lane, none), axis=-1, keepdims=True)
    e2 = jnp.exp(top2 - top1)
    w1 = pg / (1.0 + e2)
    w2 = pg * e2 / (1.0 + e2)

    hit1 = lane == i1
    hit2 = lane == i2
    assigned = jnp.where(hit1 | hit2, 1.0, 0.0)
    before = jnp.dot(tri_ref[...], assigned.astype(BF16), preferred_element_type=F32) + run_ref[...]
    rank1 = jnp.sum(jnp.where(hit1, before, 0.0), axis=-1, keepdims=True)
    rank2 = jnp.sum(jnp.where(hit2, before, 0.0), axis=-1, keepdims=True)
    run_ref[...] += jnp.sum(assigned, axis=0, keepdims=True)
    counts_ref[...] = run_ref[...]

    col = lax.broadcasted_iota(jnp.int32, (tm, ROUTE_COLS), 1)
    fields = (i1 - N_GROUPS, i2 - N_GROUPS, rank1, rank2, w1, w2)
    info = jnp.zeros((tm, ROUTE_COLS), F32)
    for c, field in enumerate(fields):
        info = jnp.where(col == c, field, info)
    info_ref[...] = info


def _route(x, g, w_route, b_route, *, tm):
    t, d = x.shape
    n_route = N_GROUPS + N_EXPERTS
    assert t % tm == 0
    return pl.pallas_call(
        _route_body,
        grid=(t // tm,),
        in_specs=[
            pl.BlockSpec((tm, d), lambda i: (i, 0)),
            pl.BlockSpec((1, d), lambda i: (0, 0)),
            pl.BlockSpec((d, n_route), lambda i: (0, 0)),
            pl.BlockSpec((1, n_route), lambda i: (0, 0)),
        ],
        out_specs=[pl.BlockSpec((tm, d), lambda i: (i, 0)),
                   pl.BlockSpec((tm, ROUTE_COLS), lambda i: (i, 0)),
                   pl.BlockSpec((1, n_route), lambda i: (0, 0))],
        out_shape=[jax.ShapeDtypeStruct((t, d), F32),
                   jax.ShapeDtypeStruct((t, ROUTE_COLS), F32),
                   jax.ShapeDtypeStruct((1, n_route), F32)],
        scratch_shapes=[pltpu.VMEM((1, n_route), F32), pltpu.VMEM((tm, tm), BF16)],
        compiler_params=_params("arbitrary"),
        name="moe_route",
    )(x, g.reshape(1, d), w_route, b_route.reshape(1, n_route))


def _row_copies_wait(src_rows, dst_rows, sem):
    pltpu.make_async_copy(src_rows, dst_rows, sem).wait()


def _experts_body(tile_expert_ref, n_tiles_ref, pad_ref, pos_ref, t_ref, wg_ref, wu_ref, wd_ref,
                  y_ref, src_ref, rows_ref, sems):
    del tile_expert_ref
    i = pl.program_id(0)
    tr = rows_ref.shape[1]
    last_tile = n_tiles_ref[0] - 1
    slot = lax.rem(i, 2)

    def row_copy(tile, r, into):
        token = src_ref[tile * tr + r]
        return pltpu.make_async_copy(t_ref.at[pl.ds(token, 1)], rows_ref.at[into, pl.ds(r, 1)],
                                     sems.at[into])

    def fetch(tile, into, unroll):
        def start(r, carry):
            row_copy(tile, r, into).start()
            return carry
        lax.fori_loop(0, tr, start, 0, unroll=unroll)

    def fetched(into):
        _row_copies_wait(t_ref.at[pl.ds(0, tr)], rows_ref.at[into], sems.at[into])

    @pl.when(i == 0)
    def _():
        def pad_segment(e, carry):
            def clear(p, c):
                src_ref[p] = 0
                return c
            return lax.fori_loop(pad_ref[0, e], pad_ref[1, e], clear, carry)

        def invert(token, carry):
            for k in range(MOE_TOPK):
                src_ref[pos_ref[k, token]] = token
            return carry

        lax.fori_loop(0, pad_ref.shape[1], pad_segment, 0)
        lax.fori_loop(0, pos_ref.shape[1], invert, 0, unroll=8)
        fetch(0, 0, 8)

    fetched(slot)
    fetch(jnp.minimum(i + 1, last_tile), 1 - slot, True)
    t = rows_ref[slot].astype(BF16)
    gate = jnp.dot(t, wg_ref[...].astype(BF16), preferred_element_type=F32)
    up = jnp.dot(t, wu_ref[...].astype(BF16), preferred_element_type=F32)
    a = jax.nn.silu(gate) * up
    y_ref[...] = jnp.dot(a.astype(BF16), wd_ref[...].astype(BF16), preferred_element_type=F32)

    @pl.when(i == pl.num_programs(0) - 1)
    def _():
        fetched(1 - slot)


def _experts(t_rows, pos, pad_slots, tile_expert, n_tiles, w_gate, w_up, w_down, n_rows):
    _, d = t_rows.shape
    ff = w_gate.shape[-1]
    tr = MOE_ROW_TILE
    assert n_rows % tr == 0
    expert_of = lambda i, te, nt, pad, pos: te[jnp.minimum(i, nt[0] - 1)]
    return pl.pallas_call(
        _experts_body,
        grid_spec=pltpu.PrefetchScalarGridSpec(
            num_scalar_prefetch=4,
            grid=(n_rows // tr,),
            in_specs=[pl.BlockSpec(memory_space=pl.ANY),
                      pl.BlockSpec((None, d, ff), lambda *a: (expert_of(*a), 0, 0)),
                      pl.BlockSpec((None, d, ff), lambda *a: (expert_of(*a), 0, 0)),
                      pl.BlockSpec((None, ff, d), lambda *a: (expert_of(*a), 0, 0))],
            out_specs=pl.BlockSpec((tr, d), lambda i, *_: (i, 0)),
            scratch_shapes=[pltpu.SMEM((n_rows,), jnp.int32),
                            pltpu.VMEM((2, tr, d), F32),
                            pltpu.SemaphoreType.DMA((2,))],
        ),
        out_shape=jax.ShapeDtypeStruct((n_rows, d), F32),
        compiler_params=pltpu.CompilerParams(dimension_semantics=("arbitrary",),
                                             vmem_limit_bytes=VMEM_LIMIT_BYTES,
                                             disable_bounds_checks=True),
        name="moe_experts",
    )(tile_expert, n_tiles, pad_slots, pos, t_rows, w_gate, w_up, w_down)


def _combine_body(pos_ref, x_ref, info_ref, gf_ref, y_sorted_ref, out_ref, rows_ref, sem):
    tm = x_ref.shape[0]
    base = pl.program_id(0) * tm

    def issue(r, carry):
        for k in range(MOE_TOPK):
            src = pos_ref[k, base + r]
            pltpu.make_async_copy(y_sorted_ref.at[pl.ds(src, 1)], rows_ref.at[k, pl.ds(r, 1)],
                                  sem).start()
        return carry

    lax.fori_loop(0, tm, issue, 0, unroll=8)
    for k in range(MOE_TOPK):
        _row_copies_wait(y_sorted_ref.at[pl.ds(0, tm)], rows_ref.at[k], sem)

    info = info_ref[...]
    y = info[:, 4:5] * rows_ref[0] + info[:, 5:6] * rows_ref[1]
    out_ref[...] = _rms(x_ref[...] + y, gf_ref[...])


def _combine(x, info, pos, y_sorted, g_final, *, tm):
    t, d = x.shape
    assert t % tm == 0
    return pl.pallas_call(
        _combine_body,
        grid_spec=pltpu.PrefetchScalarGridSpec(
            num_scalar_prefetch=1,
            grid=(t // tm,),
            in_specs=[pl.BlockSpec((tm, d), lambda i, pos: (i, 0)),
                      pl.BlockSpec((tm, ROUTE_COLS), lambda i, pos: (i, 0)),
                      pl.BlockSpec((1, d), lambda i, pos: (0, 0)),
                      pl.BlockSpec(memory_space=pl.ANY)],
            out_specs=pl.BlockSpec((tm, d), lambda i, pos: (i, 0)),
            scratch_shapes=[pltpu.VMEM((MOE_TOPK, tm, d), F32), pltpu.SemaphoreType.DMA(())],
        ),
        out_shape=jax.ShapeDtypeStruct((t, d), F32),
        compiler_params=pltpu.CompilerParams(dimension_semantics=("arbitrary",),
                                             vmem_limit_bytes=VMEM_LIMIT_BYTES,
                                             disable_bounds_checks=True),
        name="moe_combine_final_norm",
    )(pos, x, info, g_final.reshape(1, d), y_sorted)


def _moe(x, g, w_route, b_route, w_gate, w_up, w_down, g_final):
    t, d = x.shape
    tr = MOE_ROW_TILE
    assert (MOE_TOPK * t) % tr == 0
    t_rows, info, counts = _route(x, g, w_route, b_route, tm=min(512, t))

    expert = info[:, 0:MOE_TOPK].astype(jnp.int32)
    rank = info[:, MOE_TOPK:2 * MOE_TOPK].astype(jnp.int32)
    count = counts[0, N_GROUPS:].astype(jnp.int32)
    seg_tiles = (count + (tr - 1)) // tr
    seg_end = jnp.cumsum(seg_tiles)
    seg_start_row = (seg_end - seg_tiles) * tr
    pos = (seg_start_row[expert] + rank).T
    max_tiles = (MOE_TOPK * t) // tr + N_EXPERTS
    tile_id = jnp.arange(max_tiles, dtype=jnp.int32)
    tile_expert = jnp.minimum(
        jnp.sum((seg_end[None, :] <= tile_id[:, None]).astype(jnp.int32), axis=1), N_EXPERTS - 1)
    n_tiles = seg_end[-1:].astype(jnp.int32)

    pad_slots = jnp.stack([seg_start_row + count, seg_end * tr]).astype(jnp.int32)
    y_sorted = _experts(t_rows, pos, pad_slots, tile_expert, n_tiles, w_gate, w_up, w_down,
                        max_tiles * tr)
    return _combine(x, info, pos, y_sorted, g_final, tm=min(512, t))


def _layer(x, mem, attn_norm, w_in, w_up_dil, w_up_moba, w_branch_gate, w_out, cross_norm,
           mem_norm, w_q_mem, w_kv_mem, w_o_mem):
    b, s, d = x.shape
    t = b * s
    xt = x.reshape(t, d)
    h = _norm(xt, attn_norm, tm=min(512, t), name="attn_norm")
    proj, gates = _in_proj(h, w_in, w_branch_gate, tm=min(2048, t))
    proj = proj.reshape(b, s, IN_WIDTH)
    dil_slopes = _alibi_slopes(DIL_HEADS)
    dil = [_dilated_group(proj, dil_slopes, g, dilation)
           for g, (_, dilation) in enumerate(DIL_CONFIGS)]
    y_moba = _moba(proj, _alibi_slopes(MOBA_HEADS))
    mem_len = mem.shape[1]
    kv = _norm_matmul(mem.reshape(b * mem_len, d), mem_norm, w_kv_mem.astype(BF16),
                      tm=b * mem_len, tn=512, name="norm_mem_kv").reshape(b, mem_len, 2 * MEM_WIDTH)
    return _merge_cross(xt, [o for o, _ in dil], [l for _, l in dil], y_moba, gates,
                        w_up_dil.astype(BF16), w_up_moba.astype(BF16), w_out.astype(BF16),
                        cross_norm, w_q_mem.astype(BF16), kv, w_o_mem.astype(BF16), tm=min(256, s))


def kernel(x, mem, attn_norm, w_in, w_up_dil, w_up_moba, w_branch_gate, w_out, cross_norm, mem_norm,
           w_q_mem, w_kv_mem, w_o_mem, ffn_norm, w_router_group, b_router_group, w_router_expert,
           b_router_expert, w_expert_gate, w_expert_up, w_expert_down, final_norm):
    b, s, d = x.shape
    depth = attn_norm.shape[0]
    assert depth == 1, "the final norm is fused into the last layer's MoE call"
    l = 0
    x2 = _layer(x, mem, attn_norm[l], w_in[l], w_up_dil[l], w_up_moba[l], w_branch_gate[l], w_out[l],
                cross_norm[l], mem_norm[l], w_q_mem[l], w_kv_mem[l], w_o_mem[l])
    w_route = jnp.concatenate([w_router_group[l], w_router_expert[l]], axis=1)
    b_route = jnp.concatenate([b_router_group[l], b_router_expert[l]], axis=0)
    out = _moe(x2, ffn_norm[l], w_route, b_route, w_expert_gate[l], w_expert_up[l],
               w_expert_down[l], final_norm)
    return out.reshape(b, s, d)
```

```python
import functools

import numpy as np
import jax
import jax.numpy as jnp
from jax import lax
from jax.experimental import pallas as pl
from jax.experimental.pallas import tpu as pltpu

F32 = jnp.float32
BF16 = jnp.bfloat16

HEAD_DIM = 128
DIL_CONFIGS = ((128, 1), (512, 4), (2048, 16))
DIL_HEADS_PER_GROUP = 4
DIL_HEADS = DIL_HEADS_PER_GROUP * len(DIL_CONFIGS)
DIL_WIDTH = DIL_HEADS * HEAD_DIM
DIL_OUT = DIL_HEADS_PER_GROUP * HEAD_DIM
DIL_STEPS = 128
MOBA_HEADS = 8
MOBA_WIDTH = MOBA_HEADS * HEAD_DIM
MOBA_BLOCK = 256
MOBA_TOPK = 3
IN_WIDTH = 3 * (DIL_WIDTH + MOBA_WIDTH)
MEM_HEADS = 4
MEM_WIDTH = MEM_HEADS * HEAD_DIM
N_GROUPS = 4
EXPERTS_PER_GROUP = 8
N_EXPERTS = N_GROUPS * EXPERTS_PER_GROUP
MOE_TOPK = 2
RMS_EPS = 1e-6
SCALE = HEAD_DIM ** -0.5
NEG_INF = float("-inf")
LOG2E = 1.4426950408889634
MOBA_GROUP = 4
MOBA_HEADS_PER_STEP = 4
MOBA_Q_TILES = (9, 11)

VMEM_LIMIT_BYTES = 56 * 1024 * 1024
BF16_ROWS = 16
MXU_COLUMNS = 256
_NT = (((1,), (1,)), ((), ()))


def _alibi_slopes(n):
    return jnp.asarray(2.0 ** (-8.0 * np.arange(1, n + 1) / n), dtype=F32)


def _rms(x, g):
    return x * lax.rsqrt(jnp.mean(x * x, axis=-1, keepdims=True) + RMS_EPS) * g


def _dot_split3(a, b):
    a_hi = a.astype(BF16)
    b_hi = b.astype(BF16)
    a_lo = (a - a_hi.astype(F32)).astype(BF16)
    b_lo = (b - b_hi.astype(F32)).astype(BF16)
    dot = functools.partial(jnp.dot, preferred_element_type=F32)
    n = b.shape[1]
    if 2 * n <= MXU_COLUMNS:
        both = dot(a_hi, jnp.concatenate([b_hi, b_lo], axis=1))
        return both[:, :n] + (dot(a_lo, b_hi) + both[:, n:])
    return dot(a_hi, b_hi) + (dot(a_lo, b_hi) + dot(a_hi, b_lo))


def _params(*sem, flags=None):
    return pltpu.CompilerParams(dimension_semantics=sem, vmem_limit_bytes=VMEM_LIMIT_BYTES,
                                flags=flags)


def _resident(shape):
    nd = len(shape)
    return pl.BlockSpec(shape, lambda *_: (0,) * nd, pipeline_mode=pl.Buffered(1))


def _norm_matmul_body(x_ref, g_ref, w_ref, o_ref, h_ref, *, sigmoid):
    @pl.when(pl.program_id(1) == 0)
    def _():
        h_ref[...] = _rms(x_ref[...], g_ref[...]).astype(BF16)

    acc = jnp.dot(h_ref[...], w_ref[...], preferred_element_type=F32)
    if sigmoid:
        acc = jax.nn.sigmoid(acc)
    o_ref[...] = acc.astype(o_ref.dtype)


def _norm_matmul(x, g, w, *, tm, tn, sigmoid=False, name):
    m, d = x.shape
    n = w.shape[1]
    assert m % tm == 0 and n % tn == 0
    return pl.pallas_call(
        functools.partial(_norm_matmul_body, sigmoid=sigmoid),
        grid=(m // tm, n // tn),
        in_specs=[
            pl.BlockSpec((tm, d), lambda i, j: (i, 0)),
            pl.BlockSpec((1, d), lambda i, j: (0, 0)),
            pl.BlockSpec((d, tn), lambda i, j: (0, j)),
        ],
        out_specs=pl.BlockSpec((tm, tn), lambda i, j: (i, j)),
        out_shape=jax.ShapeDtypeStruct((m, n), BF16),
        scratch_shapes=[pltpu.VMEM((tm, d), BF16)],
        compiler_params=_params("parallel", "arbitrary"),
        name=name,
    )(x, g.reshape(1, d), w)


def _norm_body(x_ref, g_ref, o_ref):
    o_ref[...] = _rms(x_ref[...], g_ref[...]).astype(o_ref.dtype)


def _norm(x, g, *, tm, name):
    m, d = x.shape
    assert m % tm == 0
    return pl.pallas_call(
        _norm_body,
        grid=(m // tm,),
        in_specs=[pl.BlockSpec((tm, d), lambda i: (i, 0)), pl.BlockSpec((1, d), lambda i: (0, 0))],
        out_specs=pl.BlockSpec((tm, d), lambda i: (i, 0)),
        out_shape=jax.ShapeDtypeStruct((m, d), BF16),
        compiler_params=_params("parallel"),
        name=name,
    )(x, g.reshape(1, d))


def _in_proj_body(h_ref, win_ref, wbg_ref, proj_ref, gates_ref, perm_ref, *, n_proj_tiles):
    j = pl.program_id(1)
    tm = h_ref.shape[0]
    n = DIL_STEPS
    groups = len(DIL_CONFIGS)

    def store_regrouped(acc, dilation):
        tile = n * dilation
        for c in range(acc.shape[1] // HEAD_DIM):
            cols = slice(c * HEAD_DIM, (c + 1) * HEAD_DIM)
            perm_ref[c] = acc[:, cols]
            for t0 in range(0, tm, tile):
                for r in range(dilation):
                    rows = perm_ref[c, pl.ds(t0 + r, n, stride=dilation), :]
                    proj_ref[t0 + r * n:t0 + (r + 1) * n, cols] = rows.astype(BF16)

    @pl.when(j < n_proj_tiles)
    def _():
        acc = jnp.dot(h_ref[...], win_ref[...].astype(BF16), preferred_element_type=F32)
        group = jnp.where(j < 3 * groups, lax.rem(j, groups), 0)
        moba_q = (j >= MOBA_Q_TILES[0]) & (j < MOBA_Q_TILES[1])
        scale = jnp.where(moba_q, SCALE * LOG2E, 1.0)

        def store_natural():
            proj_ref[...] = (acc * scale).astype(BF16)

        for g, (_, dilation) in enumerate(DIL_CONFIGS):
            store = store_natural if dilation == 1 else functools.partial(store_regrouped, acc, dilation)
            pl.when(group == g)(store)

    @pl.when(j >= n_proj_tiles)
    def _():
        acc = jnp.dot(h_ref[...], wbg_ref[...].astype(BF16), preferred_element_type=F32)
        gates_ref[...] = jax.nn.sigmoid(acc).astype(BF16)


def _in_proj(h, w_in, w_bg, *, tm):
    t, d = h.shape
    tn = DIL_OUT
    assert MOBA_Q_TILES == (3 * DIL_WIDTH // tn, (3 * DIL_WIDTH + MOBA_WIDTH) // tn)
    assert t % tm == 0 and tm % (DIL_STEPS * max(dl for _, dl in DIL_CONFIGS)) == 0
    assert DIL_WIDTH == len(DIL_CONFIGS) * tn and w_in.shape[1] % tn == 0 and w_bg.shape[1] % tn == 0
    n_proj = w_in.shape[1] // tn
    n_gate = w_bg.shape[1] // tn
    return pl.pallas_call(
        functools.partial(_in_proj_body, n_proj_tiles=n_proj),
        grid=(t // tm, n_proj + n_gate),
        in_specs=[
            pl.BlockSpec((tm, d), lambda i, j: (i, 0)),
            pl.BlockSpec((d, tn), lambda i, j: (0, jnp.minimum(j, n_proj - 1))),
            pl.BlockSpec((d, tn), lambda i, j: (0, jnp.maximum(j - n_proj, 0))),
        ],
        out_specs=[
            pl.BlockSpec((tm, tn), lambda i, j: (i, jnp.minimum(j, n_proj - 1))),
            pl.BlockSpec((tm, tn), lambda i, j: (i, jnp.maximum(j - n_proj, 0))),
        ],
        out_shape=[jax.ShapeDtypeStruct((t, w_in.shape[1]), BF16),
                   jax.ShapeDtypeStruct((t, w_bg.shape[1]), BF16)],
        scratch_shapes=[pltpu.VMEM((tn // HEAD_DIM, tm, HEAD_DIM), F32)],
        compiler_params=_params("parallel", "arbitrary"),
        name="in_proj_gates",
    )(h, w_in, w_bg)


DIL_SUBBLOCKS = 4


def _dilated_body(slope_ref, q_ref, kp_ref, kc_ref, vp_ref, vc_ref, o_ref, lse_ref, *scratch,
                  dilation, group):
    n = DIL_STEPS
    first_tile = pl.program_id(1) == 0
    chunk = pl.program_id(2)
    chunks = max(dilation // DIL_SUBBLOCKS, 1)
    qi = lax.broadcasted_iota(jnp.int32, (n, n), 0)
    kj = lax.broadcasted_iota(jnp.int32, (n, n), 1)
    steps_cur = qi - kj
    valid_cur = steps_cur >= 0
    dist_cur = (steps_cur * dilation).astype(F32)
    dist_prev = ((steps_cur + n) * dilation).astype(F32)
    limit_across_tiles = jnp.where(first_tile, -n, 0)
    heads = range(DIL_HEADS_PER_GROUP)
    cols_of = [slice(h * HEAD_DIM, (h + 1) * HEAD_DIM) for h in heads]
    rows_of = [slice(s * n, (s + 1) * n) for s in range(DIL_SUBBLOCKS)]

    def prev_rows(s):
        if dilation == 1:
            return (slice(0, n), False, limit_across_tiles) if s == 0 else (rows_of[s - 1], True, 0)
        return rows_of[s], False, limit_across_tiles

    def scores(s):
        rows, from_cur, _ = prev_rows(s)
        out = []
        for h in heads:
            q = q_ref[rows_of[s], cols_of[h]]
            k_prev = (kc_ref if from_cur else kp_ref)[rows, cols_of[h]]
            out.append((lax.dot_general(q, kc_ref[rows_of[s], cols_of[h]], _NT,
                                        preferred_element_type=F32),
                        lax.dot_general(q, k_prev, _NT, preferred_element_type=F32)))
        return out

    def softmax(s, raw):
        valid_prev = steps_cur <= prev_rows(s)[2]
        out = []
        for h in heads:
            slope = slope_ref[group * DIL_HEADS_PER_GROUP + h]
            s_cur = jnp.where(valid_cur, raw[h][0] * SCALE - slope * dist_cur, NEG_INF)
            s_prev = jnp.where(valid_prev, raw[h][1] * SCALE - slope * dist_prev, NEG_INF)
            m = jnp.maximum(jnp.max(s_cur, axis=-1, keepdims=True),
                            jnp.max(s_prev, axis=-1, keepdims=True))
            p_cur = jnp.exp(s_cur - m)
            p_prev = jnp.exp(s_prev - m)
            den = jnp.sum(p_cur, axis=-1, keepdims=True) + jnp.sum(p_prev, axis=-1, keepdims=True)
            out.append((p_cur.astype(BF16), p_prev.astype(BF16), m, den))
        return out

    def values(s, probs):
        rows, from_cur, _ = prev_rows(s)
        for h in heads:
            p_cur, p_prev, m, den = probs[h]
            v_prev = (vc_ref if from_cur else vp_ref)[rows, cols_of[h]]
            o = (jnp.dot(p_cur, vc_ref[rows_of[s], cols_of[h]], preferred_element_type=F32)
                 + jnp.dot(p_prev, v_prev, preferred_element_type=F32)) / den
            lse = jnp.broadcast_to(m + jnp.log(den), (n, HEAD_DIM))
            if dilation == 1:
                dst_o, dst_lse, rows_out = o_ref, lse_ref, rows_of[s]
            elif chunks == 1:
                dst_o, dst_lse, rows_out = o_ref, lse_ref, pl.ds(s, n, stride=dilation)
            else:
                dst_o, dst_lse = scratch
                rows_out = pl.ds(pl.multiple_of((chunk * DIL_SUBBLOCKS + s) * n, n), n)
            dst_o[h, rows_out, :] = o
            dst_lse[h, rows_out, :] = lse

    raw, probs = {0: scores(0)}, {}
    for s in range(DIL_SUBBLOCKS + 1):
        if s + 1 < DIL_SUBBLOCKS:
            raw[s + 1] = scores(s + 1)
        if s < DIL_SUBBLOCKS:
            probs[s] = softmax(s, raw.pop(s))
        if s >= 1:
            values(s - 1, probs.pop(s - 1))

    if chunks > 1:
        o_tile, lse_tile = scratch

        @pl.when(chunk == chunks - 1)
        def _():
            for h in heads:
                for r in range(dilation):
                    natural = pl.ds(r, n, stride=dilation)
                    o_ref[h, natural, :] = o_tile[h, r * n:(r + 1) * n, :]
                    lse_ref[h, natural, :] = lse_tile[h, r * n:(r + 1) * n, :]


def _dilated_group(proj, slopes, group, dilation):
    b, s, _ = proj.shape
    n = DIL_STEPS
    step_rows = DIL_SUBBLOCKS * n
    assert dilation == 1 or dilation % DIL_SUBBLOCKS == 0
    chunks = max(dilation // DIL_SUBBLOCKS, 1)
    tile = step_rows * chunks
    assert s % tile == 0
    groups = len(DIL_CONFIGS)
    prev_rows = n if dilation == 1 else step_rows
    blocks_per_step = step_rows // prev_rows

    def cur(section):
        return pl.BlockSpec((None, step_rows, DIL_OUT),
                            lambda bi, i, c: (bi, i * chunks + c, section * groups + group))

    def prev(section):
        def index(bi, i, c):
            if dilation == 1:
                return (bi, jnp.maximum(i * blocks_per_step - 1, 0), section * groups + group)
            return (bi, jnp.maximum(i - 1, 0) * chunks + c, section * groups + group)
        return pl.BlockSpec((None, prev_rows, DIL_OUT), index)

    out_block = (DIL_HEADS_PER_GROUP, tile, HEAD_DIM)
    out_spec = pl.BlockSpec((None,) + out_block, lambda bi, i, c: (bi, 0, i, 0))
    out_sds = jax.ShapeDtypeStruct((b, DIL_HEADS_PER_GROUP, s, HEAD_DIM), F32)
    scratch = [pltpu.VMEM(out_block, F32)] * 2 if chunks > 1 else []
    o, lse = pl.pallas_call(
        functools.partial(_dilated_body, dilation=dilation, group=group),
        grid=(b, s // tile, chunks),
        in_specs=[pl.BlockSpec(memory_space=pltpu.SMEM),
                  cur(0), prev(1), cur(1), prev(2), cur(2)],
        out_specs=[out_spec, out_spec],
        out_shape=[out_sds, out_sds],
        scratch_shapes=scratch,
        compiler_params=_params("parallel", "arbitrary", "arbitrary"),
        name=f"dilated_attn_g{group}",
    )(slopes, proj, proj, proj, proj, proj)
    return o, lse


def _moba_body(slope_ref, q_ref, k_ref, v_ref, o_ref, kmean_ref, vt_ref, bias_ref, sel_ref,
               m_ref, acc_ref, ahead_ref, *, nblk, group):
    blk = MOBA_BLOCK
    hd = HEAD_DIM
    heads = q_ref.shape[1] // hd
    own = pl.program_id(2)
    key_off = lax.broadcasted_iota(jnp.int32, (blk, blk), 0)
    qry_off = lax.broadcasted_iota(jnp.int32, (blk, blk), 1)
    slope2 = [slope_ref[pl.program_id(1) * heads + h] * LOG2E for h in range(heads)]

    @pl.when(own == 0)
    def _():
        def fill(jb, carry):
            start = pl.multiple_of(jb * blk, blk)
            rows = k_ref[pl.ds(start, blk), :].astype(F32)
            kmean_ref[pl.ds(jb, 1), :] = jnp.mean(rows, axis=0, keepdims=True)
            vrows = v_ref[pl.ds(start, blk), :].astype(F32)
            for h in range(heads):
                vt_ref[h, :hd, pl.ds(start, blk)] = vrows[:, h * hd:(h + 1) * hd].T.astype(BF16)
            return carry
        lax.fori_loop(0, nblk, fill, 0)
        for h in range(heads):
            vt_ref[h, hd:, :] = jnp.ones((vt_ref.shape[1] - hd, vt_ref.shape[2]), BF16)
        for h in range(heads):
            bias_ref[h] = -slope2[h] * (qry_off - key_off).astype(F32)

    q = [q_ref[:, h * hd:(h + 1) * hd] for h in range(heads)]

    blk_id = lax.broadcasted_iota(jnp.int32, (nblk, blk), 0).astype(F32)
    gates = []
    for h in range(heads):
        gate = lax.dot_general(kmean_ref[:, h * hd:(h + 1) * hd], q[h].astype(F32), _NT,
                               precision=lax.Precision.HIGHEST, preferred_element_type=F32)
        gates.append(jnp.where(blk_id < own.astype(F32), gate, NEG_INF))

    def scores(h, start, rows):
        k = k_ref[pl.ds(start, rows), h * hd:(h + 1) * hd]
        return lax.dot_general(k, q[h], _NT, preferred_element_type=F32)

    own_start = pl.multiple_of(own * blk, blk)
    own_x = [scores(h, own_start, blk) for h in range(heads)]

    sels = [jnp.zeros((nblk, blk), F32) for _ in range(heads)]
    for _ in range(MOBA_TOPK):
        for h in range(heads):
            best = jnp.max(gates[h], axis=0, keepdims=True)
            is_best = (gates[h] == best) & (gates[h] > NEG_INF)
            pick = jnp.min(jnp.where(is_best, blk_id, float(nblk)), axis=0, keepdims=True)
            picked = blk_id == pick
            sels[h] = jnp.where(picked, 1.0, sels[h])
            gates[h] = jnp.where(picked, NEG_INF, gates[h])
    for h in range(heads):
        sel_ref[h] = sels[h]

    own_p = []
    for h in range(heads):
        x = jnp.where(qry_off >= key_off, own_x[h] + bias_ref[h], NEG_INF)
        m0 = jnp.max(x, axis=0, keepdims=True)
        own_p.append((m0, jnp.exp2(x - m0).astype(BF16)))
    for h in range(heads):
        m0, p = own_p[h]
        m_ref[h] = m0
        acc_ref[h] = jnp.dot(vt_ref[h, :, pl.ds(own_start, blk)], p, preferred_element_type=F32)

    def softmax_update(h, i, x, m):
        xs, chosen, shift = [], [], []
        m_new = m
        for g in range(group):
            j = i * group + g
            xs.append(x[g * blk:(g + 1) * blk] + bias_ref[h])
            chosen.append(sel_ref[h, pl.ds(j, 1), :] > 0.5)
            shift.append(-slope2[h] * ((own - j) * blk).astype(F32))
            top = jnp.max(xs[g], axis=0, keepdims=True) + shift[g]
            m_new = jnp.maximum(m_new, jnp.where(chosen[g], top, NEG_INF))
        ps = []
        for g in range(group):
            ref_g = jnp.where(chosen[g], m_new - shift[g], jnp.inf)
            ps.append(jnp.exp2(xs[g] - ref_g).astype(BF16))
        return m_new, jnp.exp2(m - m_new), jnp.concatenate(ps, axis=0)

    def past_blocks(i, carry):
        start = pl.multiple_of(i * (group * blk), group * blk)
        x, sm = {0: ahead_ref[...]}, {}
        for h in range(heads + 1):
            if h + 1 < heads:
                x[h + 1] = scores(h + 1, start, group * blk)
            elif h + 1 == heads:
                ahead_ref[...] = scores(0, group_start(i + 1), group * blk)
            if h < heads:
                sm[h] = softmax_update(h, i, x[h], m_ref[h])
            if h >= 1:
                m_new, alpha, p = sm[h - 1]
                m_ref[h - 1] = m_new
                acc_ref[h - 1] = alpha * acc_ref[h - 1] + jnp.dot(
                    vt_ref[h - 1, :, pl.ds(start, group * blk)], p, preferred_element_type=F32)
        return carry

    def group_start(i):
        return pl.multiple_of(jnp.minimum(i, nblk // group - 1) * (group * blk), group * blk)

    n_groups = lax.div(own + (group - 1), group)
    ahead_ref[...] = scores(0, group_start(0), group * blk)
    lax.fori_loop(0, n_groups, past_blocks, 0)
    for h in range(heads):
        acc = acc_ref[h]
        o_ref[:, h * hd:(h + 1) * hd] = (acc[:hd] / acc[hd:hd + 1]).T.astype(o_ref.dtype)


def _moba(proj, slopes):
    b, s, _ = proj.shape
    assert s % (MOBA_BLOCK * MOBA_GROUP) == 0
    nblk = s // MOBA_BLOCK
    hp = MOBA_HEADS_PER_STEP
    width = hp * HEAD_DIM
    q0 = 3 * DIL_WIDTH // width
    k0 = q0 + MOBA_WIDTH // width
    v0 = k0 + MOBA_WIDTH // width
    y = pl.pallas_call(
        functools.partial(_moba_body, nblk=nblk, group=MOBA_GROUP),
        grid=(b, MOBA_HEADS // hp, nblk),
        in_specs=[
            pl.BlockSpec(memory_space=pltpu.SMEM),
            pl.BlockSpec((None, MOBA_BLOCK, width), lambda bi, h, i: (bi, i, q0 + h)),
            pl.BlockSpec((None, s, width), lambda bi, h, i: (bi, 0, k0 + h)),
            pl.BlockSpec((None, s, width), lambda bi, h, i: (bi, 0, v0 + h)),
        ],
        out_specs=pl.BlockSpec((None, MOBA_BLOCK, width), lambda bi, h, i: (bi, i, h)),
        out_shape=jax.ShapeDtypeStruct((b, s, MOBA_WIDTH), BF16),
        scratch_shapes=[pltpu.VMEM((nblk, width), F32),
                        pltpu.VMEM((hp, HEAD_DIM + BF16_ROWS, s), BF16),
                        pltpu.VMEM((hp, MOBA_BLOCK, MOBA_BLOCK), F32),
                        pltpu.VMEM((hp, nblk, MOBA_BLOCK), F32),
                        pltpu.VMEM((hp, 1, MOBA_BLOCK), F32),
                        pltpu.VMEM((hp, HEAD_DIM + BF16_ROWS, MOBA_BLOCK), F32),
                        pltpu.VMEM((MOBA_GROUP * MOBA_BLOCK, MOBA_BLOCK), F32)],
        compiler_params=_params("parallel", "parallel", "arbitrary"),
        name="moba_attn",
    )(slopes, proj, proj, proj)
    return y.reshape(b * s, MOBA_WIDTH)


def _merged_branches(x, o_refs, l_refs, ym_ref, gd_ref, gm_ref, wud_ref, wum_ref, wo_ref):
    heads = []
    for h in range(DIL_HEADS_PER_GROUP):
        l0, l1, l2 = (l_ref[h] for l_ref in l_refs)
        m = jnp.maximum(jnp.maximum(l0, l1), l2)
        e0, e1, e2 = jnp.exp(l0 - m), jnp.exp(l1 - m), jnp.exp(l2 - m)
        den = e0 + e1 + e2
        mixed = ((e0 / den) * o_refs[0][h] + (e1 / den) * o_refs[1][h] + (e2 / den) * o_refs[2][h])
        heads.append(mixed.astype(BF16))
    y_dil = jnp.concatenate(heads, axis=-1)
    lift_dil = jnp.dot(y_dil, wud_ref[...], preferred_element_type=F32)
    lift_moba = jnp.dot(ym_ref[...], wum_ref[...], preferred_element_type=F32)
    merged = gd_ref[...].astype(F32) * lift_dil + gm_ref[...].astype(F32) * lift_moba
    return x + jnp.dot(merged.astype(BF16), wo_ref[...], preferred_element_type=F32)


def _cross_attended(x, g_ref, wq_ref, kv_ref, wo_ref):
    h = _rms(x, g_ref[...]).astype(BF16)
    q = jnp.dot(h, wq_ref[...], preferred_element_type=F32).astype(BF16)
    heads = []
    for hd in range(MEM_HEADS):
        k = kv_ref[:, hd * HEAD_DIM:(hd + 1) * HEAD_DIM]
        v = kv_ref[:, MEM_WIDTH + hd * HEAD_DIM:MEM_WIDTH + (hd + 1) * HEAD_DIM]
        s = lax.dot_general(q[:, hd * HEAD_DIM:(hd + 1) * HEAD_DIM], k, _NT,
                            preferred_element_type=F32) * SCALE
        p = jnp.exp(s - jnp.max(s, axis=-1, keepdims=True))
        den = jnp.sum(p, axis=-1, keepdims=True)
        heads.append((jnp.dot(p.astype(BF16), v, preferred_element_type=F32) / den).astype(BF16))
    o = jnp.concatenate(heads, axis=-1)
    return x + jnp.dot(o, wo_ref[...], preferred_element_type=F32)


def _merge_cross_body(x_ref, o0_ref, o1_ref, o2_ref, l0_ref, l1_ref, l2_ref, ym_ref, gd_ref, gm_ref,
                      wud_ref, wum_ref, wo_ref, gc_ref, wq_ref, kv_ref, wom_ref, out_ref):
    x1 = _merged_branches(x_ref[...], (o0_ref, o1_ref, o2_ref), (l0_ref, l1_ref, l2_ref),
                          ym_ref, gd_ref, gm_ref, wud_ref, wum_ref, wo_ref)
    out_ref[...] = _cross_attended(x1, gc_ref, wq_ref, kv_ref, wom_ref)


def _merge_cross(x, dil_outs, dil_lses, y_moba, gates, w_up_dil, w_up_moba, w_out,
                 g_cross, w_q, kv, w_o, *, tm):
    t, d = x.shape
    s = dil_outs[0].shape[2]
    mem_len = kv.shape[1]
    assert t % tm == 0 and s % tm == 0
    per_batch = s // tm
    row = lambda w: pl.BlockSpec((tm, w), lambda i: (i, 0))
    dil = pl.BlockSpec((None, DIL_HEADS_PER_GROUP, tm, HEAD_DIM),
                       lambda i: (i // per_batch, 0, i % per_batch, 0))
    return pl.pallas_call(
        _merge_cross_body,
        grid=(t // tm,),
        in_specs=[row(d)] + [dil] * 6 + [
            row(MOBA_WIDTH),
            pl.BlockSpec((tm, d), lambda i: (i, 0)),
            pl.BlockSpec((tm, d), lambda i: (i, 1)),
            _resident((DIL_OUT, d)), _resident((MOBA_WIDTH, d)), _resident((d, d)),
            _resident((1, d)), _resident((d, MEM_WIDTH)),
            pl.BlockSpec((None, mem_len, 2 * MEM_WIDTH), lambda i: (i // per_batch, 0, 0)),
            _resident((MEM_WIDTH, d)),
        ],
        out_specs=row(d),
        out_shape=jax.ShapeDtypeStruct((t, d), F32),
        compiler_params=_params("parallel"),
        name="merge_out_proj_cross_attn",
    )(x, *dil_outs, *dil_lses, y_moba, gates, gates, w_up_dil, w_up_moba, w_out,
      g_cross.reshape(1, d), w_q, kv, w_o)


ROUTE_COLS = 8
MOE_ROW_TILE = 256


def _route_body(x_ref, g_ref, wr_ref, br_ref, t_ref, info_ref, counts_ref, run_ref, tri_ref):
    tm = x_ref.shape[0]
    n_route = N_GROUPS + N_EXPERTS
    lane = lax.broadcasted_iota(jnp.int32, (tm, n_route), 1).astype(F32)

    @pl.when(pl.program_id(0) == 0)
    def _():
        run_ref[...] = jnp.zeros_like(run_ref)
        earlier = (lax.broadcasted_iota(jnp.int32, (tm, tm), 0)
                   > lax.broadcasted_iota(jnp.int32, (tm, tm), 1))
        tri_ref[...] = jnp.where(earlier, 1.0, 0.0).astype(BF16)

    t = _rms(x_ref[...], g_ref[...])
    t_ref[...] = t
    logits = _dot_split3(t, wr_ref[...]) + br_ref[...]
    none = float(n_route)
    glog = jnp.where(lane < N_GROUPS, logits, NEG_INF)
    gmax = jnp.max(glog, axis=-1, keepdims=True)
    gsel = jnp.min(jnp.where(glog == gmax, lane, none), axis=-1, keepdims=True)
    pg = 1.0 / jnp.sum(jnp.exp(glog - gmax), axis=-1, keepdims=True)
    first = N_GROUPS + gsel * EXPERTS_PER_GROUP
    in_group = (lane >= first) & (lane < first + EXPERTS_PER_GROUP)
    elog = jnp.where(in_group, logits, NEG_INF)
    top1 = jnp.max(elog, axis=-1, keepdims=True)
    i1 = jnp.min(jnp.where(elog == top1, lane, none), axis=-1, keepdims=True)
    rest = jnp.where(lane == i1, NEG_INF, elog)
    top2 = jnp.max(rest, axis=-1, keepdims=True)
    i2 = jnp.min(jnp.where(rest == top2, lane, none), axis=-1, keepdims=True)
    e2 = jnp.exp(top2 - top1)
    w1 = pg / (1.0 + e2)
    w2 = pg * e2 / (1.0 + e2)

    hit1 = lane == i1
    hit2 = lane == i2
    assigned = jnp.where(hit1 | hit2, 1.0, 0.0)
    before = jnp.dot(tri_ref[...], assigned.astype(BF16), preferred_element_type=F32) + run_ref[...]
    rank1 = jnp.sum(jnp.where(hit1, before, 0.0), axis=-1, keepdims=True)
    rank2 = jnp.sum(jnp.where(hit2, before, 0.0), axis=-1, keepdims=True)
    run_ref[...] += jnp.sum(assigned, axis=0, keepdims=True)
    counts_ref[...] = run_ref[...]

    col = lax.broadcasted_iota(jnp.int32, (tm, ROUTE_COLS), 1)
    fields = (i1 - N_GROUPS, i2 - N_GROUPS, rank1, rank2, w1, w2)
    info = jnp.zeros((tm, ROUTE_COLS), F32)
    for c, field in enumerate(fields):
        info = jnp.where(col == c, field, info)
    info_ref[...] = info


def _route(x, g, w_route, b_route, *, tm):
    t, d = x.shape
    n_route = N_GROUPS + N_EXPERTS
    assert t % tm == 0
    return pl.pallas_call(
        _route_body,
        grid=(t // tm,),
        in_specs=[
            pl.BlockSpec((tm, d), lambda i: (i, 0)),
            pl.BlockSpec((1, d), lambda i: (0, 0)),
            pl.BlockSpec((d, n_route), lambda i: (0, 0)),
            pl.BlockSpec((1, n_route), lambda i: (0, 0)),
        ],
        out_specs=[pl.BlockSpec((tm, d), lambda i: (i, 0)),
                   pl.BlockSpec((tm, ROUTE_COLS), lambda i: (i, 0)),
                   pl.BlockSpec((1, n_route), lambda i: (0, 0))],
        out_shape=[jax.ShapeDtypeStruct((t, d), F32),
                   jax.ShapeDtypeStruct((t, ROUTE_COLS), F32),
                   jax.ShapeDtypeStruct((1, n_route), F32)],
        scratch_shapes=[pltpu.VMEM((1, n_route), F32), pltpu.VMEM((tm, tm), BF16)],
        compiler_params=_params("arbitrary"),
        name="moe_route",
    )(x, g.reshape(1, d), w_route, b_route.reshape(1, n_route))


def _row_copies_wait(src_rows, dst_rows, sem):
    pltpu.make_async_copy(src_rows, dst_rows, sem).wait()


def _slot_tokens_body(pad_ref, pos_ref, src_ref):
    chunk = pos_ref.shape[1]
    base = pl.program_id(0) * chunk

    @pl.when(pl.program_id(0) == 0)
    def _():
        def pad_segment(e, carry):
            def clear(p, c):
                src_ref[p] = 0
                return c
            return lax.fori_loop(pad_ref[0, e], pad_ref[1, e], clear, carry)
        lax.fori_loop(0, pad_ref.shape[1], pad_segment, 0)

    def invert(j, carry):
        for k in range(MOE_TOPK):
            src_ref[pos_ref[k, j]] = base + j
        return carry

    lax.fori_loop(0, chunk, invert, 0, unroll=8)


def _slot_tokens(pos, pad_slots, n_rows, *, chunk):
    n_tok = pos.shape[1]
    assert n_tok % chunk == 0
    return pl.pallas_call(
        _slot_tokens_body,
        grid_spec=pltpu.PrefetchScalarGridSpec(
            num_scalar_prefetch=1,
            grid=(n_tok // chunk,),
            in_specs=[pl.BlockSpec((MOE_TOPK, chunk), lambda i, pad: (0, i),
                                   memory_space=pltpu.SMEM)],
            out_specs=pl.BlockSpec((n_rows,), lambda i, pad: (0,), memory_space=pltpu.SMEM),
        ),
        out_shape=jax.ShapeDtypeStruct((n_rows,), jnp.int32),
        compiler_params=_params("arbitrary"),
        name="moe_slot_tokens",
    )(pad_slots, pos)


EXPERT_FETCH_AHEAD = 2


def _experts_body(tile_expert_ref, n_tiles_ref, src_ref, t_ref, wg_ref, wu_ref, wd_ref, y_ref,
                  rows_ref, sems):
    del tile_expert_ref
    i = pl.program_id(0)
    slots, tr = rows_ref.shape[0], rows_ref.shape[1]
    last_tile = n_tiles_ref[0] - 1

    def fetch(step, unroll):
        tile = jnp.minimum(step, last_tile)
        into = lax.rem(step, slots)

        def start(r, carry):
            token = src_ref[tile * tr + r]
            pltpu.make_async_copy(t_ref.at[pl.ds(token, 1)], rows_ref.at[into, pl.ds(r, 1)],
                                  sems.at[into]).start()
            return carry
        lax.fori_loop(0, tr, start, 0, unroll=unroll)

    def fetched(step):
        into = lax.rem(step, slots)
        _row_copies_wait(t_ref.at[pl.ds(0, tr)], rows_ref.at[into], sems.at[into])

    @pl.when(i == 0)
    def _():
        for step in range(EXPERT_FETCH_AHEAD):
            fetch(step, 8)

    fetch(i + EXPERT_FETCH_AHEAD, True)
    fetched(i)
    t = rows_ref[lax.rem(i, slots)].astype(BF16)
    gate = jnp.dot(t, wg_ref[...].astype(BF16), preferred_element_type=F32)
    up = jnp.dot(t, wu_ref[...].astype(BF16), preferred_element_type=F32)
    a = jax.nn.silu(gate) * up
    y_ref[...] = jnp.dot(a.astype(BF16), wd_ref[...].astype(BF16), preferred_element_type=F32)

    @pl.when(i == pl.num_programs(0) - 1)
    def _():
        for ahead in range(1, EXPERT_FETCH_AHEAD + 1):
            fetched(i + ahead)


def _experts(t_rows, slot_tokens, tile_expert, n_tiles, w_gate, w_up, w_down):
    _, d = t_rows.shape
    n_rows = slot_tokens.shape[0]
    ff = w_gate.shape[-1]
    tr = MOE_ROW_TILE
    assert n_rows % tr == 0
    expert_of = lambda i, te, nt, src: te[jnp.minimum(i, nt[0] - 1)]
    return pl.pallas_call(
        _experts_body,
        grid_spec=pltpu.PrefetchScalarGridSpec(
            num_scalar_prefetch=3,
            grid=(n_rows // tr,),
            in_specs=[pl.BlockSpec(memory_space=pl.ANY),
                      pl.BlockSpec((None, d, ff), lambda *a: (expert_of(*a), 0, 0)),
                      pl.BlockSpec((None, d, ff), lambda *a: (expert_of(*a), 0, 0)),
                      pl.BlockSpec((None, ff, d), lambda *a: (expert_of(*a), 0, 0))],
            out_specs=pl.BlockSpec((tr, d), lambda i, *_: (i, 0)),
            scratch_shapes=[pltpu.VMEM((EXPERT_FETCH_AHEAD + 1, tr, d), F32),
                            pltpu.SemaphoreType.DMA((EXPERT_FETCH_AHEAD + 1,))],
        ),
        out_shape=jax.ShapeDtypeStruct((n_rows, d), F32),
        compiler_params=pltpu.CompilerParams(dimension_semantics=("arbitrary",),
                                             vmem_limit_bytes=VMEM_LIMIT_BYTES,
                                             disable_bounds_checks=True),
        name="moe_experts",
    )(tile_expert, n_tiles, slot_tokens, t_rows, w_gate, w_up, w_down)


def _combine_body(pos_ref, x_ref, info_ref, gf_ref, y_sorted_ref, out_ref, rows_ref, sem):
    tm = x_ref.shape[0]
    base = pl.program_id(0) * tm

    def issue(r, carry):
        for k in range(MOE_TOPK):
            src = pos_ref[k, base + r]
            pltpu.make_async_copy(y_sorted_ref.at[pl.ds(src, 1)], rows_ref.at[k, pl.ds(r, 1)],
                                  sem).start()
        return carry

    lax.fori_loop(0, tm, issue, 0, unroll=8)
    for k in range(MOE_TOPK):
        _row_copies_wait(y_sorted_ref.at[pl.ds(0, tm)], rows_ref.at[k], sem)

    info = info_ref[...]
    y = info[:, 4:5] * rows_ref[0] + info[:, 5:6] * rows_ref[1]
    out_ref[...] = _rms(x_ref[...] + y, gf_ref[...])


def _combine(x, info, pos, y_sorted, g_final, *, tm):
    t, d = x.shape
    assert t % tm == 0
    return pl.pallas_call(
        _combine_body,
        grid_spec=pltpu.PrefetchScalarGridSpec(
            num_scalar_prefetch=1,
            grid=(t // tm,),
            in_specs=[pl.BlockSpec((tm, d), lambda i, pos: (i, 0)),
                      pl.BlockSpec((tm, ROUTE_COLS), lambda i, pos: (i, 0)),
                      pl.BlockSpec((1, d), lambda i, pos: (0, 0)),
                      pl.BlockSpec(memory_space=pl.ANY)],
            out_specs=pl.BlockSpec((tm, d), lambda i, pos: (i, 0)),
            scratch_shapes=[pltpu.VMEM((MOE_TOPK, tm, d), F32), pltpu.SemaphoreType.DMA(())],
        ),
        out_shape=jax.ShapeDtypeStruct((t, d), F32),
        compiler_params=pltpu.CompilerParams(dimension_semantics=("arbitrary",),
                                             vmem_limit_bytes=VMEM_LIMIT_BYTES,
                                             disable_bounds_checks=True),
        name="moe_combine_final_norm",
    )(pos, x, info, g_final.reshape(1, d), y_sorted)


def _moe(x, g, w_route, b_route, w_gate, w_up, w_down, g_final):
    t, d = x.shape
    tr = MOE_ROW_TILE
    assert (MOE_TOPK * t) % tr == 0
    t_rows, info, counts = _route(x, g, w_route, b_route, tm=min(512, t))

    expert = info[:, 0:MOE_TOPK].astype(jnp.int32)
    rank = info[:, MOE_TOPK:2 * MOE_TOPK].astype(jnp.int32)
    count = counts[0, N_GROUPS:].astype(jnp.int32)
    seg_tiles = (count + (tr - 1)) // tr
    seg_end = jnp.cumsum(seg_tiles)
    seg_start_row = (seg_end - seg_tiles) * tr
    pos = (seg_start_row[expert] + rank).T
    max_tiles = (MOE_TOPK * t) // tr + N_EXPERTS
    tile_id = jnp.arange(max_tiles, dtype=jnp.int32)
    tile_expert = jnp.minimum(
        jnp.sum((seg_end[None, :] <= tile_id[:, None]).astype(jnp.int32), axis=1), N_EXPERTS - 1)
    n_tiles = seg_end[-1:].astype(jnp.int32)

    pad_slots = jnp.stack([jnp.append(seg_start_row + count, seg_end[-1] * tr),
                           jnp.append(seg_end * tr, max_tiles * tr)]).astype(jnp.int32)
    slot_tokens = _slot_tokens(pos, pad_slots, max_tiles * tr, chunk=min(2048, t))
    y_sorted = _experts(t_rows, slot_tokens, tile_expert, n_tiles, w_gate, w_up, w_down)
    return _combine(x, info, pos, y_sorted, g_final, tm=min(512, t))


def _layer(x, mem, attn_norm, w_in, w_up_dil, w_up_moba, w_branch_gate, w_out, cross_norm,
           mem_norm, w_q_mem, w_kv_mem, w_o_mem):
    b, s, d = x.shape
    t = b * s
    xt = x.reshape(t, d)
    h = _norm(xt, attn_norm, tm=min(512, t), name="attn_norm")
    proj, gates = _in_proj(h, w_in, w_branch_gate, tm=min(2048, t))
    proj = proj.reshape(b, s, IN_WIDTH)
    dil_slopes = _alibi_slopes(DIL_HEADS)
    dil = [_dilated_group(proj, dil_slopes, g, dilation)
           for g, (_, dilation) in enumerate(DIL_CONFIGS)]
    y_moba = _moba(proj, _alibi_slopes(MOBA_HEADS))
    mem_len = mem.shape[1]
    kv = _norm_matmul(mem.reshape(b * mem_len, d), mem_norm, w_kv_mem.astype(BF16),
                      tm=b * mem_len, tn=512, name="norm_mem_kv").reshape(b, mem_len, 2 * MEM_WIDTH)
    return _merge_cross(xt, [o for o, _ in dil], [l for _, l in dil], y_moba, gates,
                        w_up_dil.astype(BF16), w_up_moba.astype(BF16), w_out.astype(BF16),
                        cross_norm, w_q_mem.astype(BF16), kv, w_o_mem.astype(BF16), tm=min(256, s))


def kernel(x, mem, attn_norm, w_in, w_up_dil, w_up_moba, w_branch_gate, w_out, cross_norm, mem_norm,
           w_q_mem, w_kv_mem, w_o_mem, ffn_norm, w_router_group, b_router_group, w_router_expert,
           b_router_expert, w_expert_gate, w_expert_up, w_expert_down, final_norm):
    b, s, d = x.shape
    depth = attn_norm.shape[0]
    assert depth == 1, "the final norm is fused into the last layer's MoE call"
    l = 0
    x2 = _layer(x, mem, attn_norm[l], w_in[l], w_up_dil[l], w_up_moba[l], w_branch_gate[l], w_out[l],
                cross_norm[l], mem_norm[l], w_q_mem[l], w_kv_mem[l], w_o_mem[l])
    w_route = jnp.concatenate([w_router_group[l], w_router_expert[l]], axis=1)
    b_route = jnp.concatenate([b_router_group[l], b_router_expert[l]], axis=0)
    out = _moe(x2, ffn_norm[l], w_route, b_route, w_expert_gate[l], w_expert_up[l],
               w_expert_down[l], final_norm)
    return out.reshape(b, s, d)
```

```python
import functools

import numpy as np
import jax
import jax.numpy as jnp
from jax import lax
from jax.experimental import pallas as pl
from jax.experimental.pallas import tpu as pltpu

F32 = jnp.float32
BF16 = jnp.bfloat16

HEAD_DIM = 128
DIL_CONFIGS = ((128, 1), (512, 4), (2048, 16))
DIL_HEADS_PER_GROUP = 4
DIL_HEADS = DIL_HEADS_PER_GROUP * len(DIL_CONFIGS)
DIL_WIDTH = DIL_HEADS * HEAD_DIM
DIL_OUT = DIL_HEADS_PER_GROUP * HEAD_DIM
DIL_STEPS = 128
MOBA_HEADS = 8
MOBA_WIDTH = MOBA_HEADS * HEAD_DIM
MOBA_BLOCK = 256
MOBA_TOPK = 3
IN_WIDTH = 3 * (DIL_WIDTH + MOBA_WIDTH)
MEM_HEADS = 4
MEM_WIDTH = MEM_HEADS * HEAD_DIM
N_GROUPS = 4
EXPERTS_PER_GROUP = 8
N_EXPERTS = N_GROUPS * EXPERTS_PER_GROUP
MOE_TOPK = 2
RMS_EPS = 1e-6
SCALE = HEAD_DIM ** -0.5
NEG_INF = float("-inf")
LOG2E = 1.4426950408889634
MOBA_GROUP = 4
MOBA_HEADS_PER_STEP = 4
MOBA_Q_TILES = (9, 11)

VMEM_LIMIT_BYTES = 56 * 1024 * 1024
BF16_ROWS = 16
MXU_COLUMNS = 256
_NT = (((1,), (1,)), ((), ()))


def _alibi_slopes(n):
    return jnp.asarray(2.0 ** (-8.0 * np.arange(1, n + 1) / n), dtype=F32)


def _rms(x, g):
    return x * lax.rsqrt(jnp.mean(x * x, axis=-1, keepdims=True) + RMS_EPS) * g


def _dot_split3(a, b):
    a_hi = a.astype(BF16)
    b_hi = b.astype(BF16)
    a_lo = (a - a_hi.astype(F32)).astype(BF16)
    b_lo = (b - b_hi.astype(F32)).astype(BF16)
    dot = functools.partial(jnp.dot, preferred_element_type=F32)
    n = b.shape[1]
    if 2 * n <= MXU_COLUMNS:
        both = dot(a_hi, jnp.concatenate([b_hi, b_lo], axis=1))
        return both[:, :n] + (dot(a_lo, b_hi) + both[:, n:])
    return dot(a_hi, b_hi) + (dot(a_lo, b_hi) + dot(a_hi, b_lo))


def _params(*sem, flags=None):
    return pltpu.CompilerParams(dimension_semantics=sem, vmem_limit_bytes=VMEM_LIMIT_BYTES,
                                flags=flags)


def _resident(shape):
    nd = len(shape)
    return pl.BlockSpec(shape, lambda *_: (0,) * nd, pipeline_mode=pl.Buffered(1))


def _norm_matmul_body(x_ref, g_ref, w_ref, o_ref, h_ref, *, sigmoid):
    @pl.when(pl.program_id(1) == 0)
    def _():
        h_ref[...] = _rms(x_ref[...], g_ref[...]).astype(BF16)

    acc = jnp.dot(h_ref[...], w_ref[...], preferred_element_type=F32)
    if sigmoid:
        acc = jax.nn.sigmoid(acc)
    o_ref[...] = acc.astype(o_ref.dtype)


def _norm_matmul(x, g, w, *, tm, tn, sigmoid=False, name):
    m, d = x.shape
    n = w.shape[1]
    assert m % tm == 0 and n % tn == 0
    return pl.pallas_call(
        functools.partial(_norm_matmul_body, sigmoid=sigmoid),
        grid=(m // tm, n // tn),
        in_specs=[
            pl.BlockSpec((tm, d), lambda i, j: (i, 0)),
            pl.BlockSpec((1, d), lambda i, j: (0, 0)),
            pl.BlockSpec((d, tn), lambda i, j: (0, j)),
        ],
        out_specs=pl.BlockSpec((tm, tn), lambda i, j: (i, j)),
        out_shape=jax.ShapeDtypeStruct((m, n), BF16),
        scratch_shapes=[pltpu.VMEM((tm, d), BF16)],
        compiler_params=_params("parallel", "arbitrary"),
        name=name,
    )(x, g.reshape(1, d), w)


def _norm_body(x_ref, g_ref, o_ref):
    o_ref[...] = _rms(x_ref[...], g_ref[...]).astype(o_ref.dtype)


def _norm(x, g, *, tm, name):
    m, d = x.shape
    assert m % tm == 0
    return pl.pallas_call(
        _norm_body,
        grid=(m // tm,),
        in_specs=[pl.BlockSpec((tm, d), lambda i: (i, 0)), pl.BlockSpec((1, d), lambda i: (0, 0))],
        out_specs=pl.BlockSpec((tm, d), lambda i: (i, 0)),
        out_shape=jax.ShapeDtypeStruct((m, d), BF16),
        compiler_params=_params("parallel"),
        name=name,
    )(x, g.reshape(1, d))


def _in_proj_body(h_ref, win_ref, wbg_ref, proj_ref, gates_ref, perm_ref, *, n_proj_tiles):
    j = pl.program_id(1)
    tm = h_ref.shape[0]
    n = DIL_STEPS
    groups = len(DIL_CONFIGS)

    def store_regrouped(acc, dilation):
        tile = n * dilation
        for c in range(acc.shape[1] // HEAD_DIM):
            cols = slice(c * HEAD_DIM, (c + 1) * HEAD_DIM)
            perm_ref[c] = acc[:, cols]
            for t0 in range(0, tm, tile):
                for r in range(dilation):
                    rows = perm_ref[c, pl.ds(t0 + r, n, stride=dilation), :]
                    proj_ref[t0 + r * n:t0 + (r + 1) * n, cols] = rows.astype(BF16)

    @pl.when(j < n_proj_tiles)
    def _():
        acc = jnp.dot(h_ref[...], win_ref[...].astype(BF16), preferred_element_type=F32)
        group = jnp.where(j < 3 * groups, lax.rem(j, groups), 0)
        moba_q = (j >= MOBA_Q_TILES[0]) & (j < MOBA_Q_TILES[1])
        scale = jnp.where(moba_q, SCALE * LOG2E, 1.0)

        def store_natural():
            proj_ref[...] = (acc * scale).astype(BF16)

        for g, (_, dilation) in enumerate(DIL_CONFIGS):
            store = store_natural if dilation == 1 else functools.partial(store_regrouped, acc, dilation)
            pl.when(group == g)(store)

    @pl.when(j >= n_proj_tiles)
    def _():
        acc = jnp.dot(h_ref[...], wbg_ref[...].astype(BF16), preferred_element_type=F32)
        gates_ref[...] = jax.nn.sigmoid(acc).astype(BF16)


def _in_proj(h, w_in, w_bg, *, tm):
    t, d = h.shape
    tn = DIL_OUT
    assert MOBA_Q_TILES == (3 * DIL_WIDTH // tn, (3 * DIL_WIDTH + MOBA_WIDTH) // tn)
    assert t % tm == 0 and tm % (DIL_STEPS * max(dl for _, dl in DIL_CONFIGS)) == 0
    assert DIL_WIDTH == len(DIL_CONFIGS) * tn and w_in.shape[1] % tn == 0 and w_bg.shape[1] % tn == 0
    n_proj = w_in.shape[1] // tn
    n_gate = w_bg.shape[1] // tn
    return pl.pallas_call(
        functools.partial(_in_proj_body, n_proj_tiles=n_proj),
        grid=(t // tm, n_proj + n_gate),
        in_specs=[
            pl.BlockSpec((tm, d), lambda i, j: (i, 0)),
            pl.BlockSpec((d, tn), lambda i, j: (0, jnp.minimum(j, n_proj - 1))),
            pl.BlockSpec((d, tn), lambda i, j: (0, jnp.maximum(j - n_proj, 0))),
        ],
        out_specs=[
            pl.BlockSpec((tm, tn), lambda i, j: (i, jnp.minimum(j, n_proj - 1))),
            pl.BlockSpec((tm, tn), lambda i, j: (i, jnp.maximum(j - n_proj, 0))),
        ],
        out_shape=[jax.ShapeDtypeStruct((t, w_in.shape[1]), BF16),
                   jax.ShapeDtypeStruct((t, w_bg.shape[1]), BF16)],
        scratch_shapes=[pltpu.VMEM((tn // HEAD_DIM, tm, HEAD_DIM), F32)],
        compiler_params=_params("parallel", "arbitrary"),
        name="in_proj_gates",
    )(h, w_in, w_bg)


DIL_SUBBLOCKS = 4


def _dilated_body(slope_ref, q_ref, kp_ref, kc_ref, vp_ref, vc_ref, o_ref, lse_ref, *scratch,
                  dilation, group):
    n = DIL_STEPS
    first_tile = pl.program_id(1) == 0
    chunk = pl.program_id(2)
    chunks = max(dilation // DIL_SUBBLOCKS, 1)
    qi = lax.broadcasted_iota(jnp.int32, (n, n), 0)
    kj = lax.broadcasted_iota(jnp.int32, (n, n), 1)
    steps_cur = qi - kj
    valid_cur = steps_cur >= 0
    dist_cur = (steps_cur * dilation).astype(F32)
    dist_prev = ((steps_cur + n) * dilation).astype(F32)
    limit_across_tiles = jnp.where(first_tile, -n, 0)
    heads = range(DIL_HEADS_PER_GROUP)
    cols_of = [slice(h * HEAD_DIM, (h + 1) * HEAD_DIM) for h in heads]
    rows_of = [slice(s * n, (s + 1) * n) for s in range(DIL_SUBBLOCKS)]

    def prev_rows(s):
        if dilation == 1:
            return (slice(0, n), False, limit_across_tiles) if s == 0 else (rows_of[s - 1], True, 0)
        return rows_of[s], False, limit_across_tiles

    def scores(s):
        rows, from_cur, _ = prev_rows(s)
        out = []
        for h in heads:
            q = q_ref[rows_of[s], cols_of[h]]
            k_prev = (kc_ref if from_cur else kp_ref)[rows, cols_of[h]]
            out.append((lax.dot_general(q, kc_ref[rows_of[s], cols_of[h]], _NT,
                                        preferred_element_type=F32),
                        lax.dot_general(q, k_prev, _NT, preferred_element_type=F32)))
        return out

    def softmax(s, raw):
        valid_prev = steps_cur <= prev_rows(s)[2]
        out = []
        for h in heads:
            slope = slope_ref[group * DIL_HEADS_PER_GROUP + h]
            s_cur = jnp.where(valid_cur, raw[h][0] * SCALE - slope * dist_cur, NEG_INF)
            s_prev = jnp.where(valid_prev, raw[h][1] * SCALE - slope * dist_prev, NEG_INF)
            m = jnp.maximum(jnp.max(s_cur, axis=-1, keepdims=True),
                            jnp.max(s_prev, axis=-1, keepdims=True))
            p_cur = jnp.exp(s_cur - m)
            p_prev = jnp.exp(s_prev - m)
            den = jnp.sum(p_cur, axis=-1, keepdims=True) + jnp.sum(p_prev, axis=-1, keepdims=True)
            out.append((p_cur.astype(BF16), p_prev.astype(BF16), m, den))
        return out

    def values(s, probs):
        rows, from_cur, _ = prev_rows(s)
        for h in heads:
            p_cur, p_prev, m, den = probs[h]
            v_prev = (vc_ref if from_cur else vp_ref)[rows, cols_of[h]]
            o = (jnp.dot(p_cur, vc_ref[rows_of[s], cols_of[h]], preferred_element_type=F32)
                 + jnp.dot(p_prev, v_prev, preferred_element_type=F32)) / den
            lse = jnp.broadcast_to(m + jnp.log(den), (n, HEAD_DIM))
            if dilation == 1:
                dst_o, dst_lse, rows_out = o_ref, lse_ref, rows_of[s]
            elif chunks == 1:
                dst_o, dst_lse, rows_out = o_ref, lse_ref, pl.ds(s, n, stride=dilation)
            else:
                dst_o, dst_lse = scratch
                rows_out = pl.ds(pl.multiple_of((chunk * DIL_SUBBLOCKS + s) * n, n), n)
            dst_o[h, rows_out, :] = o
            dst_lse[h, rows_out, :] = lse

    raw, probs = {0: scores(0)}, {}
    for s in range(DIL_SUBBLOCKS + 1):
        if s + 1 < DIL_SUBBLOCKS:
            raw[s + 1] = scores(s + 1)
        if s < DIL_SUBBLOCKS:
            probs[s] = softmax(s, raw.pop(s))
        if s >= 1:
            values(s - 1, probs.pop(s - 1))

    if chunks > 1:
        o_tile, lse_tile = scratch

        @pl.when(chunk == chunks - 1)
        def _():
            for h in heads:
                for r in range(dilation):
                    natural = pl.ds(r, n, stride=dilation)
                    o_ref[h, natural, :] = o_tile[h, r * n:(r + 1) * n, :]
                    lse_ref[h, natural, :] = lse_tile[h, r * n:(r + 1) * n, :]


def _dilated_group(proj, slopes, group, dilation):
    b, s, _ = proj.shape
    n = DIL_STEPS
    step_rows = DIL_SUBBLOCKS * n
    assert dilation == 1 or dilation % DIL_SUBBLOCKS == 0
    chunks = max(dilation // DIL_SUBBLOCKS, 1)
    tile = step_rows * chunks
    assert s % tile == 0
    groups = len(DIL_CONFIGS)
    prev_rows = n if dilation == 1 else step_rows
    blocks_per_step = step_rows // prev_rows

    def cur(section):
        return pl.BlockSpec((None, step_rows, DIL_OUT),
                            lambda bi, i, c: (bi, i * chunks + c, section * groups + group))

    def prev(section):
        def index(bi, i, c):
            if dilation == 1:
                return (bi, jnp.maximum(i * blocks_per_step - 1, 0), section * groups + group)
            return (bi, jnp.maximum(i - 1, 0) * chunks + c, section * groups + group)
        return pl.BlockSpec((None, prev_rows, DIL_OUT), index)

    out_block = (DIL_HEADS_PER_GROUP, tile, HEAD_DIM)
    out_spec = pl.BlockSpec((None,) + out_block, lambda bi, i, c: (bi, 0, i, 0))
    out_sds = jax.ShapeDtypeStruct((b, DIL_HEADS_PER_GROUP, s, HEAD_DIM), F32)
    scratch = [pltpu.VMEM(out_block, F32)] * 2 if chunks > 1 else []
    o, lse = pl.pallas_call(
        functools.partial(_dilated_body, dilation=dilation, group=group),
        grid=(b, s // tile, chunks),
        in_specs=[pl.BlockSpec(memory_space=pltpu.SMEM),
                  cur(0), prev(1), cur(1), prev(2), cur(2)],
        out_specs=[out_spec, out_spec],
        out_shape=[out_sds, out_sds],
        scratch_shapes=scratch,
        compiler_params=_params("parallel", "arbitrary", "arbitrary"),
        name=f"dilated_attn_g{group}",
    )(slopes, proj, proj, proj, proj, proj)
    return o, lse


def _moba_body(slope_ref, q_ref, k_ref, v_ref, o_ref, kmean_ref, vt_ref, bias_ref, sel_ref,
               m_ref, acc_ref, ahead_ref, *, nblk, group):
    blk = MOBA_BLOCK
    hd = HEAD_DIM
    heads = q_ref.shape[1] // hd
    own = pl.program_id(2)
    key_off = lax.broadcasted_iota(jnp.int32, (blk, blk), 0)
    qry_off = lax.broadcasted_iota(jnp.int32, (blk, blk), 1)
    slope2 = [slope_ref[pl.program_id(1) * heads + h] * LOG2E for h in range(heads)]

    @pl.when(own == 0)
    def _():
        def fill(jb, carry):
            start = pl.multiple_of(jb * blk, blk)
            rows = k_ref[pl.ds(start, blk), :].astype(F32)
            kmean_ref[pl.ds(jb, 1), :] = jnp.mean(rows, axis=0, keepdims=True)
            vrows = v_ref[pl.ds(start, blk), :].astype(F32)
            for h in range(heads):
                vt_ref[h, :hd, pl.ds(start, blk)] = vrows[:, h * hd:(h + 1) * hd].T.astype(BF16)
            return carry
        lax.fori_loop(0, nblk, fill, 0)
        for h in range(heads):
            vt_ref[h, hd:, :] = jnp.ones((vt_ref.shape[1] - hd, vt_ref.shape[2]), BF16)
        for h in range(heads):
            bias_ref[h] = -slope2[h] * (qry_off - key_off).astype(F32)

    q = [q_ref[:, h * hd:(h + 1) * hd] for h in range(heads)]

    blk_id = lax.broadcasted_iota(jnp.int32, (nblk, blk), 0).astype(F32)
    gates = []
    for h in range(heads):
        gate = lax.dot_general(kmean_ref[:, h * hd:(h + 1) * hd], q[h].astype(F32), _NT,
                               precision=lax.Precision.HIGHEST, preferred_element_type=F32)
        gates.append(jnp.where(blk_id < own.astype(F32), gate, NEG_INF))

    def scores(h, start, rows):
        k = k_ref[pl.ds(start, rows), h * hd:(h + 1) * hd]
        return lax.dot_general(k, q[h], _NT, preferred_element_type=F32)

    own_start = pl.multiple_of(own * blk, blk)
    own_x = [scores(h, own_start, blk) for h in range(heads)]

    sels = [jnp.zeros((nblk, blk), F32) for _ in range(heads)]
    for _ in range(MOBA_TOPK):
        for h in range(heads):
            best = jnp.max(gates[h], axis=0, keepdims=True)
            is_best = (gates[h] == best) & (gates[h] > NEG_INF)
            pick = jnp.min(jnp.where(is_best, blk_id, float(nblk)), axis=0, keepdims=True)
            picked = blk_id == pick
            sels[h] = jnp.where(picked, 1.0, sels[h])
            gates[h] = jnp.where(picked, NEG_INF, gates[h])
    for h in range(heads):
        sel_ref[h] = sels[h]

    own_p = []
    for h in range(heads):
        x = jnp.where(qry_off >= key_off, own_x[h] + bias_ref[h], NEG_INF)
        m0 = jnp.max(x, axis=0, keepdims=True)
        own_p.append((m0, jnp.exp2(x - m0).astype(BF16)))
    for h in range(heads):
        m0, p = own_p[h]
        m_ref[h] = m0
        acc_ref[h] = jnp.dot(vt_ref[h, :, pl.ds(own_start, blk)], p, preferred_element_type=F32)

    def softmax_update(h, i, x, m):
        xs, chosen, shift = [], [], []
        m_new = m
        for g in range(group):
            j = i * group + g
            xs.append(x[g * blk:(g + 1) * blk] + bias_ref[h])
            chosen.append(sel_ref[h, pl.ds(j, 1), :] > 0.5)
            shift.append(-slope2[h] * ((own - j) * blk).astype(F32))
            top = jnp.max(xs[g], axis=0, keepdims=True) + shift[g]
            m_new = jnp.maximum(m_new, jnp.where(chosen[g], top, NEG_INF))
        ps = []
        for g in range(group):
            ref_g = jnp.where(chosen[g], m_new - shift[g], jnp.inf)
            ps.append(jnp.exp2(xs[g] - ref_g).astype(BF16))
        return m_new, jnp.exp2(m - m_new), jnp.concatenate(ps, axis=0)

    def past_blocks(i, carry):
        start = pl.multiple_of(i * (group * blk), group * blk)
        x, sm = {0: ahead_ref[...]}, {}
        for h in range(heads + 1):
            if h + 1 < heads:
                x[h + 1] = scores(h + 1, start, group * blk)
            elif h + 1 == heads:
                ahead_ref[...] = scores(0, group_start(i + 1), group * blk)
            if h < heads:
                sm[h] = softmax_update(h, i, x[h], m_ref[h])
            if h >= 1:
                m_new, alpha, p = sm[h - 1]
                m_ref[h - 1] = m_new
                acc_ref[h - 1] = alpha * acc_ref[h - 1] + jnp.dot(
                    vt_ref[h - 1, :, pl.ds(start, group * blk)], p, preferred_element_type=F32)
        return carry

    def group_start(i):
        return pl.multiple_of(jnp.minimum(i, nblk // group - 1) * (group * blk), group * blk)

    n_groups = lax.div(own + (group - 1), group)
    ahead_ref[...] = scores(0, group_start(0), group * blk)
    lax.fori_loop(0, n_groups, past_blocks, 0)
    for h in range(heads):
        acc = acc_ref[h]
        o_ref[:, h * hd:(h + 1) * hd] = (acc[:hd] / acc[hd:hd + 1]).T.astype(o_ref.dtype)


def _moba(proj, slopes):
    b, s, _ = proj.shape
    assert s % (MOBA_BLOCK * MOBA_GROUP) == 0
    nblk = s // MOBA_BLOCK
    hp = MOBA_HEADS_PER_STEP
    width = hp * HEAD_DIM
    q0 = 3 * DIL_WIDTH // width
    k0 = q0 + MOBA_WIDTH // width
    v0 = k0 + MOBA_WIDTH // width
    y = pl.pallas_call(
        functools.partial(_moba_body, nblk=nblk, group=MOBA_GROUP),
        grid=(b, MOBA_HEADS // hp, nblk),
        in_specs=[
            pl.BlockSpec(memory_space=pltpu.SMEM),
            pl.BlockSpec((None, MOBA_BLOCK, width), lambda bi, h, i: (bi, i, q0 + h)),
            pl.BlockSpec((None, s, width), lambda bi, h, i: (bi, 0, k0 + h)),
            pl.BlockSpec((None, s, width), lambda bi, h, i: (bi, 0, v0 + h)),
        ],
        out_specs=pl.BlockSpec((None, MOBA_BLOCK, width), lambda bi, h, i: (bi, i, h)),
        out_shape=jax.ShapeDtypeStruct((b, s, MOBA_WIDTH), BF16),
        scratch_shapes=[pltpu.VMEM((nblk, width), F32),
                        pltpu.VMEM((hp, HEAD_DIM + BF16_ROWS, s), BF16),
                        pltpu.VMEM((hp, MOBA_BLOCK, MOBA_BLOCK), F32),
                        pltpu.VMEM((hp, nblk, MOBA_BLOCK), F32),
                        pltpu.VMEM((hp, 1, MOBA_BLOCK), F32),
                        pltpu.VMEM((hp, HEAD_DIM + BF16_ROWS, MOBA_BLOCK), F32),
                        pltpu.VMEM((MOBA_GROUP * MOBA_BLOCK, MOBA_BLOCK), F32)],
        compiler_params=_params("parallel", "parallel", "arbitrary"),
        name="moba_attn",
    )(slopes, proj, proj, proj)
    return y.reshape(b * s, MOBA_WIDTH)


def _merged_branches(x, o_refs, l_refs, ym_ref, gd_ref, gm_ref, wud_ref, wum_ref, wo_ref):
    heads = []
    for h in range(DIL_HEADS_PER_GROUP):
        l0, l1, l2 = (l_ref[h] for l_ref in l_refs)
        m = jnp.maximum(jnp.maximum(l0, l1), l2)
        e0, e1, e2 = jnp.exp(l0 - m), jnp.exp(l1 - m), jnp.exp(l2 - m)
        den = e0 + e1 + e2
        mixed = ((e0 / den) * o_refs[0][h] + (e1 / den) * o_refs[1][h] + (e2 / den) * o_refs[2][h])
        heads.append(mixed.astype(BF16))
    y_dil = jnp.concatenate(heads, axis=-1)
    lift_dil = jnp.dot(y_dil, wud_ref[...], preferred_element_type=F32)
    lift_moba = jnp.dot(ym_ref[...], wum_ref[...], preferred_element_type=F32)
    merged = gd_ref[...].astype(F32) * lift_dil + gm_ref[...].astype(F32) * lift_moba
    return x + jnp.dot(merged.astype(BF16), wo_ref[...], preferred_element_type=F32)


def _cross_attended(x, g_ref, wq_ref, kv_ref, wo_ref):
    h = _rms(x, g_ref[...]).astype(BF16)
    q = jnp.dot(h, wq_ref[...], preferred_element_type=F32).astype(BF16)
    heads = []
    for hd in range(MEM_HEADS):
        k = kv_ref[:, hd * HEAD_DIM:(hd + 1) * HEAD_DIM]
        v = kv_ref[:, MEM_WIDTH + hd * HEAD_DIM:MEM_WIDTH + (hd + 1) * HEAD_DIM]
        s = lax.dot_general(q[:, hd * HEAD_DIM:(hd + 1) * HEAD_DIM], k, _NT,
                            preferred_element_type=F32) * SCALE
        p = jnp.exp(s - jnp.max(s, axis=-1, keepdims=True))
        den = jnp.sum(p, axis=-1, keepdims=True)
        heads.append((jnp.dot(p.astype(BF16), v, preferred_element_type=F32) / den).astype(BF16))
    o = jnp.concatenate(heads, axis=-1)
    return x + jnp.dot(o, wo_ref[...], preferred_element_type=F32)


def _merge_cross_body(x_ref, o0_ref, o1_ref, o2_ref, l0_ref, l1_ref, l2_ref, ym_ref, gd_ref, gm_ref,
                      wud_ref, wum_ref, wo_ref, gc_ref, wq_ref, kv_ref, wom_ref, out_ref):
    x1 = _merged_branches(x_ref[...], (o0_ref, o1_ref, o2_ref), (l0_ref, l1_ref, l2_ref),
                          ym_ref, gd_ref, gm_ref, wud_ref, wum_ref, wo_ref)
    out_ref[...] = _cross_attended(x1, gc_ref, wq_ref, kv_ref, wom_ref)


def _merge_cross(x, dil_outs, dil_lses, y_moba, gates, w_up_dil, w_up_moba, w_out,
                 g_cross, w_q, kv, w_o, *, tm):
    t, d = x.shape
    s = dil_outs[0].shape[2]
    mem_len = kv.shape[1]
    assert t % tm == 0 and s % tm == 0
    per_batch = s // tm
    row = lambda w: pl.BlockSpec((tm, w), lambda i: (i, 0))
    dil = pl.BlockSpec((None, DIL_HEADS_PER_GROUP, tm, HEAD_DIM),
                       lambda i: (i // per_batch, 0, i % per_batch, 0))
    return pl.pallas_call(
        _merge_cross_body,
        grid=(t // tm,),
        in_specs=[row(d)] + [dil] * 6 + [
            row(MOBA_WIDTH),
            pl.BlockSpec((tm, d), lambda i: (i, 0)),
            pl.BlockSpec((tm, d), lambda i: (i, 1)),
            _resident((DIL_OUT, d)), _resident((MOBA_WIDTH, d)), _resident((d, d)),
            _resident((1, d)), _resident((d, MEM_WIDTH)),
            pl.BlockSpec((None, mem_len, 2 * MEM_WIDTH), lambda i: (i // per_batch, 0, 0)),
            _resident((MEM_WIDTH, d)),
        ],
        out_specs=row(d),
        out_shape=jax.ShapeDtypeStruct((t, d), F32),
        compiler_params=_params("parallel"),
        name="merge_out_proj_cross_attn",
    )(x, *dil_outs, *dil_lses, y_moba, gates, gates, w_up_dil, w_up_moba, w_out,
      g_cross.reshape(1, d), w_q, kv, w_o)


ROUTE_COLS = 8
MOE_ROW_TILE = 256


def _route_body(x_ref, g_ref, wr_ref, br_ref, info_ref, counts_ref, run_ref, tri_ref):
    tm = x_ref.shape[0]
    n_route = N_GROUPS + N_EXPERTS
    lane = lax.broadcasted_iota(jnp.int32, (tm, n_route), 1).astype(F32)

    @pl.when(pl.program_id(0) == 0)
    def _():
        run_ref[...] = jnp.zeros_like(run_ref)
        earlier = (lax.broadcasted_iota(jnp.int32, (tm, tm), 0)
                   > lax.broadcasted_iota(jnp.int32, (tm, tm), 1))
        tri_ref[...] = jnp.where(earlier, 1.0, 0.0).astype(BF16)

    t = _rms(x_ref[...], g_ref[...])
    logits = _dot_split3(t, wr_ref[...]) + br_ref[...]
    none = float(n_route)
    glog = jnp.where(lane < N_GROUPS, logits, NEG_INF)
    gmax = jnp.max(glog, axis=-1, keepdims=True)
    gsel = jnp.min(jnp.where(glog == gmax, lane, none), axis=-1, keepdims=True)
    pg = 1.0 / jnp.sum(jnp.exp(glog - gmax), axis=-1, keepdims=True)
    first = N_GROUPS + gsel * EXPERTS_PER_GROUP
    in_group = (lane >= first) & (lane < first + EXPERTS_PER_GROUP)
    elog = jnp.where(in_group, logits, NEG_INF)
    top1 = jnp.max(elog, axis=-1, keepdims=True)
    i1 = jnp.min(jnp.where(elog == top1, lane, none), axis=-1, keepdims=True)
    rest = jnp.where(lane == i1, NEG_INF, elog)
    top2 = jnp.max(rest, axis=-1, keepdims=True)
    i2 = jnp.min(jnp.where(rest == top2, lane, none), axis=-1, keepdims=True)
    e2 = jnp.exp(top2 - top1)
    w1 = pg / (1.0 + e2)
    w2 = pg * e2 / (1.0 + e2)

    hit1 = lane == i1
    hit2 = lane == i2
    assigned = jnp.where(hit1 | hit2, 1.0, 0.0)
    before = jnp.dot(tri_ref[...], assigned.astype(BF16), preferred_element_type=F32) + run_ref[...]
    rank1 = jnp.sum(jnp.where(hit1, before, 0.0), axis=-1, keepdims=True)
    rank2 = jnp.sum(jnp.where(hit2, before, 0.0), axis=-1, keepdims=True)
    run_ref[...] += jnp.sum(assigned, axis=0, keepdims=True)
    counts_ref[...] = run_ref[...]

    col = lax.broadcasted_iota(jnp.int32, (tm, ROUTE_COLS), 1)
    fields = (i1 - N_GROUPS, i2 - N_GROUPS, rank1, rank2, w1, w2)
    info = jnp.zeros((tm, ROUTE_COLS), F32)
    for c, field in enumerate(fields):
        info = jnp.where(col == c, field, info)
    info_ref[...] = info


def _route(x, g, w_route, b_route, *, tm):
    t, d = x.shape
    n_route = N_GROUPS + N_EXPERTS
    assert t % tm == 0
    return pl.pallas_call(
        _route_body,
        grid=(t // tm,),
        in_specs=[
            pl.BlockSpec((tm, d), lambda i: (i, 0)),
            pl.BlockSpec((1, d), lambda i: (0, 0)),
            pl.BlockSpec((d, n_route), lambda i: (0, 0)),
            pl.BlockSpec((1, n_route), lambda i: (0, 0)),
        ],
        out_specs=[pl.BlockSpec((tm, ROUTE_COLS), lambda i: (i, 0)),
                   pl.BlockSpec((1, n_route), lambda i: (0, 0))],
        out_shape=[jax.ShapeDtypeStruct((t, ROUTE_COLS), F32),
                   jax.ShapeDtypeStruct((1, n_route), F32)],
        scratch_shapes=[pltpu.VMEM((1, n_route), F32), pltpu.VMEM((tm, tm), BF16)],
        compiler_params=_params("arbitrary"),
        name="moe_route",
    )(x, g.reshape(1, d), w_route, b_route.reshape(1, n_route))


def _row_copies_wait(src_rows, dst_rows, sem):
    pltpu.make_async_copy(src_rows, dst_rows, sem).wait()


def _slot_block(tm):
    return pl.BlockSpec((MOE_TOPK, tm), lambda i, *_: (0, i), memory_space=pltpu.SMEM)


def _dispatch_body(fill_ref, pos_ref, x_ref, g_ref, sorted_ref, t_ref, zero_ref, sem, fill_sem):
    tm = x_ref.shape[0]
    tr = zero_ref.shape[0]

    @pl.when(pl.program_id(0) == 0)
    def _():
        zero_ref[...] = jnp.zeros_like(zero_ref)

        def fill_copy(tile):
            return pltpu.make_async_copy(zero_ref, sorted_ref.at[pl.ds(tile * tr, tr)], fill_sem)

        def start(idx, carry):
            pl.when(fill_ref[idx] >= 0)(lambda: fill_copy(fill_ref[idx]).start())
            return carry

        def finish(idx, carry):
            pl.when(fill_ref[idx] >= 0)(lambda: fill_copy(fill_ref[idx]).wait())
            return carry

        lax.fori_loop(0, fill_ref.shape[0], start, 0)
        lax.fori_loop(0, fill_ref.shape[0], finish, 0)

    t_ref[...] = _rms(x_ref[...], g_ref[...])

    def issue(r, carry):
        for k in range(MOE_TOPK):
            dst = pos_ref[k, r]
            pltpu.make_async_copy(t_ref.at[pl.ds(r, 1)], sorted_ref.at[pl.ds(dst, 1)], sem).start()
        return carry

    lax.fori_loop(0, tm, issue, 0, unroll=8)
    for _ in range(MOE_TOPK):
        _row_copies_wait(t_ref, sorted_ref.at[pl.ds(0, tm)], sem)


def _dispatch(x, g, pos, fill_tiles, n_rows, *, tm):
    t, d = x.shape
    assert t % tm == 0
    return pl.pallas_call(
        _dispatch_body,
        grid_spec=pltpu.PrefetchScalarGridSpec(
            num_scalar_prefetch=1,
            grid=(t // tm,),
            in_specs=[_slot_block(tm),
                      pl.BlockSpec((tm, d), lambda i, fill: (i, 0)),
                      pl.BlockSpec((1, d), lambda i, fill: (0, 0))],
            out_specs=pl.BlockSpec(memory_space=pl.ANY),
            scratch_shapes=[pltpu.VMEM((tm, d), F32), pltpu.VMEM((MOE_ROW_TILE, d), F32),
                            pltpu.SemaphoreType.DMA(()), pltpu.SemaphoreType.DMA(())],
        ),
        out_shape=jax.ShapeDtypeStruct((n_rows, d), F32),
        compiler_params=pltpu.CompilerParams(dimension_semantics=("arbitrary",),
                                             vmem_limit_bytes=VMEM_LIMIT_BYTES,
                                             disable_bounds_checks=True),
        name="moe_dispatch",
    )(fill_tiles, pos, x, g.reshape(1, d))


def _experts_body(tile_expert_ref, n_tiles_ref, x_ref, wg_ref, wu_ref, wd_ref, y_ref):
    del tile_expert_ref
    in_use = pl.program_id(0) < n_tiles_ref[0]

    @pl.when(in_use)
    def _():
        t = x_ref[...].astype(BF16)
        gate = jnp.dot(t, wg_ref[...].astype(BF16), preferred_element_type=F32)
        up = jnp.dot(t, wu_ref[...].astype(BF16), preferred_element_type=F32)
        a = jax.nn.silu(gate) * up
        y_ref[...] = jnp.dot(a.astype(BF16), wd_ref[...].astype(BF16), preferred_element_type=F32)

    @pl.when(jnp.logical_not(in_use))
    def _():
        y_ref[...] = jnp.zeros_like(y_ref)


def _experts(sorted_rows, tile_expert, n_tiles, w_gate, w_up, w_down):
    p, d = sorted_rows.shape
    ff = w_gate.shape[-1]
    tr = MOE_ROW_TILE
    assert p % tr == 0
    used = lambda i, nt: jnp.minimum(i, nt[0] - 1)
    return pl.pallas_call(
        _experts_body,
        grid_spec=pltpu.PrefetchScalarGridSpec(
            num_scalar_prefetch=2,
            grid=(p // tr,),
            in_specs=[pl.BlockSpec((tr, d), lambda i, te, nt: (used(i, nt), 0)),
                      pl.BlockSpec((None, d, ff), lambda i, te, nt: (te[used(i, nt)], 0, 0)),
                      pl.BlockSpec((None, d, ff), lambda i, te, nt: (te[used(i, nt)], 0, 0)),
                      pl.BlockSpec((None, ff, d), lambda i, te, nt: (te[used(i, nt)], 0, 0))],
            out_specs=pl.BlockSpec((tr, d), lambda i, te, nt: (i, 0)),
        ),
        out_shape=jax.ShapeDtypeStruct((p, d), F32),
        compiler_params=_params("arbitrary"),
        name="moe_experts",
    )(tile_expert, n_tiles, sorted_rows, w_gate, w_up, w_down)


def _combine_body(pos_ref, x_ref, info_ref, gf_ref, y_sorted_ref, out_ref, rows_ref, sem):
    tm = x_ref.shape[0]

    def issue(r, carry):
        for k in range(MOE_TOPK):
            src = pos_ref[k, r]
            pltpu.make_async_copy(y_sorted_ref.at[pl.ds(src, 1)], rows_ref.at[k, pl.ds(r, 1)],
                                  sem).start()
        return carry

    lax.fori_loop(0, tm, issue, 0, unroll=8)
    for k in range(MOE_TOPK):
        _row_copies_wait(y_sorted_ref.at[pl.ds(0, tm)], rows_ref.at[k], sem)

    info = info_ref[...]
    y = info[:, 4:5] * rows_ref[0] + info[:, 5:6] * rows_ref[1]
    out_ref[...] = _rms(x_ref[...] + y, gf_ref[...])


def _combine(x, info, pos, y_sorted, g_final, *, tm):
    t, d = x.shape
    assert t % tm == 0
    return pl.pallas_call(
        _combine_body,
        grid=(t // tm,),
        in_specs=[_slot_block(tm),
                  pl.BlockSpec((tm, d), lambda i: (i, 0)),
                  pl.BlockSpec((tm, ROUTE_COLS), lambda i: (i, 0)),
                  pl.BlockSpec((1, d), lambda i: (0, 0)),
                  pl.BlockSpec(memory_space=pl.ANY)],
        out_specs=pl.BlockSpec((tm, d), lambda i: (i, 0)),
        scratch_shapes=[pltpu.VMEM((MOE_TOPK, tm, d), F32), pltpu.SemaphoreType.DMA(())],
        out_shape=jax.ShapeDtypeStruct((t, d), F32),
        compiler_params=pltpu.CompilerParams(dimension_semantics=("arbitrary",),
                                             vmem_limit_bytes=VMEM_LIMIT_BYTES,
                                             disable_bounds_checks=True),
        name="moe_combine_final_norm",
    )(pos, x, info, g_final.reshape(1, d), y_sorted)


def _moe(x, g, w_route, b_route, w_gate, w_up, w_down, g_final):
    t, d = x.shape
    tr = MOE_ROW_TILE
    assert (MOE_TOPK * t) % tr == 0
    info, counts = _route(x, g, w_route, b_route, tm=min(512, t))

    expert = info[:, 0:MOE_TOPK].astype(jnp.int32)
    rank = info[:, MOE_TOPK:2 * MOE_TOPK].astype(jnp.int32)
    count = counts[0, N_GROUPS:].astype(jnp.int32)
    seg_tiles = (count + (tr - 1)) // tr
    seg_end = jnp.cumsum(seg_tiles)
    seg_start_row = (seg_end - seg_tiles) * tr
    pos = (seg_start_row[expert] + rank).T
    max_tiles = (MOE_TOPK * t) // tr + N_EXPERTS
    tile_id = jnp.arange(max_tiles, dtype=jnp.int32)
    tile_expert = jnp.minimum(
        jnp.sum((seg_end[None, :] <= tile_id[:, None]).astype(jnp.int32), axis=1), N_EXPERTS - 1)
    n_tiles = seg_end[-1:].astype(jnp.int32)

    tail = n_tiles + jnp.arange(N_EXPERTS, dtype=jnp.int32)
    fill_tiles = jnp.concatenate([jnp.where(seg_tiles > 0, seg_end - 1, -1),
                                  jnp.where(tail < max_tiles, tail, -1)]).astype(jnp.int32)

    sorted_rows = _dispatch(x, g, pos, fill_tiles, max_tiles * tr, tm=min(512, t))
    y_sorted = _experts(sorted_rows, tile_expert, n_tiles, w_gate, w_up, w_down)
    return _combine(x, info, pos, y_sorted, g_final, tm=min(512, t))


def _layer(x, mem, attn_norm, w_in, w_up_dil, w_up_moba, w_branch_gate, w_out, cross_norm,
           mem_norm, w_q_mem, w_kv_mem, w_o_mem):
    b, s, d = x.shape
    t = b * s
    xt = x.reshape(t, d)
    h = _norm(xt, attn_norm, tm=min(512, t), name="attn_norm")
    proj, gates = _in_proj(h, w_in, w_branch_gate, tm=min(2048, t))
    proj = proj.reshape(b, s, IN_WIDTH)
    dil_slopes = _alibi_slopes(DIL_HEADS)
    dil = [_dilated_group(proj, dil_slopes, g, dilation)
           for g, (_, dilation) in enumerate(DIL_CONFIGS)]
    y_moba = _moba(proj, _alibi_slopes(MOBA_HEADS))
    mem_len = mem.shape[1]
    kv = _norm_matmul(mem.reshape(b * mem_len, d), mem_norm, w_kv_mem.astype(BF16),
                      tm=b * mem_len, tn=512, name="norm_mem_kv").reshape(b, mem_len, 2 * MEM_WIDTH)
    return _merge_cross(xt, [o for o, _ in dil], [l for _, l in dil], y_moba, gates,
                        w_up_dil.astype(BF16), w_up_moba.astype(BF16), w_out.astype(BF16),
                        cross_norm, w_q_mem.astype(BF16), kv, w_o_mem.astype(BF16), tm=min(256, s))


def kernel(x, mem, attn_norm, w_in, w_up_dil, w_up_moba, w_branch_gate, w_out, cross_norm, mem_norm,
           w_q_mem, w_kv_mem, w_o_mem, ffn_norm, w_router_group, b_router_group, w_router_expert,
           b_router_expert, w_expert_gate, w_expert_up, w_expert_down, final_norm):
    b, s, d = x.shape
    depth = attn_norm.shape[0]
    assert depth == 1, "the final norm is fused into the last layer's MoE call"
    l = 0
    x2 = _layer(x, mem, attn_norm[l], w_in[l], w_up_dil[l], w_up_moba[l], w_branch_gate[l], w_out[l],
                cross_norm[l], mem_norm[l], w_q_mem[l], w_kv_mem[l], w_o_mem[l])
    w_route = jnp.concatenate([w_router_group[l], w_router_expert[l]], axis=1)
    b_route = jnp.concatenate([b_router_group[l], b_router_expert[l]], axis=0)
    out = _moe(x2, ffn_norm[l], w_route, b_route, w_expert_gate[l], w_expert_up[l],
               w_expert_down[l], final_norm)
    return out.reshape(b, s, d)
```

```python
import functools

import numpy as np
import jax
import jax.numpy as jnp
from jax import lax
from jax.experimental import pallas as pl
from jax.experimental.pallas import tpu as pltpu

F32 = jnp.float32
BF16 = jnp.bfloat16

HEAD_DIM = 128
DIL_CONFIGS = ((128, 1), (512, 4), (2048, 16))
DIL_HEADS_PER_GROUP = 4
DIL_HEADS = DIL_HEADS_PER_GROUP * len(DIL_CONFIGS)
DIL_WIDTH = DIL_HEADS * HEAD_DIM
DIL_OUT = DIL_HEADS_PER_GROUP * HEAD_DIM
DIL_STEPS = 128
MOBA_HEADS = 8
MOBA_WIDTH = MOBA_HEADS * HEAD_DIM
MOBA_BLOCK = 256
MOBA_TOPK = 3
IN_WIDTH = 3 * (DIL_WIDTH + MOBA_WIDTH)
MEM_HEADS = 4
MEM_WIDTH = MEM_HEADS * HEAD_DIM
N_GROUPS = 4
EXPERTS_PER_GROUP = 8
N_EXPERTS = N_GROUPS * EXPERTS_PER_GROUP
MOE_TOPK = 2
RMS_EPS = 1e-6
SCALE = HEAD_DIM ** -0.5
NEG_INF = float("-inf")
LOG2E = 1.4426950408889634
MOBA_GROUP = 4
MOBA_HEADS_PER_STEP = 4
MOBA_Q_TILES = (9, 11)

VMEM_LIMIT_BYTES = 56 * 1024 * 1024
BF16_ROWS = 16
MXU_COLUMNS = 256
_NT = (((1,), (1,)), ((), ()))


def _alibi_slopes(n):
    return jnp.asarray(2.0 ** (-8.0 * np.arange(1, n + 1) / n), dtype=F32)


def _rms(x, g):
    return x * lax.rsqrt(jnp.mean(x * x, axis=-1, keepdims=True) + RMS_EPS) * g


def _dot_split3(a, b):
    a_hi = a.astype(BF16)
    b_hi = b.astype(BF16)
    a_lo = (a - a_hi.astype(F32)).astype(BF16)
    b_lo = (b - b_hi.astype(F32)).astype(BF16)
    dot = functools.partial(jnp.dot, preferred_element_type=F32)
    n = b.shape[1]
    if 2 * n <= MXU_COLUMNS:
        both = dot(a_hi, jnp.concatenate([b_hi, b_lo], axis=1))
        return both[:, :n] + (dot(a_lo, b_hi) + both[:, n:])
    return dot(a_hi, b_hi) + (dot(a_lo, b_hi) + dot(a_hi, b_lo))


def _params(*sem, flags=None):
    return pltpu.CompilerParams(dimension_semantics=sem, vmem_limit_bytes=VMEM_LIMIT_BYTES,
                                flags=flags)


def _resident(shape):
    nd = len(shape)
    return pl.BlockSpec(shape, lambda *_: (0,) * nd, pipeline_mode=pl.Buffered(1))


def _norm_matmul_body(x_ref, g_ref, w_ref, o_ref, h_ref, *, sigmoid):
    @pl.when(pl.program_id(1) == 0)
    def _():
        h_ref[...] = _rms(x_ref[...], g_ref[...]).astype(BF16)

    acc = jnp.dot(h_ref[...], w_ref[...], preferred_element_type=F32)
    if sigmoid:
        acc = jax.nn.sigmoid(acc)
    o_ref[...] = acc.astype(o_ref.dtype)


def _norm_matmul(x, g, w, *, tm, tn, sigmoid=False, name):
    m, d = x.shape
    n = w.shape[1]
    assert m % tm == 0 and n % tn == 0
    return pl.pallas_call(
        functools.partial(_norm_matmul_body, sigmoid=sigmoid),
        grid=(m // tm, n // tn),
        in_specs=[
            pl.BlockSpec((tm, d), lambda i, j: (i, 0)),
            pl.BlockSpec((1, d), lambda i, j: (0, 0)),
            pl.BlockSpec((d, tn), lambda i, j: (0, j)),
        ],
        out_specs=pl.BlockSpec((tm, tn), lambda i, j: (i, j)),
        out_shape=jax.ShapeDtypeStruct((m, n), BF16),
        scratch_shapes=[pltpu.VMEM((tm, d), BF16)],
        compiler_params=_params("parallel", "arbitrary"),
        name=name,
    )(x, g.reshape(1, d), w)


def _norm_body(x_ref, g_ref, o_ref):
    o_ref[...] = _rms(x_ref[...], g_ref[...]).astype(o_ref.dtype)


def _norm(x, g, *, tm, name):
    m, d = x.shape
    assert m % tm == 0
    return pl.pallas_call(
        _norm_body,
        grid=(m // tm,),
        in_specs=[pl.BlockSpec((tm, d), lambda i: (i, 0)), pl.BlockSpec((1, d), lambda i: (0, 0))],
        out_specs=pl.BlockSpec((tm, d), lambda i: (i, 0)),
        out_shape=jax.ShapeDtypeStruct((m, d), BF16),
        compiler_params=_params("parallel"),
        name=name,
    )(x, g.reshape(1, d))


def _in_proj_body(h_ref, win_ref, wbg_ref, proj_ref, gates_ref, perm_ref, *, n_proj_tiles):
    j = pl.program_id(1)
    tm = h_ref.shape[0]
    n = DIL_STEPS
    groups = len(DIL_CONFIGS)

    def store_regrouped(acc, dilation):
        tile = n * dilation
        for c in range(acc.shape[1] // HEAD_DIM):
            cols = slice(c * HEAD_DIM, (c + 1) * HEAD_DIM)
            perm_ref[c] = acc[:, cols]
            for t0 in range(0, tm, tile):
                for r in range(dilation):
                    rows = perm_ref[c, pl.ds(t0 + r, n, stride=dilation), :]
                    proj_ref[t0 + r * n:t0 + (r + 1) * n, cols] = rows.astype(BF16)

    @pl.when(j < n_proj_tiles)
    def _():
        acc = jnp.dot(h_ref[...], win_ref[...].astype(BF16), preferred_element_type=F32)
        group = jnp.where(j < 3 * groups, lax.rem(j, groups), 0)
        moba_q = (j >= MOBA_Q_TILES[0]) & (j < MOBA_Q_TILES[1])
        scale = jnp.where(moba_q, SCALE * LOG2E, 1.0)

        def store_natural():
            proj_ref[...] = (acc * scale).astype(BF16)

        for g, (_, dilation) in enumerate(DIL_CONFIGS):
            store = store_natural if dilation == 1 else functools.partial(store_regrouped, acc, dilation)
            pl.when(group == g)(store)

    @pl.when(j >= n_proj_tiles)
    def _():
        acc = jnp.dot(h_ref[...], wbg_ref[...].astype(BF16), preferred_element_type=F32)
        gates_ref[...] = jax.nn.sigmoid(acc).astype(BF16)


def _in_proj(h, w_in, w_bg, *, tm):
    t, d = h.shape
    tn = DIL_OUT
    assert MOBA_Q_TILES == (3 * DIL_WIDTH // tn, (3 * DIL_WIDTH + MOBA_WIDTH) // tn)
    assert t % tm == 0 and tm % (DIL_STEPS * max(dl for _, dl in DIL_CONFIGS)) == 0
    assert DIL_WIDTH == len(DIL_CONFIGS) * tn and w_in.shape[1] % tn == 0 and w_bg.shape[1] % tn == 0
    n_proj = w_in.shape[1] // tn
    n_gate = w_bg.shape[1] // tn
    return pl.pallas_call(
        functools.partial(_in_proj_body, n_proj_tiles=n_proj),
        grid=(t // tm, n_proj + n_gate),
        in_specs=[
            pl.BlockSpec((tm, d), lambda i, j: (i, 0)),
            pl.BlockSpec((d, tn), lambda i, j: (0, jnp.minimum(j, n_proj - 1))),
            pl.BlockSpec((d, tn), lambda i, j: (0, jnp.maximum(j - n_proj, 0))),
        ],
        out_specs=[
            pl.BlockSpec((tm, tn), lambda i, j: (i, jnp.minimum(j, n_proj - 1))),
            pl.BlockSpec((tm, tn), lambda i, j: (i, jnp.maximum(j - n_proj, 0))),
        ],
        out_shape=[jax.ShapeDtypeStruct((t, w_in.shape[1]), BF16),
                   jax.ShapeDtypeStruct((t, w_bg.shape[1]), BF16)],
        scratch_shapes=[pltpu.VMEM((tn // HEAD_DIM, tm, HEAD_DIM), F32)],
        compiler_params=_params("parallel", "arbitrary"),
        name="in_proj_gates",
    )(h, w_in, w_bg)


DIL_SUBBLOCKS = 4


def _dilated_body(slope_ref, q_ref, kp_ref, kc_ref, vp_ref, vc_ref, o_ref, lse_ref, *scratch,
                  dilation, group):
    n = DIL_STEPS
    first_tile = pl.program_id(1) == 0
    chunk = pl.program_id(2)
    chunks = max(dilation // DIL_SUBBLOCKS, 1)
    qi = lax.broadcasted_iota(jnp.int32, (n, n), 0)
    kj = lax.broadcasted_iota(jnp.int32, (n, n), 1)
    steps_cur = qi - kj
    valid_cur = steps_cur >= 0
    dist_cur = (steps_cur * dilation).astype(F32)
    dist_prev = ((steps_cur + n) * dilation).astype(F32)
    limit_across_tiles = jnp.where(first_tile, -n, 0)
    heads = range(DIL_HEADS_PER_GROUP)
    cols_of = [slice(h * HEAD_DIM, (h + 1) * HEAD_DIM) for h in heads]
    rows_of = [slice(s * n, (s + 1) * n) for s in range(DIL_SUBBLOCKS)]

    def prev_rows(s):
        if dilation == 1:
            return (slice(0, n), False, limit_across_tiles) if s == 0 else (rows_of[s - 1], True, 0)
        return rows_of[s], False, limit_across_tiles

    def scores(s):
        rows, from_cur, _ = prev_rows(s)
        out = []
        for h in heads:
            q = q_ref[rows_of[s], cols_of[h]]
            k_prev = (kc_ref if from_cur else kp_ref)[rows, cols_of[h]]
            out.append((lax.dot_general(q, kc_ref[rows_of[s], cols_of[h]], _NT,
                                        preferred_element_type=F32),
                        lax.dot_general(q, k_prev, _NT, preferred_element_type=F32)))
        return out

    def softmax(s, raw):
        valid_prev = steps_cur <= prev_rows(s)[2]
        out = []
        for h in heads:
            slope = slope_ref[group * DIL_HEADS_PER_GROUP + h]
            s_cur = jnp.where(valid_cur, raw[h][0] * SCALE - slope * dist_cur, NEG_INF)
            s_prev = jnp.where(valid_prev, raw[h][1] * SCALE - slope * dist_prev, NEG_INF)
            m = jnp.max(jnp.maximum(s_cur, s_prev), axis=-1, keepdims=True)
            p_cur = jnp.exp(s_cur - m)
            p_prev = jnp.exp(s_prev - m)
            den = jnp.sum(p_cur + p_prev, axis=-1, keepdims=True)
            out.append((p_cur.astype(BF16), p_prev.astype(BF16), m, den))
        return out

    def values(s, probs):
        rows, from_cur, _ = prev_rows(s)
        for h in heads:
            p_cur, p_prev, m, den = probs[h]
            v_prev = (vc_ref if from_cur else vp_ref)[rows, cols_of[h]]
            o = (jnp.dot(p_cur, vc_ref[rows_of[s], cols_of[h]], preferred_element_type=F32)
                 + jnp.dot(p_prev, v_prev, preferred_element_type=F32)) / den
            lse = jnp.broadcast_to(m + jnp.log(den), (n, HEAD_DIM))
            if dilation == 1:
                dst_o, dst_lse, rows_out = o_ref, lse_ref, rows_of[s]
            elif chunks == 1:
                dst_o, dst_lse, rows_out = o_ref, lse_ref, pl.ds(s, n, stride=dilation)
            else:
                dst_o, dst_lse = scratch
                rows_out = pl.ds(pl.multiple_of((chunk * DIL_SUBBLOCKS + s) * n, n), n)
            dst_o[h, rows_out, :] = o
            dst_lse[h, rows_out, :] = lse

    raw, probs = {0: scores(0)}, {}
    for s in range(DIL_SUBBLOCKS + 1):
        if s + 1 < DIL_SUBBLOCKS:
            raw[s + 1] = scores(s + 1)
        if s < DIL_SUBBLOCKS:
            probs[s] = softmax(s, raw.pop(s))
        if s >= 1:
            values(s - 1, probs.pop(s - 1))

    if chunks > 1:
        o_tile, lse_tile = scratch

        @pl.when(chunk == chunks - 1)
        def _():
            for h in heads:
                for r in range(dilation):
                    natural = pl.ds(r, n, stride=dilation)
                    o_ref[h, natural, :] = o_tile[h, r * n:(r + 1) * n, :]
                    lse_ref[h, natural, :] = lse_tile[h, r * n:(r + 1) * n, :]


def _dilated_group(proj, slopes, group, dilation):
    b, s, _ = proj.shape
    n = DIL_STEPS
    step_rows = DIL_SUBBLOCKS * n
    assert dilation == 1 or dilation % DIL_SUBBLOCKS == 0
    chunks = max(dilation // DIL_SUBBLOCKS, 1)
    tile = step_rows * chunks
    assert s % tile == 0
    groups = len(DIL_CONFIGS)
    prev_rows = n if dilation == 1 else step_rows
    blocks_per_step = step_rows // prev_rows

    def cur(section):
        return pl.BlockSpec((None, step_rows, DIL_OUT),
                            lambda bi, i, c: (bi, i * chunks + c, section * groups + group))

    def prev(section):
        def index(bi, i, c):
            if dilation == 1:
                return (bi, jnp.maximum(i * blocks_per_step - 1, 0), section * groups + group)
            return (bi, jnp.maximum(i - 1, 0) * chunks + c, section * groups + group)
        return pl.BlockSpec((None, prev_rows, DIL_OUT), index)

    out_block = (DIL_HEADS_PER_GROUP, tile, HEAD_DIM)
    out_spec = pl.BlockSpec((None,) + out_block, lambda bi, i, c: (bi, 0, i, 0))
    out_sds = jax.ShapeDtypeStruct((b, DIL_HEADS_PER_GROUP, s, HEAD_DIM), F32)
    scratch = [pltpu.VMEM(out_block, F32)] * 2 if chunks > 1 else []
    o, lse = pl.pallas_call(
        functools.partial(_dilated_body, dilation=dilation, group=group),
        grid=(b, s // tile, chunks),
        in_specs=[pl.BlockSpec(memory_space=pltpu.SMEM),
                  cur(0), prev(1), cur(1), prev(2), cur(2)],
        out_specs=[out_spec, out_spec],
        out_shape=[out_sds, out_sds],
        scratch_shapes=scratch,
        compiler_params=_params("parallel", "arbitrary", "arbitrary"),
        name=f"dilated_attn_g{group}",
    )(slopes, proj, proj, proj, proj, proj)
    return o, lse


def _moba_body(slope_ref, q_ref, k_ref, v_ref, o_ref, kmean_ref, vt_ref, bias_ref, sel_ref,
               m_ref, acc_ref, ahead_ref, *, nblk, group):
    blk = MOBA_BLOCK
    hd = HEAD_DIM
    heads = q_ref.shape[1] // hd
    own = pl.program_id(2)
    key_off = lax.broadcasted_iota(jnp.int32, (blk, blk), 0)
    qry_off = lax.broadcasted_iota(jnp.int32, (blk, blk), 1)
    slope2 = [slope_ref[pl.program_id(1) * heads + h] * LOG2E for h in range(heads)]

    @pl.when(own == 0)
    def _():
        def fill(jb, carry):
            start = pl.multiple_of(jb * blk, blk)
            rows = k_ref[pl.ds(start, blk), :].astype(F32)
            kmean_ref[pl.ds(jb, 1), :] = jnp.mean(rows, axis=0, keepdims=True)
            vrows = v_ref[pl.ds(start, blk), :].astype(F32)
            for h in range(heads):
                vt_ref[h, :hd, pl.ds(start, blk)] = vrows[:, h * hd:(h + 1) * hd].T.astype(BF16)
            return carry
        lax.fori_loop(0, nblk, fill, 0)
        for h in range(heads):
            vt_ref[h, hd:, :] = jnp.ones((vt_ref.shape[1] - hd, vt_ref.shape[2]), BF16)
        for h in range(heads):
            bias_ref[h] = -slope2[h] * (qry_off - key_off).astype(F32)

    q = [q_ref[:, h * hd:(h + 1) * hd] for h in range(heads)]

    blk_id = lax.broadcasted_iota(jnp.int32, (nblk, blk), 0).astype(F32)
    kmean = kmean_ref[...]
    km_hi = kmean.astype(BF16)
    rest = kmean - km_hi.astype(F32)
    km_mid = rest.astype(BF16)
    km_lo = (rest - km_mid.astype(F32)).astype(BF16)
    km3 = jnp.concatenate([km_hi, km_mid, km_lo], axis=0)
    gates = []
    for h in range(heads):
        parts = lax.dot_general(km3[:, h * hd:(h + 1) * hd], q[h], _NT, preferred_element_type=F32)
        gate = parts[:nblk] + (parts[nblk:2 * nblk] + parts[2 * nblk:])
        gates.append(jnp.where(blk_id < own.astype(F32), gate, NEG_INF))

    def scores(h, start, rows):
        k = k_ref[pl.ds(start, rows), h * hd:(h + 1) * hd]
        return lax.dot_general(k, q[h], _NT, preferred_element_type=F32)

    own_start = pl.multiple_of(own * blk, blk)
    own_x = [scores(h, own_start, blk) for h in range(heads)]

    sels = [jnp.zeros((nblk, blk), F32) for _ in range(heads)]
    for _ in range(MOBA_TOPK):
        for h in range(heads):
            best = jnp.max(gates[h], axis=0, keepdims=True)
            is_best = (gates[h] == best) & (gates[h] > NEG_INF)
            pick = jnp.min(jnp.where(is_best, blk_id, float(nblk)), axis=0, keepdims=True)
            picked = blk_id == pick
            sels[h] = jnp.where(picked, 1.0, sels[h])
            gates[h] = jnp.where(picked, NEG_INF, gates[h])
    for h in range(heads):
        sel_ref[h] = sels[h]

    own_p = []
    for h in range(heads):
        x = jnp.where(qry_off >= key_off, own_x[h] + bias_ref[h], NEG_INF)
        m0 = jnp.max(x, axis=0, keepdims=True)
        own_p.append((m0, jnp.exp2(x - m0).astype(BF16)))
    for h in range(heads):
        m0, p = own_p[h]
        m_ref[h] = m0
        acc_ref[h] = jnp.dot(vt_ref[h, :, pl.ds(own_start, blk)], p, preferred_element_type=F32)

    def softmax_update(h, i, x, m):
        xs, chosen, shift = [], [], []
        m_new = m
        for g in range(group):
            j = i * group + g
            xs.append(x[g * blk:(g + 1) * blk] + bias_ref[h])
            chosen.append(sel_ref[h, pl.ds(j, 1), :] > 0.5)
            shift.append(-slope2[h] * ((own - j) * blk).astype(F32))
            top = jnp.max(xs[g], axis=0, keepdims=True) + shift[g]
            m_new = jnp.maximum(m_new, jnp.where(chosen[g], top, NEG_INF))
        ps = []
        for g in range(group):
            ref_g = jnp.where(chosen[g], m_new - shift[g], jnp.inf)
            ps.append(jnp.exp2(xs[g] - ref_g).astype(BF16))
        return m_new, jnp.exp2(m - m_new), jnp.concatenate(ps, axis=0)

    def past_blocks(i, carry):
        start = pl.multiple_of(i * (group * blk), group * blk)
        x, sm = {0: ahead_ref[...]}, {}
        for h in range(heads + 1):
            if h + 1 < heads:
                x[h + 1] = scores(h + 1, start, group * blk)
            elif h + 1 == heads:
                ahead_ref[...] = scores(0, group_start(i + 1), group * blk)
            if h < heads:
                sm[h] = softmax_update(h, i, x[h], m_ref[h])
            if h >= 1:
                m_new, alpha, p = sm[h - 1]
                m_ref[h - 1] = m_new
                acc_ref[h - 1] = alpha * acc_ref[h - 1] + jnp.dot(
                    vt_ref[h - 1, :, pl.ds(start, group * blk)], p, preferred_element_type=F32)
        return carry

    def group_start(i):
        return pl.multiple_of(jnp.minimum(i, nblk // group - 1) * (group * blk), group * blk)

    n_groups = lax.div(own + (group - 1), group)
    ahead_ref[...] = scores(0, group_start(0), group * blk)
    lax.fori_loop(0, n_groups, past_blocks, 0)
    for h in range(heads):
        acc = acc_ref[h]
        o_ref[:, h * hd:(h + 1) * hd] = (acc[:hd] / acc[hd:hd + 1]).T.astype(o_ref.dtype)


def _moba(proj, slopes):
    b, s, _ = proj.shape
    assert s % (MOBA_BLOCK * MOBA_GROUP) == 0
    nblk = s // MOBA_BLOCK
    hp = MOBA_HEADS_PER_STEP
    width = hp * HEAD_DIM
    q0 = 3 * DIL_WIDTH // width
    k0 = q0 + MOBA_WIDTH // width
    v0 = k0 + MOBA_WIDTH // width
    y = pl.pallas_call(
        functools.partial(_moba_body, nblk=nblk, group=MOBA_GROUP),
        grid=(b, MOBA_HEADS // hp, nblk),
        in_specs=[
            pl.BlockSpec(memory_space=pltpu.SMEM),
            pl.BlockSpec((None, MOBA_BLOCK, width), lambda bi, h, i: (bi, i, q0 + h)),
            pl.BlockSpec((None, s, width), lambda bi, h, i: (bi, 0, k0 + h)),
            pl.BlockSpec((None, s, width), lambda bi, h, i: (bi, 0, v0 + h)),
        ],
        out_specs=pl.BlockSpec((None, MOBA_BLOCK, width), lambda bi, h, i: (bi, i, h)),
        out_shape=jax.ShapeDtypeStruct((b, s, MOBA_WIDTH), BF16),
        scratch_shapes=[pltpu.VMEM((nblk, width), F32),
                        pltpu.VMEM((hp, HEAD_DIM + BF16_ROWS, s), BF16),
                        pltpu.VMEM((hp, MOBA_BLOCK, MOBA_BLOCK), F32),
                        pltpu.VMEM((hp, nblk, MOBA_BLOCK), F32),
                        pltpu.VMEM((hp, 1, MOBA_BLOCK), F32),
                        pltpu.VMEM((hp, HEAD_DIM + BF16_ROWS, MOBA_BLOCK), F32),
                        pltpu.VMEM((MOBA_GROUP * MOBA_BLOCK, MOBA_BLOCK), F32)],
        compiler_params=_params("parallel", "parallel", "arbitrary"),
        name="moba_attn",
    )(slopes, proj, proj, proj)
    return y.reshape(b * s, MOBA_WIDTH)


def _merged_branches(x, o_refs, l_refs, ym_ref, gd_ref, gm_ref, wud_ref, wum_ref, wo_ref):
    heads = []
    for h in range(DIL_HEADS_PER_GROUP):
        l0, l1, l2 = (l_ref[h] for l_ref in l_refs)
        m = jnp.maximum(jnp.maximum(l0, l1), l2)
        e0, e1, e2 = jnp.exp(l0 - m), jnp.exp(l1 - m), jnp.exp(l2 - m)
        den = e0 + e1 + e2
        mixed = ((e0 / den) * o_refs[0][h] + (e1 / den) * o_refs[1][h] + (e2 / den) * o_refs[2][h])
        heads.append(mixed.astype(BF16))
    y_dil = jnp.concatenate(heads, axis=-1)
    lift_dil = jnp.dot(y_dil, wud_ref[...], preferred_element_type=F32)
    lift_moba = jnp.dot(ym_ref[...], wum_ref[...], preferred_element_type=F32)
    merged = gd_ref[...].astype(F32) * lift_dil + gm_ref[...].astype(F32) * lift_moba
    return x + jnp.dot(merged.astype(BF16), wo_ref[...], preferred_element_type=F32)


def _cross_attended(x, g_ref, wq_ref, kv_ref, wo_ref):
    h = _rms(x, g_ref[...]).astype(BF16)
    q = jnp.dot(h, wq_ref[...], preferred_element_type=F32).astype(BF16)
    heads = []
    for hd in range(MEM_HEADS):
        k = kv_ref[:, hd * HEAD_DIM:(hd + 1) * HEAD_DIM]
        v = kv_ref[:, MEM_WIDTH + hd * HEAD_DIM:MEM_WIDTH + (hd + 1) * HEAD_DIM]
        s = lax.dot_general(q[:, hd * HEAD_DIM:(hd + 1) * HEAD_DIM], k, _NT,
                            preferred_element_type=F32) * SCALE
        p = jnp.exp(s - jnp.max(s, axis=-1, keepdims=True))
        den = jnp.sum(p, axis=-1, keepdims=True)
        heads.append((jnp.dot(p.astype(BF16), v, preferred_element_type=F32) / den).astype(BF16))
    o = jnp.concatenate(heads, axis=-1)
    return x + jnp.dot(o, wo_ref[...], preferred_element_type=F32)


def _merge_cross_body(x_ref, o0_ref, o1_ref, o2_ref, l0_ref, l1_ref, l2_ref, ym_ref, gd_ref, gm_ref,
                      wud_ref, wum_ref, wo_ref, gc_ref, wq_ref, kv_ref, wom_ref, out_ref):
    x1 = _merged_branches(x_ref[...], (o0_ref, o1_ref, o2_ref), (l0_ref, l1_ref, l2_ref),
                          ym_ref, gd_ref, gm_ref, wud_ref, wum_ref, wo_ref)
    out_ref[...] = _cross_attended(x1, gc_ref, wq_ref, kv_ref, wom_ref)


def _merge_cross(x, dil_outs, dil_lses, y_moba, gates, w_up_dil, w_up_moba, w_out,
                 g_cross, w_q, kv, w_o, *, tm):
    t, d = x.shape
    s = dil_outs[0].shape[2]
    mem_len = kv.shape[1]
    assert t % tm == 0 and s % tm == 0
    per_batch = s // tm
    row = lambda w: pl.BlockSpec((tm, w), lambda i: (i, 0))
    dil = pl.BlockSpec((None, DIL_HEADS_PER_GROUP, tm, HEAD_DIM),
                       lambda i: (i // per_batch, 0, i % per_batch, 0))
    return pl.pallas_call(
        _merge_cross_body,
        grid=(t // tm,),
        in_specs=[row(d)] + [dil] * 6 + [
            row(MOBA_WIDTH),
            pl.BlockSpec((tm, d), lambda i: (i, 0)),
            pl.BlockSpec((tm, d), lambda i: (i, 1)),
            _resident((DIL_OUT, d)), _resident((MOBA_WIDTH, d)), _resident((d, d)),
            _resident((1, d)), _resident((d, MEM_WIDTH)),
            pl.BlockSpec((None, mem_len, 2 * MEM_WIDTH), lambda i: (i // per_batch, 0, 0)),
            _resident((MEM_WIDTH, d)),
        ],
        out_specs=row(d),
        out_shape=jax.ShapeDtypeStruct((t, d), F32),
        compiler_params=_params("parallel"),
        name="merge_out_proj_cross_attn",
    )(x, *dil_outs, *dil_lses, y_moba, gates, gates, w_up_dil, w_up_moba, w_out,
      g_cross.reshape(1, d), w_q, kv, w_o)


ROUTE_COLS = 8
MOE_ROW_TILE = 256


def _route_body(x_ref, g_ref, wr_ref, br_ref, info_ref, counts_ref, run_ref, tri_ref):
    tm = x_ref.shape[0]
    n_route = N_GROUPS + N_EXPERTS
    lane = lax.broadcasted_iota(jnp.int32, (tm, n_route), 1).astype(F32)

    @pl.when(pl.program_id(0) == 0)
    def _():
        run_ref[...] = jnp.zeros_like(run_ref)
        earlier = (lax.broadcasted_iota(jnp.int32, (tm, tm), 0)
                   > lax.broadcasted_iota(jnp.int32, (tm, tm), 1))
        tri_ref[...] = jnp.where(earlier, 1.0, 0.0).astype(BF16)

    t = _rms(x_ref[...], g_ref[...])
    logits = _dot_split3(t, wr_ref[...]) + br_ref[...]
    none = float(n_route)
    glog = jnp.where(lane < N_GROUPS, logits, NEG_INF)
    gmax = jnp.max(glog, axis=-1, keepdims=True)
    gsel = jnp.min(jnp.where(glog == gmax, lane, none), axis=-1, keepdims=True)
    pg = 1.0 / jnp.sum(jnp.exp(glog - gmax), axis=-1, keepdims=True)
    first = N_GROUPS + gsel * EXPERTS_PER_GROUP
    in_group = (lane >= first) & (lane < first + EXPERTS_PER_GROUP)
    elog = jnp.where(in_group, logits, NEG_INF)
    top1 = jnp.max(elog, axis=-1, keepdims=True)
    i1 = jnp.min(jnp.where(elog == top1, lane, none), axis=-1, keepdims=True)
    rest = jnp.where(lane == i1, NEG_INF, elog)
    top2 = jnp.max(rest, axis=-1, keepdims=True)
    i2 = jnp.min(jnp.where(rest == top2, lane, none), axis=-1, keepdims=True)
    e2 = jnp.exp(top2 - top1)
    w1 = pg / (1.0 + e2)
    w2 = pg * e2 / (1.0 + e2)

    hit1 = lane == i1
    hit2 = lane == i2
    assigned = jnp.where(hit1 | hit2, 1.0, 0.0)
    before = jnp.dot(tri_ref[...], assigned.astype(BF16), preferred_element_type=F32) + run_ref[...]
    rank1 = jnp.sum(jnp.where(hit1, before, 0.0), axis=-1, keepdims=True)
    rank2 = jnp.sum(jnp.where(hit2, before, 0.0), axis=-1, keepdims=True)
    run_ref[...] += jnp.sum(assigned, axis=0, keepdims=True)
    counts_ref[...] = run_ref[...]

    col = lax.broadcasted_iota(jnp.int32, (tm, ROUTE_COLS), 1)
    fields = (i1 - N_GROUPS, i2 - N_GROUPS, rank1, rank2, w1, w2)
    info = jnp.zeros((tm, ROUTE_COLS), F32)
    for c, field in enumerate(fields):
        info = jnp.where(col == c, field, info)
    info_ref[...] = info


def _route(x, g, w_route, b_route, *, tm):
    t, d = x.shape
    n_route = N_GROUPS + N_EXPERTS
    assert t % tm == 0
    return pl.pallas_call(
        _route_body,
        grid=(t // tm,),
        in_specs=[
            pl.BlockSpec((tm, d), lambda i: (i, 0)),
            pl.BlockSpec((1, d), lambda i: (0, 0)),
            pl.BlockSpec((d, n_route), lambda i: (0, 0)),
            pl.BlockSpec((1, n_route), lambda i: (0, 0)),
        ],
        out_specs=[pl.BlockSpec((tm, ROUTE_COLS), lambda i: (i, 0)),
                   pl.BlockSpec((1, n_route), lambda i: (0, 0))],
        out_shape=[jax.ShapeDtypeStruct((t, ROUTE_COLS), F32),
                   jax.ShapeDtypeStruct((1, n_route), F32)],
        scratch_shapes=[pltpu.VMEM((1, n_route), F32), pltpu.VMEM((tm, tm), BF16)],
        compiler_params=_params("arbitrary"),
        name="moe_route",
    )(x, g.reshape(1, d), w_route, b_route.reshape(1, n_route))


def _row_copies_wait(src_rows, dst_rows, sem):
    pltpu.make_async_copy(src_rows, dst_rows, sem).wait()


def _slot_block(tm):
    return pl.BlockSpec((MOE_TOPK, tm), lambda i, *_: (0, i), memory_space=pltpu.SMEM)


def _dispatch_body(fill_ref, pos_ref, x_ref, g_ref, sorted_ref, t_ref, zero_ref, sem, fill_sem):
    tm = x_ref.shape[0]
    tr = zero_ref.shape[0]

    @pl.when(pl.program_id(0) == 0)
    def _():
        zero_ref[...] = jnp.zeros_like(zero_ref)

        def fill_copy(tile):
            return pltpu.make_async_copy(zero_ref, sorted_ref.at[pl.ds(tile * tr, tr)], fill_sem)

        def start(idx, carry):
            pl.when(fill_ref[idx] >= 0)(lambda: fill_copy(fill_ref[idx]).start())
            return carry

        def finish(idx, carry):
            pl.when(fill_ref[idx] >= 0)(lambda: fill_copy(fill_ref[idx]).wait())
            return carry

        lax.fori_loop(0, fill_ref.shape[0], start, 0)
        lax.fori_loop(0, fill_ref.shape[0], finish, 0)

    t_ref[...] = _rms(x_ref[...], g_ref[...])

    def issue(r, carry):
        for k in range(MOE_TOPK):
            dst = pos_ref[k, r]
            pltpu.make_async_copy(t_ref.at[pl.ds(r, 1)], sorted_ref.at[pl.ds(dst, 1)], sem).start()
        return carry

    lax.fori_loop(0, tm, issue, 0, unroll=8)
    for _ in range(MOE_TOPK):
        _row_copies_wait(t_ref, sorted_ref.at[pl.ds(0, tm)], sem)


def _dispatch(x, g, pos, fill_tiles, n_rows, *, tm):
    t, d = x.shape
    assert t % tm == 0
    return pl.pallas_call(
        _dispatch_body,
        grid_spec=pltpu.PrefetchScalarGridSpec(
            num_scalar_prefetch=1,
            grid=(t // tm,),
            in_specs=[_slot_block(tm),
                      pl.BlockSpec((tm, d), lambda i, fill: (i, 0)),
                      pl.BlockSpec((1, d), lambda i, fill: (0, 0))],
            out_specs=pl.BlockSpec(memory_space=pl.ANY),
            scratch_shapes=[pltpu.VMEM((tm, d), F32), pltpu.VMEM((MOE_ROW_TILE, d), F32),
                            pltpu.SemaphoreType.DMA(()), pltpu.SemaphoreType.DMA(())],
        ),
        out_shape=jax.ShapeDtypeStruct((n_rows, d), F32),
        compiler_params=pltpu.CompilerParams(dimension_semantics=("arbitrary",),
                                             vmem_limit_bytes=VMEM_LIMIT_BYTES,
                                             disable_bounds_checks=True),
        name="moe_dispatch",
    )(fill_tiles, pos, x, g.reshape(1, d))


def _experts_body(tile_expert_ref, n_tiles_ref, x_ref, wg_ref, wu_ref, wd_ref, y_ref):
    del tile_expert_ref
    in_use = pl.program_id(0) < n_tiles_ref[0]

    @pl.when(in_use)
    def _():
        t = x_ref[...].astype(BF16)
        gate = jnp.dot(t, wg_ref[...].astype(BF16), preferred_element_type=F32)
        up = jnp.dot(t, wu_ref[...].astype(BF16), preferred_element_type=F32)
        a = jax.nn.silu(gate) * up
        y_ref[...] = jnp.dot(a.astype(BF16), wd_ref[...].astype(BF16), preferred_element_type=F32)

    @pl.when(jnp.logical_not(in_use))
    def _():
        y_ref[...] = jnp.zeros_like(y_ref)


def _experts(sorted_rows, tile_expert, n_tiles, w_gate, w_up, w_down):
    p, d = sorted_rows.shape
    ff = w_gate.shape[-1]
    tr = MOE_ROW_TILE
    assert p % tr == 0
    used = lambda i, nt: jnp.minimum(i, nt[0] - 1)
    return pl.pallas_call(
        _experts_body,
        grid_spec=pltpu.PrefetchScalarGridSpec(
            num_scalar_prefetch=2,
            grid=(p // tr,),
            in_specs=[pl.BlockSpec((tr, d), lambda i, te, nt: (used(i, nt), 0)),
                      pl.BlockSpec((None, d, ff), lambda i, te, nt: (te[used(i, nt)], 0, 0)),
                      pl.BlockSpec((None, d, ff), lambda i, te, nt: (te[used(i, nt)], 0, 0)),
                      pl.BlockSpec((None, ff, d), lambda i, te, nt: (te[used(i, nt)], 0, 0))],
            out_specs=pl.BlockSpec((tr, d), lambda i, te, nt: (i, 0)),
        ),
        out_shape=jax.ShapeDtypeStruct((p, d), F32),
        compiler_params=_params("arbitrary"),
        name="moe_experts",
    )(tile_expert, n_tiles, sorted_rows, w_gate, w_up, w_down)


def _combine_body(pos_ref, x_ref, info_ref, gf_ref, y_sorted_ref, out_ref, rows_ref, sem):
    tm = x_ref.shape[0]

    def issue(r, carry):
        for k in range(MOE_TOPK):
            src = pos_ref[k, r]
            pltpu.make_async_copy(y_sorted_ref.at[pl.ds(src, 1)], rows_ref.at[k, pl.ds(r, 1)],
                                  sem).start()
        return carry

    lax.fori_loop(0, tm, issue, 0, unroll=8)
    for k in range(MOE_TOPK):
        _row_copies_wait(y_sorted_ref.at[pl.ds(0, tm)], rows_ref.at[k], sem)

    info = info_ref[...]
    y = info[:, 4:5] * rows_ref[0] + info[:, 5:6] * rows_ref[1]
    out_ref[...] = _rms(x_ref[...] + y, gf_ref[...])


def _combine(x, info, pos, y_sorted, g_final, *, tm):
    t, d = x.shape
    assert t % tm == 0
    return pl.pallas_call(
        _combine_body,
        grid=(t // tm,),
        in_specs=[_slot_block(tm),
                  pl.BlockSpec((tm, d), lambda i: (i, 0)),
                  pl.BlockSpec((tm, ROUTE_COLS), lambda i: (i, 0)),
                  pl.BlockSpec((1, d), lambda i: (0, 0)),
                  pl.BlockSpec(memory_space=pl.ANY)],
        out_specs=pl.BlockSpec((tm, d), lambda i: (i, 0)),
        scratch_shapes=[pltpu.VMEM((MOE_TOPK, tm, d), F32), pltpu.SemaphoreType.DMA(())],
        out_shape=jax.ShapeDtypeStruct((t, d), F32),
        compiler_params=pltpu.CompilerParams(dimension_semantics=("arbitrary",),
                                             vmem_limit_bytes=VMEM_LIMIT_BYTES,
                                             disable_bounds_checks=True),
        name="moe_combine_final_norm",
    )(pos, x, info, g_final.reshape(1, d), y_sorted)


def _moe(x, g, w_route, b_route, w_gate, w_up, w_down, g_final):
    t, d = x.shape
    tr = MOE_ROW_TILE
    assert (MOE_TOPK * t) % tr == 0
    info, counts = _route(x, g, w_route, b_route, tm=min(512, t))

    expert = info[:, 0:MOE_TOPK].astype(jnp.int32)
    rank = info[:, MOE_TOPK:2 * MOE_TOPK].astype(jnp.int32)
    count = counts[0, N_GROUPS:].astype(jnp.int32)
    seg_tiles = (count + (tr - 1)) // tr
    seg_end = jnp.cumsum(seg_tiles)
    seg_start_row = (seg_end - seg_tiles) * tr
    pos = (seg_start_row[expert] + rank).T
    max_tiles = (MOE_TOPK * t) // tr + N_EXPERTS
    tile_id = jnp.arange(max_tiles, dtype=jnp.int32)
    tile_expert = jnp.minimum(
        jnp.sum((seg_end[None, :] <= tile_id[:, None]).astype(jnp.int32), axis=1), N_EXPERTS - 1)
    n_tiles = seg_end[-1:].astype(jnp.int32)

    tail = n_tiles + jnp.arange(N_EXPERTS, dtype=jnp.int32)
    fill_tiles = jnp.concatenate([jnp.where(seg_tiles > 0, seg_end - 1, -1),
                                  jnp.where(tail < max_tiles, tail, -1)]).astype(jnp.int32)

    sorted_rows = _dispatch(x, g, pos, fill_tiles, max_tiles * tr, tm=min(512, t))
    y_sorted = _experts(sorted_rows, tile_expert, n_tiles, w_gate, w_up, w_down)
    return _combine(x, info, pos, y_sorted, g_final, tm=min(512, t))


def _layer(x, mem, attn_norm, w_in, w_up_dil, w_up_moba, w_branch_gate, w_out, cross_norm,
           mem_norm, w_q_mem, w_kv_mem, w_o_mem):
    b, s, d = x.shape
    t = b * s
    xt = x.reshape(t, d)
    h = _norm(xt, attn_norm, tm=min(512, t), name="attn_norm")
    proj, gates = _in_proj(h, w_in, w_branch_gate, tm=min(2048, t))
    proj = proj.reshape(b, s, IN_WIDTH)
    dil_slopes = _alibi_slopes(DIL_HEADS)
    dil = [_dilated_group(proj, dil_slopes, g, dilation)
           for g, (_, dilation) in enumerate(DIL_CONFIGS)]
    y_moba = _moba(proj, _alibi_slopes(MOBA_HEADS))
    mem_len = mem.shape[1]
    kv = _norm_matmul(mem.reshape(b * mem_len, d), mem_norm, w_kv_mem.astype(BF16),
                      tm=b * mem_len, tn=512, name="norm_mem_kv").reshape(b, mem_len, 2 * MEM_WIDTH)
    return _merge_cross(xt, [o for o, _ in dil], [l for _, l in dil], y_moba, gates,
                        w_up_dil.astype(BF16), w_up_moba.astype(BF16), w_out.astype(BF16),
                        cross_norm, w_q_mem.astype(BF16), kv, w_o_mem.astype(BF16), tm=min(256, s))


def kernel(x, mem, attn_norm, w_in, w_up_dil, w_up_moba, w_branch_gate, w_out, cross_norm, mem_norm,
           w_q_mem, w_kv_mem, w_o_mem, ffn_norm, w_router_group, b_router_group, w_router_expert,
           b_router_expert, w_expert_gate, w_expert_up, w_expert_down, final_norm):
    b, s, d = x.shape
    depth = attn_norm.shape[0]
    assert depth == 1, "the final norm is fused into the last layer's MoE call"
    l = 0
    x2 = _layer(x, mem, attn_norm[l], w_in[l], w_up_dil[l], w_up_moba[l], w_branch_gate[l], w_out[l],
                cross_norm[l], mem_norm[l], w_q_mem[l], w_kv_mem[l], w_o_mem[l])
    w_route = jnp.concatenate([w_router_group[l], w_router_expert[l]], axis=1)
    b_route = jnp.concatenate([b_router_group[l], b_router_expert[l]], axis=0)
    out = _moe(x2, ffn_norm[l], w_route, b_route, w_expert_gate[l], w_expert_up[l],
               w_expert_down[l], final_norm)
    return out.reshape(b, s, d)
```

```python
import functools

import numpy as np
import jax
import jax.numpy as jnp
from jax import lax
from jax.experimental import pallas as pl
from jax.experimental.pallas import tpu as pltpu

F32 = jnp.float32
BF16 = jnp.bfloat16

HEAD_DIM = 128
DIL_CONFIGS = ((128, 1), (512, 4), (2048, 16))
DIL_HEADS_PER_GROUP = 4
DIL_HEADS = DIL_HEADS_PER_GROUP * len(DIL_CONFIGS)
DIL_WIDTH = DIL_HEADS * HEAD_DIM
DIL_OUT = DIL_HEADS_PER_GROUP * HEAD_DIM
DIL_STEPS = 128
MOBA_HEADS = 8
MOBA_WIDTH = MOBA_HEADS * HEAD_DIM
MOBA_BLOCK = 256
MOBA_TOPK = 3
IN_WIDTH = 3 * (DIL_WIDTH + MOBA_WIDTH)
MEM_HEADS = 4
MEM_WIDTH = MEM_HEADS * HEAD_DIM
N_GROUPS = 4
EXPERTS_PER_GROUP = 8
N_EXPERTS = N_GROUPS * EXPERTS_PER_GROUP
MOE_TOPK = 2
RMS_EPS = 1e-6
SCALE = HEAD_DIM ** -0.5
NEG_INF = float("-inf")
LOG2E = 1.4426950408889634
MOBA_GROUP = 4
MOBA_HEADS_PER_STEP = 4
MOBA_Q_TILES = (9, 11)

VMEM_LIMIT_BYTES = 56 * 1024 * 1024
BF16_ROWS = 16
MXU_COLUMNS = 256
_NT = (((1,), (1,)), ((), ()))


def _alibi_slopes(n):
    return jnp.asarray(2.0 ** (-8.0 * np.arange(1, n + 1) / n), dtype=F32)


def _rms(x, g):
    return x * lax.rsqrt(jnp.mean(x * x, axis=-1, keepdims=True) + RMS_EPS) * g


def _dot_split3(a, b):
    a_hi = a.astype(BF16)
    b_hi = b.astype(BF16)
    a_lo = (a - a_hi.astype(F32)).astype(BF16)
    b_lo = (b - b_hi.astype(F32)).astype(BF16)
    dot = functools.partial(jnp.dot, preferred_element_type=F32)
    n = b.shape[1]
    if 2 * n <= MXU_COLUMNS:
        both = dot(a_hi, jnp.concatenate([b_hi, b_lo], axis=1))
        return both[:, :n] + (dot(a_lo, b_hi) + both[:, n:])
    return dot(a_hi, b_hi) + (dot(a_lo, b_hi) + dot(a_hi, b_lo))


def _params(*sem, flags=None):
    return pltpu.CompilerParams(dimension_semantics=sem, vmem_limit_bytes=VMEM_LIMIT_BYTES,
                                flags=flags)


def _resident(shape):
    nd = len(shape)
    return pl.BlockSpec(shape, lambda *_: (0,) * nd, pipeline_mode=pl.Buffered(1))


def _norm_matmul_body(x_ref, g_ref, w_ref, o_ref, h_ref, *, sigmoid):
    @pl.when(pl.program_id(1) == 0)
    def _():
        h_ref[...] = _rms(x_ref[...], g_ref[...]).astype(BF16)

    acc = jnp.dot(h_ref[...], w_ref[...], preferred_element_type=F32)
    if sigmoid:
        acc = jax.nn.sigmoid(acc)
    o_ref[...] = acc.astype(o_ref.dtype)


def _norm_matmul(x, g, w, *, tm, tn, sigmoid=False, name):
    m, d = x.shape
    n = w.shape[1]
    assert m % tm == 0 and n % tn == 0
    return pl.pallas_call(
        functools.partial(_norm_matmul_body, sigmoid=sigmoid),
        grid=(m // tm, n // tn),
        in_specs=[
            pl.BlockSpec((tm, d), lambda i, j: (i, 0)),
            pl.BlockSpec((1, d), lambda i, j: (0, 0)),
            pl.BlockSpec((d, tn), lambda i, j: (0, j)),
        ],
        out_specs=pl.BlockSpec((tm, tn), lambda i, j: (i, j)),
        out_shape=jax.ShapeDtypeStruct((m, n), BF16),
        scratch_shapes=[pltpu.VMEM((tm, d), BF16)],
        compiler_params=_params("parallel", "arbitrary"),
        name=name,
    )(x, g.reshape(1, d), w)


def _norm_body(x_ref, g_ref, o_ref):
    o_ref[...] = _rms(x_ref[...], g_ref[...]).astype(o_ref.dtype)


def _norm(x, g, *, tm, name):
    m, d = x.shape
    assert m % tm == 0
    return pl.pallas_call(
        _norm_body,
        grid=(m // tm,),
        in_specs=[pl.BlockSpec((tm, d), lambda i: (i, 0)), pl.BlockSpec((1, d), lambda i: (0, 0))],
        out_specs=pl.BlockSpec((tm, d), lambda i: (i, 0)),
        out_shape=jax.ShapeDtypeStruct((m, d), BF16),
        compiler_params=_params("parallel"),
        name=name,
    )(x, g.reshape(1, d))


def _in_proj_body(h_ref, win_ref, wbg_ref, proj_ref, gates_ref, perm_ref, *, n_proj_tiles):
    j = pl.program_id(1)
    tm = h_ref.shape[0]
    n = DIL_STEPS
    groups = len(DIL_CONFIGS)

    def store_regrouped(acc, dilation):
        tile = n * dilation
        for c in range(acc.shape[1] // HEAD_DIM):
            cols = slice(c * HEAD_DIM, (c + 1) * HEAD_DIM)
            perm_ref[c] = acc[:, cols]
            for t0 in range(0, tm, tile):
                for r in range(dilation):
                    rows = perm_ref[c, pl.ds(t0 + r, n, stride=dilation), :]
                    proj_ref[t0 + r * n:t0 + (r + 1) * n, cols] = rows.astype(BF16)

    @pl.when(j < n_proj_tiles)
    def _():
        acc = jnp.dot(h_ref[...], win_ref[...].astype(BF16), preferred_element_type=F32)
        group = jnp.where(j < 3 * groups, lax.rem(j, groups), 0)
        moba_q = (j >= MOBA_Q_TILES[0]) & (j < MOBA_Q_TILES[1])
        scale = jnp.where(moba_q, SCALE * LOG2E, 1.0)

        def store_natural():
            proj_ref[...] = (acc * scale).astype(BF16)

        for g, (_, dilation) in enumerate(DIL_CONFIGS):
            store = store_natural if dilation == 1 else functools.partial(store_regrouped, acc, dilation)
            pl.when(group == g)(store)

    @pl.when(j >= n_proj_tiles)
    def _():
        acc = jnp.dot(h_ref[...], wbg_ref[...].astype(BF16), preferred_element_type=F32)
        gates_ref[...] = jax.nn.sigmoid(acc).astype(BF16)


def _in_proj(h, w_in, w_bg, *, tm):
    t, d = h.shape
    tn = DIL_OUT
    assert MOBA_Q_TILES == (3 * DIL_WIDTH // tn, (3 * DIL_WIDTH + MOBA_WIDTH) // tn)
    assert t % tm == 0 and tm % (DIL_STEPS * max(dl for _, dl in DIL_CONFIGS)) == 0
    assert DIL_WIDTH == len(DIL_CONFIGS) * tn and w_in.shape[1] % tn == 0 and w_bg.shape[1] % tn == 0
    n_proj = w_in.shape[1] // tn
    n_gate = w_bg.shape[1] // tn
    return pl.pallas_call(
        functools.partial(_in_proj_body, n_proj_tiles=n_proj),
        grid=(t // tm, n_proj + n_gate),
        in_specs=[
            pl.BlockSpec((tm, d), lambda i, j: (i, 0)),
            pl.BlockSpec((d, tn), lambda i, j: (0, jnp.minimum(j, n_proj - 1))),
            pl.BlockSpec((d, tn), lambda i, j: (0, jnp.maximum(j - n_proj, 0))),
        ],
        out_specs=[
            pl.BlockSpec((tm, tn), lambda i, j: (i, jnp.minimum(j, n_proj - 1))),
            pl.BlockSpec((tm, tn), lambda i, j: (i, jnp.maximum(j - n_proj, 0))),
        ],
        out_shape=[jax.ShapeDtypeStruct((t, w_in.shape[1]), BF16),
                   jax.ShapeDtypeStruct((t, w_bg.shape[1]), BF16)],
        scratch_shapes=[pltpu.VMEM((tn // HEAD_DIM, tm, HEAD_DIM), F32)],
        compiler_params=_params("parallel", "arbitrary"),
        name="in_proj_gates",
    )(h, w_in, w_bg)


DIL_SUBBLOCKS = 4


def _dilated_body(slope_ref, q_ref, kp_ref, kc_ref, vp_ref, vc_ref, o_ref, lse_ref, *scratch,
                  dilation, group):
    n = DIL_STEPS
    first_tile = pl.program_id(1) == 0
    chunk = pl.program_id(2)
    chunks = max(dilation // DIL_SUBBLOCKS, 1)
    qi = lax.broadcasted_iota(jnp.int32, (n, n), 0)
    kj = lax.broadcasted_iota(jnp.int32, (n, n), 1)
    steps_cur = qi - kj
    valid_cur = steps_cur >= 0
    dist_cur = (steps_cur * dilation).astype(F32)
    dist_prev = ((steps_cur + n) * dilation).astype(F32)
    limit_across_tiles = jnp.where(first_tile, -n, 0)
    heads = range(DIL_HEADS_PER_GROUP)
    cols_of = [slice(h * HEAD_DIM, (h + 1) * HEAD_DIM) for h in heads]
    rows_of = [slice(s * n, (s + 1) * n) for s in range(DIL_SUBBLOCKS)]

    def prev_rows(s):
        if dilation == 1:
            return (slice(0, n), False, limit_across_tiles) if s == 0 else (rows_of[s - 1], True, 0)
        return rows_of[s], False, limit_across_tiles

    def scores(s):
        rows, from_cur, _ = prev_rows(s)
        out = []
        for h in heads:
            q = q_ref[rows_of[s], cols_of[h]]
            k_prev = (kc_ref if from_cur else kp_ref)[rows, cols_of[h]]
            out.append((lax.dot_general(q, kc_ref[rows_of[s], cols_of[h]], _NT,
                                        preferred_element_type=F32),
                        lax.dot_general(q, k_prev, _NT, preferred_element_type=F32)))
        return out

    def softmax(s, raw):
        valid_prev = steps_cur <= prev_rows(s)[2]
        out = []
        for h in heads:
            slope = slope_ref[group * DIL_HEADS_PER_GROUP + h]
            s_cur = jnp.where(valid_cur, raw[h][0] * SCALE - slope * dist_cur, NEG_INF)
            s_prev = jnp.where(valid_prev, raw[h][1] * SCALE - slope * dist_prev, NEG_INF)
            m = jnp.max(jnp.maximum(s_cur, s_prev), axis=-1, keepdims=True)
            p_cur = jnp.exp(s_cur - m)
            p_prev = jnp.exp(s_prev - m)
            den = jnp.sum(p_cur + p_prev, axis=-1, keepdims=True)
            out.append((p_cur.astype(BF16), p_prev.astype(BF16), m, den))
        return out

    def values(s, probs):
        rows, from_cur, _ = prev_rows(s)
        for h in heads:
            p_cur, p_prev, m, den = probs[h]
            v_prev = (vc_ref if from_cur else vp_ref)[rows, cols_of[h]]
            o = (jnp.dot(p_cur, vc_ref[rows_of[s], cols_of[h]], preferred_element_type=F32)
                 + jnp.dot(p_prev, v_prev, preferred_element_type=F32)) / den
            lse = jnp.broadcast_to(m + jnp.log(den), (n, HEAD_DIM))
            if dilation == 1:
                dst_o, dst_lse, rows_out = o_ref, lse_ref, rows_of[s]
            elif chunks == 1:
                dst_o, dst_lse, rows_out = o_ref, lse_ref, pl.ds(s, n, stride=dilation)
            else:
                dst_o, dst_lse = scratch
                rows_out = pl.ds(pl.multiple_of((chunk * DIL_SUBBLOCKS + s) * n, n), n)
            dst_o[h, rows_out, :] = o
            dst_lse[h, rows_out, :] = lse

    raw, probs = {0: scores(0)}, {}
    for s in range(DIL_SUBBLOCKS + 1):
        if s + 1 < DIL_SUBBLOCKS:
            raw[s + 1] = scores(s + 1)
        if s < DIL_SUBBLOCKS:
            probs[s] = softmax(s, raw.pop(s))
        if s >= 1:
            values(s - 1, probs.pop(s - 1))

    if chunks > 1:
        o_tile, lse_tile = scratch

        @pl.when(chunk == chunks - 1)
        def _():
            for h in heads:
                for r in range(dilation):
                    natural = pl.ds(r, n, stride=dilation)
                    o_ref[h, natural, :] = o_tile[h, r * n:(r + 1) * n, :]
                    lse_ref[h, natural, :] = lse_tile[h, r * n:(r + 1) * n, :]


def _dilated_group(proj, slopes, group, dilation):
    b, s, _ = proj.shape
    n = DIL_STEPS
    step_rows = DIL_SUBBLOCKS * n
    assert dilation == 1 or dilation % DIL_SUBBLOCKS == 0
    chunks = max(dilation // DIL_SUBBLOCKS, 1)
    tile = step_rows * chunks
    assert s % tile == 0
    groups = len(DIL_CONFIGS)
    prev_rows = n if dilation == 1 else step_rows
    blocks_per_step = step_rows // prev_rows

    def cur(section):
        return pl.BlockSpec((None, step_rows, DIL_OUT),
                            lambda bi, i, c: (bi, i * chunks + c, section * groups + group))

    def prev(section):
        def index(bi, i, c):
            if dilation == 1:
                return (bi, jnp.maximum(i * blocks_per_step - 1, 0), section * groups + group)
            return (bi, jnp.maximum(i - 1, 0) * chunks + c, section * groups + group)
        return pl.BlockSpec((None, prev_rows, DIL_OUT), index)

    out_block = (DIL_HEADS_PER_GROUP, tile, HEAD_DIM)
    out_spec = pl.BlockSpec((None,) + out_block, lambda bi, i, c: (bi, 0, i, 0))
    out_sds = jax.ShapeDtypeStruct((b, DIL_HEADS_PER_GROUP, s, HEAD_DIM), F32)
    scratch = [pltpu.VMEM(out_block, F32)] * 2 if chunks > 1 else []
    o, lse = pl.pallas_call(
        functools.partial(_dilated_body, dilation=dilation, group=group),
        grid=(b, s // tile, chunks),
        in_specs=[pl.BlockSpec(memory_space=pltpu.SMEM),
                  cur(0), prev(1), cur(1), prev(2), cur(2)],
        out_specs=[out_spec, out_spec],
        out_shape=[out_sds, out_sds],
        scratch_shapes=scratch,
        compiler_params=_params("parallel", "arbitrary", "arbitrary"),
        name=f"dilated_attn_g{group}",
    )(slopes, proj, proj, proj, proj, proj)
    return o, lse


def _moba_body(slope_ref, q_ref, k_ref, v_ref, o_ref, kmean_ref, vt_ref, bias_ref, sel_ref,
               m_ref, acc_ref, ahead_ref, *, nblk, group):
    blk = MOBA_BLOCK
    hd = HEAD_DIM
    heads = q_ref.shape[1] // hd
    own = pl.program_id(2)
    key_off = lax.broadcasted_iota(jnp.int32, (blk, blk), 0)
    qry_off = lax.broadcasted_iota(jnp.int32, (blk, blk), 1)
    slope2 = [slope_ref[pl.program_id(1) * heads + h] * LOG2E for h in range(heads)]

    @pl.when(own == 0)
    def _():
        def fill(jb, carry):
            start = pl.multiple_of(jb * blk, blk)
            rows = k_ref[pl.ds(start, blk), :].astype(F32)
            kmean_ref[pl.ds(jb, 1), :] = jnp.mean(rows, axis=0, keepdims=True)
            vrows = v_ref[pl.ds(start, blk), :].astype(F32)
            for h in range(heads):
                vt_ref[h, :hd, pl.ds(start, blk)] = vrows[:, h * hd:(h + 1) * hd].T.astype(BF16)
            return carry
        lax.fori_loop(0, nblk, fill, 0)
        for h in range(heads):
            vt_ref[h, hd:, :] = jnp.ones((vt_ref.shape[1] - hd, vt_ref.shape[2]), BF16)
        for h in range(heads):
            bias_ref[h] = -slope2[h] * (qry_off - key_off).astype(F32)

    q = [q_ref[:, h * hd:(h + 1) * hd] for h in range(heads)]

    blk_id = lax.broadcasted_iota(jnp.int32, (nblk, blk), 0).astype(F32)
    kmean = kmean_ref[...]
    km_hi = kmean.astype(BF16)
    rest = kmean - km_hi.astype(F32)
    km_mid = rest.astype(BF16)
    km_lo = (rest - km_mid.astype(F32)).astype(BF16)
    km3 = jnp.concatenate([km_hi, km_mid, km_lo], axis=0)
    gates = []
    for h in range(heads):
        parts = lax.dot_general(km3[:, h * hd:(h + 1) * hd], q[h], _NT, preferred_element_type=F32)
        gate = parts[:nblk] + (parts[nblk:2 * nblk] + parts[2 * nblk:])
        gates.append(jnp.where(blk_id < own.astype(F32), gate, NEG_INF))

    def scores(h, start, rows):
        k = k_ref[pl.ds(start, rows), h * hd:(h + 1) * hd]
        return lax.dot_general(k, q[h], _NT, preferred_element_type=F32)

    own_start = pl.multiple_of(own * blk, blk)
    own_x = [scores(h, own_start, blk) for h in range(heads)]

    sels = [jnp.zeros((nblk, blk), F32) for _ in range(heads)]
    for _ in range(MOBA_TOPK):
        for h in range(heads):
            best = jnp.max(gates[h], axis=0, keepdims=True)
            is_best = (gates[h] == best) & (gates[h] > NEG_INF)
            pick = jnp.min(jnp.where(is_best, blk_id, float(nblk)), axis=0, keepdims=True)
            picked = blk_id == pick
            sels[h] = jnp.where(picked, 1.0, sels[h])
            gates[h] = jnp.where(picked, NEG_INF, gates[h])
    for h in range(heads):
        sel_ref[h] = sels[h]

    own_p = []
    for h in range(heads):
        x = jnp.where(qry_off >= key_off, own_x[h] + bias_ref[h], NEG_INF)
        m0 = jnp.max(x, axis=0, keepdims=True)
        own_p.append((m0, jnp.exp2(x - m0).astype(BF16)))
    for h in range(heads):
        m0, p = own_p[h]
        m_ref[h] = m0
        acc_ref[h] = jnp.dot(vt_ref[h, :, pl.ds(own_start, blk)], p, preferred_element_type=F32)

    def softmax_update(h, i, x, m):
        xs, chosen, shift = [], [], []
        m_new = m
        for g in range(group):
            j = i * group + g
            xs.append(x[g * blk:(g + 1) * blk] + bias_ref[h])
            chosen.append(sel_ref[h, pl.ds(j, 1), :] > 0.5)
            shift.append(-slope2[h] * ((own - j) * blk).astype(F32))
            top = jnp.max(xs[g], axis=0, keepdims=True) + shift[g]
            m_new = jnp.maximum(m_new, jnp.where(chosen[g], top, NEG_INF))
        ps = []
        for g in range(group):
            ref_g = jnp.where(chosen[g], m_new - shift[g], jnp.inf)
            ps.append(jnp.exp2(xs[g] - ref_g).astype(BF16))
        return m_new, jnp.exp2(m - m_new), jnp.concatenate(ps, axis=0)

    def past_blocks(i, carry):
        start = pl.multiple_of(i * (group * blk), group * blk)
        x, sm = {0: ahead_ref[...]}, {}
        for h in range(heads + 1):
            if h + 1 < heads:
                x[h + 1] = scores(h + 1, start, group * blk)
            elif h + 1 == heads:
                ahead_ref[...] = scores(0, group_start(i + 1), group * blk)
            if h < heads:
                sm[h] = softmax_update(h, i, x[h], m_ref[h])
            if h >= 1:
                m_new, alpha, p = sm[h - 1]
                m_ref[h - 1] = m_new
                acc_ref[h - 1] = alpha * acc_ref[h - 1] + jnp.dot(
                    vt_ref[h - 1, :, pl.ds(start, group * blk)], p, preferred_element_type=F32)
        return carry

    def group_start(i):
        return pl.multiple_of(jnp.minimum(i, nblk // group - 1) * (group * blk), group * blk)

    n_groups = lax.div(own + (group - 1), group)
    ahead_ref[...] = scores(0, group_start(0), group * blk)
    lax.fori_loop(0, n_groups, past_blocks, 0)
    for h in range(heads):
        acc = acc_ref[h]
        o_ref[:, h * hd:(h + 1) * hd] = (acc[:hd] / acc[hd:hd + 1]).T.astype(o_ref.dtype)


def _moba(proj, slopes):
    b, s, _ = proj.shape
    assert s % (MOBA_BLOCK * MOBA_GROUP) == 0
    nblk = s // MOBA_BLOCK
    hp = MOBA_HEADS_PER_STEP
    width = hp * HEAD_DIM
    q0 = 3 * DIL_WIDTH // width
    k0 = q0 + MOBA_WIDTH // width
    v0 = k0 + MOBA_WIDTH // width
    y = pl.pallas_call(
        functools.partial(_moba_body, nblk=nblk, group=MOBA_GROUP),
        grid=(b, MOBA_HEADS // hp, nblk),
        in_specs=[
            pl.BlockSpec(memory_space=pltpu.SMEM),
            pl.BlockSpec((None, MOBA_BLOCK, width), lambda bi, h, i: (bi, i, q0 + h)),
            pl.BlockSpec((None, s, width), lambda bi, h, i: (bi, 0, k0 + h)),
            pl.BlockSpec((None, s, width), lambda bi, h, i: (bi, 0, v0 + h)),
        ],
        out_specs=pl.BlockSpec((None, MOBA_BLOCK, width), lambda bi, h, i: (bi, i, h)),
        out_shape=jax.ShapeDtypeStruct((b, s, MOBA_WIDTH), BF16),
        scratch_shapes=[pltpu.VMEM((nblk, width), F32),
                        pltpu.VMEM((hp, HEAD_DIM + BF16_ROWS, s), BF16),
                        pltpu.VMEM((hp, MOBA_BLOCK, MOBA_BLOCK), F32),
                        pltpu.VMEM((hp, nblk, MOBA_BLOCK), F32),
                        pltpu.VMEM((hp, 1, MOBA_BLOCK), F32),
                        pltpu.VMEM((hp, HEAD_DIM + BF16_ROWS, MOBA_BLOCK), F32),
                        pltpu.VMEM((MOBA_GROUP * MOBA_BLOCK, MOBA_BLOCK), F32)],
        compiler_params=_params("parallel", "parallel", "arbitrary"),
        name="moba_attn",
    )(slopes, proj, proj, proj)
    return y.reshape(b * s, MOBA_WIDTH)


def _merged_branches(x, o_refs, l_refs, ym_ref, gd_ref, gm_ref, wud_ref, wum_ref, wo_ref):
    heads = []
    for h in range(DIL_HEADS_PER_GROUP):
        l0, l1, l2 = (l_ref[h] for l_ref in l_refs)
        m = jnp.maximum(jnp.maximum(l0, l1), l2)
        e0, e1, e2 = jnp.exp(l0 - m), jnp.exp(l1 - m), jnp.exp(l2 - m)
        den = e0 + e1 + e2
        mixed = ((e0 / den) * o_refs[0][h] + (e1 / den) * o_refs[1][h] + (e2 / den) * o_refs[2][h])
        heads.append(mixed.astype(BF16))
    y_dil = jnp.concatenate(heads, axis=-1)
    lift_dil = jnp.dot(y_dil, wud_ref[...], preferred_element_type=F32)
    lift_moba = jnp.dot(ym_ref[...], wum_ref[...], preferred_element_type=F32)
    merged = gd_ref[...].astype(F32) * lift_dil + gm_ref[...].astype(F32) * lift_moba
    return x + jnp.dot(merged.astype(BF16), wo_ref[...], preferred_element_type=F32)


def _cross_attended(x, g_ref, wq_ref, kv_ref, wo_ref):
    h = _rms(x, g_ref[...]).astype(BF16)
    q = jnp.dot(h, wq_ref[...], preferred_element_type=F32).astype(BF16)
    heads = []
    for hd in range(MEM_HEADS):
        k = kv_ref[:, hd * HEAD_DIM:(hd + 1) * HEAD_DIM]
        v = kv_ref[:, MEM_WIDTH + hd * HEAD_DIM:MEM_WIDTH + (hd + 1) * HEAD_DIM]
        s = lax.dot_general(q[:, hd * HEAD_DIM:(hd + 1) * HEAD_DIM], k, _NT,
                            preferred_element_type=F32) * SCALE
        p = jnp.exp(s - jnp.max(s, axis=-1, keepdims=True))
        den = jnp.sum(p, axis=-1, keepdims=True)
        heads.append((jnp.dot(p.astype(BF16), v, preferred_element_type=F32) / den).astype(BF16))
    o = jnp.concatenate(heads, axis=-1)
    return x + jnp.dot(o, wo_ref[...], preferred_element_type=F32)


def _merge_cross_body(x_ref, o0_ref, o1_ref, o2_ref, l0_ref, l1_ref, l2_ref, ym_ref, gd_ref, gm_ref,
                      wud_ref, wum_ref, wo_ref, gc_ref, wq_ref, kv_ref, wom_ref, out_ref):
    x1 = _merged_branches(x_ref[...], (o0_ref, o1_ref, o2_ref), (l0_ref, l1_ref, l2_ref),
                          ym_ref, gd_ref, gm_ref, wud_ref, wum_ref, wo_ref)
    out_ref[...] = _cross_attended(x1, gc_ref, wq_ref, kv_ref, wom_ref)


def _merge_cross(x, dil_outs, dil_lses, y_moba, gates, w_up_dil, w_up_moba, w_out,
                 g_cross, w_q, kv, w_o, *, tm):
    t, d = x.shape
    s = dil_outs[0].shape[2]
    mem_len = kv.shape[1]
    assert t % tm == 0 and s % tm == 0
    per_batch = s // tm
    row = lambda w: pl.BlockSpec((tm, w), lambda i: (i, 0))
    dil = pl.BlockSpec((None, DIL_HEADS_PER_GROUP, tm, HEAD_DIM),
                       lambda i: (i // per_batch, 0, i % per_batch, 0))
    return pl.pallas_call(
        _merge_cross_body,
        grid=(t // tm,),
        in_specs=[row(d)] + [dil] * 6 + [
            row(MOBA_WIDTH),
            pl.BlockSpec((tm, d), lambda i: (i, 0)),
            pl.BlockSpec((tm, d), lambda i: (i, 1)),
            _resident((DIL_OUT, d)), _resident((MOBA_WIDTH, d)), _resident((d, d)),
            _resident((1, d)), _resident((d, MEM_WIDTH)),
            pl.BlockSpec((None, mem_len, 2 * MEM_WIDTH), lambda i: (i // per_batch, 0, 0)),
            _resident((MEM_WIDTH, d)),
        ],
        out_specs=row(d),
        out_shape=jax.ShapeDtypeStruct((t, d), F32),
        compiler_params=_params("parallel"),
        name="merge_out_proj_cross_attn",
    )(x, *dil_outs, *dil_lses, y_moba, gates, gates, w_up_dil, w_up_moba, w_out,
      g_cross.reshape(1, d), w_q, kv, w_o)


ROUTE_COLS = 8
MOE_ROW_TILE = 256


def _route_body(x_ref, g_ref, wr_ref, br_ref, info_ref, counts_ref, run_ref, tri_ref):
    tm = x_ref.shape[0]
    n_route = N_GROUPS + N_EXPERTS
    lane = lax.broadcasted_iota(jnp.int32, (tm, n_route), 1).astype(F32)

    @pl.when(pl.program_id(0) == 0)
    def _():
        run_ref[...] = jnp.zeros_like(run_ref)
        earlier = (lax.broadcasted_iota(jnp.int32, (tm, tm), 0)
                   > lax.broadcasted_iota(jnp.int32, (tm, tm), 1))
        tri_ref[...] = jnp.where(earlier, 1.0, 0.0).astype(BF16)

    t = _rms(x_ref[...], g_ref[...])
    logits = _dot_split3(t, wr_ref[...]) + br_ref[...]
    none = float(n_route)
    glog = jnp.where(lane < N_GROUPS, logits, NEG_INF)
    gmax = jnp.max(glog, axis=-1, keepdims=True)
    gsel = jnp.min(jnp.where(glog == gmax, lane, none), axis=-1, keepdims=True)
    pg = 1.0 / jnp.sum(jnp.exp(glog - gmax), axis=-1, keepdims=True)
    first = N_GROUPS + gsel * EXPERTS_PER_GROUP
    in_group = (lane >= first) & (lane < first + EXPERTS_PER_GROUP)
    elog = jnp.where(in_group, logits, NEG_INF)
    top1 = jnp.max(elog, axis=-1, keepdims=True)
    i1 = jnp.min(jnp.where(elog == top1, lane, none), axis=-1, keepdims=True)
    rest = jnp.where(lane == i1, NEG_INF, elog)
    top2 = jnp.max(rest, axis=-1, keepdims=True)
    i2 = jnp.min(jnp.where(rest == top2, lane, none), axis=-1, keepdims=True)
    e2 = jnp.exp(top2 - top1)
    w1 = pg / (1.0 + e2)
    w2 = pg * e2 / (1.0 + e2)

    hit1 = lane == i1
    hit2 = lane == i2
    assigned = jnp.where(hit1 | hit2, 1.0, 0.0)
    before = jnp.dot(tri_ref[...], assigned.astype(BF16), preferred_element_type=F32) + run_ref[...]
    rank1 = jnp.sum(jnp.where(hit1, before, 0.0), axis=-1, keepdims=True)
    rank2 = jnp.sum(jnp.where(hit2, before, 0.0), axis=-1, keepdims=True)
    run_ref[...] += jnp.sum(assigned, axis=0, keepdims=True)
    counts_ref[...] = run_ref[...]

    col = lax.broadcasted_iota(jnp.int32, (tm, ROUTE_COLS), 1)
    fields = (i1 - N_GROUPS, i2 - N_GROUPS, rank1, rank2, w1, w2)
    info = jnp.zeros((tm, ROUTE_COLS), F32)
    for c, field in enumerate(fields):
        info = jnp.where(col == c, field, info)
    info_ref[...] = info


def _route(x, g, w_route, b_route, *, tm):
    t, d = x.shape
    n_route = N_GROUPS + N_EXPERTS
    assert t % tm == 0
    return pl.pallas_call(
        _route_body,
        grid=(t // tm,),
        in_specs=[
            pl.BlockSpec((tm, d), lambda i: (i, 0)),
            pl.BlockSpec((1, d), lambda i: (0, 0)),
            pl.BlockSpec((d, n_route), lambda i: (0, 0)),
            pl.BlockSpec((1, n_route), lambda i: (0, 0)),
        ],
        out_specs=[pl.BlockSpec((tm, ROUTE_COLS), lambda i: (i, 0)),
                   pl.BlockSpec((1, n_route), lambda i: (0, 0))],
        out_shape=[jax.ShapeDtypeStruct((t, ROUTE_COLS), F32),
                   jax.ShapeDtypeStruct((1, n_route), F32)],
        scratch_shapes=[pltpu.VMEM((1, n_route), F32), pltpu.VMEM((tm, tm), BF16)],
        compiler_params=_params("arbitrary"),
        name="moe_route",
    )(x, g.reshape(1, d), w_route, b_route.reshape(1, n_route))


def _row_copies_wait(src_rows, dst_rows, sem):
    pltpu.make_async_copy(src_rows, dst_rows, sem).wait()


def _slot_block(tm):
    return pl.BlockSpec((MOE_TOPK, tm), lambda i, *_: (0, i), memory_space=pltpu.SMEM)


def _dispatch_body(fill_ref, pos_ref, x_ref, g_ref, sorted_ref, t_ref, zero_ref, sems, fill_sem):
    step = pl.program_id(0)
    tm = x_ref.shape[0]
    tr = zero_ref.shape[0]
    slot = lax.rem(step, 2)

    @pl.when(pl.program_id(0) == 0)
    def _():
        zero_ref[...] = jnp.zeros_like(zero_ref)

        def fill_copy(tile):
            return pltpu.make_async_copy(zero_ref, sorted_ref.at[pl.ds(tile * tr, tr)], fill_sem)

        def start(idx, carry):
            pl.when(fill_ref[idx] >= 0)(lambda: fill_copy(fill_ref[idx]).start())
            return carry

        def finish(idx, carry):
            pl.when(fill_ref[idx] >= 0)(lambda: fill_copy(fill_ref[idx]).wait())
            return carry

        lax.fori_loop(0, fill_ref.shape[0], start, 0)
        lax.fori_loop(0, fill_ref.shape[0], finish, 0)

    t_ref[slot] = _rms(x_ref[...], g_ref[...])

    def issue(r, carry):
        for k in range(MOE_TOPK):
            dst = pos_ref[k, r]
            pltpu.make_async_copy(t_ref.at[slot, pl.ds(r, 1)], sorted_ref.at[pl.ds(dst, 1)],
                                  sems.at[slot]).start()
        return carry

    def copies_done(buf):
        for _ in range(MOE_TOPK):
            _row_copies_wait(t_ref.at[buf], sorted_ref.at[pl.ds(0, tm)], sems.at[buf])

    lax.fori_loop(0, tm, issue, 0, unroll=8)
    pl.when(step > 0)(lambda: copies_done(1 - slot))
    pl.when(step == pl.num_programs(0) - 1)(lambda: copies_done(slot))


def _dispatch(x, g, pos, fill_tiles, n_rows, *, tm):
    t, d = x.shape
    assert t % tm == 0
    return pl.pallas_call(
        _dispatch_body,
        grid_spec=pltpu.PrefetchScalarGridSpec(
            num_scalar_prefetch=1,
            grid=(t // tm,),
            in_specs=[_slot_block(tm),
                      pl.BlockSpec((tm, d), lambda i, fill: (i, 0)),
                      pl.BlockSpec((1, d), lambda i, fill: (0, 0))],
            out_specs=pl.BlockSpec(memory_space=pl.ANY),
            scratch_shapes=[pltpu.VMEM((2, tm, d), F32), pltpu.VMEM((MOE_ROW_TILE, d), F32),
                            pltpu.SemaphoreType.DMA((2,)), pltpu.SemaphoreType.DMA(())],
        ),
        out_shape=jax.ShapeDtypeStruct((n_rows, d), F32),
        compiler_params=pltpu.CompilerParams(dimension_semantics=("arbitrary",),
                                             vmem_limit_bytes=VMEM_LIMIT_BYTES,
                                             disable_bounds_checks=True),
        name="moe_dispatch",
    )(fill_tiles, pos, x, g.reshape(1, d))


def _experts_body(tile_expert_ref, n_tiles_ref, x_ref, wg_ref, wu_ref, wd_ref, y_ref):
    del tile_expert_ref
    in_use = pl.program_id(0) < n_tiles_ref[0]

    @pl.when(in_use)
    def _():
        t = x_ref[...].astype(BF16)
        gate = jnp.dot(t, wg_ref[...].astype(BF16), preferred_element_type=F32)
        up = jnp.dot(t, wu_ref[...].astype(BF16), preferred_element_type=F32)
        a = jax.nn.silu(gate) * up
        y_ref[...] = jnp.dot(a.astype(BF16), wd_ref[...].astype(BF16), preferred_element_type=F32)

    @pl.when(jnp.logical_not(in_use))
    def _():
        y_ref[...] = jnp.zeros_like(y_ref)


def _experts(sorted_rows, tile_expert, n_tiles, w_gate, w_up, w_down):
    p, d = sorted_rows.shape
    ff = w_gate.shape[-1]
    tr = MOE_ROW_TILE
    assert p % tr == 0
    used = lambda i, nt: jnp.minimum(i, nt[0] - 1)
    return pl.pallas_call(
        _experts_body,
        grid_spec=pltpu.PrefetchScalarGridSpec(
            num_scalar_prefetch=2,
            grid=(p // tr,),
            in_specs=[pl.BlockSpec((tr, d), lambda i, te, nt: (used(i, nt), 0)),
                      pl.BlockSpec((None, d, ff), lambda i, te, nt: (te[used(i, nt)], 0, 0)),
                      pl.BlockSpec((None, d, ff), lambda i, te, nt: (te[used(i, nt)], 0, 0)),
                      pl.BlockSpec((None, ff, d), lambda i, te, nt: (te[used(i, nt)], 0, 0))],
            out_specs=pl.BlockSpec((tr, d), lambda i, te, nt: (i, 0)),
        ),
        out_shape=jax.ShapeDtypeStruct((p, d), F32),
        compiler_params=_params("arbitrary"),
        name="moe_experts",
    )(tile_expert, n_tiles, sorted_rows, w_gate, w_up, w_down)


def _combine_body(pos_ref, pos_next_ref, x_ref, info_ref, gf_ref, y_sorted_ref, out_ref,
                  rows_ref, sems):
    step = pl.program_id(0)
    tm = x_ref.shape[0]
    slot = lax.rem(step, 2)

    def fetch(table_ref, buf):
        def issue(r, carry):
            for k in range(MOE_TOPK):
                src = table_ref[k, r]
                pltpu.make_async_copy(y_sorted_ref.at[pl.ds(src, 1)],
                                      rows_ref.at[buf, k, pl.ds(r, 1)], sems.at[buf]).start()
            return carry
        lax.fori_loop(0, tm, issue, 0, unroll=8)

    pl.when(step == 0)(lambda: fetch(pos_ref, 0))
    pl.when(step + 1 < pl.num_programs(0))(lambda: fetch(pos_next_ref, 1 - slot))
    for k in range(MOE_TOPK):
        _row_copies_wait(y_sorted_ref.at[pl.ds(0, tm)], rows_ref.at[slot, k], sems.at[slot])

    info = info_ref[...]
    y = info[:, 4:5] * rows_ref[slot, 0] + info[:, 5:6] * rows_ref[slot, 1]
    out_ref[...] = _rms(x_ref[...] + y, gf_ref[...])


def _combine(x, info, pos, y_sorted, g_final, *, tm):
    t, d = x.shape
    assert t % tm == 0
    return pl.pallas_call(
        _combine_body,
        grid=(t // tm,),
        in_specs=[_slot_block(tm),
                  pl.BlockSpec((MOE_TOPK, tm), lambda i: (0, jnp.minimum(i + 1, t // tm - 1)),
                               memory_space=pltpu.SMEM),
                  pl.BlockSpec((tm, d), lambda i: (i, 0)),
                  pl.BlockSpec((tm, ROUTE_COLS), lambda i: (i, 0)),
                  pl.BlockSpec((1, d), lambda i: (0, 0)),
                  pl.BlockSpec(memory_space=pl.ANY)],
        out_specs=pl.BlockSpec((tm, d), lambda i: (i, 0)),
        scratch_shapes=[pltpu.VMEM((2, MOE_TOPK, tm, d), F32), pltpu.SemaphoreType.DMA((2,))],
        out_shape=jax.ShapeDtypeStruct((t, d), F32),
        compiler_params=pltpu.CompilerParams(dimension_semantics=("arbitrary",),
                                             vmem_limit_bytes=VMEM_LIMIT_BYTES,
                                             disable_bounds_checks=True),
        name="moe_combine_final_norm",
    )(pos, pos, x, info, g_final.reshape(1, d), y_sorted)


def _moe(x, g, w_route, b_route, w_gate, w_up, w_down, g_final):
    t, d = x.shape
    tr = MOE_ROW_TILE
    assert (MOE_TOPK * t) % tr == 0
    info, counts = _route(x, g, w_route, b_route, tm=min(512, t))

    expert = info[:, 0:MOE_TOPK].astype(jnp.int32)
    rank = info[:, MOE_TOPK:2 * MOE_TOPK].astype(jnp.int32)
    count = counts[0, N_GROUPS:].astype(jnp.int32)
    seg_tiles = (count + (tr - 1)) // tr
    seg_end = jnp.cumsum(seg_tiles)
    seg_start_row = (seg_end - seg_tiles) * tr
    pos = (seg_start_row[expert] + rank).T
    max_tiles = (MOE_TOPK * t) // tr + N_EXPERTS
    tile_id = jnp.arange(max_tiles, dtype=jnp.int32)
    tile_expert = jnp.minimum(
        jnp.sum((seg_end[None, :] <= tile_id[:, None]).astype(jnp.int32), axis=1), N_EXPERTS - 1)
    n_tiles = seg_end[-1:].astype(jnp.int32)

    tail = n_tiles + jnp.arange(N_EXPERTS, dtype=jnp.int32)
    fill_tiles = jnp.concatenate([jnp.where(seg_tiles > 0, seg_end - 1, -1),
                                  jnp.where(tail < max_tiles, tail, -1)]).astype(jnp.int32)

    sorted_rows = _dispatch(x, g, pos, fill_tiles, max_tiles * tr, tm=min(512, t))
    y_sorted = _experts(sorted_rows, tile_expert, n_tiles, w_gate, w_up, w_down)
    return _combine(x, info, pos, y_sorted, g_final, tm=min(512, t))


def _layer(x, mem, attn_norm, w_in, w_up_dil, w_up_moba, w_branch_gate, w_out, cross_norm,
           mem_norm, w_q_mem, w_kv_mem, w_o_mem):
    b, s, d = x.shape
    t = b * s
    xt = x.reshape(t, d)
    h = _norm(xt, attn_norm, tm=min(512, t), name="attn_norm")
    proj, gates = _in_proj(h, w_in, w_branch_gate, tm=min(2048, t))
    proj = proj.reshape(b, s, IN_WIDTH)
    dil_slopes = _alibi_slopes(DIL_HEADS)
    dil = [_dilated_group(proj, dil_slopes, g, dilation)
           for g, (_, dilation) in enumerate(DIL_CONFIGS)]
    y_moba = _moba(proj, _alibi_slopes(MOBA_HEADS))
    mem_len = mem.shape[1]
    kv = _norm_matmul(mem.reshape(b * mem_len, d), mem_norm, w_kv_mem.astype(BF16),
                      tm=b * mem_len, tn=512, name="norm_mem_kv").reshape(b, mem_len, 2 * MEM_WIDTH)
    return _merge_cross(xt, [o for o, _ in dil], [l for _, l in dil], y_moba, gates,
                        w_up_dil.astype(BF16), w_up_moba.astype(BF16), w_out.astype(BF16),
                        cross_norm, w_q_mem.astype(BF16), kv, w_o_mem.astype(BF16), tm=min(256, s))


def kernel(x, mem, attn_norm, w_in, w_up_dil, w_up_moba, w_branch_gate, w_out, cross_norm, mem_norm,
           w_q_mem, w_kv_mem, w_o_mem, ffn_norm, w_router_group, b_router_group, w_router_expert,
           b_router_expert, w_expert_gate, w_expert_up, w_expert_down, final_norm):
    b, s, d = x.shape
    depth = attn_norm.shape[0]
    assert depth == 1, "the final norm is fused into the last layer's MoE call"
    l = 0
    x2 = _layer(x, mem, attn_norm[l], w_in[l], w_up_dil[l], w_up_moba[l], w_branch_gate[l], w_out[l],
                cross_norm[l], mem_norm[l], w_q_mem[l], w_kv_mem[l], w_o_mem[l])
    w_route = jnp.concatenate([w_router_group[l], w_router_expert[l]], axis=1)
    b_route = jnp.concatenate([b_router_group[l], b_router_expert[l]], axis=0)
    out = _moe(x2, ffn_norm[l], w_route, b_route, w_expert_gate[l], w_expert_up[l],
               w_expert_down[l], final_norm)
    return out.reshape(b, s, d)
```

```python
import functools

import numpy as np
import jax
import jax.numpy as jnp
from jax import lax
from jax.experimental import pallas as pl
from jax.experimental.pallas import tpu as pltpu

F32 = jnp.float32
BF16 = jnp.bfloat16

HEAD_DIM = 128
DIL_CONFIGS = ((128, 1), (512, 4), (2048, 16))
DIL_HEADS_PER_GROUP = 4
DIL_HEADS = DIL_HEADS_PER_GROUP * len(DIL_CONFIGS)
DIL_WIDTH = DIL_HEADS * HEAD_DIM
DIL_OUT = DIL_HEADS_PER_GROUP * HEAD_DIM
DIL_STEPS = 128
MOBA_HEADS = 8
MOBA_WIDTH = MOBA_HEADS * HEAD_DIM
MOBA_BLOCK = 256
MOBA_TOPK = 3
IN_WIDTH = 3 * (DIL_WIDTH + MOBA_WIDTH)
MEM_HEADS = 4
MEM_WIDTH = MEM_HEADS * HEAD_DIM
N_GROUPS = 4
EXPERTS_PER_GROUP = 8
N_EXPERTS = N_GROUPS * EXPERTS_PER_GROUP
MOE_TOPK = 2
RMS_EPS = 1e-6
SCALE = HEAD_DIM ** -0.5
NEG_INF = float("-inf")
LOG2E = 1.4426950408889634
MOBA_GROUP = 4
MOBA_HEADS_PER_STEP = 4
MOBA_Q_TILES = (9, 11)

VMEM_LIMIT_BYTES = 56 * 1024 * 1024
BF16_ROWS = 16
MXU_COLUMNS = 256
_NT = (((1,), (1,)), ((), ()))


def _alibi_slopes(n):
    return jnp.asarray(2.0 ** (-8.0 * np.arange(1, n + 1) / n), dtype=F32)


def _rms(x, g):
    return x * lax.rsqrt(jnp.mean(x * x, axis=-1, keepdims=True) + RMS_EPS) * g


def _dot_split3(a, b):
    a_hi = a.astype(BF16)
    b_hi = b.astype(BF16)
    a_lo = (a - a_hi.astype(F32)).astype(BF16)
    b_lo = (b - b_hi.astype(F32)).astype(BF16)
    dot = functools.partial(jnp.dot, preferred_element_type=F32)
    n = b.shape[1]
    if 2 * n <= MXU_COLUMNS:
        both = dot(a_hi, jnp.concatenate([b_hi, b_lo], axis=1))
        return both[:, :n] + (dot(a_lo, b_hi) + both[:, n:])
    return dot(a_hi, b_hi) + (dot(a_lo, b_hi) + dot(a_hi, b_lo))


def _params(*sem, flags=None):
    return pltpu.CompilerParams(dimension_semantics=sem, vmem_limit_bytes=VMEM_LIMIT_BYTES,
                                flags=flags)


def _resident(shape):
    nd = len(shape)
    return pl.BlockSpec(shape, lambda *_: (0,) * nd, pipeline_mode=pl.Buffered(1))


def _norm_matmul_body(x_ref, g_ref, w_ref, o_ref, h_ref, *, sigmoid):
    @pl.when(pl.program_id(1) == 0)
    def _():
        h_ref[...] = _rms(x_ref[...], g_ref[...]).astype(BF16)

    acc = jnp.dot(h_ref[...], w_ref[...], preferred_element_type=F32)
    if sigmoid:
        acc = jax.nn.sigmoid(acc)
    o_ref[...] = acc.astype(o_ref.dtype)


def _norm_matmul(x, g, w, *, tm, tn, sigmoid=False, name):
    m, d = x.shape
    n = w.shape[1]
    assert m % tm == 0 and n % tn == 0
    return pl.pallas_call(
        functools.partial(_norm_matmul_body, sigmoid=sigmoid),
        grid=(m // tm, n // tn),
        in_specs=[
            pl.BlockSpec((tm, d), lambda i, j: (i, 0)),
            pl.BlockSpec((1, d), lambda i, j: (0, 0)),
            pl.BlockSpec((d, tn), lambda i, j: (0, j)),
        ],
        out_specs=pl.BlockSpec((tm, tn), lambda i, j: (i, j)),
        out_shape=jax.ShapeDtypeStruct((m, n), BF16),
        scratch_shapes=[pltpu.VMEM((tm, d), BF16)],
        compiler_params=_params("parallel", "arbitrary"),
        name=name,
    )(x, g.reshape(1, d), w)


def _norm_body(x_ref, g_ref, o_ref):
    o_ref[...] = _rms(x_ref[...], g_ref[...]).astype(o_ref.dtype)


def _norm(x, g, *, tm, name):
    m, d = x.shape
    assert m % tm == 0
    return pl.pallas_call(
        _norm_body,
        grid=(m // tm,),
        in_specs=[pl.BlockSpec((tm, d), lambda i: (i, 0)), pl.BlockSpec((1, d), lambda i: (0, 0))],
        out_specs=pl.BlockSpec((tm, d), lambda i: (i, 0)),
        out_shape=jax.ShapeDtypeStruct((m, d), BF16),
        compiler_params=_params("parallel"),
        name=name,
    )(x, g.reshape(1, d))


def _in_proj_body(h_ref, win_ref, wbg_ref, proj_ref, gates_ref, perm_ref, *, n_proj_tiles):
    j = pl.program_id(1)
    tm = h_ref.shape[0]
    n = DIL_STEPS
    groups = len(DIL_CONFIGS)

    def store_regrouped(acc, dilation):
        tile = n * dilation
        for c in range(acc.shape[1] // HEAD_DIM):
            cols = slice(c * HEAD_DIM, (c + 1) * HEAD_DIM)
            perm_ref[c] = acc[:, cols]
            for t0 in range(0, tm, tile):
                for r in range(dilation):
                    rows = perm_ref[c, pl.ds(t0 + r, n, stride=dilation), :]
                    proj_ref[t0 + r * n:t0 + (r + 1) * n, cols] = rows.astype(BF16)

    @pl.when(j < n_proj_tiles)
    def _():
        acc = jnp.dot(h_ref[...], win_ref[...].astype(BF16), preferred_element_type=F32)
        group = jnp.where(j < 3 * groups, lax.rem(j, groups), 0)
        moba_q = (j >= MOBA_Q_TILES[0]) & (j < MOBA_Q_TILES[1])
        scale = jnp.where(moba_q, SCALE * LOG2E, 1.0)

        def store_natural():
            proj_ref[...] = (acc * scale).astype(BF16)

        for g, (_, dilation) in enumerate(DIL_CONFIGS):
            store = store_natural if dilation == 1 else functools.partial(store_regrouped, acc, dilation)
            pl.when(group == g)(store)

    @pl.when(j >= n_proj_tiles)
    def _():
        acc = jnp.dot(h_ref[...], wbg_ref[...].astype(BF16), preferred_element_type=F32)
        gates_ref[...] = jax.nn.sigmoid(acc).astype(BF16)


def _in_proj(h, w_in, w_bg, *, tm):
    t, d = h.shape
    tn = DIL_OUT
    assert MOBA_Q_TILES == (3 * DIL_WIDTH // tn, (3 * DIL_WIDTH + MOBA_WIDTH) // tn)
    assert t % tm == 0 and tm % (DIL_STEPS * max(dl for _, dl in DIL_CONFIGS)) == 0
    assert DIL_WIDTH == len(DIL_CONFIGS) * tn and w_in.shape[1] % tn == 0 and w_bg.shape[1] % tn == 0
    n_proj = w_in.shape[1] // tn
    n_gate = w_bg.shape[1] // tn
    return pl.pallas_call(
        functools.partial(_in_proj_body, n_proj_tiles=n_proj),
        grid=(t // tm, n_proj + n_gate),
        in_specs=[
            pl.BlockSpec((tm, d), lambda i, j: (i, 0)),
            pl.BlockSpec((d, tn), lambda i, j: (0, jnp.minimum(j, n_proj - 1))),
            pl.BlockSpec((d, tn), lambda i, j: (0, jnp.maximum(j - n_proj, 0))),
        ],
        out_specs=[
            pl.BlockSpec((tm, tn), lambda i, j: (i, jnp.minimum(j, n_proj - 1))),
            pl.BlockSpec((tm, tn), lambda i, j: (i, jnp.maximum(j - n_proj, 0))),
        ],
        out_shape=[jax.ShapeDtypeStruct((t, w_in.shape[1]), BF16),
                   jax.ShapeDtypeStruct((t, w_bg.shape[1]), BF16)],
        scratch_shapes=[pltpu.VMEM((tn // HEAD_DIM, tm, HEAD_DIM), F32)],
        compiler_params=_params("parallel", "arbitrary"),
        name="in_proj_gates",
    )(h, w_in, w_bg)


DIL_SUBBLOCKS = 4


def _dilated_body(slope_ref, q_ref, kp_ref, kc_ref, vp_ref, vc_ref, o_ref, lse_ref, *scratch,
                  dilation, group):
    n = DIL_STEPS
    first_tile = pl.program_id(1) == 0
    chunk = pl.program_id(2)
    chunks = max(dilation // DIL_SUBBLOCKS, 1)
    qi = lax.broadcasted_iota(jnp.int32, (n, n), 0)
    kj = lax.broadcasted_iota(jnp.int32, (n, n), 1)
    steps_cur = qi - kj
    valid_cur = steps_cur >= 0
    dist_cur = (steps_cur * dilation).astype(F32)
    dist_prev = ((steps_cur + n) * dilation).astype(F32)
    limit_across_tiles = jnp.where(first_tile, -n, 0)
    heads = range(DIL_HEADS_PER_GROUP)
    cols_of = [slice(h * HEAD_DIM, (h + 1) * HEAD_DIM) for h in heads]
    rows_of = [slice(s * n, (s + 1) * n) for s in range(DIL_SUBBLOCKS)]

    def prev_rows(s):
        if dilation == 1:
            return (slice(0, n), False, limit_across_tiles) if s == 0 else (rows_of[s - 1], True, 0)
        return rows_of[s], False, limit_across_tiles

    def scores(s):
        rows, from_cur, _ = prev_rows(s)
        out = []
        for h in heads:
            q = q_ref[rows_of[s], cols_of[h]]
            k_prev = (kc_ref if from_cur else kp_ref)[rows, cols_of[h]]
            out.append((lax.dot_general(q, kc_ref[rows_of[s], cols_of[h]], _NT,
                                        preferred_element_type=F32),
                        lax.dot_general(q, k_prev, _NT, preferred_element_type=F32)))
        return out

    def softmax(s, raw):
        valid_prev = steps_cur <= prev_rows(s)[2]
        out = []
        for h in heads:
            slope = slope_ref[group * DIL_HEADS_PER_GROUP + h]
            s_cur = jnp.where(valid_cur, raw[h][0] * SCALE - slope * dist_cur, NEG_INF)
            s_prev = jnp.where(valid_prev, raw[h][1] * SCALE - slope * dist_prev, NEG_INF)
            m = jnp.max(jnp.maximum(s_cur, s_prev), axis=-1, keepdims=True)
            p_cur = jnp.exp(s_cur - m)
            p_prev = jnp.exp(s_prev - m)
            den = jnp.sum(p_cur + p_prev, axis=-1, keepdims=True)
            out.append((p_cur.astype(BF16), p_prev.astype(BF16), m, den))
        return out

    def values(s, probs):
        rows, from_cur, _ = prev_rows(s)
        for h in heads:
            p_cur, p_prev, m, den = probs[h]
            v_prev = (vc_ref if from_cur else vp_ref)[rows, cols_of[h]]
            o = (jnp.dot(p_cur, vc_ref[rows_of[s], cols_of[h]], preferred_element_type=F32)
                 + jnp.dot(p_prev, v_prev, preferred_element_type=F32)) / den
            lse = jnp.broadcast_to(m + jnp.log(den), (n, HEAD_DIM))
            if dilation == 1:
                dst_o, dst_lse, rows_out = o_ref, lse_ref, rows_of[s]
            elif chunks == 1:
                dst_o, dst_lse, rows_out = o_ref, lse_ref, pl.ds(s, n, stride=dilation)
            else:
                dst_o, dst_lse = scratch
                rows_out = pl.ds(pl.multiple_of((chunk * DIL_SUBBLOCKS + s) * n, n), n)
            dst_o[h, rows_out, :] = o
            dst_lse[h, rows_out, :] = lse

    raw, probs = {0: scores(0)}, {}
    for s in range(DIL_SUBBLOCKS + 1):
        if s + 1 < DIL_SUBBLOCKS:
            raw[s + 1] = scores(s + 1)
        if s < DIL_SUBBLOCKS:
            probs[s] = softmax(s, raw.pop(s))
        if s >= 1:
            values(s - 1, probs.pop(s - 1))

    if chunks > 1:
        o_tile, lse_tile = scratch

        @pl.when(chunk == chunks - 1)
        def _():
            for h in heads:
                for r in range(dilation):
                    natural = pl.ds(r, n, stride=dilation)
                    o_ref[h, natural, :] = o_tile[h, r * n:(r + 1) * n, :]
                    lse_ref[h, natural, :] = lse_tile[h, r * n:(r + 1) * n, :]


def _dilated_group(proj, slopes, group, dilation):
    b, s, _ = proj.shape
    n = DIL_STEPS
    step_rows = DIL_SUBBLOCKS * n
    assert dilation == 1 or dilation % DIL_SUBBLOCKS == 0
    chunks = max(dilation // DIL_SUBBLOCKS, 1)
    tile = step_rows * chunks
    assert s % tile == 0
    groups = len(DIL_CONFIGS)
    prev_rows = n if dilation == 1 else step_rows
    blocks_per_step = step_rows // prev_rows

    def cur(section):
        return pl.BlockSpec((None, step_rows, DIL_OUT),
                            lambda bi, i, c: (bi, i * chunks + c, section * groups + group))

    def prev(section):
        def index(bi, i, c):
            if dilation == 1:
                return (bi, jnp.maximum(i * blocks_per_step - 1, 0), section * groups + group)
            return (bi, jnp.maximum(i - 1, 0) * chunks + c, section * groups + group)
        return pl.BlockSpec((None, prev_rows, DIL_OUT), index)

    out_block = (DIL_HEADS_PER_GROUP, tile, HEAD_DIM)
    out_spec = pl.BlockSpec((None,) + out_block, lambda bi, i, c: (bi, 0, i, 0))
    out_sds = jax.ShapeDtypeStruct((b, DIL_HEADS_PER_GROUP, s, HEAD_DIM), F32)
    scratch = [pltpu.VMEM(out_block, F32)] * 2 if chunks > 1 else []
    o, lse = pl.pallas_call(
        functools.partial(_dilated_body, dilation=dilation, group=group),
        grid=(b, s // tile, chunks),
        in_specs=[pl.BlockSpec(memory_space=pltpu.SMEM),
                  cur(0), prev(1), cur(1), prev(2), cur(2)],
        out_specs=[out_spec, out_spec],
        out_shape=[out_sds, out_sds],
        scratch_shapes=scratch,
        compiler_params=_params("parallel", "arbitrary", "arbitrary"),
        name=f"dilated_attn_g{group}",
    )(slopes, proj, proj, proj, proj, proj)
    return o, lse


def _moba_body(slope_ref, q_ref, k_ref, v_ref, o_ref, kmean_ref, vt_ref, bias_ref, sel_ref,
               m_ref, acc_ref, ahead_ref, *, nblk, group):
    blk = MOBA_BLOCK
    hd = HEAD_DIM
    heads = q_ref.shape[1] // hd
    own = pl.program_id(2)
    key_off = lax.broadcasted_iota(jnp.int32, (blk, blk), 0)
    qry_off = lax.broadcasted_iota(jnp.int32, (blk, blk), 1)
    slope2 = [slope_ref[pl.program_id(1) * heads + h] * LOG2E for h in range(heads)]

    @pl.when(own == 0)
    def _():
        def fill(jb, carry):
            start = pl.multiple_of(jb * blk, blk)
            rows = k_ref[pl.ds(start, blk), :].astype(F32)
            kmean_ref[pl.ds(jb, 1), :] = jnp.mean(rows, axis=0, keepdims=True)
            vrows = v_ref[pl.ds(start, blk), :].astype(F32)
            for h in range(heads):
                vt_ref[h, :hd, pl.ds(start, blk)] = vrows[:, h * hd:(h + 1) * hd].T.astype(BF16)
            return carry
        lax.fori_loop(0, nblk, fill, 0)
        for h in range(heads):
            vt_ref[h, hd:, :] = jnp.ones((vt_ref.shape[1] - hd, vt_ref.shape[2]), BF16)
        for h in range(heads):
            bias_ref[h] = -slope2[h] * (qry_off - key_off).astype(F32)

    q = [q_ref[:, h * hd:(h + 1) * hd] for h in range(heads)]

    blk_id = lax.broadcasted_iota(jnp.int32, (nblk, blk), 0).astype(F32)
    kmean = kmean_ref[...]
    km_hi = kmean.astype(BF16)
    rest = kmean - km_hi.astype(F32)
    km_mid = rest.astype(BF16)
    km_lo = (rest - km_mid.astype(F32)).astype(BF16)
    km3 = jnp.concatenate([km_hi, km_mid, km_lo], axis=0)
    gates = []
    for h in range(heads):
        parts = lax.dot_general(km3[:, h * hd:(h + 1) * hd], q[h], _NT, preferred_element_type=F32)
        gate = parts[:nblk] + (parts[nblk:2 * nblk] + parts[2 * nblk:])
        gates.append(jnp.where(blk_id < own.astype(F32), gate, NEG_INF))

    def scores(h, start, rows):
        k = k_ref[pl.ds(start, rows), h * hd:(h + 1) * hd]
        return lax.dot_general(k, q[h], _NT, preferred_element_type=F32)

    own_start = pl.multiple_of(own * blk, blk)
    own_x = [scores(h, own_start, blk) for h in range(heads)]

    sels = [jnp.zeros((nblk, blk), F32) for _ in range(heads)]
    for _ in range(MOBA_TOPK):
        for h in range(heads):
            best = jnp.max(gates[h], axis=0, keepdims=True)
            is_best = (gates[h] == best) & (gates[h] > NEG_INF)
            pick = jnp.min(jnp.where(is_best, blk_id, float(nblk)), axis=0, keepdims=True)
            picked = blk_id == pick
            sels[h] = jnp.where(picked, 1.0, sels[h])
            gates[h] = jnp.where(picked, NEG_INF, gates[h])
    for h in range(heads):
        sel_ref[h] = sels[h]

    own_p = []
    for h in range(heads):
        x = jnp.where(qry_off >= key_off, own_x[h] + bias_ref[h], NEG_INF)
        m0 = jnp.max(x, axis=0, keepdims=True)
        own_p.append((m0, jnp.exp2(x - m0).astype(BF16)))
    for h in range(heads):
        m0, p = own_p[h]
        m_ref[h] = m0
        acc_ref[h] = jnp.dot(vt_ref[h, :, pl.ds(own_start, blk)], p, preferred_element_type=F32)

    def softmax_update(h, i, x, m):
        xs, chosen, shift = [], [], []
        m_new = m
        for g in range(group):
            j = i * group + g
            xs.append(x[g * blk:(g + 1) * blk] + bias_ref[h])
            chosen.append(sel_ref[h, pl.ds(j, 1), :] > 0.5)
            shift.append(-slope2[h] * ((own - j) * blk).astype(F32))
            top = jnp.max(xs[g], axis=0, keepdims=True) + shift[g]
            m_new = jnp.maximum(m_new, jnp.where(chosen[g], top, NEG_INF))
        ps = []
        for g in range(group):
            ref_g = jnp.where(chosen[g], m_new - shift[g], jnp.inf)
            ps.append(jnp.exp2(xs[g] - ref_g).astype(BF16))
        return m_new, jnp.exp2(m - m_new), jnp.concatenate(ps, axis=0)

    def past_blocks(i, carry):
        start = pl.multiple_of(i * (group * blk), group * blk)
        x, sm = {0: ahead_ref[...]}, {}
        for h in range(heads + 1):
            if h + 1 < heads:
                x[h + 1] = scores(h + 1, start, group * blk)
            elif h + 1 == heads:
                ahead_ref[...] = scores(0, group_start(i + 1), group * blk)
            if h < heads:
                sm[h] = softmax_update(h, i, x[h], m_ref[h])
            if h >= 1:
                m_new, alpha, p = sm[h - 1]
                m_ref[h - 1] = m_new
                acc_ref[h - 1] = alpha * acc_ref[h - 1] + jnp.dot(
                    vt_ref[h - 1, :, pl.ds(start, group * blk)], p, preferred_element_type=F32)
        return carry

    def group_start(i):
        return pl.multiple_of(jnp.minimum(i, nblk // group - 1) * (group * blk), group * blk)

    n_groups = lax.div(own + (group - 1), group)
    ahead_ref[...] = scores(0, group_start(0), group * blk)
    lax.fori_loop(0, n_groups, past_blocks, 0)
    for h in range(heads):
        acc = acc_ref[h]
        o_ref[:, h * hd:(h + 1) * hd] = (acc[:hd] / acc[hd:hd + 1]).T.astype(o_ref.dtype)


def _moba(proj, slopes):
    b, s, _ = proj.shape
    assert s % (MOBA_BLOCK * MOBA_GROUP) == 0
    nblk = s // MOBA_BLOCK
    hp = MOBA_HEADS_PER_STEP
    width = hp * HEAD_DIM
    q0 = 3 * DIL_WIDTH // width
    k0 = q0 + MOBA_WIDTH // width
    v0 = k0 + MOBA_WIDTH // width
    y = pl.pallas_call(
        functools.partial(_moba_body, nblk=nblk, group=MOBA_GROUP),
        grid=(b, MOBA_HEADS // hp, nblk),
        in_specs=[
            pl.BlockSpec(memory_space=pltpu.SMEM),
            pl.BlockSpec((None, MOBA_BLOCK, width), lambda bi, h, i: (bi, i, q0 + h)),
            pl.BlockSpec((None, s, width), lambda bi, h, i: (bi, 0, k0 + h)),
            pl.BlockSpec((None, s, width), lambda bi, h, i: (bi, 0, v0 + h)),
        ],
        out_specs=pl.BlockSpec((None, MOBA_BLOCK, width), lambda bi, h, i: (bi, i, h)),
        out_shape=jax.ShapeDtypeStruct((b, s, MOBA_WIDTH), BF16),
        scratch_shapes=[pltpu.VMEM((nblk, width), F32),
                        pltpu.VMEM((hp, HEAD_DIM + BF16_ROWS, s), BF16),
                        pltpu.VMEM((hp, MOBA_BLOCK, MOBA_BLOCK), F32),
                        pltpu.VMEM((hp, nblk, MOBA_BLOCK), F32),
                        pltpu.VMEM((hp, 1, MOBA_BLOCK), F32),
                        pltpu.VMEM((hp, HEAD_DIM + BF16_ROWS, MOBA_BLOCK), F32),
                        pltpu.VMEM((MOBA_GROUP * MOBA_BLOCK, MOBA_BLOCK), F32)],
        compiler_params=_params("parallel", "parallel", "arbitrary"),
        name="moba_attn",
    )(slopes, proj, proj, proj)
    return y.reshape(b * s, MOBA_WIDTH)


def _merged_branches(x, o_refs, l_refs, ym_ref, gd_ref, gm_ref, wud_ref, wum_ref, wo_ref):
    heads = []
    for h in range(DIL_HEADS_PER_GROUP):
        l0, l1, l2 = (l_ref[h] for l_ref in l_refs)
        m = jnp.maximum(jnp.maximum(l0, l1), l2)
        e0, e1, e2 = jnp.exp(l0 - m), jnp.exp(l1 - m), jnp.exp(l2 - m)
        den = e0 + e1 + e2
        mixed = ((e0 / den) * o_refs[0][h] + (e1 / den) * o_refs[1][h] + (e2 / den) * o_refs[2][h])
        heads.append(mixed.astype(BF16))
    y_dil = jnp.concatenate(heads, axis=-1)
    lift_dil = jnp.dot(y_dil, wud_ref[...], preferred_element_type=F32)
    lift_moba = jnp.dot(ym_ref[...], wum_ref[...], preferred_element_type=F32)
    merged = gd_ref[...].astype(F32) * lift_dil + gm_ref[...].astype(F32) * lift_moba
    return x + jnp.dot(merged.astype(BF16), wo_ref[...], preferred_element_type=F32)


def _cross_attended(x, g_ref, wq_ref, kv_ref, wo_ref):
    h = _rms(x, g_ref[...]).astype(BF16)
    q = jnp.dot(h, wq_ref[...], preferred_element_type=F32).astype(BF16)
    heads = []
    for hd in range(MEM_HEADS):
        k = kv_ref[:, hd * HEAD_DIM:(hd + 1) * HEAD_DIM]
        v = kv_ref[:, MEM_WIDTH + hd * HEAD_DIM:MEM_WIDTH + (hd + 1) * HEAD_DIM]
        s = lax.dot_general(q[:, hd * HEAD_DIM:(hd + 1) * HEAD_DIM], k, _NT,
                            preferred_element_type=F32) * SCALE
        p = jnp.exp(s - jnp.max(s, axis=-1, keepdims=True))
        den = jnp.sum(p, axis=-1, keepdims=True)
        heads.append((jnp.dot(p.astype(BF16), v, preferred_element_type=F32) / den).astype(BF16))
    o = jnp.concatenate(heads, axis=-1)
    return x + jnp.dot(o, wo_ref[...], preferred_element_type=F32)


def _merge_cross_body(x_ref, o0_ref, o1_ref, o2_ref, l0_ref, l1_ref, l2_ref, ym_ref, gd_ref, gm_ref,
                      wud_ref, wum_ref, wo_ref, gc_ref, wq_ref, kv_ref, wom_ref, out_ref):
    x1 = _merged_branches(x_ref[...], (o0_ref, o1_ref, o2_ref), (l0_ref, l1_ref, l2_ref),
                          ym_ref, gd_ref, gm_ref, wud_ref, wum_ref, wo_ref)
    out_ref[...] = _cross_attended(x1, gc_ref, wq_ref, kv_ref, wom_ref)


def _merge_cross(x, dil_outs, dil_lses, y_moba, gates, w_up_dil, w_up_moba, w_out,
                 g_cross, w_q, kv, w_o, *, tm):
    t, d = x.shape
    s = dil_outs[0].shape[2]
    mem_len = kv.shape[1]
    assert t % tm == 0 and s % tm == 0
    per_batch = s // tm
    row = lambda w: pl.BlockSpec((tm, w), lambda i: (i, 0))
    dil = pl.BlockSpec((None, DIL_HEADS_PER_GROUP, tm, HEAD_DIM),
                       lambda i: (i // per_batch, 0, i % per_batch, 0))
    return pl.pallas_call(
        _merge_cross_body,
        grid=(t // tm,),
        in_specs=[row(d)] + [dil] * 6 + [
            row(MOBA_WIDTH),
            pl.BlockSpec((tm, d), lambda i: (i, 0)),
            pl.BlockSpec((tm, d), lambda i: (i, 1)),
            _resident((DIL_OUT, d)), _resident((MOBA_WIDTH, d)), _resident((d, d)),
            _resident((1, d)), _resident((d, MEM_WIDTH)),
            pl.BlockSpec((None, mem_len, 2 * MEM_WIDTH), lambda i: (i // per_batch, 0, 0)),
            _resident((MEM_WIDTH, d)),
        ],
        out_specs=row(d),
        out_shape=jax.ShapeDtypeStruct((t, d), F32),
        compiler_params=_params("parallel"),
        name="merge_out_proj_cross_attn",
    )(x, *dil_outs, *dil_lses, y_moba, gates, gates, w_up_dil, w_up_moba, w_out,
      g_cross.reshape(1, d), w_q, kv, w_o)


ROUTE_COLS = 8
MOE_ROW_TILE = 256


def _route_body(x_ref, g_ref, wr_ref, br_ref, info_ref, counts_ref, run_ref, tri_ref):
    tm = x_ref.shape[0]
    n_route = N_GROUPS + N_EXPERTS
    lane = lax.broadcasted_iota(jnp.int32, (tm, n_route), 1).astype(F32)

    @pl.when(pl.program_id(0) == 0)
    def _():
        run_ref[...] = jnp.zeros_like(run_ref)
        earlier = (lax.broadcasted_iota(jnp.int32, (tm, tm), 0)
                   > lax.broadcasted_iota(jnp.int32, (tm, tm), 1))
        tri_ref[...] = jnp.where(earlier, 1.0, 0.0).astype(BF16)

    t = _rms(x_ref[...], g_ref[...])
    logits = _dot_split3(t, wr_ref[...]) + br_ref[...]
    none = float(n_route)
    glog = jnp.where(lane < N_GROUPS, logits, NEG_INF)
    gmax = jnp.max(glog, axis=-1, keepdims=True)
    gsel = jnp.min(jnp.where(glog == gmax, lane, none), axis=-1, keepdims=True)
    pg = 1.0 / jnp.sum(jnp.exp(glog - gmax), axis=-1, keepdims=True)
    first = N_GROUPS + gsel * EXPERTS_PER_GROUP
    in_group = (lane >= first) & (lane < first + EXPERTS_PER_GROUP)
    elog = jnp.where(in_group, logits, NEG_INF)
    top1 = jnp.max(elog, axis=-1, keepdims=True)
    i1 = jnp.min(jnp.where(elog == top1, lane, none), axis=-1, keepdims=True)
    rest = jnp.where(lane == i1, NEG_INF, elog)
    top2 = jnp.max(rest, axis=-1, keepdims=True)
    i2 = jnp.min(jnp.where(rest == top2, lane, none), axis=-1, keepdims=True)
    e2 = jnp.exp(top2 - top1)
    w1 = pg / (1.0 + e2)
    w2 = pg * e2 / (1.0 + e2)

    hit1 = lane == i1
    hit2 = lane == i2
    assigned = jnp.where(hit1 | hit2, 1.0, 0.0)
    before = jnp.dot(tri_ref[...], assigned.astype(BF16), preferred_element_type=F32) + run_ref[...]
    rank1 = jnp.sum(jnp.where(hit1, before, 0.0), axis=-1, keepdims=True)
    rank2 = jnp.sum(jnp.where(hit2, before, 0.0), axis=-1, keepdims=True)
    run_ref[...] += jnp.sum(assigned, axis=0, keepdims=True)
    counts_ref[...] = run_ref[...]

    col = lax.broadcasted_iota(jnp.int32, (tm, ROUTE_COLS), 1)
    fields = (i1 - N_GROUPS, i2 - N_GROUPS, rank1, rank2, w1, w2)
    info = jnp.zeros((tm, ROUTE_COLS), F32)
    for c, field in enumerate(fields):
        info = jnp.where(col == c, field, info)
    info_ref[...] = info


def _route(x, g, w_route, b_route, *, tm):
    t, d = x.shape
    n_route = N_GROUPS + N_EXPERTS
    assert t % tm == 0
    return pl.pallas_call(
        _route_body,
        grid=(t // tm,),
        in_specs=[
            pl.BlockSpec((tm, d), lambda i: (i, 0)),
            pl.BlockSpec((1, d), lambda i: (0, 0)),
            pl.BlockSpec((d, n_route), lambda i: (0, 0)),
            pl.BlockSpec((1, n_route), lambda i: (0, 0)),
        ],
        out_specs=[pl.BlockSpec((tm, ROUTE_COLS), lambda i: (i, 0)),
                   pl.BlockSpec((1, n_route), lambda i: (0, 0))],
        out_shape=[jax.ShapeDtypeStruct((t, ROUTE_COLS), F32),
                   jax.ShapeDtypeStruct((1, n_route), F32)],
        scratch_shapes=[pltpu.VMEM((1, n_route), F32), pltpu.VMEM((tm, tm), BF16)],
        compiler_params=_params("arbitrary"),
        name="moe_route",
    )(x, g.reshape(1, d), w_route, b_route.reshape(1, n_route))


def _row_copies_wait(src_rows, dst_rows, sem):
    pltpu.make_async_copy(src_rows, dst_rows, sem).wait()


def _slot_block(tm):
    return pl.BlockSpec((MOE_TOPK, tm), lambda i, *_: (0, i), memory_space=pltpu.SMEM)


def _dispatch_body(fill_ref, pos_ref, x_ref, g_ref, sorted_ref, t_ref, zero_ref, sems, fill_sem):
    step = pl.program_id(0)
    tm = x_ref.shape[0]
    tr = zero_ref.shape[0]
    slot = lax.rem(step, 2)

    @pl.when(pl.program_id(0) == 0)
    def _():
        zero_ref[...] = jnp.zeros_like(zero_ref)

        def fill_copy(tile):
            return pltpu.make_async_copy(zero_ref, sorted_ref.at[pl.ds(tile * tr, tr)], fill_sem)

        def start(idx, carry):
            pl.when(fill_ref[idx] >= 0)(lambda: fill_copy(fill_ref[idx]).start())
            return carry

        def finish(idx, carry):
            pl.when(fill_ref[idx] >= 0)(lambda: fill_copy(fill_ref[idx]).wait())
            return carry

        lax.fori_loop(0, fill_ref.shape[0], start, 0)
        lax.fori_loop(0, fill_ref.shape[0], finish, 0)

    t_ref[slot] = _rms(x_ref[...], g_ref[...])

    def issue(r, carry):
        for k in range(MOE_TOPK):
            dst = pos_ref[k, r]
            pltpu.make_async_copy(t_ref.at[slot, pl.ds(r, 1)], sorted_ref.at[pl.ds(dst, 1)],
                                  sems.at[slot]).start(priority=k)
        return carry

    def copies_done(buf):
        for _ in range(MOE_TOPK):
            _row_copies_wait(t_ref.at[buf], sorted_ref.at[pl.ds(0, tm)], sems.at[buf])

    lax.fori_loop(0, tm, issue, 0, unroll=8)
    pl.when(step > 0)(lambda: copies_done(1 - slot))
    pl.when(step == pl.num_programs(0) - 1)(lambda: copies_done(slot))


def _dispatch(x, g, pos, fill_tiles, n_rows, *, tm):
    t, d = x.shape
    assert t % tm == 0
    return pl.pallas_call(
        _dispatch_body,
        grid_spec=pltpu.PrefetchScalarGridSpec(
            num_scalar_prefetch=1,
            grid=(t // tm,),
            in_specs=[_slot_block(tm),
                      pl.BlockSpec((tm, d), lambda i, fill: (i, 0)),
                      pl.BlockSpec((1, d), lambda i, fill: (0, 0))],
            out_specs=pl.BlockSpec(memory_space=pl.ANY),
            scratch_shapes=[pltpu.VMEM((2, tm, d), F32), pltpu.VMEM((MOE_ROW_TILE, d), F32),
                            pltpu.SemaphoreType.DMA((2,)), pltpu.SemaphoreType.DMA(())],
        ),
        out_shape=jax.ShapeDtypeStruct((n_rows, d), F32),
        compiler_params=pltpu.CompilerParams(dimension_semantics=("arbitrary",),
                                             vmem_limit_bytes=VMEM_LIMIT_BYTES,
                                             disable_bounds_checks=True),
        name="moe_dispatch",
    )(fill_tiles, pos, x, g.reshape(1, d))


def _experts_body(tile_expert_ref, n_tiles_ref, x_ref, wg_ref, wu_ref, wd_ref, y_ref):
    del tile_expert_ref
    in_use = pl.program_id(0) < n_tiles_ref[0]

    @pl.when(in_use)
    def _():
        t = x_ref[...].astype(BF16)
        gate = jnp.dot(t, wg_ref[...].astype(BF16), preferred_element_type=F32)
        up = jnp.dot(t, wu_ref[...].astype(BF16), preferred_element_type=F32)
        a = jax.nn.silu(gate) * up
        y_ref[...] = jnp.dot(a.astype(BF16), wd_ref[...].astype(BF16), preferred_element_type=F32)

    @pl.when(jnp.logical_not(in_use))
    def _():
        y_ref[...] = jnp.zeros_like(y_ref)


def _experts(sorted_rows, tile_expert, n_tiles, w_gate, w_up, w_down):
    p, d = sorted_rows.shape
    ff = w_gate.shape[-1]
    tr = MOE_ROW_TILE
    assert p % tr == 0
    used = lambda i, nt: jnp.minimum(i, nt[0] - 1)
    return pl.pallas_call(
        _experts_body,
        grid_spec=pltpu.PrefetchScalarGridSpec(
            num_scalar_prefetch=2,
            grid=(p // tr,),
            in_specs=[pl.BlockSpec((tr, d), lambda i, te, nt: (used(i, nt), 0)),
                      pl.BlockSpec((None, d, ff), lambda i, te, nt: (te[used(i, nt)], 0, 0)),
                      pl.BlockSpec((None, d, ff), lambda i, te, nt: (te[used(i, nt)], 0, 0)),
                      pl.BlockSpec((None, ff, d), lambda i, te, nt: (te[used(i, nt)], 0, 0))],
            out_specs=pl.BlockSpec((tr, d), lambda i, te, nt: (i, 0)),
        ),
        out_shape=jax.ShapeDtypeStruct((p, d), F32),
        compiler_params=_params("arbitrary"),
        name="moe_experts",
    )(tile_expert, n_tiles, sorted_rows, w_gate, w_up, w_down)


def _combine_body(pos_ref, pos_next_ref, x_ref, info_ref, gf_ref, y_sorted_ref, out_ref,
                  rows_ref, sems):
    step = pl.program_id(0)
    tm = x_ref.shape[0]
    slot = lax.rem(step, 2)

    def fetch(table_ref, buf):
        def issue(r, carry):
            for k in range(MOE_TOPK):
                src = table_ref[k, r]
                pltpu.make_async_copy(y_sorted_ref.at[pl.ds(src, 1)],
                                      rows_ref.at[buf, k, pl.ds(r, 1)],
                                      sems.at[buf]).start(priority=k)
            return carry
        lax.fori_loop(0, tm, issue, 0, unroll=8)

    pl.when(step == 0)(lambda: fetch(pos_ref, 0))
    pl.when(step + 1 < pl.num_programs(0))(lambda: fetch(pos_next_ref, 1 - slot))
    for k in range(MOE_TOPK):
        _row_copies_wait(y_sorted_ref.at[pl.ds(0, tm)], rows_ref.at[slot, k], sems.at[slot])

    info = info_ref[...]
    y = info[:, 4:5] * rows_ref[slot, 0] + info[:, 5:6] * rows_ref[slot, 1]
    out_ref[...] = _rms(x_ref[...] + y, gf_ref[...])


def _combine(x, info, pos, y_sorted, g_final, *, tm):
    t, d = x.shape
    assert t % tm == 0
    return pl.pallas_call(
        _combine_body,
        grid=(t // tm,),
        in_specs=[_slot_block(tm),
                  pl.BlockSpec((MOE_TOPK, tm), lambda i: (0, jnp.minimum(i + 1, t // tm - 1)),
                               memory_space=pltpu.SMEM),
                  pl.BlockSpec((tm, d), lambda i: (i, 0)),
                  pl.BlockSpec((tm, ROUTE_COLS), lambda i: (i, 0)),
                  pl.BlockSpec((1, d), lambda i: (0, 0)),
                  pl.BlockSpec(memory_space=pl.ANY)],
        out_specs=pl.BlockSpec((tm, d), lambda i: (i, 0)),
        scratch_shapes=[pltpu.VMEM((2, MOE_TOPK, tm, d), F32), pltpu.SemaphoreType.DMA((2,))],
        out_shape=jax.ShapeDtypeStruct((t, d), F32),
        compiler_params=pltpu.CompilerParams(dimension_semantics=("arbitrary",),
                                             vmem_limit_bytes=VMEM_LIMIT_BYTES,
                                             disable_bounds_checks=True),
        name="moe_combine_final_norm",
    )(pos, pos, x, info, g_final.reshape(1, d), y_sorted)


def _moe(x, g, w_route, b_route, w_gate, w_up, w_down, g_final):
    t, d = x.shape
    tr = MOE_ROW_TILE
    assert (MOE_TOPK * t) % tr == 0
    info, counts = _route(x, g, w_route, b_route, tm=min(512, t))

    expert = info[:, 0:MOE_TOPK].astype(jnp.int32)
    rank = info[:, MOE_TOPK:2 * MOE_TOPK].astype(jnp.int32)
    count = counts[0, N_GROUPS:].astype(jnp.int32)
    seg_tiles = (count + (tr - 1)) // tr
    seg_end = jnp.cumsum(seg_tiles)
    seg_start_row = (seg_end - seg_tiles) * tr
    pos = (seg_start_row[expert] + rank).T
    max_tiles = (MOE_TOPK * t) // tr + N_EXPERTS
    tile_id = jnp.arange(max_tiles, dtype=jnp.int32)
    tile_expert = jnp.minimum(
        jnp.sum((seg_end[None, :] <= tile_id[:, None]).astype(jnp.int32), axis=1), N_EXPERTS - 1)
    n_tiles = seg_end[-1:].astype(jnp.int32)

    tail = n_tiles + jnp.arange(N_EXPERTS, dtype=jnp.int32)
    fill_tiles = jnp.concatenate([jnp.where(seg_tiles > 0, seg_end - 1, -1),
                                  jnp.where(tail < max_tiles, tail, -1)]).astype(jnp.int32)

    sorted_rows = _dispatch(x, g, pos, fill_tiles, max_tiles * tr, tm=min(512, t))
    y_sorted = _experts(sorted_rows, tile_expert, n_tiles, w_gate, w_up, w_down)
    return _combine(x, info, pos, y_sorted, g_final, tm=min(512, t))


def _layer(x, mem, attn_norm, w_in, w_up_dil, w_up_moba, w_branch_gate, w_out, cross_norm,
           mem_norm, w_q_mem, w_kv_mem, w_o_mem):
    b, s, d = x.shape
    t = b * s
    xt = x.reshape(t, d)
    h = _norm(xt, attn_norm, tm=min(512, t), name="attn_norm")
    proj, gates = _in_proj(h, w_in, w_branch_gate, tm=min(2048, t))
    proj = proj.reshape(b, s, IN_WIDTH)
    dil_slopes = _alibi_slopes(DIL_HEADS)
    dil = [_dilated_group(proj, dil_slopes, g, dilation)
           for g, (_, dilation) in enumerate(DIL_CONFIGS)]
    y_moba = _moba(proj, _alibi_slopes(MOBA_HEADS))
    mem_len = mem.shape[1]
    kv = _norm_matmul(mem.reshape(b * mem_len, d), mem_norm, w_kv_mem.astype(BF16),
                      tm=b * mem_len, tn=512, name="norm_mem_kv").reshape(b, mem_len, 2 * MEM_WIDTH)
    return _merge_cross(xt, [o for o, _ in dil], [l for _, l in dil], y_moba, gates,
                        w_up_dil.astype(BF16), w_up_moba.astype(BF16), w_out.astype(BF16),
                        cross_norm, w_q_mem.astype(BF16), kv, w_o_mem.astype(BF16), tm=min(256, s))


def kernel(x, mem, attn_norm, w_in, w_up_dil, w_up_moba, w_branch_gate, w_out, cross_norm, mem_norm,
           w_q_mem, w_kv_mem, w_o_mem, ffn_norm, w_router_group, b_router_group, w_router_expert,
           b_router_expert, w_expert_gate, w_expert_up, w_expert_down, final_norm):
    b, s, d = x.shape
    depth = attn_norm.shape[0]
    assert depth == 1, "the final norm is fused into the last layer's MoE call"
    l = 0
    x2 = _layer(x, mem, attn_norm[l], w_in[l], w_up_dil[l], w_up_moba[l], w_branch_gate[l], w_out[l],
                cross_norm[l], mem_norm[l], w_q_mem[l], w_kv_mem[l], w_o_mem[l])
    w_route = jnp.concatenate([w_router_group[l], w_router_expert[l]], axis=1)
    b_route = jnp.concatenate([b_router_group[l], b_router_expert[l]], axis=0)
    out = _moe(x2, ffn_norm[l], w_route, b_route, w_expert_gate[l], w_expert_up[l],
               w_expert_down[l], final_norm)
    return out.reshape(b, s, d)
```

```python
import functools

import numpy as np
import jax
import jax.numpy as jnp
from jax import lax
from jax.experimental import pallas as pl
from jax.experimental.pallas import tpu as pltpu

F32 = jnp.float32
BF16 = jnp.bfloat16

HEAD_DIM = 128
DIL_CONFIGS = ((128, 1), (512, 4), (2048, 16))
DIL_HEADS_PER_GROUP = 4
DIL_HEADS = DIL_HEADS_PER_GROUP * len(DIL_CONFIGS)
DIL_WIDTH = DIL_HEADS * HEAD_DIM
DIL_OUT = DIL_HEADS_PER_GROUP * HEAD_DIM
DIL_STEPS = 128
MOBA_HEADS = 8
MOBA_WIDTH = MOBA_HEADS * HEAD_DIM
MOBA_BLOCK = 256
MOBA_TOPK = 3
IN_WIDTH = 3 * (DIL_WIDTH + MOBA_WIDTH)
MEM_HEADS = 4
MEM_WIDTH = MEM_HEADS * HEAD_DIM
N_GROUPS = 4
EXPERTS_PER_GROUP = 8
N_EXPERTS = N_GROUPS * EXPERTS_PER_GROUP
MOE_TOPK = 2
RMS_EPS = 1e-6
SCALE = HEAD_DIM ** -0.5
NEG_INF = float("-inf")
LOG2E = 1.4426950408889634
MOBA_GROUP = 4
MOBA_HEADS_PER_STEP = 4
MOBA_Q_TILES = (9, 11)
IN_PROJ_ROW_CHUNK = 512

VMEM_LIMIT_BYTES = 56 * 1024 * 1024
BF16_ROWS = 16
MXU_COLUMNS = 256
_NT = (((1,), (1,)), ((), ()))


def _alibi_slopes(n):
    return jnp.asarray(2.0 ** (-8.0 * np.arange(1, n + 1) / n), dtype=F32)


def _rms(x, g):
    return x * lax.rsqrt(jnp.mean(x * x, axis=-1, keepdims=True) + RMS_EPS) * g


def _dot_split3(a, b):
    a_hi = a.astype(BF16)
    b_hi = b.astype(BF16)
    a_lo = (a - a_hi.astype(F32)).astype(BF16)
    b_lo = (b - b_hi.astype(F32)).astype(BF16)
    dot = functools.partial(jnp.dot, preferred_element_type=F32)
    n = b.shape[1]
    if 2 * n <= MXU_COLUMNS:
        both = dot(a_hi, jnp.concatenate([b_hi, b_lo], axis=1))
        return both[:, :n] + (dot(a_lo, b_hi) + both[:, n:])
    return dot(a_hi, b_hi) + (dot(a_lo, b_hi) + dot(a_hi, b_lo))


def _params(*sem):
    return pltpu.CompilerParams(dimension_semantics=sem, vmem_limit_bytes=VMEM_LIMIT_BYTES)


def _resident(shape):
    nd = len(shape)
    return pl.BlockSpec(shape, lambda *_: (0,) * nd, pipeline_mode=pl.Buffered(1))


def _norm_matmul_body(x_ref, g_ref, w_ref, o_ref, h_ref):
    @pl.when(pl.program_id(1) == 0)
    def _():
        h_ref[...] = _rms(x_ref[...], g_ref[...]).astype(BF16)

    o_ref[...] = jnp.dot(h_ref[...], w_ref[...], preferred_element_type=F32).astype(o_ref.dtype)


def _norm_matmul(x, g, w, *, tm, tn, name):
    m, d = x.shape
    n = w.shape[1]
    assert m % tm == 0 and n % tn == 0
    return pl.pallas_call(
        _norm_matmul_body,
        grid=(m // tm, n // tn),
        in_specs=[
            pl.BlockSpec((tm, d), lambda i, j: (i, 0)),
            pl.BlockSpec((1, d), lambda i, j: (0, 0)),
            pl.BlockSpec((d, tn), lambda i, j: (0, j)),
        ],
        out_specs=pl.BlockSpec((tm, tn), lambda i, j: (i, j)),
        out_shape=jax.ShapeDtypeStruct((m, n), BF16),
        scratch_shapes=[pltpu.VMEM((tm, d), BF16)],
        compiler_params=_params("parallel", "arbitrary"),
        name=name,
    )(x, g.reshape(1, d), w)


def _norm_body(x_ref, g_ref, o_ref):
    o_ref[...] = _rms(x_ref[...], g_ref[...]).astype(o_ref.dtype)


def _norm(x, g, *, tm, name):
    m, d = x.shape
    assert m % tm == 0
    return pl.pallas_call(
        _norm_body,
        grid=(m // tm,),
        in_specs=[pl.BlockSpec((tm, d), lambda i: (i, 0)), pl.BlockSpec((1, d), lambda i: (0, 0))],
        out_specs=pl.BlockSpec((tm, d), lambda i: (i, 0)),
        out_shape=jax.ShapeDtypeStruct((m, d), BF16),
        compiler_params=_params("parallel"),
        name=name,
    )(x, g.reshape(1, d))


def _in_proj_body(h_ref, win_ref, wbg_ref, proj_ref, gates_ref, perm_ref, *, n_proj_tiles):
    j = pl.program_id(1)
    tm = h_ref.shape[0]
    n = DIL_STEPS
    groups = len(DIL_CONFIGS)

    def store_regrouped(acc, dilation):
        tile = n * dilation
        for c in range(acc.shape[1] // HEAD_DIM):
            cols = slice(c * HEAD_DIM, (c + 1) * HEAD_DIM)
            perm_ref[c] = acc[:, cols]
            for t0 in range(0, tm, tile):
                for r in range(dilation):
                    rows = perm_ref[c, pl.ds(t0 + r, n, stride=dilation), :]
                    proj_ref[t0 + r * n:t0 + (r + 1) * n, cols] = rows.astype(BF16)

    @pl.when(j < n_proj_tiles)
    def _():
        acc = jnp.dot(h_ref[...], win_ref[...].astype(BF16), preferred_element_type=F32)
        group = jnp.where(j < 3 * groups, lax.rem(j, groups), 0)
        moba_q = (j >= MOBA_Q_TILES[0]) & (j < MOBA_Q_TILES[1])
        scale = jnp.where(moba_q, SCALE * LOG2E, 1.0)

        def store_natural():
            proj_ref[...] = (acc * scale).astype(BF16)

        for g, (_, dilation) in enumerate(DIL_CONFIGS):
            store = store_natural if dilation == 1 else functools.partial(store_regrouped, acc, dilation)
            pl.when(group == g)(store)

    @pl.when(j >= n_proj_tiles)
    def _():
        w = wbg_ref[...].astype(BF16)
        for r0 in range(0, tm, IN_PROJ_ROW_CHUNK):
            rows = slice(r0, r0 + IN_PROJ_ROW_CHUNK)
            acc = jnp.dot(h_ref[rows, :], w, preferred_element_type=F32)
            gates_ref[rows, :] = jax.nn.sigmoid(acc).astype(BF16)


def _in_proj(h, w_in, w_bg, *, tm):
    t, d = h.shape
    tn = DIL_OUT
    assert MOBA_Q_TILES == (3 * DIL_WIDTH // tn, (3 * DIL_WIDTH + MOBA_WIDTH) // tn)
    assert t % tm == 0 and tm % (DIL_STEPS * max(dl for _, dl in DIL_CONFIGS)) == 0
    assert DIL_WIDTH == len(DIL_CONFIGS) * tn and w_in.shape[1] % tn == 0 and w_bg.shape[1] % tn == 0
    n_proj = w_in.shape[1] // tn
    n_gate = w_bg.shape[1] // tn
    return pl.pallas_call(
        functools.partial(_in_proj_body, n_proj_tiles=n_proj),
        grid=(t // tm, n_proj + n_gate),
        in_specs=[
            pl.BlockSpec((tm, d), lambda i, j: (i, 0)),
            pl.BlockSpec((d, tn), lambda i, j: (0, jnp.minimum(j, n_proj - 1))),
            pl.BlockSpec((d, tn), lambda i, j: (0, jnp.maximum(j - n_proj, 0))),
        ],
        out_specs=[
            pl.BlockSpec((tm, tn), lambda i, j: (i, jnp.minimum(j, n_proj - 1))),
            pl.BlockSpec((tm, tn), lambda i, j: (i, jnp.maximum(j - n_proj, 0))),
        ],
        out_shape=[jax.ShapeDtypeStruct((t, w_in.shape[1]), BF16),
                   jax.ShapeDtypeStruct((t, w_bg.shape[1]), BF16)],
        scratch_shapes=[pltpu.VMEM((tn // HEAD_DIM, tm, HEAD_DIM), F32)],
        compiler_params=_params("parallel", "arbitrary"),
        name="in_proj_gates",
    )(h, w_in, w_bg)


DIL_SUBBLOCKS = 4


def _dilated_body(slope_ref, q_ref, kp_ref, kc_ref, vp_ref, vc_ref, o_ref, lse_ref, *scratch,
                  dilation, group):
    n = DIL_STEPS
    first_tile = pl.program_id(1) == 0
    chunk = pl.program_id(2)
    chunks = max(dilation // DIL_SUBBLOCKS, 1)
    qi = lax.broadcasted_iota(jnp.int32, (n, n), 0)
    kj = lax.broadcasted_iota(jnp.int32, (n, n), 1)
    steps_cur = qi - kj
    valid_cur = steps_cur >= 0
    dist_cur = (steps_cur * dilation).astype(F32)
    dist_prev = ((steps_cur + n) * dilation).astype(F32)
    limit_across_tiles = jnp.where(first_tile, -n, 0)
    heads = range(DIL_HEADS_PER_GROUP)
    cols_of = [slice(h * HEAD_DIM, (h + 1) * HEAD_DIM) for h in heads]
    rows_of = [slice(s * n, (s + 1) * n) for s in range(DIL_SUBBLOCKS)]

    def prev_rows(s):
        if dilation == 1:
            return (slice(0, n), False, limit_across_tiles) if s == 0 else (rows_of[s - 1], True, 0)
        return rows_of[s], False, limit_across_tiles

    def scores(s):
        rows, from_cur, _ = prev_rows(s)
        out = []
        for h in heads:
            q = q_ref[rows_of[s], cols_of[h]]
            k_prev = (kc_ref if from_cur else kp_ref)[rows, cols_of[h]]
            out.append((lax.dot_general(q, kc_ref[rows_of[s], cols_of[h]], _NT,
                                        preferred_element_type=F32),
                        lax.dot_general(q, k_prev, _NT, preferred_element_type=F32)))
        return out

    def softmax(s, raw):
        valid_prev = steps_cur <= prev_rows(s)[2]
        out = []
        for h in heads:
            slope = slope_ref[group * DIL_HEADS_PER_GROUP + h]
            s_cur = jnp.where(valid_cur, raw[h][0] * SCALE - slope * dist_cur, NEG_INF)
            s_prev = jnp.where(valid_prev, raw[h][1] * SCALE - slope * dist_prev, NEG_INF)
            m = jnp.max(jnp.maximum(s_cur, s_prev), axis=-1, keepdims=True)
            p_cur = jnp.exp(s_cur - m)
            p_prev = jnp.exp(s_prev - m)
            den = jnp.sum(p_cur + p_prev, axis=-1, keepdims=True)
            out.append((p_cur.astype(BF16), p_prev.astype(BF16), m, den))
        return out

    def values(s, probs):
        rows, from_cur, _ = prev_rows(s)
        for h in heads:
            p_cur, p_prev, m, den = probs[h]
            v_prev = (vc_ref if from_cur else vp_ref)[rows, cols_of[h]]
            o = (jnp.dot(p_cur, vc_ref[rows_of[s], cols_of[h]], preferred_element_type=F32)
                 + jnp.dot(p_prev, v_prev, preferred_element_type=F32)) / den
            lse = jnp.broadcast_to(m + jnp.log(den), (n, HEAD_DIM))
            if dilation == 1:
                dst_o, dst_lse, rows_out = o_ref, lse_ref, rows_of[s]
            elif chunks == 1:
                dst_o, dst_lse, rows_out = o_ref, lse_ref, pl.ds(s, n, stride=dilation)
            else:
                dst_o, dst_lse = scratch
                rows_out = pl.ds(pl.multiple_of((chunk * DIL_SUBBLOCKS + s) * n, n), n)
            dst_o[h, rows_out, :] = o
            dst_lse[h, rows_out, :] = lse

    raw, probs = {0: scores(0)}, {}
    for s in range(DIL_SUBBLOCKS + 1):
        if s + 1 < DIL_SUBBLOCKS:
            raw[s + 1] = scores(s + 1)
        if s < DIL_SUBBLOCKS:
            probs[s] = softmax(s, raw.pop(s))
        if s >= 1:
            values(s - 1, probs.pop(s - 1))

    if chunks > 1:
        o_tile, lse_tile = scratch

        @pl.when(chunk == chunks - 1)
        def _():
            for h in heads:
                for r in range(dilation):
                    natural = pl.ds(r, n, stride=dilation)
                    o_ref[h, natural, :] = o_tile[h, r * n:(r + 1) * n, :]
                    lse_ref[h, natural, :] = lse_tile[h, r * n:(r + 1) * n, :]


def _dilated_group(proj, slopes, group, dilation):
    b, s, _ = proj.shape
    n = DIL_STEPS
    step_rows = DIL_SUBBLOCKS * n
    assert dilation == 1 or dilation % DIL_SUBBLOCKS == 0
    chunks = max(dilation // DIL_SUBBLOCKS, 1)
    tile = step_rows * chunks
    assert s % tile == 0
    groups = len(DIL_CONFIGS)
    prev_rows = n if dilation == 1 else step_rows
    blocks_per_step = step_rows // prev_rows

    def cur(section):
        return pl.BlockSpec((None, step_rows, DIL_OUT),
                            lambda bi, i, c: (bi, i * chunks + c, section * groups + group))

    def prev(section):
        def index(bi, i, c):
            if dilation == 1:
                return (bi, jnp.maximum(i * blocks_per_step - 1, 0), section * groups + group)
            return (bi, jnp.maximum(i - 1, 0) * chunks + c, section * groups + group)
        return pl.BlockSpec((None, prev_rows, DIL_OUT), index)

    out_block = (DIL_HEADS_PER_GROUP, tile, HEAD_DIM)
    out_spec = pl.BlockSpec((None,) + out_block, lambda bi, i, c: (bi, 0, i, 0))
    out_sds = jax.ShapeDtypeStruct((b, DIL_HEADS_PER_GROUP, s, HEAD_DIM), F32)
    scratch = [pltpu.VMEM(out_block, F32)] * 2 if chunks > 1 else []
    o, lse = pl.pallas_call(
        functools.partial(_dilated_body, dilation=dilation, group=group),
        grid=(b, s // tile, chunks),
        in_specs=[pl.BlockSpec(memory_space=pltpu.SMEM),
                  cur(0), prev(1), cur(1), prev(2), cur(2)],
        out_specs=[out_spec, out_spec],
        out_shape=[out_sds, out_sds],
        scratch_shapes=scratch,
        compiler_params=_params("parallel", "arbitrary", "arbitrary"),
        name=f"dilated_attn_g{group}",
    )(slopes, proj, proj, proj, proj, proj)
    return o, lse


def _moba_body(slope_ref, q_ref, k_ref, v_ref, o_ref, kmean_ref, vt_ref, bias_ref, sel_ref,
               m_ref, acc_ref, ahead_ref, *, nblk, group):
    blk = MOBA_BLOCK
    hd = HEAD_DIM
    heads = q_ref.shape[1] // hd
    own = pl.program_id(2)
    key_off = lax.broadcasted_iota(jnp.int32, (blk, blk), 0)
    qry_off = lax.broadcasted_iota(jnp.int32, (blk, blk), 1)
    slope2 = [slope_ref[pl.program_id(1) * heads + h] * LOG2E for h in range(heads)]

    @pl.when(own == 0)
    def _():
        def fill(jb, carry):
            start = pl.multiple_of(jb * blk, blk)
            rows = k_ref[pl.ds(start, blk), :].astype(F32)
            kmean_ref[pl.ds(jb, 1), :] = jnp.mean(rows, axis=0, keepdims=True)
            vrows = v_ref[pl.ds(start, blk), :].astype(F32)
            for h in range(heads):
                vt_ref[h, :hd, pl.ds(start, blk)] = vrows[:, h * hd:(h + 1) * hd].T.astype(BF16)
            return carry
        lax.fori_loop(0, nblk, fill, 0)
        for h in range(heads):
            vt_ref[h, hd:, :] = jnp.ones((vt_ref.shape[1] - hd, vt_ref.shape[2]), BF16)
        for h in range(heads):
            bias_ref[h] = -slope2[h] * (qry_off - key_off).astype(F32)

    q = [q_ref[:, h * hd:(h + 1) * hd] for h in range(heads)]

    blk_id = lax.broadcasted_iota(jnp.int32, (nblk, blk), 0).astype(F32)
    kmean = kmean_ref[...]
    km_hi = kmean.astype(BF16)
    rest = kmean - km_hi.astype(F32)
    km_mid = rest.astype(BF16)
    km_lo = (rest - km_mid.astype(F32)).astype(BF16)
    km3 = jnp.concatenate([km_hi, km_mid, km_lo], axis=0)
    gates = []
    for h in range(heads):
        parts = lax.dot_general(km3[:, h * hd:(h + 1) * hd], q[h], _NT, preferred_element_type=F32)
        gate = parts[:nblk] + (parts[nblk:2 * nblk] + parts[2 * nblk:])
        gates.append(jnp.where(blk_id < own.astype(F32), gate, NEG_INF))

    def scores(h, start, rows):
        k = k_ref[pl.ds(start, rows), h * hd:(h + 1) * hd]
        return lax.dot_general(k, q[h], _NT, preferred_element_type=F32)

    own_start = pl.multiple_of(own * blk, blk)
    own_x = [scores(h, own_start, blk) for h in range(heads)]

    sels = [jnp.zeros((nblk, blk), F32) for _ in range(heads)]
    for _ in range(MOBA_TOPK):
        for h in range(heads):
            best = jnp.max(gates[h], axis=0, keepdims=True)
            is_best = (gates[h] == best) & (gates[h] > NEG_INF)
            pick = jnp.min(jnp.where(is_best, blk_id, float(nblk)), axis=0, keepdims=True)
            picked = blk_id == pick
            sels[h] = jnp.where(picked, 1.0, sels[h])
            gates[h] = jnp.where(picked, NEG_INF, gates[h])
    for h in range(heads):
        sel_ref[h] = sels[h]

    own_p = []
    for h in range(heads):
        x = jnp.where(qry_off >= key_off, own_x[h] + bias_ref[h], NEG_INF)
        m0 = jnp.max(x, axis=0, keepdims=True)
        own_p.append((m0, jnp.exp2(x - m0).astype(BF16)))
    for h in range(heads):
        m0, p = own_p[h]
        m_ref[h] = m0
        acc_ref[h] = jnp.dot(vt_ref[h, :, pl.ds(own_start, blk)], p, preferred_element_type=F32)

    def softmax_update(h, i, x, m):
        xs, chosen, shift = [], [], []
        m_new = m
        for g in range(group):
            j = i * group + g
            xs.append(x[g * blk:(g + 1) * blk] + bias_ref[h])
            chosen.append(sel_ref[h, pl.ds(j, 1), :] > 0.5)
            shift.append(-slope2[h] * ((own - j) * blk).astype(F32))
            top = jnp.max(xs[g], axis=0, keepdims=True) + shift[g]
            m_new = jnp.maximum(m_new, jnp.where(chosen[g], top, NEG_INF))
        ps = []
        for g in range(group):
            ref_g = jnp.where(chosen[g], m_new - shift[g], jnp.inf)
            ps.append(jnp.exp2(xs[g] - ref_g).astype(BF16))
        return m_new, jnp.exp2(m - m_new), jnp.concatenate(ps, axis=0)

    def past_blocks(i, carry):
        start = pl.multiple_of(i * (group * blk), group * blk)
        x, sm = {0: ahead_ref[...]}, {}
        for h in range(heads + 1):
            if h + 1 < heads:
                x[h + 1] = scores(h + 1, start, group * blk)
            elif h + 1 == heads:
                ahead_ref[...] = scores(0, group_start(i + 1), group * blk)
            if h < heads:
                sm[h] = softmax_update(h, i, x[h], m_ref[h])
            if h >= 1:
                m_new, alpha, p = sm[h - 1]
                m_ref[h - 1] = m_new
                acc_ref[h - 1] = alpha * acc_ref[h - 1] + jnp.dot(
                    vt_ref[h - 1, :, pl.ds(start, group * blk)], p, preferred_element_type=F32)
        return carry

    def group_start(i):
        return pl.multiple_of(jnp.minimum(i, nblk // group - 1) * (group * blk), group * blk)

    n_groups = lax.div(own + (group - 1), group)
    ahead_ref[...] = scores(0, group_start(0), group * blk)
    lax.fori_loop(0, n_groups, past_blocks, 0)
    for h in range(heads):
        acc = acc_ref[h]
        o_ref[:, h * hd:(h + 1) * hd] = (acc[:hd] / acc[hd:hd + 1]).T.astype(o_ref.dtype)


def _moba(proj, slopes):
    b, s, _ = proj.shape
    assert s % (MOBA_BLOCK * MOBA_GROUP) == 0
    nblk = s // MOBA_BLOCK
    hp = MOBA_HEADS_PER_STEP
    width = hp * HEAD_DIM
    q0 = 3 * DIL_WIDTH // width
    k0 = q0 + MOBA_WIDTH // width
    v0 = k0 + MOBA_WIDTH // width
    y = pl.pallas_call(
        functools.partial(_moba_body, nblk=nblk, group=MOBA_GROUP),
        grid=(b, MOBA_HEADS // hp, nblk),
        in_specs=[
            pl.BlockSpec(memory_space=pltpu.SMEM),
            pl.BlockSpec((None, MOBA_BLOCK, width), lambda bi, h, i: (bi, i, q0 + h)),
            pl.BlockSpec((None, s, width), lambda bi, h, i: (bi, 0, k0 + h)),
            pl.BlockSpec((None, s, width), lambda bi, h, i: (bi, 0, v0 + h)),
        ],
        out_specs=pl.BlockSpec((None, MOBA_BLOCK, width), lambda bi, h, i: (bi, i, h)),
        out_shape=jax.ShapeDtypeStruct((b, s, MOBA_WIDTH), BF16),
        scratch_shapes=[pltpu.VMEM((nblk, width), F32),
                        pltpu.VMEM((hp, HEAD_DIM + BF16_ROWS, s), BF16),
                        pltpu.VMEM((hp, MOBA_BLOCK, MOBA_BLOCK), F32),
                        pltpu.VMEM((hp, nblk, MOBA_BLOCK), F32),
                        pltpu.VMEM((hp, 1, MOBA_BLOCK), F32),
                        pltpu.VMEM((hp, HEAD_DIM + BF16_ROWS, MOBA_BLOCK), F32),
                        pltpu.VMEM((MOBA_GROUP * MOBA_BLOCK, MOBA_BLOCK), F32)],
        compiler_params=_params("parallel", "parallel", "arbitrary"),
        name="moba_attn",
    )(slopes, proj, proj, proj)
    return y.reshape(b * s, MOBA_WIDTH)


def _merged_branches(x, o_refs, l_refs, ym_ref, gd_ref, gm_ref, wud_ref, wum_ref, wo_ref):
    heads = []
    for h in range(DIL_HEADS_PER_GROUP):
        l0, l1, l2 = (l_ref[h] for l_ref in l_refs)
        m = jnp.maximum(jnp.maximum(l0, l1), l2)
        e0, e1, e2 = jnp.exp(l0 - m), jnp.exp(l1 - m), jnp.exp(l2 - m)
        den = e0 + e1 + e2
        mixed = ((e0 / den) * o_refs[0][h] + (e1 / den) * o_refs[1][h] + (e2 / den) * o_refs[2][h])
        heads.append(mixed.astype(BF16))
    y_dil = jnp.concatenate(heads, axis=-1)
    lift_dil = jnp.dot(y_dil, wud_ref[...], preferred_element_type=F32)
    lift_moba = jnp.dot(ym_ref[...], wum_ref[...], preferred_element_type=F32)
    merged = gd_ref[...].astype(F32) * lift_dil + gm_ref[...].astype(F32) * lift_moba
    return x + jnp.dot(merged.astype(BF16), wo_ref[...], preferred_element_type=F32)


def _cross_attended(x, g_ref, wq_ref, kv_ref, wo_ref):
    h = _rms(x, g_ref[...]).astype(BF16)
    q = jnp.dot(h, wq_ref[...], preferred_element_type=F32).astype(BF16)
    heads = []
    for hd in range(MEM_HEADS):
        k = kv_ref[:, hd * HEAD_DIM:(hd + 1) * HEAD_DIM]
        v = kv_ref[:, MEM_WIDTH + hd * HEAD_DIM:MEM_WIDTH + (hd + 1) * HEAD_DIM]
        s = lax.dot_general(q[:, hd * HEAD_DIM:(hd + 1) * HEAD_DIM], k, _NT,
                            preferred_element_type=F32) * SCALE
        p = jnp.exp(s - jnp.max(s, axis=-1, keepdims=True))
        den = jnp.sum(p, axis=-1, keepdims=True)
        heads.append((jnp.dot(p.astype(BF16), v, preferred_element_type=F32) / den).astype(BF16))
    o = jnp.concatenate(heads, axis=-1)
    return x + jnp.dot(o, wo_ref[...], preferred_element_type=F32)


def _merge_cross_body(x_ref, o0_ref, o1_ref, o2_ref, l0_ref, l1_ref, l2_ref, ym_ref, gd_ref, gm_ref,
                      wud_ref, wum_ref, wo_ref, gc_ref, wq_ref, kv_ref, wom_ref, out_ref):
    x1 = _merged_branches(x_ref[...], (o0_ref, o1_ref, o2_ref), (l0_ref, l1_ref, l2_ref),
                          ym_ref, gd_ref, gm_ref, wud_ref, wum_ref, wo_ref)
    out_ref[...] = _cross_attended(x1, gc_ref, wq_ref, kv_ref, wom_ref)


def _merge_cross(x, dil_outs, dil_lses, y_moba, gates, w_up_dil, w_up_moba, w_out,
                 g_cross, w_q, kv, w_o, *, tm):
    t, d = x.shape
    s = dil_outs[0].shape[2]
    mem_len = kv.shape[1]
    assert t % tm == 0 and s % tm == 0
    per_batch = s // tm
    row = lambda w: pl.BlockSpec((tm, w), lambda i: (i, 0))
    dil = pl.BlockSpec((None, DIL_HEADS_PER_GROUP, tm, HEAD_DIM),
                       lambda i: (i // per_batch, 0, i % per_batch, 0))
    return pl.pallas_call(
        _merge_cross_body,
        grid=(t // tm,),
        in_specs=[row(d)] + [dil] * 6 + [
            row(MOBA_WIDTH),
            pl.BlockSpec((tm, d), lambda i: (i, 0)),
            pl.BlockSpec((tm, d), lambda i: (i, 1)),
            _resident((DIL_OUT, d)), _resident((MOBA_WIDTH, d)), _resident((d, d)),
            _resident((1, d)), _resident((d, MEM_WIDTH)),
            pl.BlockSpec((None, mem_len, 2 * MEM_WIDTH), lambda i: (i // per_batch, 0, 0)),
            _resident((MEM_WIDTH, d)),
        ],
        out_specs=row(d),
        out_shape=jax.ShapeDtypeStruct((t, d), F32),
        compiler_params=_params("parallel"),
        name="merge_out_proj_cross_attn",
    )(x, *dil_outs, *dil_lses, y_moba, gates, gates, w_up_dil, w_up_moba, w_out,
      g_cross.reshape(1, d), w_q, kv, w_o)


ROUTE_COLS = 8
MOE_ROW_TILE = 256


def _route_body(x_ref, g_ref, wr_ref, br_ref, info_ref, counts_ref, run_ref, tri_ref):
    tm = x_ref.shape[0]
    n_route = N_GROUPS + N_EXPERTS
    lane = lax.broadcasted_iota(jnp.int32, (tm, n_route), 1).astype(F32)

    @pl.when(pl.program_id(0) == 0)
    def _():
        run_ref[...] = jnp.zeros_like(run_ref)
        earlier = (lax.broadcasted_iota(jnp.int32, (tm, tm), 0)
                   > lax.broadcasted_iota(jnp.int32, (tm, tm), 1))
        tri_ref[...] = jnp.where(earlier, 1.0, 0.0).astype(BF16)

    t = _rms(x_ref[...], g_ref[...])
    logits = _dot_split3(t, wr_ref[...]) + br_ref[...]
    none = float(n_route)
    glog = jnp.where(lane < N_GROUPS, logits, NEG_INF)
    gmax = jnp.max(glog, axis=-1, keepdims=True)
    gsel = jnp.min(jnp.where(glog == gmax, lane, none), axis=-1, keepdims=True)
    pg = 1.0 / jnp.sum(jnp.exp(glog - gmax), axis=-1, keepdims=True)
    first = N_GROUPS + gsel * EXPERTS_PER_GROUP
    in_group = (lane >= first) & (lane < first + EXPERTS_PER_GROUP)
    elog = jnp.where(in_group, logits, NEG_INF)
    top1 = jnp.max(elog, axis=-1, keepdims=True)
    i1 = jnp.min(jnp.where(elog == top1, lane, none), axis=-1, keepdims=True)
    rest = jnp.where(lane == i1, NEG_INF, elog)
    top2 = jnp.max(rest, axis=-1, keepdims=True)
    i2 = jnp.min(jnp.where(rest == top2, lane, none), axis=-1, keepdims=True)
    e2 = jnp.exp(top2 - top1)
    w1 = pg / (1.0 + e2)
    w2 = pg * e2 / (1.0 + e2)

    hit1 = lane == i1
    hit2 = lane == i2
    assigned = jnp.where(hit1 | hit2, 1.0, 0.0)
    before = jnp.dot(tri_ref[...], assigned.astype(BF16), preferred_element_type=F32) + run_ref[...]
    rank1 = jnp.sum(jnp.where(hit1, before, 0.0), axis=-1, keepdims=True)
    rank2 = jnp.sum(jnp.where(hit2, before, 0.0), axis=-1, keepdims=True)
    run_ref[...] += jnp.sum(assigned, axis=0, keepdims=True)
    counts_ref[...] = run_ref[...]

    col = lax.broadcasted_iota(jnp.int32, (tm, ROUTE_COLS), 1)
    fields = (i1 - N_GROUPS, i2 - N_GROUPS, rank1, rank2, w1, w2)
    info = jnp.zeros((tm, ROUTE_COLS), F32)
    for c, field in enumerate(fields):
        info = jnp.where(col == c, field, info)
    info_ref[...] = info


def _route(x, g, w_route, b_route, *, tm):
    t, d = x.shape
    n_route = N_GROUPS + N_EXPERTS
    assert t % tm == 0
    return pl.pallas_call(
        _route_body,
        grid=(t // tm,),
        in_specs=[
            pl.BlockSpec((tm, d), lambda i: (i, 0)),
            pl.BlockSpec((1, d), lambda i: (0, 0)),
            pl.BlockSpec((d, n_route), lambda i: (0, 0)),
            pl.BlockSpec((1, n_route), lambda i: (0, 0)),
        ],
        out_specs=[pl.BlockSpec((tm, ROUTE_COLS), lambda i: (i, 0)),
                   pl.BlockSpec((1, n_route), lambda i: (0, 0))],
        out_shape=[jax.ShapeDtypeStruct((t, ROUTE_COLS), F32),
                   jax.ShapeDtypeStruct((1, n_route), F32)],
        scratch_shapes=[pltpu.VMEM((1, n_route), F32), pltpu.VMEM((tm, tm), BF16)],
        compiler_params=_params("arbitrary"),
        name="moe_route",
    )(x, g.reshape(1, d), w_route, b_route.reshape(1, n_route))


def _row_copies_wait(src_rows, dst_rows, sem):
    pltpu.make_async_copy(src_rows, dst_rows, sem).wait()


def _slot_block(tm):
    return pl.BlockSpec((MOE_TOPK, tm), lambda i, *_: (0, i), memory_space=pltpu.SMEM)


def _dispatch_body(fill_ref, pos_ref, x_ref, g_ref, sorted_ref, t_ref, zero_ref, sems, fill_sem):
    step = pl.program_id(0)
    tm = x_ref.shape[0]
    tr = zero_ref.shape[0]
    slot = lax.rem(step, 2)

    @pl.when(pl.program_id(0) == 0)
    def _():
        zero_ref[...] = jnp.zeros_like(zero_ref)

        def fill_copy(tile):
            return pltpu.make_async_copy(zero_ref, sorted_ref.at[pl.ds(tile * tr, tr)], fill_sem)

        def start(idx, carry):
            pl.when(fill_ref[idx] >= 0)(lambda: fill_copy(fill_ref[idx]).start())
            return carry

        def finish(idx, carry):
            pl.when(fill_ref[idx] >= 0)(lambda: fill_copy(fill_ref[idx]).wait())
            return carry

        lax.fori_loop(0, fill_ref.shape[0], start, 0)
        lax.fori_loop(0, fill_ref.shape[0], finish, 0)

    t_ref[slot] = _rms(x_ref[...], g_ref[...])

    def issue(r, carry):
        for k in range(MOE_TOPK):
            dst = pos_ref[k, r]
            pltpu.make_async_copy(t_ref.at[slot, pl.ds(r, 1)], sorted_ref.at[pl.ds(dst, 1)],
                                  sems.at[slot]).start()
        return carry

    def copies_done(buf):
        for _ in range(MOE_TOPK):
            _row_copies_wait(t_ref.at[buf], sorted_ref.at[pl.ds(0, tm)], sems.at[buf])

    lax.fori_loop(0, tm, issue, 0, unroll=8)
    pl.when(step > 0)(lambda: copies_done(1 - slot))
    pl.when(step == pl.num_programs(0) - 1)(lambda: copies_done(slot))


def _dispatch(x, g, pos, fill_tiles, n_rows, *, tm):
    t, d = x.shape
    assert t % tm == 0
    return pl.pallas_call(
        _dispatch_body,
        grid_spec=pltpu.PrefetchScalarGridSpec(
            num_scalar_prefetch=1,
            grid=(t // tm,),
            in_specs=[_slot_block(tm),
                      pl.BlockSpec((tm, d), lambda i, fill: (i, 0)),
                      pl.BlockSpec((1, d), lambda i, fill: (0, 0))],
            out_specs=pl.BlockSpec(memory_space=pl.ANY),
            scratch_shapes=[pltpu.VMEM((2, tm, d), F32), pltpu.VMEM((MOE_ROW_TILE, d), F32),
                            pltpu.SemaphoreType.DMA((2,)), pltpu.SemaphoreType.DMA(())],
        ),
        out_shape=jax.ShapeDtypeStruct((n_rows, d), F32),
        compiler_params=pltpu.CompilerParams(dimension_semantics=("arbitrary",),
                                             vmem_limit_bytes=VMEM_LIMIT_BYTES,
                                             disable_bounds_checks=True),
        name="moe_dispatch",
    )(fill_tiles, pos, x, g.reshape(1, d))


def _experts_body(tile_expert_ref, n_tiles_ref, x_ref, wg_ref, wu_ref, wd_ref, y_ref):
    del tile_expert_ref
    in_use = pl.program_id(0) < n_tiles_ref[0]

    @pl.when(in_use)
    def _():
        t = x_ref[...].astype(BF16)
        gate = jnp.dot(t, wg_ref[...].astype(BF16), preferred_element_type=F32)
        up = jnp.dot(t, wu_ref[...].astype(BF16), preferred_element_type=F32)
        a = jax.nn.silu(gate) * up
        y_ref[...] = jnp.dot(a.astype(BF16), wd_ref[...].astype(BF16), preferred_element_type=F32)

    @pl.when(jnp.logical_not(in_use))
    def _():
        y_ref[...] = jnp.zeros_like(y_ref)


def _experts(sorted_rows, tile_expert, n_tiles, w_gate, w_up, w_down):
    p, d = sorted_rows.shape
    ff = w_gate.shape[-1]
    tr = MOE_ROW_TILE
    assert p % tr == 0
    used = lambda i, nt: jnp.minimum(i, nt[0] - 1)
    return pl.pallas_call(
        _experts_body,
        grid_spec=pltpu.PrefetchScalarGridSpec(
            num_scalar_prefetch=2,
            grid=(p // tr,),
            in_specs=[pl.BlockSpec((tr, d), lambda i, te, nt: (used(i, nt), 0)),
                      pl.BlockSpec((None, d, ff), lambda i, te, nt: (te[used(i, nt)], 0, 0)),
                      pl.BlockSpec((None, d, ff), lambda i, te, nt: (te[used(i, nt)], 0, 0)),
                      pl.BlockSpec((None, ff, d), lambda i, te, nt: (te[used(i, nt)], 0, 0))],
            out_specs=pl.BlockSpec((tr, d), lambda i, te, nt: (i, 0)),
        ),
        out_shape=jax.ShapeDtypeStruct((p, d), F32),
        compiler_params=_params("arbitrary"),
        name="moe_experts",
    )(tile_expert, n_tiles, sorted_rows, w_gate, w_up, w_down)


def _combine_body(pos_ref, pos_next_ref, x_ref, info_ref, gf_ref, y_sorted_ref, out_ref,
                  rows_ref, sems):
    step = pl.program_id(0)
    tm = x_ref.shape[0]
    slot = lax.rem(step, 2)

    def fetch(table_ref, buf):
        def issue(r, carry):
            for k in range(MOE_TOPK):
                src = table_ref[k, r]
                pltpu.make_async_copy(y_sorted_ref.at[pl.ds(src, 1)],
                                      rows_ref.at[buf, k, pl.ds(r, 1)], sems.at[buf]).start()
            return carry
        lax.fori_loop(0, tm, issue, 0, unroll=8)

    pl.when(step == 0)(lambda: fetch(pos_ref, 0))
    pl.when(step + 1 < pl.num_programs(0))(lambda: fetch(pos_next_ref, 1 - slot))
    for k in range(MOE_TOPK):
        _row_copies_wait(y_sorted_ref.at[pl.ds(0, tm)], rows_ref.at[slot, k], sems.at[slot])

    info = info_ref[...]
    y = info[:, 4:5] * rows_ref[slot, 0] + info[:, 5:6] * rows_ref[slot, 1]
    out_ref[...] = _rms(x_ref[...] + y, gf_ref[...])


def _combine(x, info, pos, y_sorted, g_final, *, tm):
    t, d = x.shape
    assert t % tm == 0
    return pl.pallas_call(
        _combine_body,
        grid=(t // tm,),
        in_specs=[_slot_block(tm),
                  pl.BlockSpec((MOE_TOPK, tm), lambda i: (0, jnp.minimum(i + 1, t // tm - 1)),
                               memory_space=pltpu.SMEM),
                  pl.BlockSpec((tm, d), lambda i: (i, 0)),
                  pl.BlockSpec((tm, ROUTE_COLS), lambda i: (i, 0)),
                  pl.BlockSpec((1, d), lambda i: (0, 0)),
                  pl.BlockSpec(memory_space=pl.ANY)],
        out_specs=pl.BlockSpec((tm, d), lambda i: (i, 0)),
        scratch_shapes=[pltpu.VMEM((2, MOE_TOPK, tm, d), F32), pltpu.SemaphoreType.DMA((2,))],
        out_shape=jax.ShapeDtypeStruct((t, d), F32),
        compiler_params=pltpu.CompilerParams(dimension_semantics=("arbitrary",),
                                             vmem_limit_bytes=VMEM_LIMIT_BYTES,
                                             disable_bounds_checks=True),
        name="moe_combine_final_norm",
    )(pos, pos, x, info, g_final.reshape(1, d), y_sorted)


def _moe(x, g, w_route, b_route, w_gate, w_up, w_down, g_final):
    t, d = x.shape
    tr = MOE_ROW_TILE
    assert (MOE_TOPK * t) % tr == 0
    info, counts = _route(x, g, w_route, b_route, tm=min(512, t))

    expert = info[:, 0:MOE_TOPK].astype(jnp.int32)
    rank = info[:, MOE_TOPK:2 * MOE_TOPK].astype(jnp.int32)
    count = counts[0, N_GROUPS:].astype(jnp.int32)
    seg_tiles = (count + (tr - 1)) // tr
    seg_end = jnp.cumsum(seg_tiles)
    seg_start_row = (seg_end - seg_tiles) * tr
    pos = (seg_start_row[expert] + rank).T
    max_tiles = (MOE_TOPK * t) // tr + N_EXPERTS
    tile_id = jnp.arange(max_tiles, dtype=jnp.int32)
    tile_expert = jnp.minimum(
        jnp.sum((seg_end[None, :] <= tile_id[:, None]).astype(jnp.int32), axis=1), N_EXPERTS - 1)
    n_tiles = seg_end[-1:].astype(jnp.int32)

    tail = n_tiles + jnp.arange(N_EXPERTS, dtype=jnp.int32)
    fill_tiles = jnp.concatenate([jnp.where(seg_tiles > 0, seg_end - 1, -1),
                                  jnp.where(tail < max_tiles, tail, -1)]).astype(jnp.int32)

    sorted_rows = _dispatch(x, g, pos, fill_tiles, max_tiles * tr, tm=min(512, t))
    y_sorted = _experts(sorted_rows, tile_expert, n_tiles, w_gate, w_up, w_down)
    return _combine(x, info, pos, y_sorted, g_final, tm=min(512, t))


def _layer(x, mem, attn_norm, w_in, w_up_dil, w_up_moba, w_branch_gate, w_out, cross_norm,
           mem_norm, w_q_mem, w_kv_mem, w_o_mem):
    b, s, d = x.shape
    t = b * s
    xt = x.reshape(t, d)
    h = _norm(xt, attn_norm, tm=min(512, t), name="attn_norm")
    proj, gates = _in_proj(h, w_in, w_branch_gate, tm=min(2048, t))
    proj = proj.reshape(b, s, IN_WIDTH)
    dil_slopes = _alibi_slopes(DIL_HEADS)
    dil = [_dilated_group(proj, dil_slopes, g, dilation)
           for g, (_, dilation) in enumerate(DIL_CONFIGS)]
    y_moba = _moba(proj, _alibi_slopes(MOBA_HEADS))
    mem_len = mem.shape[1]
    kv = _norm_matmul(mem.reshape(b * mem_len, d), mem_norm, w_kv_mem.astype(BF16),
                      tm=b * mem_len, tn=512, name="norm_mem_kv").reshape(b, mem_len, 2 * MEM_WIDTH)
    return _merge_cross(xt, [o for o, _ in dil], [l for _, l in dil], y_moba, gates,
                        w_up_dil.astype(BF16), w_up_moba.astype(BF16), w_out.astype(BF16),
                        cross_norm, w_q_mem.astype(BF16), kv, w_o_mem.astype(BF16), tm=min(256, s))


def kernel(x, mem, attn_norm, w_in, w_up_dil, w_up_moba, w_branch_gate, w_out, cross_norm, mem_norm,
           w_q_mem, w_kv_mem, w_o_mem, ffn_norm, w_router_group, b_router_group, w_router_expert,
           b_router_expert, w_expert_gate, w_expert_up, w_expert_down, final_norm):
    b, s, d = x.shape
    depth = attn_norm.shape[0]
    assert depth == 1, "the final norm is fused into the last layer's MoE call"
    l = 0
    x2 = _layer(x, mem, attn_norm[l], w_in[l], w_up_dil[l], w_up_moba[l], w_branch_gate[l], w_out[l],
                cross_norm[l], mem_norm[l], w_q_mem[l], w_kv_mem[l], w_o_mem[l])
    w_route = jnp.concatenate([w_router_group[l], w_router_expert[l]], axis=1)
    b_route = jnp.concatenate([b_router_group[l], b_router_expert[l]], axis=0)
    out = _moe(x2, ffn_norm[l], w_route, b_route, w_expert_gate[l], w_expert_up[l],
               w_expert_down[l], final_norm)
    return out.reshape(b, s, d)
```

```python
import functools

import numpy as np
import jax
import jax.numpy as jnp
from jax import lax
from jax.experimental import pallas as pl
from jax.experimental.pallas import tpu as pltpu

F32 = jnp.float32
BF16 = jnp.bfloat16

HEAD_DIM = 128
DIL_CONFIGS = ((128, 1), (512, 4), (2048, 16))
DIL_HEADS_PER_GROUP = 4
DIL_HEADS = DIL_HEADS_PER_GROUP * len(DIL_CONFIGS)
DIL_WIDTH = DIL_HEADS * HEAD_DIM
DIL_OUT = DIL_HEADS_PER_GROUP * HEAD_DIM
DIL_STEPS = 128
MOBA_HEADS = 8
MOBA_WIDTH = MOBA_HEADS * HEAD_DIM
MOBA_BLOCK = 256
MOBA_TOPK = 3
IN_WIDTH = 3 * (DIL_WIDTH + MOBA_WIDTH)
MEM_HEADS = 4
MEM_WIDTH = MEM_HEADS * HEAD_DIM
N_GROUPS = 4
EXPERTS_PER_GROUP = 8
N_EXPERTS = N_GROUPS * EXPERTS_PER_GROUP
MOE_TOPK = 2
RMS_EPS = 1e-6
SCALE = HEAD_DIM ** -0.5
NEG_INF = float("-inf")
LOG2E = 1.4426950408889634
MOBA_GROUP = 4
MOBA_HEADS_PER_STEP = 4
MOBA_Q_TILES = (9, 11)
IN_PROJ_ROW_CHUNK = 512

VMEM_LIMIT_BYTES = 56 * 1024 * 1024
BF16_ROWS = 16
MXU_COLUMNS = 256
_NT = (((1,), (1,)), ((), ()))


def _alibi_slopes(n):
    return jnp.asarray(2.0 ** (-8.0 * np.arange(1, n + 1) / n), dtype=F32)


def _rms(x, g):
    return x * lax.rsqrt(jnp.mean(x * x, axis=-1, keepdims=True) + RMS_EPS) * g


def _dot_split3(a, b):
    a_hi = a.astype(BF16)
    b_hi = b.astype(BF16)
    a_lo = (a - a_hi.astype(F32)).astype(BF16)
    b_lo = (b - b_hi.astype(F32)).astype(BF16)
    dot = functools.partial(jnp.dot, preferred_element_type=F32)
    n = b.shape[1]
    if 2 * n <= MXU_COLUMNS:
        both = dot(a_hi, jnp.concatenate([b_hi, b_lo], axis=1))
        return both[:, :n] + (dot(a_lo, b_hi) + both[:, n:])
    return dot(a_hi, b_hi) + (dot(a_lo, b_hi) + dot(a_hi, b_lo))


def _params(*sem):
    return pltpu.CompilerParams(dimension_semantics=sem, vmem_limit_bytes=VMEM_LIMIT_BYTES)


def _resident(shape):
    nd = len(shape)
    return pl.BlockSpec(shape, lambda *_: (0,) * nd, pipeline_mode=pl.Buffered(1))


def _norm_matmul_body(x_ref, g_ref, w_ref, o_ref, h_ref):
    @pl.when(pl.program_id(1) == 0)
    def _():
        h_ref[...] = _rms(x_ref[...], g_ref[...]).astype(BF16)

    o_ref[...] = jnp.dot(h_ref[...], w_ref[...], preferred_element_type=F32).astype(o_ref.dtype)


def _norm_matmul(x, g, w, *, tm, tn, name):
    m, d = x.shape
    n = w.shape[1]
    assert m % tm == 0 and n % tn == 0
    return pl.pallas_call(
        _norm_matmul_body,
        grid=(m // tm, n // tn),
        in_specs=[
            pl.BlockSpec((tm, d), lambda i, j: (i, 0)),
            pl.BlockSpec((1, d), lambda i, j: (0, 0)),
            pl.BlockSpec((d, tn), lambda i, j: (0, j)),
        ],
        out_specs=pl.BlockSpec((tm, tn), lambda i, j: (i, j)),
        out_shape=jax.ShapeDtypeStruct((m, n), BF16),
        scratch_shapes=[pltpu.VMEM((tm, d), BF16)],
        compiler_params=_params("parallel", "arbitrary"),
        name=name,
    )(x, g.reshape(1, d), w)


def _norm_body(x_ref, g_ref, o_ref):
    o_ref[...] = _rms(x_ref[...], g_ref[...]).astype(o_ref.dtype)


def _norm(x, g, *, tm, name):
    m, d = x.shape
    assert m % tm == 0
    return pl.pallas_call(
        _norm_body,
        grid=(m // tm,),
        in_specs=[pl.BlockSpec((tm, d), lambda i: (i, 0)), pl.BlockSpec((1, d), lambda i: (0, 0))],
        out_specs=pl.BlockSpec((tm, d), lambda i: (i, 0)),
        out_shape=jax.ShapeDtypeStruct((m, d), BF16),
        compiler_params=_params("parallel"),
        name=name,
    )(x, g.reshape(1, d))


def _in_proj_body(h_ref, win_ref, wbg_ref, proj_ref, gates_ref, perm_ref, *, n_proj_tiles):
    j = pl.program_id(1)
    tm = h_ref.shape[0]
    n = DIL_STEPS
    groups = len(DIL_CONFIGS)

    def store_regrouped(acc, dilation):
        tile = n * dilation
        for c in range(acc.shape[1] // HEAD_DIM):
            cols = slice(c * HEAD_DIM, (c + 1) * HEAD_DIM)
            perm_ref[c] = acc[:, cols]
            for t0 in range(0, tm, tile):
                for r in range(dilation):
                    rows = perm_ref[c, pl.ds(t0 + r, n, stride=dilation), :]
                    proj_ref[t0 + r * n:t0 + (r + 1) * n, cols] = rows.astype(BF16)

    @pl.when(j < n_proj_tiles)
    def _():
        acc = jnp.dot(h_ref[...], win_ref[...].astype(BF16), preferred_element_type=F32)
        group = jnp.where(j < 3 * groups, lax.rem(j, groups), 0)
        moba_q = (j >= MOBA_Q_TILES[0]) & (j < MOBA_Q_TILES[1])
        scale = jnp.where(moba_q, SCALE * LOG2E, 1.0)

        def store_natural():
            proj_ref[...] = (acc * scale).astype(BF16)

        for g, (_, dilation) in enumerate(DIL_CONFIGS):
            store = store_natural if dilation == 1 else functools.partial(store_regrouped, acc, dilation)
            pl.when(group == g)(store)

    @pl.when(j >= n_proj_tiles)
    def _():
        w = wbg_ref[...].astype(BF16)
        for r0 in range(0, tm, IN_PROJ_ROW_CHUNK):
            rows = slice(r0, r0 + IN_PROJ_ROW_CHUNK)
            acc = jnp.dot(h_ref[rows, :], w, preferred_element_type=F32)
            gates_ref[rows, :] = jax.nn.sigmoid(acc).astype(BF16)


def _in_proj(h, w_in, w_bg, *, tm):
    t, d = h.shape
    tn = DIL_OUT
    assert MOBA_Q_TILES == (3 * DIL_WIDTH // tn, (3 * DIL_WIDTH + MOBA_WIDTH) // tn)
    assert t % tm == 0 and tm % (DIL_STEPS * max(dl for _, dl in DIL_CONFIGS)) == 0
    assert DIL_WIDTH == len(DIL_CONFIGS) * tn and w_in.shape[1] % tn == 0 and w_bg.shape[1] % tn == 0
    n_proj = w_in.shape[1] // tn
    n_gate = w_bg.shape[1] // tn
    return pl.pallas_call(
        functools.partial(_in_proj_body, n_proj_tiles=n_proj),
        grid=(t // tm, n_proj + n_gate),
        in_specs=[
            pl.BlockSpec((tm, d), lambda i, j: (i, 0)),
            pl.BlockSpec((d, tn), lambda i, j: (0, jnp.minimum(j, n_proj - 1))),
            pl.BlockSpec((d, tn), lambda i, j: (0, jnp.maximum(j - n_proj, 0))),
        ],
        out_specs=[
            pl.BlockSpec((tm, tn), lambda i, j: (i, jnp.minimum(j, n_proj - 1))),
            pl.BlockSpec((tm, tn), lambda i, j: (i, jnp.maximum(j - n_proj, 0))),
        ],
        out_shape=[jax.ShapeDtypeStruct((t, w_in.shape[1]), BF16),
                   jax.ShapeDtypeStruct((t, w_bg.shape[1]), BF16)],
        scratch_shapes=[pltpu.VMEM((tn // HEAD_DIM, tm, HEAD_DIM), F32)],
        compiler_params=_params("parallel", "arbitrary"),
        name="in_proj_gates",
    )(h, w_in, w_bg)


DIL_SUBBLOCKS = 4


def _dilated_body(slope_ref, q_ref, kp_ref, kc_ref, vp_ref, vc_ref, o_ref, lse_ref, *scratch,
                  dilation, group):
    n = DIL_STEPS
    first_tile = pl.program_id(1) == 0
    chunk = pl.program_id(2)
    chunks = max(dilation // DIL_SUBBLOCKS, 1)
    qi = lax.broadcasted_iota(jnp.int32, (n, n), 0)
    kj = lax.broadcasted_iota(jnp.int32, (n, n), 1)
    steps_cur = qi - kj
    valid_cur = steps_cur >= 0
    dist_cur = (steps_cur * dilation).astype(F32)
    dist_prev = ((steps_cur + n) * dilation).astype(F32)
    limit_across_tiles = jnp.where(first_tile, -n, 0)
    heads = range(DIL_HEADS_PER_GROUP)
    cols_of = [slice(h * HEAD_DIM, (h + 1) * HEAD_DIM) for h in heads]
    rows_of = [slice(s * n, (s + 1) * n) for s in range(DIL_SUBBLOCKS)]

    def prev_rows(s):
        if dilation == 1:
            return (slice(0, n), False, limit_across_tiles) if s == 0 else (rows_of[s - 1], True, 0)
        return rows_of[s], False, limit_across_tiles

    def scores(s):
        rows, from_cur, _ = prev_rows(s)
        out = []
        for h in heads:
            q = q_ref[rows_of[s], cols_of[h]]
            k_prev = (kc_ref if from_cur else kp_ref)[rows, cols_of[h]]
            out.append((lax.dot_general(q, kc_ref[rows_of[s], cols_of[h]], _NT,
                                        preferred_element_type=F32),
                        lax.dot_general(q, k_prev, _NT, preferred_element_type=F32)))
        return out

    def softmax(s, raw):
        valid_prev = steps_cur <= prev_rows(s)[2]
        out = []
        for h in heads:
            slope = slope_ref[group * DIL_HEADS_PER_GROUP + h]
            s_cur = jnp.where(valid_cur, raw[h][0] * SCALE - slope * dist_cur, NEG_INF)
            s_prev = jnp.where(valid_prev, raw[h][1] * SCALE - slope * dist_prev, NEG_INF)
            m = jnp.max(jnp.maximum(s_cur, s_prev), axis=-1, keepdims=True)
            p_cur = jnp.exp(s_cur - m)
            p_prev = jnp.exp(s_prev - m)
            den = jnp.sum(p_cur + p_prev, axis=-1, keepdims=True)
            out.append((p_cur.astype(BF16), p_prev.astype(BF16), m, den))
        return out

    def values(s, probs):
        rows, from_cur, _ = prev_rows(s)
        for h in heads:
            p_cur, p_prev, m, den = probs[h]
            v_prev = (vc_ref if from_cur else vp_ref)[rows, cols_of[h]]
            o = (jnp.dot(p_cur, vc_ref[rows_of[s], cols_of[h]], preferred_element_type=F32)
                 + jnp.dot(p_prev, v_prev, preferred_element_type=F32)) / den
            lse = jnp.broadcast_to(m + jnp.log(den), (n, HEAD_DIM))
            if dilation == 1:
                dst_o, dst_lse, rows_out = o_ref, lse_ref, rows_of[s]
            elif chunks == 1:
                dst_o, dst_lse, rows_out = o_ref, lse_ref, pl.ds(s, n, stride=dilation)
            else:
                dst_o, dst_lse = scratch
                rows_out = pl.ds(pl.multiple_of((chunk * DIL_SUBBLOCKS + s) * n, n), n)
            dst_o[h, rows_out, :] = o
            dst_lse[h, rows_out, :] = lse

    raw, probs = {0: scores(0)}, {}
    for s in range(DIL_SUBBLOCKS + 1):
        if s + 1 < DIL_SUBBLOCKS:
            raw[s + 1] = scores(s + 1)
        if s < DIL_SUBBLOCKS:
            probs[s] = softmax(s, raw.pop(s))
        if s >= 1:
            values(s - 1, probs.pop(s - 1))

    if chunks > 1:
        o_tile, lse_tile = scratch

        @pl.when(chunk == chunks - 1)
        def _():
            for h in heads:
                for r in range(dilation):
                    natural = pl.ds(r, n, stride=dilation)
                    o_ref[h, natural, :] = o_tile[h, r * n:(r + 1) * n, :]
                    lse_ref[h, natural, :] = lse_tile[h, r * n:(r + 1) * n, :]


def _dilated_group(proj, slopes, group, dilation):
    b, s, _ = proj.shape
    n = DIL_STEPS
    step_rows = DIL_SUBBLOCKS * n
    assert dilation == 1 or dilation % DIL_SUBBLOCKS == 0
    chunks = max(dilation // DIL_SUBBLOCKS, 1)
    tile = step_rows * chunks
    assert s % tile == 0
    groups = len(DIL_CONFIGS)
    prev_rows = n if dilation == 1 else step_rows
    blocks_per_step = step_rows // prev_rows

    def cur(section):
        return pl.BlockSpec((None, step_rows, DIL_OUT),
                            lambda bi, i, c: (bi, i * chunks + c, section * groups + group))

    def prev(section):
        def index(bi, i, c):
            if dilation == 1:
                return (bi, jnp.maximum(i * blocks_per_step - 1, 0), section * groups + group)
            return (bi, jnp.maximum(i - 1, 0) * chunks + c, section * groups + group)
        return pl.BlockSpec((None, prev_rows, DIL_OUT), index)

    out_block = (DIL_HEADS_PER_GROUP, tile, HEAD_DIM)
    out_spec = pl.BlockSpec((None,) + out_block, lambda bi, i, c: (bi, 0, i, 0))
    out_sds = jax.ShapeDtypeStruct((b, DIL_HEADS_PER_GROUP, s, HEAD_DIM), F32)
    scratch = [pltpu.VMEM(out_block, F32)] * 2 if chunks > 1 else []
    o, lse = pl.pallas_call(
        functools.partial(_dilated_body, dilation=dilation, group=group),
        grid=(b, s // tile, chunks),
        in_specs=[pl.BlockSpec(memory_space=pltpu.SMEM),
                  cur(0), prev(1), cur(1), prev(2), cur(2)],
        out_specs=[out_spec, out_spec],
        out_shape=[out_sds, out_sds],
        scratch_shapes=scratch,
        compiler_params=_params("parallel", "arbitrary", "arbitrary"),
        name=f"dilated_attn_g{group}",
    )(slopes, proj, proj, proj, proj, proj)
    return o, lse


def _moba_body(slope_ref, q_ref, k_ref, v_ref, o_ref, kmean_ref, vt_ref, bias_ref, sel_ref,
               m_ref, acc_ref, even_ref, odd_ref, *, nblk, group):
    blk = MOBA_BLOCK
    hd = HEAD_DIM
    heads = q_ref.shape[1] // hd
    own = pl.program_id(2)
    key_off = lax.broadcasted_iota(jnp.int32, (blk, blk), 0)
    qry_off = lax.broadcasted_iota(jnp.int32, (blk, blk), 1)
    slope2 = [slope_ref[pl.program_id(1) * heads + h] * LOG2E for h in range(heads)]

    @pl.when(own == 0)
    def _():
        def fill(jb, carry):
            start = pl.multiple_of(jb * blk, blk)
            rows = k_ref[pl.ds(start, blk), :].astype(F32)
            kmean_ref[pl.ds(jb, 1), :] = jnp.mean(rows, axis=0, keepdims=True)
            vrows = v_ref[pl.ds(start, blk), :].astype(F32)
            for h in range(heads):
                vt_ref[h, :hd, pl.ds(start, blk)] = vrows[:, h * hd:(h + 1) * hd].T.astype(BF16)
            return carry
        lax.fori_loop(0, nblk, fill, 0)
        for h in range(heads):
            vt_ref[h, hd:, :] = jnp.ones((vt_ref.shape[1] - hd, vt_ref.shape[2]), BF16)
        for h in range(heads):
            bias_ref[h] = -slope2[h] * (qry_off - key_off).astype(F32)

    q = [q_ref[:, h * hd:(h + 1) * hd] for h in range(heads)]

    blk_id = lax.broadcasted_iota(jnp.int32, (nblk, blk), 0).astype(F32)
    kmean = kmean_ref[...]
    km_hi = kmean.astype(BF16)
    rest = kmean - km_hi.astype(F32)
    km_mid = rest.astype(BF16)
    km_lo = (rest - km_mid.astype(F32)).astype(BF16)
    km3 = jnp.concatenate([km_hi, km_mid, km_lo], axis=0)
    gates = []
    for h in range(heads):
        parts = lax.dot_general(km3[:, h * hd:(h + 1) * hd], q[h], _NT, preferred_element_type=F32)
        gate = parts[:nblk] + (parts[nblk:2 * nblk] + parts[2 * nblk:])
        gates.append(jnp.where(blk_id < own.astype(F32), gate, NEG_INF))

    def scores(h, start, rows):
        k = k_ref[pl.ds(start, rows), h * hd:(h + 1) * hd]
        return lax.dot_general(k, q[h], _NT, preferred_element_type=F32)

    own_start = pl.multiple_of(own * blk, blk)
    own_x = [scores(h, own_start, blk) for h in range(heads)]

    sels = [jnp.zeros((nblk, blk), F32) for _ in range(heads)]
    for _ in range(MOBA_TOPK):
        for h in range(heads):
            best = jnp.max(gates[h], axis=0, keepdims=True)
            is_best = (gates[h] == best) & (gates[h] > NEG_INF)
            pick = jnp.min(jnp.where(is_best, blk_id, float(nblk)), axis=0, keepdims=True)
            picked = blk_id == pick
            sels[h] = jnp.where(picked, 1.0, sels[h])
            gates[h] = jnp.where(picked, NEG_INF, gates[h])
    for h in range(heads):
        sel_ref[h] = sels[h]

    own_p = []
    for h in range(heads):
        x = jnp.where(qry_off >= key_off, own_x[h] + bias_ref[h], NEG_INF)
        m0 = jnp.max(x, axis=0, keepdims=True)
        own_p.append((m0, jnp.exp2(x - m0).astype(BF16)))
    for h in range(heads):
        m0, p = own_p[h]
        m_ref[h] = m0
        acc_ref[h] = jnp.dot(vt_ref[h, :, pl.ds(own_start, blk)], p, preferred_element_type=F32)

    def softmax_update(h, i, x, m):
        xs, chosen, shift = [], [], []
        m_new = m
        for g in range(group):
            j = i * group + g
            xs.append(x[g * blk:(g + 1) * blk] + bias_ref[h])
            chosen.append(sel_ref[h, pl.ds(j, 1), :] > 0.5)
            shift.append(-slope2[h] * ((own - j) * blk).astype(F32))
            top = jnp.max(xs[g], axis=0, keepdims=True) + shift[g]
            m_new = jnp.maximum(m_new, jnp.where(chosen[g], top, NEG_INF))
        ps = []
        for g in range(group):
            ref_g = jnp.where(chosen[g], m_new - shift[g], jnp.inf)
            ps.append(jnp.exp2(xs[g] - ref_g).astype(BF16))
        return m_new, jnp.exp2(m - m_new), jnp.concatenate(ps, axis=0)

    def group_start(i):
        return pl.multiple_of(i * (group * blk), group * blk)

    def key_group(i, cur_ref, next_ref):
        sm = {}
        for h in range(heads + 1):
            if next_ref is not None and h < heads:
                next_ref[h] = scores(h, group_start(i + 1), group * blk)
            if h < heads:
                sm[h] = softmax_update(h, i, cur_ref[h], m_ref[h])
            if h >= 1:
                m_new, alpha, p = sm[h - 1]
                m_ref[h - 1] = m_new
                acc_ref[h - 1] = alpha * acc_ref[h - 1] + jnp.dot(
                    vt_ref[h - 1, :, pl.ds(group_start(i), group * blk)], p,
                    preferred_element_type=F32)

    def two_groups(pair, carry):
        key_group(2 * pair, even_ref, odd_ref)
        key_group(2 * pair + 1, odd_ref, even_ref)
        return carry

    n_groups = lax.div(own + (group - 1), group)
    ahead = n_groups - 1
    for h in range(heads):
        even_ref[h] = scores(h, group_start(0), group * blk)
    lax.fori_loop(0, lax.div(ahead, 2), two_groups, 0)
    last_is_odd = lax.rem(ahead, 2) == 1

    @pl.when((n_groups > 0) & last_is_odd)
    def _():
        key_group(ahead - 1, even_ref, odd_ref)
        key_group(ahead, odd_ref, None)

    @pl.when((n_groups > 0) & jnp.logical_not(last_is_odd))
    def _():
        key_group(ahead, even_ref, None)

    for h in range(heads):
        acc = acc_ref[h]
        o_ref[:, h * hd:(h + 1) * hd] = (acc[:hd] / acc[hd:hd + 1]).T.astype(o_ref.dtype)


def _moba(proj, slopes):
    b, s, _ = proj.shape
    assert s % (MOBA_BLOCK * MOBA_GROUP) == 0
    nblk = s // MOBA_BLOCK
    hp = MOBA_HEADS_PER_STEP
    width = hp * HEAD_DIM
    q0 = 3 * DIL_WIDTH // width
    k0 = q0 + MOBA_WIDTH // width
    v0 = k0 + MOBA_WIDTH // width
    y = pl.pallas_call(
        functools.partial(_moba_body, nblk=nblk, group=MOBA_GROUP),
        grid=(b, MOBA_HEADS // hp, nblk),
        in_specs=[
            pl.BlockSpec(memory_space=pltpu.SMEM),
            pl.BlockSpec((None, MOBA_BLOCK, width), lambda bi, h, i: (bi, i, q0 + h)),
            pl.BlockSpec((None, s, width), lambda bi, h, i: (bi, 0, k0 + h),
                         pipeline_mode=pl.Buffered(1)),
            pl.BlockSpec((None, s, width), lambda bi, h, i: (bi, 0, v0 + h),
                         pipeline_mode=pl.Buffered(1)),
        ],
        out_specs=pl.BlockSpec((None, MOBA_BLOCK, width), lambda bi, h, i: (bi, i, h)),
        out_shape=jax.ShapeDtypeStruct((b, s, MOBA_WIDTH), BF16),
        scratch_shapes=[pltpu.VMEM((nblk, width), F32),
                        pltpu.VMEM((hp, HEAD_DIM + BF16_ROWS, s), BF16),
                        pltpu.VMEM((hp, MOBA_BLOCK, MOBA_BLOCK), F32),
                        pltpu.VMEM((hp, nblk, MOBA_BLOCK), F32),
                        pltpu.VMEM((hp, 1, MOBA_BLOCK), F32),
                        pltpu.VMEM((hp, HEAD_DIM + BF16_ROWS, MOBA_BLOCK), F32),
                        pltpu.VMEM((hp, MOBA_GROUP * MOBA_BLOCK, MOBA_BLOCK), F32),
                        pltpu.VMEM((hp, MOBA_GROUP * MOBA_BLOCK, MOBA_BLOCK), F32)],
        compiler_params=_params("parallel", "parallel", "arbitrary"),
        name="moba_attn",
    )(slopes, proj, proj, proj)
    return y.reshape(b * s, MOBA_WIDTH)


def _merged_branches(x, o_refs, l_refs, ym_ref, gd_ref, gm_ref, wud_ref, wum_ref, wo_ref):
    heads = []
    for h in range(DIL_HEADS_PER_GROUP):
        l0, l1, l2 = (l_ref[h] for l_ref in l_refs)
        m = jnp.maximum(jnp.maximum(l0, l1), l2)
        e0, e1, e2 = jnp.exp(l0 - m), jnp.exp(l1 - m), jnp.exp(l2 - m)
        den = e0 + e1 + e2
        mixed = ((e0 / den) * o_refs[0][h] + (e1 / den) * o_refs[1][h] + (e2 / den) * o_refs[2][h])
        heads.append(mixed.astype(BF16))
    y_dil = jnp.concatenate(heads, axis=-1)
    lift_dil = jnp.dot(y_dil, wud_ref[...], preferred_element_type=F32)
    lift_moba = jnp.dot(ym_ref[...], wum_ref[...], preferred_element_type=F32)
    merged = gd_ref[...].astype(F32) * lift_dil + gm_ref[...].astype(F32) * lift_moba
    return x + jnp.dot(merged.astype(BF16), wo_ref[...], preferred_element_type=F32)


def _cross_attended(x, g_ref, wq_ref, kv_ref, wo_ref):
    h = _rms(x, g_ref[...]).astype(BF16)
    q = jnp.dot(h, wq_ref[...], preferred_element_type=F32).astype(BF16)
    heads = []
    for hd in range(MEM_HEADS):
        k = kv_ref[:, hd * HEAD_DIM:(hd + 1) * HEAD_DIM]
        v = kv_ref[:, MEM_WIDTH + hd * HEAD_DIM:MEM_WIDTH + (hd + 1) * HEAD_DIM]
        s = lax.dot_general(q[:, hd * HEAD_DIM:(hd + 1) * HEAD_DIM], k, _NT,
                            preferred_element_type=F32) * SCALE
        p = jnp.exp(s - jnp.max(s, axis=-1, keepdims=True))
        den = jnp.sum(p, axis=-1, keepdims=True)
        heads.append((jnp.dot(p.astype(BF16), v, preferred_element_type=F32) / den).astype(BF16))
    o = jnp.concatenate(heads, axis=-1)
    return x + jnp.dot(o, wo_ref[...], preferred_element_type=F32)


def _merge_cross_body(x_ref, o0_ref, o1_ref, o2_ref, l0_ref, l1_ref, l2_ref, ym_ref, gd_ref, gm_ref,
                      wud_ref, wum_ref, wo_ref, gc_ref, wq_ref, kv_ref, wom_ref, out_ref):
    x1 = _merged_branches(x_ref[...], (o0_ref, o1_ref, o2_ref), (l0_ref, l1_ref, l2_ref),
                          ym_ref, gd_ref, gm_ref, wud_ref, wum_ref, wo_ref)
    out_ref[...] = _cross_attended(x1, gc_ref, wq_ref, kv_ref, wom_ref)


def _merge_cross(x, dil_outs, dil_lses, y_moba, gates, w_up_dil, w_up_moba, w_out,
                 g_cross, w_q, kv, w_o, *, tm):
    t, d = x.shape
    s = dil_outs[0].shape[2]
    mem_len = kv.shape[1]
    assert t % tm == 0 and s % tm == 0
    per_batch = s // tm
    row = lambda w: pl.BlockSpec((tm, w), lambda i: (i, 0))
    dil = pl.BlockSpec((None, DIL_HEADS_PER_GROUP, tm, HEAD_DIM),
                       lambda i: (i // per_batch, 0, i % per_batch, 0))
    return pl.pallas_call(
        _merge_cross_body,
        grid=(t // tm,),
        in_specs=[row(d)] + [dil] * 6 + [
            row(MOBA_WIDTH),
            pl.BlockSpec((tm, d), lambda i: (i, 0)),
            pl.BlockSpec((tm, d), lambda i: (i, 1)),
            _resident((DIL_OUT, d)), _resident((MOBA_WIDTH, d)), _resident((d, d)),
            _resident((1, d)), _resident((d, MEM_WIDTH)),
            pl.BlockSpec((None, mem_len, 2 * MEM_WIDTH), lambda i: (i // per_batch, 0, 0)),
            _resident((MEM_WIDTH, d)),
        ],
        out_specs=row(d),
        out_shape=jax.ShapeDtypeStruct((t, d), F32),
        compiler_params=_params("parallel"),
        name="merge_out_proj_cross_attn",
    )(x, *dil_outs, *dil_lses, y_moba, gates, gates, w_up_dil, w_up_moba, w_out,
      g_cross.reshape(1, d), w_q, kv, w_o)


ROUTE_COLS = 8
MOE_ROW_TILE = 256


def _route_body(x_ref, g_ref, wr_ref, br_ref, info_ref, counts_ref, run_ref, tri_ref):
    tm = x_ref.shape[0]
    n_route = N_GROUPS + N_EXPERTS
    lane = lax.broadcasted_iota(jnp.int32, (tm, n_route), 1).astype(F32)

    @pl.when(pl.program_id(0) == 0)
    def _():
        run_ref[...] = jnp.zeros_like(run_ref)
        earlier = (lax.broadcasted_iota(jnp.int32, (tm, tm), 0)
                   > lax.broadcasted_iota(jnp.int32, (tm, tm), 1))
        tri_ref[...] = jnp.where(earlier, 1.0, 0.0).astype(BF16)

    t = _rms(x_ref[...], g_ref[...])
    logits = _dot_split3(t, wr_ref[...]) + br_ref[...]
    none = float(n_route)
    glog = jnp.where(lane < N_GROUPS, logits, NEG_INF)
    gmax = jnp.max(glog, axis=-1, keepdims=True)
    gsel = jnp.min(jnp.where(glog == gmax, lane, none), axis=-1, keepdims=True)
    pg = 1.0 / jnp.sum(jnp.exp(glog - gmax), axis=-1, keepdims=True)
    first = N_GROUPS + gsel * EXPERTS_PER_GROUP
    in_group = (lane >= first) & (lane < first + EXPERTS_PER_GROUP)
    elog = jnp.where(in_group, logits, NEG_INF)
    top1 = jnp.max(elog, axis=-1, keepdims=True)
    i1 = jnp.min(jnp.where(elog == top1, lane, none), axis=-1, keepdims=True)
    rest = jnp.where(lane == i1, NEG_INF, elog)
    top2 = jnp.max(rest, axis=-1, keepdims=True)
    i2 = jnp.min(jnp.where(rest == top2, lane, none), axis=-1, keepdims=True)
    e2 = jnp.exp(top2 - top1)
    w1 = pg / (1.0 + e2)
    w2 = pg * e2 / (1.0 + e2)

    hit1 = lane == i1
    hit2 = lane == i2
    assigned = jnp.where(hit1 | hit2, 1.0, 0.0)
    before = jnp.dot(tri_ref[...], assigned.astype(BF16), preferred_element_type=F32) + run_ref[...]
    rank1 = jnp.sum(jnp.where(hit1, before, 0.0), axis=-1, keepdims=True)
    rank2 = jnp.sum(jnp.where(hit2, before, 0.0), axis=-1, keepdims=True)
    run_ref[...] += jnp.sum(assigned, axis=0, keepdims=True)
    counts_ref[...] = run_ref[...]

    col = lax.broadcasted_iota(jnp.int32, (tm, ROUTE_COLS), 1)
    fields = (i1 - N_GROUPS, i2 - N_GROUPS, rank1, rank2, w1, w2)
    info = jnp.zeros((tm, ROUTE_COLS), F32)
    for c, field in enumerate(fields):
        info = jnp.where(col == c, field, info)
    info_ref[...] = info


def _route(x, g, w_route, b_route, *, tm):
    t, d = x.shape
    n_route = N_GROUPS + N_EXPERTS
    assert t % tm == 0
    return pl.pallas_call(
        _route_body,
        grid=(t // tm,),
        in_specs=[
            pl.BlockSpec((tm, d), lambda i: (i, 0)),
            pl.BlockSpec((1, d), lambda i: (0, 0)),
            pl.BlockSpec((d, n_route), lambda i: (0, 0)),
            pl.BlockSpec((1, n_route), lambda i: (0, 0)),
        ],
        out_specs=[pl.BlockSpec((tm, ROUTE_COLS), lambda i: (i, 0)),
                   pl.BlockSpec((1, n_route), lambda i: (0, 0))],
        out_shape=[jax.ShapeDtypeStruct((t, ROUTE_COLS), F32),
                   jax.ShapeDtypeStruct((1, n_route), F32)],
        scratch_shapes=[pltpu.VMEM((1, n_route), F32), pltpu.VMEM((tm, tm), BF16)],
        compiler_params=_params("arbitrary"),
        name="moe_route",
    )(x, g.reshape(1, d), w_route, b_route.reshape(1, n_route))


def _row_copies_wait(src_rows, dst_rows, sem):
    pltpu.make_async_copy(src_rows, dst_rows, sem).wait()


def _slot_block(tm):
    return pl.BlockSpec((MOE_TOPK, tm), lambda i, *_: (0, i), memory_space=pltpu.SMEM)


def _dispatch_body(fill_ref, pos_ref, x_ref, g_ref, sorted_ref, t_ref, zero_ref, sems, fill_sem):
    step = pl.program_id(0)
    tm = x_ref.shape[0]
    tr = zero_ref.shape[0]
    slot = lax.rem(step, 2)

    @pl.when(pl.program_id(0) == 0)
    def _():
        zero_ref[...] = jnp.zeros_like(zero_ref)

        def fill_copy(tile):
            return pltpu.make_async_copy(zero_ref, sorted_ref.at[pl.ds(tile * tr, tr)], fill_sem)

        def start(idx, carry):
            pl.when(fill_ref[idx] >= 0)(lambda: fill_copy(fill_ref[idx]).start())
            return carry

        def finish(idx, carry):
            pl.when(fill_ref[idx] >= 0)(lambda: fill_copy(fill_ref[idx]).wait())
            return carry

        lax.fori_loop(0, fill_ref.shape[0], start, 0)
        lax.fori_loop(0, fill_ref.shape[0], finish, 0)

    t_ref[slot] = _rms(x_ref[...], g_ref[...])

    def issue(r, carry):
        for k in range(MOE_TOPK):
            dst = pos_ref[k, r]
            pltpu.make_async_copy(t_ref.at[slot, pl.ds(r, 1)], sorted_ref.at[pl.ds(dst, 1)],
                                  sems.at[slot]).start()
        return carry

    def copies_done(buf):
        for _ in range(MOE_TOPK):
            _row_copies_wait(t_ref.at[buf], sorted_ref.at[pl.ds(0, tm)], sems.at[buf])

    lax.fori_loop(0, tm, issue, 0, unroll=8)
    pl.when(step > 0)(lambda: copies_done(1 - slot))
    pl.when(step == pl.num_programs(0) - 1)(lambda: copies_done(slot))


def _dispatch(x, g, pos, fill_tiles, n_rows, *, tm):
    t, d = x.shape
    assert t % tm == 0
    return pl.pallas_call(
        _dispatch_body,
        grid_spec=pltpu.PrefetchScalarGridSpec(
            num_scalar_prefetch=1,
            grid=(t // tm,),
            in_specs=[_slot_block(tm),
                      pl.BlockSpec((tm, d), lambda i, fill: (i, 0)),
                      pl.BlockSpec((1, d), lambda i, fill: (0, 0))],
            out_specs=pl.BlockSpec(memory_space=pl.ANY),
            scratch_shapes=[pltpu.VMEM((2, tm, d), F32), pltpu.VMEM((MOE_ROW_TILE, d), F32),
                            pltpu.SemaphoreType.DMA((2,)), pltpu.SemaphoreType.DMA(())],
        ),
        out_shape=jax.ShapeDtypeStruct((n_rows, d), F32),
        compiler_params=pltpu.CompilerParams(dimension_semantics=("arbitrary",),
                                             vmem_limit_bytes=VMEM_LIMIT_BYTES,
                                             disable_bounds_checks=True),
        name="moe_dispatch",
    )(fill_tiles, pos, x, g.reshape(1, d))


def _experts_body(tile_expert_ref, n_tiles_ref, x_ref, wg_ref, wu_ref, wd_ref, y_ref):
    del tile_expert_ref
    in_use = pl.program_id(0) < n_tiles_ref[0]

    @pl.when(in_use)
    def _():
        t = x_ref[...].astype(BF16)
        gate = jnp.dot(t, wg_ref[...].astype(BF16), preferred_element_type=F32)
        up = jnp.dot(t, wu_ref[...].astype(BF16), preferred_element_type=F32)
        a = jax.nn.silu(gate) * up
        y_ref[...] = jnp.dot(a.astype(BF16), wd_ref[...].astype(BF16), preferred_element_type=F32)

    @pl.when(jnp.logical_not(in_use))
    def _():
        y_ref[...] = jnp.zeros_like(y_ref)


def _experts(sorted_rows, tile_expert, n_tiles, w_gate, w_up, w_down):
    p, d = sorted_rows.shape
    ff = w_gate.shape[-1]
    tr = MOE_ROW_TILE
    assert p % tr == 0
    used = lambda i, nt: jnp.minimum(i, nt[0] - 1)
    return pl.pallas_call(
        _experts_body,
        grid_spec=pltpu.PrefetchScalarGridSpec(
            num_scalar_prefetch=2,
            grid=(p // tr,),
            in_specs=[pl.BlockSpec((tr, d), lambda i, te, nt: (used(i, nt), 0)),
                      pl.BlockSpec((None, d, ff), lambda i, te, nt: (te[used(i, nt)], 0, 0)),
                      pl.BlockSpec((None, d, ff), lambda i, te, nt: (te[used(i, nt)], 0, 0)),
                      pl.BlockSpec((None, ff, d), lambda i, te, nt: (te[used(i, nt)], 0, 0))],
            out_specs=pl.BlockSpec((tr, d), lambda i, te, nt: (i, 0)),
        ),
        out_shape=jax.ShapeDtypeStruct((p, d), F32),
        compiler_params=_params("arbitrary"),
        name="moe_experts",
    )(tile_expert, n_tiles, sorted_rows, w_gate, w_up, w_down)


def _combine_body(pos_ref, pos_next_ref, x_ref, info_ref, gf_ref, y_sorted_ref, out_ref,
                  rows_ref, sems):
    step = pl.program_id(0)
    tm = x_ref.shape[0]
    slot = lax.rem(step, 2)

    def fetch(table_ref, buf):
        def issue(r, carry):
            for k in range(MOE_TOPK):
                src = table_ref[k, r]
                pltpu.make_async_copy(y_sorted_ref.at[pl.ds(src, 1)],
                                      rows_ref.at[buf, k, pl.ds(r, 1)], sems.at[buf]).start()
            return carry
        lax.fori_loop(0, tm, issue, 0, unroll=8)

    pl.when(step == 0)(lambda: fetch(pos_ref, 0))
    pl.when(step + 1 < pl.num_programs(0))(lambda: fetch(pos_next_ref, 1 - slot))
    for k in range(MOE_TOPK):
        _row_copies_wait(y_sorted_ref.at[pl.ds(0, tm)], rows_ref.at[slot, k], sems.at[slot])

    info = info_ref[...]
    y = info[:, 4:5] * rows_ref[slot, 0] + info[:, 5:6] * rows_ref[slot, 1]
    out_ref[...] = _rms(x_ref[...] + y, gf_ref[...])


def _combine(x, info, pos, y_sorted, g_final, *, tm):
    t, d = x.shape
    assert t % tm == 0
    return pl.pallas_call(
        _combine_body,
        grid=(t // tm,),
        in_specs=[_slot_block(tm),
                  pl.BlockSpec((MOE_TOPK, tm), lambda i: (0, jnp.minimum(i + 1, t // tm - 1)),
                               memory_space=pltpu.SMEM),
                  pl.BlockSpec((tm, d), lambda i: (i, 0)),
                  pl.BlockSpec((tm, ROUTE_COLS), lambda i: (i, 0)),
                  pl.BlockSpec((1, d), lambda i: (0, 0)),
                  pl.BlockSpec(memory_space=pl.ANY)],
        out_specs=pl.BlockSpec((tm, d), lambda i: (i, 0)),
        scratch_shapes=[pltpu.VMEM((2, MOE_TOPK, tm, d), F32), pltpu.SemaphoreType.DMA((2,))],
        out_shape=jax.ShapeDtypeStruct((t, d), F32),
        compiler_params=pltpu.CompilerParams(dimension_semantics=("arbitrary",),
                                             vmem_limit_bytes=VMEM_LIMIT_BYTES,
                                             disable_bounds_checks=True),
        name="moe_combine_final_norm",
    )(pos, pos, x, info, g_final.reshape(1, d), y_sorted)


def _moe(x, g, w_route, b_route, w_gate, w_up, w_down, g_final):
    t, d = x.shape
    tr = MOE_ROW_TILE
    assert (MOE_TOPK * t) % tr == 0
    info, counts = _route(x, g, w_route, b_route, tm=min(512, t))

    expert = info[:, 0:MOE_TOPK].astype(jnp.int32)
    rank = info[:, MOE_TOPK:2 * MOE_TOPK].astype(jnp.int32)
    count = counts[0, N_GROUPS:].astype(jnp.int32)
    seg_tiles = (count + (tr - 1)) // tr
    seg_end = jnp.cumsum(seg_tiles)
    seg_start_row = (seg_end - seg_tiles) * tr
    pos = (seg_start_row[expert] + rank).T
    max_tiles = (MOE_TOPK * t) // tr + N_EXPERTS
    tile_id = jnp.arange(max_tiles, dtype=jnp.int32)
    tile_expert = jnp.minimum(
        jnp.sum((seg_end[None, :] <= tile_id[:, None]).astype(jnp.int32), axis=1), N_EXPERTS - 1)
    n_tiles = seg_end[-1:].astype(jnp.int32)

    tail = n_tiles + jnp.arange(N_EXPERTS, dtype=jnp.int32)
    fill_tiles = jnp.concatenate([jnp.where(seg_tiles > 0, seg_end - 1, -1),
                                  jnp.where(tail < max_tiles, tail, -1)]).astype(jnp.int32)

    sorted_rows = _dispatch(x, g, pos, fill_tiles, max_tiles * tr, tm=min(512, t))
    y_sorted = _experts(sorted_rows, tile_expert, n_tiles, w_gate, w_up, w_down)
    return _combine(x, info, pos, y_sorted, g_final, tm=min(512, t))


def _layer(x, mem, attn_norm, w_in, w_up_dil, w_up_moba, w_branch_gate, w_out, cross_norm,
           mem_norm, w_q_mem, w_kv_mem, w_o_mem):
    b, s, d = x.shape
    t = b * s
    xt = x.reshape(t, d)
    h = _norm(xt, attn_norm, tm=min(512, t), name="attn_norm")
    proj, gates = _in_proj(h, w_in, w_branch_gate, tm=min(2048, t))
    proj = proj.reshape(b, s, IN_WIDTH)
    dil_slopes = _alibi_slopes(DIL_HEADS)
    dil = [_dilated_group(proj, dil_slopes, g, dilation)
           for g, (_, dilation) in enumerate(DIL_CONFIGS)]
    y_moba = _moba(proj, _alibi_slopes(MOBA_HEADS))
    mem_len = mem.shape[1]
    kv = _norm_matmul(mem.reshape(b * mem_len, d), mem_norm, w_kv_mem.astype(BF16),
                      tm=b * mem_len, tn=512, name="norm_mem_kv").reshape(b, mem_len, 2 * MEM_WIDTH)
    return _merge_cross(xt, [o for o, _ in dil], [l for _, l in dil], y_moba, gates,
                        w_up_dil.astype(BF16), w_up_moba.astype(BF16), w_out.astype(BF16),
                        cross_norm, w_q_mem.astype(BF16), kv, w_o_mem.astype(BF16), tm=min(256, s))


def kernel(x, mem, attn_norm, w_in, w_up_dil, w_up_moba, w_branch_gate, w_out, cross_norm, mem_norm,
           w_q_mem, w_kv_mem, w_o_mem, ffn_norm, w_router_group, b_router_group, w_router_expert,
           b_router_expert, w_expert_gate, w_expert_up, w_expert_down, final_norm):
    b, s, d = x.shape
    depth = attn_norm.shape[0]
    assert depth == 1, "the final norm is fused into the last layer's MoE call"
    l = 0
    x2 = _layer(x, mem, attn_norm[l], w_in[l], w_up_dil[l], w_up_moba[l], w_branch_gate[l], w_out[l],
                cross_norm[l], mem_norm[l], w_q_mem[l], w_kv_mem[l], w_o_mem[l])
    w_route = jnp.concatenate([w_router_group[l], w_router_expert[l]], axis=1)
    b_route = jnp.concatenate([b_router_group[l], b_router_expert[l]], axis=0)
    out = _moe(x2, ffn_norm[l], w_route, b_route, w_expert_gate[l], w_expert_up[l],
               w_expert_down[l], final_norm)
    return out.reshape(b, s, d)
```

```python
import functools

import numpy as np
import jax
import jax.numpy as jnp
from jax import lax
from jax.experimental import pallas as pl
from jax.experimental.pallas import tpu as pltpu

F32 = jnp.float32
BF16 = jnp.bfloat16

HEAD_DIM = 128
DIL_CONFIGS = ((128, 1), (512, 4), (2048, 16))
DIL_HEADS_PER_GROUP = 4
DIL_HEADS = DIL_HEADS_PER_GROUP * len(DIL_CONFIGS)
DIL_WIDTH = DIL_HEADS * HEAD_DIM
DIL_OUT = DIL_HEADS_PER_GROUP * HEAD_DIM
DIL_STEPS = 128
MOBA_HEADS = 8
MOBA_WIDTH = MOBA_HEADS * HEAD_DIM
MOBA_BLOCK = 256
MOBA_TOPK = 3
IN_WIDTH = 3 * (DIL_WIDTH + MOBA_WIDTH)
MEM_HEADS = 4
MEM_WIDTH = MEM_HEADS * HEAD_DIM
N_GROUPS = 4
EXPERTS_PER_GROUP = 8
N_EXPERTS = N_GROUPS * EXPERTS_PER_GROUP
MOE_TOPK = 2
RMS_EPS = 1e-6
SCALE = HEAD_DIM ** -0.5
NEG_INF = float("-inf")
LOG2E = 1.4426950408889634
MOBA_GROUP = 4
MOBA_HEADS_PER_STEP = 4
MOBA_Q_TILES = (9, 11)
IN_PROJ_ROW_CHUNK = 512

VMEM_LIMIT_BYTES = 56 * 1024 * 1024
BF16_ROWS = 16
MXU_COLUMNS = 256
_NT = (((1,), (1,)), ((), ()))


def _alibi_slopes(n):
    return jnp.asarray(2.0 ** (-8.0 * np.arange(1, n + 1) / n), dtype=F32)


def _rms(x, g):
    return x * lax.rsqrt(jnp.mean(x * x, axis=-1, keepdims=True) + RMS_EPS) * g


def _dot_split3(a, b):
    a_hi = a.astype(BF16)
    b_hi = b.astype(BF16)
    a_lo = (a - a_hi.astype(F32)).astype(BF16)
    b_lo = (b - b_hi.astype(F32)).astype(BF16)
    dot = functools.partial(jnp.dot, preferred_element_type=F32)
    n = b.shape[1]
    if 2 * n <= MXU_COLUMNS:
        both = dot(a_hi, jnp.concatenate([b_hi, b_lo], axis=1))
        return both[:, :n] + (dot(a_lo, b_hi) + both[:, n:])
    return dot(a_hi, b_hi) + (dot(a_lo, b_hi) + dot(a_hi, b_lo))


def _params(*sem):
    return pltpu.CompilerParams(dimension_semantics=sem, vmem_limit_bytes=VMEM_LIMIT_BYTES)


def _resident(shape):
    nd = len(shape)
    return pl.BlockSpec(shape, lambda *_: (0,) * nd, pipeline_mode=pl.Buffered(1))


def _norm_matmul_body(x_ref, g_ref, w_ref, o_ref, h_ref):
    @pl.when(pl.program_id(1) == 0)
    def _():
        h_ref[...] = _rms(x_ref[...], g_ref[...]).astype(BF16)

    o_ref[...] = jnp.dot(h_ref[...], w_ref[...], preferred_element_type=F32).astype(o_ref.dtype)


def _norm_matmul(x, g, w, *, tm, tn, name):
    m, d = x.shape
    n = w.shape[1]
    assert m % tm == 0 and n % tn == 0
    return pl.pallas_call(
        _norm_matmul_body,
        grid=(m // tm, n // tn),
        in_specs=[
            pl.BlockSpec((tm, d), lambda i, j: (i, 0)),
            pl.BlockSpec((1, d), lambda i, j: (0, 0)),
            pl.BlockSpec((d, tn), lambda i, j: (0, j)),
        ],
        out_specs=pl.BlockSpec((tm, tn), lambda i, j: (i, j)),
        out_shape=jax.ShapeDtypeStruct((m, n), BF16),
        scratch_shapes=[pltpu.VMEM((tm, d), BF16)],
        compiler_params=_params("parallel", "arbitrary"),
        name=name,
    )(x, g.reshape(1, d), w)


def _norm_body(x_ref, g_ref, o_ref):
    o_ref[...] = _rms(x_ref[...], g_ref[...]).astype(o_ref.dtype)


def _norm(x, g, *, tm, name):
    m, d = x.shape
    assert m % tm == 0
    return pl.pallas_call(
        _norm_body,
        grid=(m // tm,),
        in_specs=[pl.BlockSpec((tm, d), lambda i: (i, 0)), pl.BlockSpec((1, d), lambda i: (0, 0))],
        out_specs=pl.BlockSpec((tm, d), lambda i: (i, 0)),
        out_shape=jax.ShapeDtypeStruct((m, d), BF16),
        compiler_params=_params("parallel"),
        name=name,
    )(x, g.reshape(1, d))


def _in_proj_body(h_ref, win_ref, wbg_ref, proj_ref, gates_ref, perm_ref, *, n_proj_tiles):
    j = pl.program_id(1)
    tm = h_ref.shape[0]
    n = DIL_STEPS
    groups = len(DIL_CONFIGS)

    def store_regrouped(acc, dilation):
        tile = n * dilation
        for c in range(acc.shape[1] // HEAD_DIM):
            cols = slice(c * HEAD_DIM, (c + 1) * HEAD_DIM)
            perm_ref[c] = acc[:, cols]
            for t0 in range(0, tm, tile):
                for r in range(dilation):
                    rows = perm_ref[c, pl.ds(t0 + r, n, stride=dilation), :]
                    proj_ref[t0 + r * n:t0 + (r + 1) * n, cols] = rows.astype(BF16)

    @pl.when(j < n_proj_tiles)
    def _():
        acc = jnp.dot(h_ref[...], win_ref[...].astype(BF16), preferred_element_type=F32)
        group = jnp.where(j < 3 * groups, lax.rem(j, groups), 0)
        moba_q = (j >= MOBA_Q_TILES[0]) & (j < MOBA_Q_TILES[1])
        scale = jnp.where(moba_q, SCALE * LOG2E, 1.0)

        def store_natural():
            proj_ref[...] = (acc * scale).astype(BF16)

        for g, (_, dilation) in enumerate(DIL_CONFIGS):
            store = store_natural if dilation == 1 else functools.partial(store_regrouped, acc, dilation)
            pl.when(group == g)(store)

    @pl.when(j >= n_proj_tiles)
    def _():
        w = wbg_ref[...].astype(BF16)
        for r0 in range(0, tm, IN_PROJ_ROW_CHUNK):
            rows = slice(r0, r0 + IN_PROJ_ROW_CHUNK)
            acc = jnp.dot(h_ref[rows, :], w, preferred_element_type=F32)
            gates_ref[rows, :] = jax.nn.sigmoid(acc).astype(BF16)


def _in_proj(h, w_in, w_bg, *, tm):
    t, d = h.shape
    tn = DIL_OUT
    assert MOBA_Q_TILES == (3 * DIL_WIDTH // tn, (3 * DIL_WIDTH + MOBA_WIDTH) // tn)
    assert t % tm == 0 and tm % (DIL_STEPS * max(dl for _, dl in DIL_CONFIGS)) == 0
    assert DIL_WIDTH == len(DIL_CONFIGS) * tn and w_in.shape[1] % tn == 0 and w_bg.shape[1] % tn == 0
    n_proj = w_in.shape[1] // tn
    n_gate = w_bg.shape[1] // tn
    return pl.pallas_call(
        functools.partial(_in_proj_body, n_proj_tiles=n_proj),
        grid=(t // tm, n_proj + n_gate),
        in_specs=[
            pl.BlockSpec((tm, d), lambda i, j: (i, 0)),
            pl.BlockSpec((d, tn), lambda i, j: (0, jnp.minimum(j, n_proj - 1))),
            pl.BlockSpec((d, tn), lambda i, j: (0, jnp.maximum(j - n_proj, 0))),
        ],
        out_specs=[
            pl.BlockSpec((tm, tn), lambda i, j: (i, jnp.minimum(j, n_proj - 1))),
            pl.BlockSpec((tm, tn), lambda i, j: (i, jnp.maximum(j - n_proj, 0))),
        ],
        out_shape=[jax.ShapeDtypeStruct((t, w_in.shape[1]), BF16),
                   jax.ShapeDtypeStruct((t, w_bg.shape[1]), BF16)],
        scratch_shapes=[pltpu.VMEM((tn // HEAD_DIM, tm, HEAD_DIM), F32)],
        compiler_params=_params("parallel", "arbitrary"),
        name="in_proj_gates",
    )(h, w_in, w_bg)


DIL_SUBBLOCKS = 4


def _dilated_body(slope_ref, q_ref, kp_ref, kc_ref, vp_ref, vc_ref, o_ref, lse_ref, *scratch,
                  dilation, group):
    n = DIL_STEPS
    first_tile = pl.program_id(1) == 0
    chunk = pl.program_id(2)
    chunks = max(dilation // DIL_SUBBLOCKS, 1)
    qi = lax.broadcasted_iota(jnp.int32, (n, n), 0)
    kj = lax.broadcasted_iota(jnp.int32, (n, n), 1)
    steps_cur = qi - kj
    valid_cur = steps_cur >= 0
    dist_cur = (steps_cur * dilation).astype(F32)
    dist_prev = ((steps_cur + n) * dilation).astype(F32)
    limit_across_tiles = jnp.where(first_tile, -n, 0)
    heads = range(DIL_HEADS_PER_GROUP)
    cols_of = [slice(h * HEAD_DIM, (h + 1) * HEAD_DIM) for h in heads]
    rows_of = [slice(s * n, (s + 1) * n) for s in range(DIL_SUBBLOCKS)]

    def prev_rows(s):
        if dilation == 1:
            return (slice(0, n), False, limit_across_tiles) if s == 0 else (rows_of[s - 1], True, 0)
        return rows_of[s], False, limit_across_tiles

    def scores(s):
        rows, from_cur, _ = prev_rows(s)
        out = []
        for h in heads:
            q = q_ref[rows_of[s], cols_of[h]]
            k_prev = (kc_ref if from_cur else kp_ref)[rows, cols_of[h]]
            out.append((lax.dot_general(q, kc_ref[rows_of[s], cols_of[h]], _NT,
                                        preferred_element_type=F32),
                        lax.dot_general(q, k_prev, _NT, preferred_element_type=F32)))
        return out

    def softmax(s, raw):
        valid_prev = steps_cur <= prev_rows(s)[2]
        out = []
        for h in heads:
            slope = slope_ref[group * DIL_HEADS_PER_GROUP + h]
            s_cur = jnp.where(valid_cur, raw[h][0] * SCALE - slope * dist_cur, NEG_INF)
            s_prev = jnp.where(valid_prev, raw[h][1] * SCALE - slope * dist_prev, NEG_INF)
            m = jnp.max(jnp.maximum(s_cur, s_prev), axis=-1, keepdims=True)
            p_cur = jnp.exp(s_cur - m)
            p_prev = jnp.exp(s_prev - m)
            den = jnp.sum(p_cur + p_prev, axis=-1, keepdims=True)
            out.append((p_cur.astype(BF16), p_prev.astype(BF16), m, den))
        return out

    def values(s, probs):
        rows, from_cur, _ = prev_rows(s)
        for h in heads:
            p_cur, p_prev, m, den = probs[h]
            v_prev = (vc_ref if from_cur else vp_ref)[rows, cols_of[h]]
            o = (jnp.dot(p_cur, vc_ref[rows_of[s], cols_of[h]], preferred_element_type=F32)
                 + jnp.dot(p_prev, v_prev, preferred_element_type=F32)) / den
            lse = jnp.broadcast_to(m + jnp.log(den), (n, HEAD_DIM))
            if dilation == 1:
                dst_o, dst_lse, rows_out = o_ref, lse_ref, rows_of[s]
            elif chunks == 1:
                dst_o, dst_lse, rows_out = o_ref, lse_ref, pl.ds(s, n, stride=dilation)
            else:
                dst_o, dst_lse = scratch
                rows_out = pl.ds(pl.multiple_of((chunk * DIL_SUBBLOCKS + s) * n, n), n)
            dst_o[h, rows_out, :] = o
            dst_lse[h, rows_out, :] = lse

    raw, probs = {0: scores(0)}, {}
    for s in range(DIL_SUBBLOCKS + 1):
        if s + 1 < DIL_SUBBLOCKS:
            raw[s + 1] = scores(s + 1)
        if s < DIL_SUBBLOCKS:
            probs[s] = softmax(s, raw.pop(s))
        if s >= 1:
            values(s - 1, probs.pop(s - 1))

    if chunks > 1:
        o_tile, lse_tile = scratch

        @pl.when(chunk == chunks - 1)
        def _():
            for h in heads:
                for r in range(dilation):
                    natural = pl.ds(r, n, stride=dilation)
                    o_ref[h, natural, :] = o_tile[h, r * n:(r + 1) * n, :]
                    lse_ref[h, natural, :] = lse_tile[h, r * n:(r + 1) * n, :]


def _dilated_group(proj, slopes, group, dilation):
    b, s, _ = proj.shape
    n = DIL_STEPS
    step_rows = DIL_SUBBLOCKS * n
    assert dilation == 1 or dilation % DIL_SUBBLOCKS == 0
    chunks = max(dilation // DIL_SUBBLOCKS, 1)
    tile = step_rows * chunks
    assert s % tile == 0
    groups = len(DIL_CONFIGS)
    prev_rows = n if dilation == 1 else step_rows
    blocks_per_step = step_rows // prev_rows

    def cur(section):
        return pl.BlockSpec((None, step_rows, DIL_OUT),
                            lambda bi, i, c: (bi, i * chunks + c, section * groups + group))

    def prev(section):
        def index(bi, i, c):
            if dilation == 1:
                return (bi, jnp.maximum(i * blocks_per_step - 1, 0), section * groups + group)
            return (bi, jnp.maximum(i - 1, 0) * chunks + c, section * groups + group)
        return pl.BlockSpec((None, prev_rows, DIL_OUT), index)

    out_block = (DIL_HEADS_PER_GROUP, tile, HEAD_DIM)
    out_spec = pl.BlockSpec((None,) + out_block, lambda bi, i, c: (bi, 0, i, 0))
    out_sds = jax.ShapeDtypeStruct((b, DIL_HEADS_PER_GROUP, s, HEAD_DIM), F32)
    scratch = [pltpu.VMEM(out_block, F32)] * 2 if chunks > 1 else []
    o, lse = pl.pallas_call(
        functools.partial(_dilated_body, dilation=dilation, group=group),
        grid=(b, s // tile, chunks),
        in_specs=[pl.BlockSpec(memory_space=pltpu.SMEM),
                  cur(0), prev(1), cur(1), prev(2), cur(2)],
        out_specs=[out_spec, out_spec],
        out_shape=[out_sds, out_sds],
        scratch_shapes=scratch,
        compiler_params=_params("parallel", "arbitrary", "arbitrary"),
        name=f"dilated_attn_g{group}",
    )(slopes, proj, proj, proj, proj, proj)
    return o, lse


def _moba_body(slope_ref, q_ref, k_ref, v_ref, o_ref, kmean_ref, vt_ref, bias_ref, sel_ref,
               m_ref, acc_ref, even_ref, odd_ref, *, nblk, group):
    blk = MOBA_BLOCK
    hd = HEAD_DIM
    heads = q_ref.shape[1] // hd
    own = pl.program_id(2)
    key_off = lax.broadcasted_iota(jnp.int32, (blk, blk), 0)
    qry_off = lax.broadcasted_iota(jnp.int32, (blk, blk), 1)
    slope2 = [slope_ref[pl.program_id(1) * heads + h] * LOG2E for h in range(heads)]

    @pl.when(own == 0)
    def _():
        def fill(jb, carry):
            start = pl.multiple_of(jb * blk, blk)
            rows = k_ref[pl.ds(start, blk), :].astype(F32)
            kmean_ref[pl.ds(jb, 1), :] = jnp.mean(rows, axis=0, keepdims=True)
            vrows = v_ref[pl.ds(start, blk), :].astype(F32)
            for h in range(heads):
                vt_ref[h, :hd, pl.ds(start, blk)] = vrows[:, h * hd:(h + 1) * hd].T.astype(BF16)
            return carry
        lax.fori_loop(0, nblk, fill, 0)
        for h in range(heads):
            vt_ref[h, hd:, :] = jnp.ones((vt_ref.shape[1] - hd, vt_ref.shape[2]), BF16)
        for h in range(heads):
            bias_ref[h] = -slope2[h] * (qry_off - key_off).astype(F32)

    q = [q_ref[:, h * hd:(h + 1) * hd] for h in range(heads)]

    blk_id = lax.broadcasted_iota(jnp.int32, (nblk, blk), 0).astype(F32)
    kmean = kmean_ref[...]
    km_hi = kmean.astype(BF16)
    rest = kmean - km_hi.astype(F32)
    km_mid = rest.astype(BF16)
    km_lo = (rest - km_mid.astype(F32)).astype(BF16)
    km3 = jnp.concatenate([km_hi, km_mid, km_lo], axis=0)
    gates = []
    for h in range(heads):
        parts = lax.dot_general(km3[:, h * hd:(h + 1) * hd], q[h], _NT, preferred_element_type=F32)
        gate = parts[:nblk] + (parts[nblk:2 * nblk] + parts[2 * nblk:])
        gates.append(jnp.where(blk_id < own.astype(F32), gate, NEG_INF))

    def scores(h, start, rows):
        k = k_ref[pl.ds(start, rows), h * hd:(h + 1) * hd]
        return lax.dot_general(k, q[h], _NT, preferred_element_type=F32)

    own_start = pl.multiple_of(own * blk, blk)
    own_x = [scores(h, own_start, blk) for h in range(heads)]

    sels = [jnp.zeros((nblk, blk), F32) for _ in range(heads)]
    for _ in range(MOBA_TOPK):
        for h in range(heads):
            best = jnp.max(gates[h], axis=0, keepdims=True)
            is_best = (gates[h] == best) & (gates[h] > NEG_INF)
            pick = jnp.min(jnp.where(is_best, blk_id, float(nblk)), axis=0, keepdims=True)
            picked = blk_id == pick
            sels[h] = jnp.where(picked, 1.0, sels[h])
            gates[h] = jnp.where(picked, NEG_INF, gates[h])
    for h in range(heads):
        sel_ref[h] = sels[h]

    own_p = []
    for h in range(heads):
        x = jnp.where(qry_off >= key_off, own_x[h] + bias_ref[h], NEG_INF)
        m0 = jnp.max(x, axis=0, keepdims=True)
        own_p.append((m0, jnp.exp2(x - m0).astype(BF16)))
    for h in range(heads):
        m0, p = own_p[h]
        m_ref[h] = m0
        acc_ref[h] = jnp.dot(vt_ref[h, :, pl.ds(own_start, blk)], p, preferred_element_type=F32)

    def softmax_update(h, i, x, m):
        xs, chosen, shift = [], [], []
        m_new = m
        for g in range(group):
            j = i * group + g
            xs.append(x[g * blk:(g + 1) * blk] + bias_ref[h])
            chosen.append(sel_ref[h, pl.ds(j, 1), :] > 0.5)
            shift.append(-slope2[h] * ((own - j) * blk).astype(F32))
            top = jnp.max(xs[g], axis=0, keepdims=True) + shift[g]
            m_new = jnp.maximum(m_new, jnp.where(chosen[g], top, NEG_INF))
        ps = []
        for g in range(group):
            ref_g = jnp.where(chosen[g], m_new - shift[g], jnp.inf)
            ps.append(jnp.exp2(xs[g] - ref_g).astype(BF16))
        return m_new, jnp.exp2(m - m_new), jnp.concatenate(ps, axis=0)

    def group_start(i):
        return pl.multiple_of(i * (group * blk), group * blk)

    def key_group(i, cur_ref, next_ref):
        sm = {}
        for h in range(heads + 1):
            if next_ref is not None and h < heads:
                next_ref[h] = scores(h, group_start(i + 1), group * blk)
            if h < heads:
                sm[h] = softmax_update(h, i, cur_ref[h], m_ref[h])
            if h >= 1:
                m_new, alpha, p = sm[h - 1]
                m_ref[h - 1] = m_new
                acc_ref[h - 1] = alpha * acc_ref[h - 1] + jnp.dot(
                    vt_ref[h - 1, :, pl.ds(group_start(i), group * blk)], p,
                    preferred_element_type=F32)

    def two_groups(pair, carry):
        key_group(2 * pair, even_ref, odd_ref)
        key_group(2 * pair + 1, odd_ref, even_ref)
        return carry

    n_groups = lax.div(own + (group - 1), group)
    ahead = n_groups - 1
    for h in range(heads):
        even_ref[h] = scores(h, group_start(0), group * blk)
    lax.fori_loop(0, lax.div(ahead, 2), two_groups, 0)
    last_is_odd = lax.rem(ahead, 2) == 1

    @pl.when((n_groups > 0) & last_is_odd)
    def _():
        key_group(ahead - 1, even_ref, odd_ref)
        key_group(ahead, odd_ref, None)

    @pl.when((n_groups > 0) & jnp.logical_not(last_is_odd))
    def _():
        key_group(ahead, even_ref, None)

    for h in range(heads):
        acc = acc_ref[h]
        o_ref[:, h * hd:(h + 1) * hd] = (acc[:hd] / acc[hd:hd + 1]).T.astype(o_ref.dtype)


def _moba(proj, slopes):
    b, s, _ = proj.shape
    assert s % (MOBA_BLOCK * MOBA_GROUP) == 0
    nblk = s // MOBA_BLOCK
    hp = MOBA_HEADS_PER_STEP
    width = hp * HEAD_DIM
    q0 = 3 * DIL_WIDTH // width
    k0 = q0 + MOBA_WIDTH // width
    v0 = k0 + MOBA_WIDTH // width
    y = pl.pallas_call(
        functools.partial(_moba_body, nblk=nblk, group=MOBA_GROUP),
        grid=(b, MOBA_HEADS // hp, nblk),
        in_specs=[
            pl.BlockSpec(memory_space=pltpu.SMEM),
            pl.BlockSpec((None, MOBA_BLOCK, width), lambda bi, h, i: (bi, i, q0 + h)),
            pl.BlockSpec((None, s, width), lambda bi, h, i: (bi, 0, k0 + h),
                         pipeline_mode=pl.Buffered(1)),
            pl.BlockSpec((None, s, width), lambda bi, h, i: (bi, 0, v0 + h),
                         pipeline_mode=pl.Buffered(1)),
        ],
        out_specs=pl.BlockSpec((None, MOBA_BLOCK, width), lambda bi, h, i: (bi, i, h)),
        out_shape=jax.ShapeDtypeStruct((b, s, MOBA_WIDTH), BF16),
        scratch_shapes=[pltpu.VMEM((nblk, width), F32),
                        pltpu.VMEM((hp, HEAD_DIM + BF16_ROWS, s), BF16),
                        pltpu.VMEM((hp, MOBA_BLOCK, MOBA_BLOCK), F32),
                        pltpu.VMEM((hp, nblk, MOBA_BLOCK), F32),
                        pltpu.VMEM((hp, 1, MOBA_BLOCK), F32),
                        pltpu.VMEM((hp, HEAD_DIM + BF16_ROWS, MOBA_BLOCK), F32),
                        pltpu.VMEM((hp, MOBA_GROUP * MOBA_BLOCK, MOBA_BLOCK), F32),
                        pltpu.VMEM((hp, MOBA_GROUP * MOBA_BLOCK, MOBA_BLOCK), F32)],
        compiler_params=_params("parallel", "parallel", "arbitrary"),
        name="moba_attn",
    )(slopes, proj, proj, proj)
    return y.reshape(b * s, MOBA_WIDTH)


def _merged_branches(x, o_refs, l_refs, ym_ref, gd_ref, gm_ref, wud_ref, wum_ref, wo_ref):
    heads = []
    for h in range(DIL_HEADS_PER_GROUP):
        l0, l1, l2 = (l_ref[h] for l_ref in l_refs)
        m = jnp.maximum(jnp.maximum(l0, l1), l2)
        e0, e1, e2 = jnp.exp(l0 - m), jnp.exp(l1 - m), jnp.exp(l2 - m)
        den = e0 + e1 + e2
        mixed = ((e0 / den) * o_refs[0][h] + (e1 / den) * o_refs[1][h] + (e2 / den) * o_refs[2][h])
        heads.append(mixed.astype(BF16))
    y_dil = jnp.concatenate(heads, axis=-1)
    lift_dil = jnp.dot(y_dil, wud_ref[...], preferred_element_type=F32)
    lift_moba = jnp.dot(ym_ref[...], wum_ref[...], preferred_element_type=F32)
    merged = gd_ref[...].astype(F32) * lift_dil + gm_ref[...].astype(F32) * lift_moba
    return x + jnp.dot(merged.astype(BF16), wo_ref[...], preferred_element_type=F32)


def _cross_attended(x, g_ref, wq_ref, kv_ref, wo_ref):
    h = _rms(x, g_ref[...]).astype(BF16)
    q = jnp.dot(h, wq_ref[...], preferred_element_type=F32).astype(BF16)
    heads = []
    for hd in range(MEM_HEADS):
        k = kv_ref[:, hd * HEAD_DIM:(hd + 1) * HEAD_DIM]
        v = kv_ref[:, MEM_WIDTH + hd * HEAD_DIM:MEM_WIDTH + (hd + 1) * HEAD_DIM]
        s = lax.dot_general(q[:, hd * HEAD_DIM:(hd + 1) * HEAD_DIM], k, _NT,
                            preferred_element_type=F32) * SCALE
        p = jnp.exp(s - jnp.max(s, axis=-1, keepdims=True))
        den = jnp.sum(p, axis=-1, keepdims=True)
        heads.append((jnp.dot(p.astype(BF16), v, preferred_element_type=F32) / den).astype(BF16))
    o = jnp.concatenate(heads, axis=-1)
    return x + jnp.dot(o, wo_ref[...], preferred_element_type=F32)


def _merge_cross_body(x_ref, o0_ref, o1_ref, o2_ref, l0_ref, l1_ref, l2_ref, ym_ref, gd_ref, gm_ref,
                      wud_ref, wum_ref, wo_ref, gc_ref, wq_ref, kv_ref, wom_ref, out_ref):
    x1 = _merged_branches(x_ref[...], (o0_ref, o1_ref, o2_ref), (l0_ref, l1_ref, l2_ref),
                          ym_ref, gd_ref, gm_ref, wud_ref, wum_ref, wo_ref)
    out_ref[...] = _cross_attended(x1, gc_ref, wq_ref, kv_ref, wom_ref)


def _merge_cross(x, dil_outs, dil_lses, y_moba, gates, w_up_dil, w_up_moba, w_out,
                 g_cross, w_q, kv, w_o, *, tm):
    t, d = x.shape
    s = dil_outs[0].shape[2]
    mem_len = kv.shape[1]
    assert t % tm == 0 and s % tm == 0
    per_batch = s // tm
    row = lambda w: pl.BlockSpec((tm, w), lambda i: (i, 0))
    dil = pl.BlockSpec((None, DIL_HEADS_PER_GROUP, tm, HEAD_DIM),
                       lambda i: (i // per_batch, 0, i % per_batch, 0))
    return pl.pallas_call(
        _merge_cross_body,
        grid=(t // tm,),
        in_specs=[row(d)] + [dil] * 6 + [
            row(MOBA_WIDTH),
            pl.BlockSpec((tm, d), lambda i: (i, 0)),
            pl.BlockSpec((tm, d), lambda i: (i, 1)),
            _resident((DIL_OUT, d)), _resident((MOBA_WIDTH, d)), _resident((d, d)),
            _resident((1, d)), _resident((d, MEM_WIDTH)),
            pl.BlockSpec((None, mem_len, 2 * MEM_WIDTH), lambda i: (i // per_batch, 0, 0)),
            _resident((MEM_WIDTH, d)),
        ],
        out_specs=row(d),
        out_shape=jax.ShapeDtypeStruct((t, d), F32),
        compiler_params=_params("parallel"),
        name="merge_out_proj_cross_attn",
    )(x, *dil_outs, *dil_lses, y_moba, gates, gates, w_up_dil, w_up_moba, w_out,
      g_cross.reshape(1, d), w_q, kv, w_o)


ROUTE_COLS = 8
MOE_ROW_TILE = 512


def _route_body(x_ref, g_ref, wr_ref, br_ref, info_ref, counts_ref, run_ref, tri_ref):
    tm = x_ref.shape[0]
    n_route = N_GROUPS + N_EXPERTS
    lane = lax.broadcasted_iota(jnp.int32, (tm, n_route), 1).astype(F32)

    @pl.when(pl.program_id(0) == 0)
    def _():
        run_ref[...] = jnp.zeros_like(run_ref)
        earlier = (lax.broadcasted_iota(jnp.int32, (tm, tm), 0)
                   > lax.broadcasted_iota(jnp.int32, (tm, tm), 1))
        tri_ref[...] = jnp.where(earlier, 1.0, 0.0).astype(BF16)

    t = _rms(x_ref[...], g_ref[...])
    logits = _dot_split3(t, wr_ref[...]) + br_ref[...]
    none = float(n_route)
    glog = jnp.where(lane < N_GROUPS, logits, NEG_INF)
    gmax = jnp.max(glog, axis=-1, keepdims=True)
    gsel = jnp.min(jnp.where(glog == gmax, lane, none), axis=-1, keepdims=True)
    pg = 1.0 / jnp.sum(jnp.exp(glog - gmax), axis=-1, keepdims=True)
    first = N_GROUPS + gsel * EXPERTS_PER_GROUP
    in_group = (lane >= first) & (lane < first + EXPERTS_PER_GROUP)
    elog = jnp.where(in_group, logits, NEG_INF)
    top1 = jnp.max(elog, axis=-1, keepdims=True)
    i1 = jnp.min(jnp.where(elog == top1, lane, none), axis=-1, keepdims=True)
    rest = jnp.where(lane == i1, NEG_INF, elog)
    top2 = jnp.max(rest, axis=-1, keepdims=True)
    i2 = jnp.min(jnp.where(rest == top2, lane, none), axis=-1, keepdims=True)
    e2 = jnp.exp(top2 - top1)
    w1 = pg / (1.0 + e2)
    w2 = pg * e2 / (1.0 + e2)

    hit1 = lane == i1
    hit2 = lane == i2
    assigned = jnp.where(hit1 | hit2, 1.0, 0.0)
    before = jnp.dot(tri_ref[...], assigned.astype(BF16), preferred_element_type=F32) + run_ref[...]
    rank1 = jnp.sum(jnp.where(hit1, before, 0.0), axis=-1, keepdims=True)
    rank2 = jnp.sum(jnp.where(hit2, before, 0.0), axis=-1, keepdims=True)
    run_ref[...] += jnp.sum(assigned, axis=0, keepdims=True)
    counts_ref[...] = run_ref[...]

    col = lax.broadcasted_iota(jnp.int32, (tm, ROUTE_COLS), 1)
    fields = (i1 - N_GROUPS, i2 - N_GROUPS, rank1, rank2, w1, w2)
    info = jnp.zeros((tm, ROUTE_COLS), F32)
    for c, field in enumerate(fields):
        info = jnp.where(col == c, field, info)
    info_ref[...] = info


def _route(x, g, w_route, b_route, *, tm):
    t, d = x.shape
    n_route = N_GROUPS + N_EXPERTS
    assert t % tm == 0
    return pl.pallas_call(
        _route_body,
        grid=(t // tm,),
        in_specs=[
            pl.BlockSpec((tm, d), lambda i: (i, 0)),
            pl.BlockSpec((1, d), lambda i: (0, 0)),
            pl.BlockSpec((d, n_route), lambda i: (0, 0)),
            pl.BlockSpec((1, n_route), lambda i: (0, 0)),
        ],
        out_specs=[pl.BlockSpec((tm, ROUTE_COLS), lambda i: (i, 0)),
                   pl.BlockSpec((1, n_route), lambda i: (0, 0))],
        out_shape=[jax.ShapeDtypeStruct((t, ROUTE_COLS), F32),
                   jax.ShapeDtypeStruct((1, n_route), F32)],
        scratch_shapes=[pltpu.VMEM((1, n_route), F32), pltpu.VMEM((tm, tm), BF16)],
        compiler_params=_params("arbitrary"),
        name="moe_route",
    )(x, g.reshape(1, d), w_route, b_route.reshape(1, n_route))


def _row_copies_wait(src_rows, dst_rows, sem):
    pltpu.make_async_copy(src_rows, dst_rows, sem).wait()


def _slot_block(tm):
    return pl.BlockSpec((MOE_TOPK, tm), lambda i, *_: (0, i), memory_space=pltpu.SMEM)


def _dispatch_body(fill_ref, pos_ref, x_ref, g_ref, sorted_ref, t_ref, zero_ref, sems, fill_sem):
    step = pl.program_id(0)
    tm = x_ref.shape[0]
    tr = zero_ref.shape[0]
    slot = lax.rem(step, 2)

    @pl.when(pl.program_id(0) == 0)
    def _():
        zero_ref[...] = jnp.zeros_like(zero_ref)

        def fill_copy(tile):
            return pltpu.make_async_copy(zero_ref, sorted_ref.at[pl.ds(tile * tr, tr)], fill_sem)

        def start(idx, carry):
            pl.when(fill_ref[idx] >= 0)(lambda: fill_copy(fill_ref[idx]).start())
            return carry

        def finish(idx, carry):
            pl.when(fill_ref[idx] >= 0)(lambda: fill_copy(fill_ref[idx]).wait())
            return carry

        lax.fori_loop(0, fill_ref.shape[0], start, 0)
        lax.fori_loop(0, fill_ref.shape[0], finish, 0)

    t_ref[slot] = _rms(x_ref[...], g_ref[...])

    def issue(r, carry):
        for k in range(MOE_TOPK):
            dst = pos_ref[k, r]
            pltpu.make_async_copy(t_ref.at[slot, pl.ds(r, 1)], sorted_ref.at[pl.ds(dst, 1)],
                                  sems.at[slot]).start()
        return carry

    def copies_done(buf):
        for _ in range(MOE_TOPK):
            _row_copies_wait(t_ref.at[buf], sorted_ref.at[pl.ds(0, tm)], sems.at[buf])

    lax.fori_loop(0, tm, issue, 0, unroll=8)
    pl.when(step > 0)(lambda: copies_done(1 - slot))
    pl.when(step == pl.num_programs(0) - 1)(lambda: copies_done(slot))


def _dispatch(x, g, pos, fill_tiles, n_rows, *, tm):
    t, d = x.shape
    assert t % tm == 0
    return pl.pallas_call(
        _dispatch_body,
        grid_spec=pltpu.PrefetchScalarGridSpec(
            num_scalar_prefetch=1,
            grid=(t // tm,),
            in_specs=[_slot_block(tm),
                      pl.BlockSpec((tm, d), lambda i, fill: (i, 0)),
                      pl.BlockSpec((1, d), lambda i, fill: (0, 0))],
            out_specs=pl.BlockSpec(memory_space=pl.ANY),
            scratch_shapes=[pltpu.VMEM((2, tm, d), F32), pltpu.VMEM((MOE_ROW_TILE, d), F32),
                            pltpu.SemaphoreType.DMA((2,)), pltpu.SemaphoreType.DMA(())],
        ),
        out_shape=jax.ShapeDtypeStruct((n_rows, d), F32),
        compiler_params=pltpu.CompilerParams(dimension_semantics=("arbitrary",),
                                             vmem_limit_bytes=VMEM_LIMIT_BYTES,
                                             disable_bounds_checks=True),
        name="moe_dispatch",
    )(fill_tiles, pos, x, g.reshape(1, d))


def _experts_body(tile_expert_ref, n_tiles_ref, x_ref, wg_ref, wu_ref, wd_ref, y_ref):
    del tile_expert_ref
    in_use = pl.program_id(0) < n_tiles_ref[0]

    @pl.when(in_use)
    def _():
        t = x_ref[...].astype(BF16)
        gate = jnp.dot(t, wg_ref[...].astype(BF16), preferred_element_type=F32)
        up = jnp.dot(t, wu_ref[...].astype(BF16), preferred_element_type=F32)
        a = jax.nn.silu(gate) * up
        y_ref[...] = jnp.dot(a.astype(BF16), wd_ref[...].astype(BF16), preferred_element_type=F32)

    @pl.when(jnp.logical_not(in_use))
    def _():
        y_ref[...] = jnp.zeros_like(y_ref)


def _experts(sorted_rows, tile_expert, n_tiles, w_gate, w_up, w_down):
    p, d = sorted_rows.shape
    ff = w_gate.shape[-1]
    tr = MOE_ROW_TILE
    assert p % tr == 0
    used = lambda i, nt: jnp.minimum(i, nt[0] - 1)
    return pl.pallas_call(
        _experts_body,
        grid_spec=pltpu.PrefetchScalarGridSpec(
            num_scalar_prefetch=2,
            grid=(p // tr,),
            in_specs=[pl.BlockSpec((tr, d), lambda i, te, nt: (used(i, nt), 0)),
                      pl.BlockSpec((None, d, ff), lambda i, te, nt: (te[used(i, nt)], 0, 0)),
                      pl.BlockSpec((None, d, ff), lambda i, te, nt: (te[used(i, nt)], 0, 0)),
                      pl.BlockSpec((None, ff, d), lambda i, te, nt: (te[used(i, nt)], 0, 0))],
            out_specs=pl.BlockSpec((tr, d), lambda i, te, nt: (i, 0)),
        ),
        out_shape=jax.ShapeDtypeStruct((p, d), F32),
        compiler_params=_params("arbitrary"),
        name="moe_experts",
    )(tile_expert, n_tiles, sorted_rows, w_gate, w_up, w_down)


def _combine_body(pos_ref, pos_next_ref, x_ref, info_ref, gf_ref, y_sorted_ref, out_ref,
                  rows_ref, sems):
    step = pl.program_id(0)
    tm = x_ref.shape[0]
    slot = lax.rem(step, 2)

    def fetch(table_ref, buf):
        def issue(r, carry):
            for k in range(MOE_TOPK):
                src = table_ref[k, r]
                pltpu.make_async_copy(y_sorted_ref.at[pl.ds(src, 1)],
                                      rows_ref.at[buf, k, pl.ds(r, 1)], sems.at[buf]).start()
            return carry
        lax.fori_loop(0, tm, issue, 0, unroll=8)

    pl.when(step == 0)(lambda: fetch(pos_ref, 0))
    pl.when(step + 1 < pl.num_programs(0))(lambda: fetch(pos_next_ref, 1 - slot))
    for k in range(MOE_TOPK):
        _row_copies_wait(y_sorted_ref.at[pl.ds(0, tm)], rows_ref.at[slot, k], sems.at[slot])

    info = info_ref[...]
    y = info[:, 4:5] * rows_ref[slot, 0] + info[:, 5:6] * rows_ref[slot, 1]
    out_ref[...] = _rms(x_ref[...] + y, gf_ref[...])


def _combine(x, info, pos, y_sorted, g_final, *, tm):
    t, d = x.shape
    assert t % tm == 0
    return pl.pallas_call(
        _combine_body,
        grid=(t // tm,),
        in_specs=[_slot_block(tm),
                  pl.BlockSpec((MOE_TOPK, tm), lambda i: (0, jnp.minimum(i + 1, t // tm - 1)),
                               memory_space=pltpu.SMEM),
                  pl.BlockSpec((tm, d), lambda i: (i, 0)),
                  pl.BlockSpec((tm, ROUTE_COLS), lambda i: (i, 0)),
                  pl.BlockSpec((1, d), lambda i: (0, 0)),
                  pl.BlockSpec(memory_space=pl.ANY)],
        out_specs=pl.BlockSpec((tm, d), lambda i: (i, 0)),
        scratch_shapes=[pltpu.VMEM((2, MOE_TOPK, tm, d), F32), pltpu.SemaphoreType.DMA((2,))],
        out_shape=jax.ShapeDtypeStruct((t, d), F32),
        compiler_params=pltpu.CompilerParams(dimension_semantics=("arbitrary",),
                                             vmem_limit_bytes=VMEM_LIMIT_BYTES,
                                             disable_bounds_checks=True),
        name="moe_combine_final_norm",
    )(pos, pos, x, info, g_final.reshape(1, d), y_sorted)


def _moe(x, g, w_route, b_route, w_gate, w_up, w_down, g_final):
    t, d = x.shape
    tr = MOE_ROW_TILE
    assert (MOE_TOPK * t) % tr == 0
    info, counts = _route(x, g, w_route, b_route, tm=min(512, t))

    expert = info[:, 0:MOE_TOPK].astype(jnp.int32)
    rank = info[:, MOE_TOPK:2 * MOE_TOPK].astype(jnp.int32)
    count = counts[0, N_GROUPS:].astype(jnp.int32)
    seg_tiles = (count + (tr - 1)) // tr
    seg_end = jnp.cumsum(seg_tiles)
    seg_start_row = (seg_end - seg_tiles) * tr
    pos = (seg_start_row[expert] + rank).T
    max_tiles = (MOE_TOPK * t) // tr + N_EXPERTS
    tile_id = jnp.arange(max_tiles, dtype=jnp.int32)
    tile_expert = jnp.minimum(
        jnp.sum((seg_end[None, :] <= tile_id[:, None]).astype(jnp.int32), axis=1), N_EXPERTS - 1)
    n_tiles = seg_end[-1:].astype(jnp.int32)

    tail = n_tiles + jnp.arange(N_EXPERTS, dtype=jnp.int32)
    fill_tiles = jnp.concatenate([jnp.where(seg_tiles > 0, seg_end - 1, -1),
                                  jnp.where(tail < max_tiles, tail, -1)]).astype(jnp.int32)

    sorted_rows = _dispatch(x, g, pos, fill_tiles, max_tiles * tr, tm=min(512, t))
    y_sorted = _experts(sorted_rows, tile_expert, n_tiles, w_gate, w_up, w_down)
    return _combine(x, info, pos, y_sorted, g_final, tm=min(512, t))


def _layer(x, mem, attn_norm, w_in, w_up_dil, w_up_moba, w_branch_gate, w_out, cross_norm,
           mem_norm, w_q_mem, w_kv_mem, w_o_mem):
    b, s, d = x.shape
    t = b * s
    xt = x.reshape(t, d)
    h = _norm(xt, attn_norm, tm=min(512, t), name="attn_norm")
    proj, gates = _in_proj(h, w_in, w_branch_gate, tm=min(2048, t))
    proj = proj.reshape(b, s, IN_WIDTH)
    dil_slopes = _alibi_slopes(DIL_HEADS)
    dil = [_dilated_group(proj, dil_slopes, g, dilation)
           for g, (_, dilation) in enumerate(DIL_CONFIGS)]
    y_moba = _moba(proj, _alibi_slopes(MOBA_HEADS))
    mem_len = mem.shape[1]
    kv = _norm_matmul(mem.reshape(b * mem_len, d), mem_norm, w_kv_mem.astype(BF16),
                      tm=b * mem_len, tn=512, name="norm_mem_kv").reshape(b, mem_len, 2 * MEM_WIDTH)
    return _merge_cross(xt, [o for o, _ in dil], [l for _, l in dil], y_moba, gates,
                        w_up_dil.astype(BF16), w_up_moba.astype(BF16), w_out.astype(BF16),
                        cross_norm, w_q_mem.astype(BF16), kv, w_o_mem.astype(BF16), tm=min(256, s))


def kernel(x, mem, attn_norm, w_in, w_up_dil, w_up_moba, w_branch_gate, w_out, cross_norm, mem_norm,
           w_q_mem, w_kv_mem, w_o_mem, ffn_norm, w_router_group, b_router_group, w_router_expert,
           b_router_expert, w_expert_gate, w_expert_up, w_expert_down, final_norm):
    b, s, d = x.shape
    depth = attn_norm.shape[0]
    assert depth == 1, "the final norm is fused into the last layer's MoE call"
    l = 0
    x2 = _layer(x, mem, attn_norm[l], w_in[l], w_up_dil[l], w_up_moba[l], w_branch_gate[l], w_out[l],
                cross_norm[l], mem_norm[l], w_q_mem[l], w_kv_mem[l], w_o_mem[l])
    w_route = jnp.concatenate([w_router_group[l], w_router_expert[l]], axis=1)
    b_route = jnp.concatenate([b_router_group[l], b_router_expert[l]], axis=0)
    out = _moe(x2, ffn_norm[l], w_route, b_route, w_expert_gate[l], w_expert_up[l],
               w_expert_down[l], final_norm)
    return out.reshape(b, s, d)
```

```python
import functools

import numpy as np
import jax
import jax.numpy as jnp
from jax import lax
from jax.experimental import pallas as pl
from jax.experimental.pallas import tpu as pltpu

F32 = jnp.float32
BF16 = jnp.bfloat16

HEAD_DIM = 128
DIL_CONFIGS = ((128, 1), (512, 4), (2048, 16))
DIL_HEADS_PER_GROUP = 4
DIL_HEADS = DIL_HEADS_PER_GROUP * len(DIL_CONFIGS)
DIL_WIDTH = DIL_HEADS * HEAD_DIM
DIL_OUT = DIL_HEADS_PER_GROUP * HEAD_DIM
DIL_STEPS = 128
MOBA_HEADS = 8
MOBA_WIDTH = MOBA_HEADS * HEAD_DIM
MOBA_BLOCK = 256
MOBA_TOPK = 3
IN_WIDTH = 3 * (DIL_WIDTH + MOBA_WIDTH)
MEM_HEADS = 4
MEM_WIDTH = MEM_HEADS * HEAD_DIM
N_GROUPS = 4
EXPERTS_PER_GROUP = 8
N_EXPERTS = N_GROUPS * EXPERTS_PER_GROUP
MOE_TOPK = 2
RMS_EPS = 1e-6
SCALE = HEAD_DIM ** -0.5
NEG_INF = float("-inf")
LOG2E = 1.4426950408889634
MOBA_GROUP = 4
MOBA_HEADS_PER_STEP = 4
MOBA_Q_TILES = (9, 11)
IN_PROJ_ROW_CHUNK = 512

VMEM_LIMIT_BYTES = 56 * 1024 * 1024
BF16_ROWS = 16
MXU_COLUMNS = 256
_NT = (((1,), (1,)), ((), ()))


def _alibi_slopes(n):
    return jnp.asarray(2.0 ** (-8.0 * np.arange(1, n + 1) / n), dtype=F32)


def _rms(x, g):
    return x * lax.rsqrt(jnp.mean(x * x, axis=-1, keepdims=True) + RMS_EPS) * g


def _dot_split3(a, b):
    a_hi = a.astype(BF16)
    b_hi = b.astype(BF16)
    a_lo = (a - a_hi.astype(F32)).astype(BF16)
    b_lo = (b - b_hi.astype(F32)).astype(BF16)
    dot = functools.partial(jnp.dot, preferred_element_type=F32)
    n = b.shape[1]
    if 2 * n <= MXU_COLUMNS:
        both = dot(a_hi, jnp.concatenate([b_hi, b_lo], axis=1))
        return both[:, :n] + (dot(a_lo, b_hi) + both[:, n:])
    return dot(a_hi, b_hi) + (dot(a_lo, b_hi) + dot(a_hi, b_lo))


def _params(*sem):
    return pltpu.CompilerParams(dimension_semantics=sem, vmem_limit_bytes=VMEM_LIMIT_BYTES)


def _resident(shape):
    nd = len(shape)
    return pl.BlockSpec(shape, lambda *_: (0,) * nd, pipeline_mode=pl.Buffered(1))


def _norm_matmul_body(x_ref, g_ref, w_ref, o_ref, h_ref):
    @pl.when(pl.program_id(1) == 0)
    def _():
        h_ref[...] = _rms(x_ref[...], g_ref[...]).astype(BF16)

    o_ref[...] = jnp.dot(h_ref[...], w_ref[...], preferred_element_type=F32).astype(o_ref.dtype)


def _norm_matmul(x, g, w, *, tm, tn, name):
    m, d = x.shape
    n = w.shape[1]
    assert m % tm == 0 and n % tn == 0
    return pl.pallas_call(
        _norm_matmul_body,
        grid=(m // tm, n // tn),
        in_specs=[
            pl.BlockSpec((tm, d), lambda i, j: (i, 0)),
            pl.BlockSpec((1, d), lambda i, j: (0, 0)),
            pl.BlockSpec((d, tn), lambda i, j: (0, j)),
        ],
        out_specs=pl.BlockSpec((tm, tn), lambda i, j: (i, j)),
        out_shape=jax.ShapeDtypeStruct((m, n), BF16),
        scratch_shapes=[pltpu.VMEM((tm, d), BF16)],
        compiler_params=_params("parallel", "arbitrary"),
        name=name,
    )(x, g.reshape(1, d), w)


def _norm_body(x_ref, g_ref, o_ref):
    o_ref[...] = _rms(x_ref[...], g_ref[...]).astype(o_ref.dtype)


def _norm(x, g, *, tm, name):
    m, d = x.shape
    assert m % tm == 0
    return pl.pallas_call(
        _norm_body,
        grid=(m // tm,),
        in_specs=[pl.BlockSpec((tm, d), lambda i: (i, 0)), pl.BlockSpec((1, d), lambda i: (0, 0))],
        out_specs=pl.BlockSpec((tm, d), lambda i: (i, 0)),
        out_shape=jax.ShapeDtypeStruct((m, d), BF16),
        compiler_params=_params("parallel"),
        name=name,
    )(x, g.reshape(1, d))


def _in_proj_body(h_ref, win_ref, wbg_ref, proj_ref, gates_ref, perm_ref, *, n_proj_tiles):
    j = pl.program_id(1)
    tm = h_ref.shape[0]
    n = DIL_STEPS
    groups = len(DIL_CONFIGS)

    def store_regrouped(acc, dilation):
        tile = n * dilation
        for c in range(acc.shape[1] // HEAD_DIM):
            cols = slice(c * HEAD_DIM, (c + 1) * HEAD_DIM)
            perm_ref[c] = acc[:, cols]
            for t0 in range(0, tm, tile):
                for r in range(dilation):
                    rows = perm_ref[c, pl.ds(t0 + r, n, stride=dilation), :]
                    proj_ref[t0 + r * n:t0 + (r + 1) * n, cols] = rows.astype(BF16)

    @pl.when(j < n_proj_tiles)
    def _():
        acc = jnp.dot(h_ref[...], win_ref[...].astype(BF16), preferred_element_type=F32)
        group = jnp.where(j < 3 * groups, lax.rem(j, groups), 0)
        moba_q = (j >= MOBA_Q_TILES[0]) & (j < MOBA_Q_TILES[1])
        scale = jnp.where(moba_q, SCALE * LOG2E, 1.0)

        def store_natural():
            proj_ref[...] = (acc * scale).astype(BF16)

        for g, (_, dilation) in enumerate(DIL_CONFIGS):
            store = store_natural if dilation == 1 else functools.partial(store_regrouped, acc, dilation)
            pl.when(group == g)(store)

    @pl.when(j >= n_proj_tiles)
    def _():
        w = wbg_ref[...].astype(BF16)
        for r0 in range(0, tm, IN_PROJ_ROW_CHUNK):
            rows = slice(r0, r0 + IN_PROJ_ROW_CHUNK)
            acc = jnp.dot(h_ref[rows, :], w, preferred_element_type=F32)
            gates_ref[rows, :] = jax.nn.sigmoid(acc).astype(BF16)


def _in_proj(h, w_in, w_bg, *, tm):
    t, d = h.shape
    tn = DIL_OUT
    assert MOBA_Q_TILES == (3 * DIL_WIDTH // tn, (3 * DIL_WIDTH + MOBA_WIDTH) // tn)
    assert t % tm == 0 and tm % (DIL_STEPS * max(dl for _, dl in DIL_CONFIGS)) == 0
    assert DIL_WIDTH == len(DIL_CONFIGS) * tn and w_in.shape[1] % tn == 0 and w_bg.shape[1] % tn == 0
    n_proj = w_in.shape[1] // tn
    n_gate = w_bg.shape[1] // tn
    return pl.pallas_call(
        functools.partial(_in_proj_body, n_proj_tiles=n_proj),
        grid=(t // tm, n_proj + n_gate),
        in_specs=[
            pl.BlockSpec((tm, d), lambda i, j: (i, 0)),
            pl.BlockSpec((d, tn), lambda i, j: (0, jnp.minimum(j, n_proj - 1))),
            pl.BlockSpec((d, tn), lambda i, j: (0, jnp.maximum(j - n_proj, 0))),
        ],
        out_specs=[
            pl.BlockSpec((tm, tn), lambda i, j: (i, jnp.minimum(j, n_proj - 1))),
            pl.BlockSpec((tm, tn), lambda i, j: (i, jnp.maximum(j - n_proj, 0))),
        ],
        out_shape=[jax.ShapeDtypeStruct((t, w_in.shape[1]), BF16),
                   jax.ShapeDtypeStruct((t, w_bg.shape[1]), BF16)],
        scratch_shapes=[pltpu.VMEM((tn // HEAD_DIM, tm, HEAD_DIM), F32)],
        compiler_params=_params("parallel", "arbitrary"),
        name="in_proj_gates",
    )(h, w_in, w_bg)


DIL_SUBBLOCKS = 4


def _dilated_body(slope_ref, q_ref, kp_ref, kc_ref, vp_ref, vc_ref, o_ref, lse_ref, *scratch,
                  dilation, group):
    n = DIL_STEPS
    first_tile = pl.program_id(1) == 0
    chunk = pl.program_id(2)
    chunks = max(dilation // DIL_SUBBLOCKS, 1)
    qi = lax.broadcasted_iota(jnp.int32, (n, n), 0)
    kj = lax.broadcasted_iota(jnp.int32, (n, n), 1)
    steps_cur = qi - kj
    valid_cur = steps_cur >= 0
    dist_cur = (steps_cur * dilation).astype(F32)
    dist_prev = ((steps_cur + n) * dilation).astype(F32)
    limit_across_tiles = jnp.where(first_tile, -n, 0)
    heads = range(DIL_HEADS_PER_GROUP)
    cols_of = [slice(h * HEAD_DIM, (h + 1) * HEAD_DIM) for h in heads]
    rows_of = [slice(s * n, (s + 1) * n) for s in range(DIL_SUBBLOCKS)]

    def prev_rows(s):
        if dilation == 1:
            return (slice(0, n), False, limit_across_tiles) if s == 0 else (rows_of[s - 1], True, 0)
        return rows_of[s], False, limit_across_tiles

    def scores(s):
        rows, from_cur, _ = prev_rows(s)
        out = []
        for h in heads:
            q = q_ref[rows_of[s], cols_of[h]]
            k_prev = (kc_ref if from_cur else kp_ref)[rows, cols_of[h]]
            out.append((lax.dot_general(q, kc_ref[rows_of[s], cols_of[h]], _NT,
                                        preferred_element_type=F32),
                        lax.dot_general(q, k_prev, _NT, preferred_element_type=F32)))
        return out

    def softmax(s, raw):
        valid_prev = steps_cur <= prev_rows(s)[2]
        out = []
        for h in heads:
            slope = slope_ref[group * DIL_HEADS_PER_GROUP + h]
            s_cur = jnp.where(valid_cur, raw[h][0] * SCALE - slope * dist_cur, NEG_INF)
            s_prev = jnp.where(valid_prev, raw[h][1] * SCALE - slope * dist_prev, NEG_INF)
            m = jnp.max(jnp.maximum(s_cur, s_prev), axis=-1, keepdims=True)
            p_cur = jnp.exp(s_cur - m)
            p_prev = jnp.exp(s_prev - m)
            den = jnp.sum(p_cur + p_prev, axis=-1, keepdims=True)
            out.append((p_cur.astype(BF16), p_prev.astype(BF16), m, den))
        return out

    def values(s, probs):
        rows, from_cur, _ = prev_rows(s)
        for h in heads:
            p_cur, p_prev, m, den = probs[h]
            v_prev = (vc_ref if from_cur else vp_ref)[rows, cols_of[h]]
            o = (jnp.dot(p_cur, vc_ref[rows_of[s], cols_of[h]], preferred_element_type=F32)
                 + jnp.dot(p_prev, v_prev, preferred_element_type=F32)) / den
            lse = jnp.broadcast_to(m + jnp.log(den), (n, HEAD_DIM))
            if dilation == 1:
                dst_o, dst_lse, rows_out = o_ref, lse_ref, rows_of[s]
            elif chunks == 1:
                dst_o, dst_lse, rows_out = o_ref, lse_ref, pl.ds(s, n, stride=dilation)
            else:
                dst_o, dst_lse = scratch
                rows_out = pl.ds(pl.multiple_of((chunk * DIL_SUBBLOCKS + s) * n, n), n)
            dst_o[h, rows_out, :] = o
            dst_lse[h, rows_out, :] = lse

    raw, probs = {0: scores(0)}, {}
    for s in range(DIL_SUBBLOCKS + 1):
        if s + 1 < DIL_SUBBLOCKS:
            raw[s + 1] = scores(s + 1)
        if s < DIL_SUBBLOCKS:
            probs[s] = softmax(s, raw.pop(s))
        if s >= 1:
            values(s - 1, probs.pop(s - 1))

    if chunks > 1:
        o_tile, lse_tile = scratch

        @pl.when(chunk == chunks - 1)
        def _():
            for h in heads:
                for r in range(dilation):
                    natural = pl.ds(r, n, stride=dilation)
                    o_ref[h, natural, :] = o_tile[h, r * n:(r + 1) * n, :]
                    lse_ref[h, natural, :] = lse_tile[h, r * n:(r + 1) * n, :]


def _dilated_group(proj, slopes, group, dilation):
    b, s, _ = proj.shape
    n = DIL_STEPS
    step_rows = DIL_SUBBLOCKS * n
    assert dilation == 1 or dilation % DIL_SUBBLOCKS == 0
    chunks = max(dilation // DIL_SUBBLOCKS, 1)
    tile = step_rows * chunks
    assert s % tile == 0
    groups = len(DIL_CONFIGS)
    prev_rows = n if dilation == 1 else step_rows
    blocks_per_step = step_rows // prev_rows

    def cur(section):
        return pl.BlockSpec((None, step_rows, DIL_OUT),
                            lambda bi, i, c: (bi, i * chunks + c, section * groups + group))

    def prev(section):
        def index(bi, i, c):
            if dilation == 1:
                return (bi, jnp.maximum(i * blocks_per_step - 1, 0), section * groups + group)
            return (bi, jnp.maximum(i - 1, 0) * chunks + c, section * groups + group)
        return pl.BlockSpec((None, prev_rows, DIL_OUT), index)

    out_block = (DIL_HEADS_PER_GROUP, tile, HEAD_DIM)
    out_spec = pl.BlockSpec((None,) + out_block, lambda bi, i, c: (bi, 0, i, 0))
    out_sds = jax.ShapeDtypeStruct((b, DIL_HEADS_PER_GROUP, s, HEAD_DIM), F32)
    scratch = [pltpu.VMEM(out_block, F32)] * 2 if chunks > 1 else []
    o, lse = pl.pallas_call(
        functools.partial(_dilated_body, dilation=dilation, group=group),
        grid=(b, s // tile, chunks),
        in_specs=[pl.BlockSpec(memory_space=pltpu.SMEM),
                  cur(0), prev(1), cur(1), prev(2), cur(2)],
        out_specs=[out_spec, out_spec],
        out_shape=[out_sds, out_sds],
        scratch_shapes=scratch,
        compiler_params=_params("parallel", "arbitrary", "arbitrary"),
        name=f"dilated_attn_g{group}",
    )(slopes, proj, proj, proj, proj, proj)
    return o, lse


def _moba_body(slope_ref, q_ref, k_ref, v_ref, o_ref, kmean_ref, vt_ref, bias_ref, sel_ref,
               m_ref, acc_ref, even_ref, odd_ref, *, nblk, group):
    blk = MOBA_BLOCK
    hd = HEAD_DIM
    heads = q_ref.shape[1] // hd
    own = pl.program_id(2)
    key_off = lax.broadcasted_iota(jnp.int32, (blk, blk), 0)
    qry_off = lax.broadcasted_iota(jnp.int32, (blk, blk), 1)
    slope2 = [slope_ref[pl.program_id(1) * heads + h] * LOG2E for h in range(heads)]

    @pl.when(own == 0)
    def _():
        def fill(jb, carry):
            start = pl.multiple_of(jb * blk, blk)
            rows = k_ref[pl.ds(start, blk), :].astype(F32)
            kmean_ref[pl.ds(jb, 1), :] = jnp.mean(rows, axis=0, keepdims=True)
            vrows = v_ref[pl.ds(start, blk), :].astype(F32)
            for h in range(heads):
                vt_ref[h, :hd, pl.ds(start, blk)] = vrows[:, h * hd:(h + 1) * hd].T.astype(BF16)
            return carry
        lax.fori_loop(0, nblk, fill, 0)
        for h in range(heads):
            vt_ref[h, hd:, :] = jnp.ones((vt_ref.shape[1] - hd, vt_ref.shape[2]), BF16)
        for h in range(heads):
            bias_ref[h] = -slope2[h] * (qry_off - key_off).astype(F32)

    q = [q_ref[:, h * hd:(h + 1) * hd] for h in range(heads)]

    blk_id = lax.broadcasted_iota(jnp.int32, (nblk, blk), 0).astype(F32)
    kmean = kmean_ref[...]
    km_hi = kmean.astype(BF16)
    rest = kmean - km_hi.astype(F32)
    km_mid = rest.astype(BF16)
    km_lo = (rest - km_mid.astype(F32)).astype(BF16)
    km3 = jnp.concatenate([km_hi, km_mid, km_lo], axis=0)
    gates = []
    for h in range(heads):
        parts = lax.dot_general(km3[:, h * hd:(h + 1) * hd], q[h], _NT, preferred_element_type=F32)
        gate = parts[:nblk] + (parts[nblk:2 * nblk] + parts[2 * nblk:])
        gates.append(jnp.where(blk_id < own.astype(F32), gate, NEG_INF))

    def scores(h, start, rows):
        k = k_ref[pl.ds(start, rows), h * hd:(h + 1) * hd]
        return lax.dot_general(k, q[h], _NT, preferred_element_type=F32)

    own_start = pl.multiple_of(own * blk, blk)
    own_x = [scores(h, own_start, blk) for h in range(heads)]

    sels = [jnp.zeros((nblk, blk), F32) for _ in range(heads)]
    for _ in range(MOBA_TOPK):
        for h in range(heads):
            best = jnp.max(gates[h], axis=0, keepdims=True)
            is_best = (gates[h] == best) & (gates[h] > NEG_INF)
            pick = jnp.min(jnp.where(is_best, blk_id, float(nblk)), axis=0, keepdims=True)
            picked = blk_id == pick
            sels[h] = jnp.where(picked, 1.0, sels[h])
            gates[h] = jnp.where(picked, NEG_INF, gates[h])
    for h in range(heads):
        sel_ref[h] = sels[h]

    own_p = []
    for h in range(heads):
        x = jnp.where(qry_off >= key_off, own_x[h] + bias_ref[h], NEG_INF)
        m0 = jnp.max(x, axis=0, keepdims=True)
        own_p.append((m0, jnp.exp2(x - m0).astype(BF16)))
    for h in range(heads):
        m0, p = own_p[h]
        m_ref[h] = m0
        acc_ref[h] = jnp.dot(vt_ref[h, :, pl.ds(own_start, blk)], p, preferred_element_type=F32)

    def softmax_update(h, i, x, m):
        xs, chosen, shift = [], [], []
        m_new = m
        for g in range(group):
            j = i * group + g
            xs.append(x[g * blk:(g + 1) * blk] + bias_ref[h])
            chosen.append(sel_ref[h, pl.ds(j, 1), :] > 0.5)
            shift.append(-slope2[h] * ((own - j) * blk).astype(F32))
            top = jnp.max(xs[g], axis=0, keepdims=True) + shift[g]
            m_new = jnp.maximum(m_new, jnp.where(chosen[g], top, NEG_INF))
        ps = []
        for g in range(group):
            ref_g = jnp.where(chosen[g], m_new - shift[g], jnp.inf)
            ps.append(jnp.exp2(xs[g] - ref_g).astype(BF16))
        return m_new, jnp.exp2(m - m_new), jnp.concatenate(ps, axis=0)

    def group_start(i):
        return pl.multiple_of(i * (group * blk), group * blk)

    def key_group(i, cur_ref, next_ref):
        sm = {}
        for h in range(heads + 1):
            if next_ref is not None and h < heads:
                next_ref[h] = scores(h, group_start(i + 1), group * blk)
            if h < heads:
                sm[h] = softmax_update(h, i, cur_ref[h], m_ref[h])
            if h >= 1:
                m_new, alpha, p = sm[h - 1]
                m_ref[h - 1] = m_new
                acc_ref[h - 1] = alpha * acc_ref[h - 1] + jnp.dot(
                    vt_ref[h - 1, :, pl.ds(group_start(i), group * blk)], p,
                    preferred_element_type=F32)

    def two_groups(pair, carry):
        key_group(2 * pair, even_ref, odd_ref)
        key_group(2 * pair + 1, odd_ref, even_ref)
        return carry

    n_groups = lax.div(own + (group - 1), group)
    ahead = n_groups - 1
    for h in range(heads):
        even_ref[h] = scores(h, group_start(0), group * blk)
    lax.fori_loop(0, lax.div(ahead, 2), two_groups, 0)
    last_is_odd = lax.rem(ahead, 2) == 1

    @pl.when((n_groups > 0) & last_is_odd)
    def _():
        key_group(ahead - 1, even_ref, odd_ref)
        key_group(ahead, odd_ref, None)

    @pl.when((n_groups > 0) & jnp.logical_not(last_is_odd))
    def _():
        key_group(ahead, even_ref, None)

    for h in range(heads):
        acc = acc_ref[h]
        o_ref[:, h * hd:(h + 1) * hd] = (acc[:hd] / acc[hd:hd + 1]).T.astype(o_ref.dtype)


def _moba(proj, slopes):
    b, s, _ = proj.shape
    assert s % (MOBA_BLOCK * MOBA_GROUP) == 0
    nblk = s // MOBA_BLOCK
    hp = MOBA_HEADS_PER_STEP
    width = hp * HEAD_DIM
    q0 = 3 * DIL_WIDTH // width
    k0 = q0 + MOBA_WIDTH // width
    v0 = k0 + MOBA_WIDTH // width
    y = pl.pallas_call(
        functools.partial(_moba_body, nblk=nblk, group=MOBA_GROUP),
        grid=(b, MOBA_HEADS // hp, nblk),
        in_specs=[
            pl.BlockSpec(memory_space=pltpu.SMEM),
            pl.BlockSpec((None, MOBA_BLOCK, width), lambda bi, h, i: (bi, i, q0 + h)),
            pl.BlockSpec((None, s, width), lambda bi, h, i: (bi, 0, k0 + h),
                         pipeline_mode=pl.Buffered(1)),
            pl.BlockSpec((None, s, width), lambda bi, h, i: (bi, 0, v0 + h),
                         pipeline_mode=pl.Buffered(1)),
        ],
        out_specs=pl.BlockSpec((None, MOBA_BLOCK, width), lambda bi, h, i: (bi, i, h)),
        out_shape=jax.ShapeDtypeStruct((b, s, MOBA_WIDTH), BF16),
        scratch_shapes=[pltpu.VMEM((nblk, width), F32),
                        pltpu.VMEM((hp, HEAD_DIM + BF16_ROWS, s), BF16),
                        pltpu.VMEM((hp, MOBA_BLOCK, MOBA_BLOCK), F32),
                        pltpu.VMEM((hp, nblk, MOBA_BLOCK), F32),
                        pltpu.VMEM((hp, 1, MOBA_BLOCK), F32),
                        pltpu.VMEM((hp, HEAD_DIM + BF16_ROWS, MOBA_BLOCK), F32),
                        pltpu.VMEM((hp, MOBA_GROUP * MOBA_BLOCK, MOBA_BLOCK), F32),
                        pltpu.VMEM((hp, MOBA_GROUP * MOBA_BLOCK, MOBA_BLOCK), F32)],
        compiler_params=_params("parallel", "parallel", "arbitrary"),
        name="moba_attn",
    )(slopes, proj, proj, proj)
    return y.reshape(b * s, MOBA_WIDTH)


def _merged_branches(x, o_refs, l_refs, ym_ref, gd_ref, gm_ref, wud_ref, wum_ref, wo_ref):
    heads = []
    for h in range(DIL_HEADS_PER_GROUP):
        l0, l1, l2 = (l_ref[h] for l_ref in l_refs)
        m = jnp.maximum(jnp.maximum(l0, l1), l2)
        e0, e1, e2 = jnp.exp(l0 - m), jnp.exp(l1 - m), jnp.exp(l2 - m)
        den = e0 + e1 + e2
        mixed = ((e0 / den) * o_refs[0][h] + (e1 / den) * o_refs[1][h] + (e2 / den) * o_refs[2][h])
        heads.append(mixed.astype(BF16))
    y_dil = jnp.concatenate(heads, axis=-1)
    lift_dil = jnp.dot(y_dil, wud_ref[...], preferred_element_type=F32)
    lift_moba = jnp.dot(ym_ref[...], wum_ref[...], preferred_element_type=F32)
    merged = gd_ref[...].astype(F32) * lift_dil + gm_ref[...].astype(F32) * lift_moba
    return x + jnp.dot(merged.astype(BF16), wo_ref[...], preferred_element_type=F32)


def _cross_attended(x, g_ref, wq_ref, kv_ref, wo_ref):
    h = _rms(x, g_ref[...]).astype(BF16)
    q = jnp.dot(h, wq_ref[...], preferred_element_type=F32).astype(BF16)
    heads = []
    for hd in range(MEM_HEADS):
        k = kv_ref[:, hd * HEAD_DIM:(hd + 1) * HEAD_DIM]
        v = kv_ref[:, MEM_WIDTH + hd * HEAD_DIM:MEM_WIDTH + (hd + 1) * HEAD_DIM]
        s = lax.dot_general(q[:, hd * HEAD_DIM:(hd + 1) * HEAD_DIM], k, _NT,
                            preferred_element_type=F32) * SCALE
        p = jnp.exp(s - jnp.max(s, axis=-1, keepdims=True))
        den = jnp.sum(p, axis=-1, keepdims=True)
        heads.append((jnp.dot(p.astype(BF16), v, preferred_element_type=F32) / den).astype(BF16))
    o = jnp.concatenate(heads, axis=-1)
    return x + jnp.dot(o, wo_ref[...], preferred_element_type=F32)


def _merge_cross_body(x_ref, o0_ref, o1_ref, o2_ref, l0_ref, l1_ref, l2_ref, ym_ref, gd_ref, gm_ref,
                      wud_ref, wum_ref, wo_ref, gc_ref, wq_ref, kv_ref, wom_ref, out_ref):
    x1 = _merged_branches(x_ref[...], (o0_ref, o1_ref, o2_ref), (l0_ref, l1_ref, l2_ref),
                          ym_ref, gd_ref, gm_ref, wud_ref, wum_ref, wo_ref)
    out_ref[...] = _cross_attended(x1, gc_ref, wq_ref, kv_ref, wom_ref)


def _merge_cross(x, dil_outs, dil_lses, y_moba, gates, w_up_dil, w_up_moba, w_out,
                 g_cross, w_q, kv, w_o, *, tm):
    t, d = x.shape
    s = dil_outs[0].shape[2]
    mem_len = kv.shape[1]
    assert t % tm == 0 and s % tm == 0
    per_batch = s // tm
    row = lambda w: pl.BlockSpec((tm, w), lambda i: (i, 0))
    dil = pl.BlockSpec((None, DIL_HEADS_PER_GROUP, tm, HEAD_DIM),
                       lambda i: (i // per_batch, 0, i % per_batch, 0))
    return pl.pallas_call(
        _merge_cross_body,
        grid=(t // tm,),
        in_specs=[row(d)] + [dil] * 6 + [
            row(MOBA_WIDTH),
            pl.BlockSpec((tm, d), lambda i: (i, 0)),
            pl.BlockSpec((tm, d), lambda i: (i, 1)),
            _resident((DIL_OUT, d)), _resident((MOBA_WIDTH, d)), _resident((d, d)),
            _resident((1, d)), _resident((d, MEM_WIDTH)),
            pl.BlockSpec((None, mem_len, 2 * MEM_WIDTH), lambda i: (i // per_batch, 0, 0)),
            _resident((MEM_WIDTH, d)),
        ],
        out_specs=row(d),
        out_shape=jax.ShapeDtypeStruct((t, d), F32),
        compiler_params=_params("parallel"),
        name="merge_out_proj_cross_attn",
    )(x, *dil_outs, *dil_lses, y_moba, gates, gates, w_up_dil, w_up_moba, w_out,
      g_cross.reshape(1, d), w_q, kv, w_o)


ROUTE_COLS = 8
MOE_ROW_TILE = 256


def _route_body(x_ref, g_ref, wr_ref, br_ref, info_ref, counts_ref, run_ref, tri_ref):
    tm = x_ref.shape[0]
    n_route = N_GROUPS + N_EXPERTS
    lane = lax.broadcasted_iota(jnp.int32, (tm, n_route), 1).astype(F32)

    @pl.when(pl.program_id(0) == 0)
    def _():
        run_ref[...] = jnp.zeros_like(run_ref)
        earlier = (lax.broadcasted_iota(jnp.int32, (tm, tm), 0)
                   > lax.broadcasted_iota(jnp.int32, (tm, tm), 1))
        tri_ref[...] = jnp.where(earlier, 1.0, 0.0).astype(BF16)

    t = _rms(x_ref[...], g_ref[...])
    logits = _dot_split3(t, wr_ref[...]) + br_ref[...]
    none = float(n_route)
    glog = jnp.where(lane < N_GROUPS, logits, NEG_INF)
    gmax = jnp.max(glog, axis=-1, keepdims=True)
    gsel = jnp.min(jnp.where(glog == gmax, lane, none), axis=-1, keepdims=True)
    pg = 1.0 / jnp.sum(jnp.exp(glog - gmax), axis=-1, keepdims=True)
    first = N_GROUPS + gsel * EXPERTS_PER_GROUP
    in_group = (lane >= first) & (lane < first + EXPERTS_PER_GROUP)
    elog = jnp.where(in_group, logits, NEG_INF)
    top1 = jnp.max(elog, axis=-1, keepdims=True)
    i1 = jnp.min(jnp.where(elog == top1, lane, none), axis=-1, keepdims=True)
    rest = jnp.where(lane == i1, NEG_INF, elog)
    top2 = jnp.max(rest, axis=-1, keepdims=True)
    i2 = jnp.min(jnp.where(rest == top2, lane, none), axis=-1, keepdims=True)
    e2 = jnp.exp(top2 - top1)
    w1 = pg / (1.0 + e2)
    w2 = pg * e2 / (1.0 + e2)

    hit1 = lane == i1
    hit2 = lane == i2
    assigned = jnp.where(hit1 | hit2, 1.0, 0.0)
    before = jnp.dot(tri_ref[...], assigned.astype(BF16), preferred_element_type=F32) + run_ref[...]
    rank1 = jnp.sum(jnp.where(hit1, before, 0.0), axis=-1, keepdims=True)
    rank2 = jnp.sum(jnp.where(hit2, before, 0.0), axis=-1, keepdims=True)
    run_ref[...] += jnp.sum(assigned, axis=0, keepdims=True)
    counts_ref[...] = run_ref[...]

    col = lax.broadcasted_iota(jnp.int32, (tm, ROUTE_COLS), 1)
    fields = (i1 - N_GROUPS, i2 - N_GROUPS, rank1, rank2, w1, w2)
    info = jnp.zeros((tm, ROUTE_COLS), F32)
    for c, field in enumerate(fields):
        info = jnp.where(col == c, field, info)
    info_ref[...] = info


def _route(x, g, w_route, b_route, *, tm):
    t, d = x.shape
    n_route = N_GROUPS + N_EXPERTS
    assert t % tm == 0
    return pl.pallas_call(
        _route_body,
        grid=(t // tm,),
        in_specs=[
            pl.BlockSpec((tm, d), lambda i: (i, 0)),
            pl.BlockSpec((1, d), lambda i: (0, 0)),
            pl.BlockSpec((d, n_route), lambda i: (0, 0)),
            pl.BlockSpec((1, n_route), lambda i: (0, 0)),
        ],
        out_specs=[pl.BlockSpec((tm, ROUTE_COLS), lambda i: (i, 0)),
                   pl.BlockSpec((1, n_route), lambda i: (0, 0))],
        out_shape=[jax.ShapeDtypeStruct((t, ROUTE_COLS), F32),
                   jax.ShapeDtypeStruct((1, n_route), F32)],
        scratch_shapes=[pltpu.VMEM((1, n_route), F32), pltpu.VMEM((tm, tm), BF16)],
        compiler_params=_params("arbitrary"),
        name="moe_route",
    )(x, g.reshape(1, d), w_route, b_route.reshape(1, n_route))


def _row_copies_wait(src_rows, dst_rows, sem):
    pltpu.make_async_copy(src_rows, dst_rows, sem).wait()


def _slot_block(tm):
    return pl.BlockSpec((MOE_TOPK, tm), lambda i, *_: (0, i), memory_space=pltpu.SMEM)


def _dispatch_body(fill_ref, pos_ref, x_ref, g_ref, sorted_ref, t_ref, zero_ref, sems, fill_sem):
    step = pl.program_id(0)
    tm = x_ref.shape[0]
    tr = zero_ref.shape[0]
    slot = lax.rem(step, 2)

    @pl.when(pl.program_id(0) == 0)
    def _():
        zero_ref[...] = jnp.zeros_like(zero_ref)

        def fill_copy(tile):
            return pltpu.make_async_copy(zero_ref, sorted_ref.at[pl.ds(tile * tr, tr)], fill_sem)

        def start(idx, carry):
            pl.when(fill_ref[idx] >= 0)(lambda: fill_copy(fill_ref[idx]).start())
            return carry

        def finish(idx, carry):
            pl.when(fill_ref[idx] >= 0)(lambda: fill_copy(fill_ref[idx]).wait())
            return carry

        lax.fori_loop(0, fill_ref.shape[0], start, 0)
        lax.fori_loop(0, fill_ref.shape[0], finish, 0)

    t_ref[slot] = _rms(x_ref[...], g_ref[...])

    def issue(r, carry):
        for k in range(MOE_TOPK):
            dst = pos_ref[k, r]
            pltpu.make_async_copy(t_ref.at[slot, pl.ds(r, 1)], sorted_ref.at[pl.ds(dst, 1)],
                                  sems.at[slot]).start()
        return carry

    def copies_done(buf):
        for _ in range(MOE_TOPK):
            _row_copies_wait(t_ref.at[buf], sorted_ref.at[pl.ds(0, tm)], sems.at[buf])

    lax.fori_loop(0, tm, issue, 0, unroll=8)
    pl.when(step > 0)(lambda: copies_done(1 - slot))
    pl.when(step == pl.num_programs(0) - 1)(lambda: copies_done(slot))


def _dispatch(x, g, pos, fill_tiles, n_rows, *, tm):
    t, d = x.shape
    assert t % tm == 0
    return pl.pallas_call(
        _dispatch_body,
        grid_spec=pltpu.PrefetchScalarGridSpec(
            num_scalar_prefetch=1,
            grid=(t // tm,),
            in_specs=[_slot_block(tm),
                      pl.BlockSpec((tm, d), lambda i, fill: (i, 0)),
                      pl.BlockSpec((1, d), lambda i, fill: (0, 0))],
            out_specs=pl.BlockSpec(memory_space=pl.ANY),
            scratch_shapes=[pltpu.VMEM((2, tm, d), F32), pltpu.VMEM((MOE_ROW_TILE, d), F32),
                            pltpu.SemaphoreType.DMA((2,)), pltpu.SemaphoreType.DMA(())],
        ),
        out_shape=jax.ShapeDtypeStruct((n_rows, d), F32),
        compiler_params=pltpu.CompilerParams(dimension_semantics=("arbitrary",),
                                             vmem_limit_bytes=VMEM_LIMIT_BYTES,
                                             disable_bounds_checks=True),
        name="moe_dispatch",
    )(fill_tiles, pos, x, g.reshape(1, d))


def _experts_body(plan_ref, n_tiles_ref, x_ref, wg_hbm, wu_hbm, wd_hbm, y_ref,
                  wg_buf, wu_buf, wd_buf, sems):
    i = pl.program_id(0)
    in_use = i < n_tiles_ref[0]

    def weights(expert, buf):
        return [pltpu.make_async_copy(hbm.at[expert], vmem.at[buf], sems.at[buf])
                for hbm, vmem in ((wg_hbm, wg_buf), (wu_hbm, wu_buf), (wd_hbm, wd_buf))]

    for buf in range(2):
        @pl.when(in_use & (plan_ref[2, i] == buf))
        def _(buf=buf):
            @pl.when(plan_ref[1, i] == 1)
            def _():
                @pl.when(i == 0)
                def _():
                    for copy in weights(plan_ref[0, 0], buf):
                        copy.start()
                for copy in weights(plan_ref[0, i], buf):
                    copy.wait()

                @pl.when(plan_ref[3, i] >= 0)
                def _():
                    for copy in weights(plan_ref[3, i], 1 - buf):
                        copy.start()

            t = x_ref[...].astype(BF16)
            gate = jnp.dot(t, wg_buf[buf].astype(BF16), preferred_element_type=F32)
            up = jnp.dot(t, wu_buf[buf].astype(BF16), preferred_element_type=F32)
            a = jax.nn.silu(gate) * up
            y_ref[...] = jnp.dot(a.astype(BF16), wd_buf[buf].astype(BF16),
                                 preferred_element_type=F32)

    @pl.when(jnp.logical_not(in_use))
    def _():
        y_ref[...] = jnp.zeros_like(y_ref)


def _experts(sorted_rows, plan, n_tiles, w_gate, w_up, w_down):
    p, d = sorted_rows.shape
    ff = w_gate.shape[-1]
    tr = MOE_ROW_TILE
    assert p % tr == 0
    used = lambda i, nt: jnp.minimum(i, nt[0] - 1)
    return pl.pallas_call(
        _experts_body,
        grid_spec=pltpu.PrefetchScalarGridSpec(
            num_scalar_prefetch=2,
            grid=(p // tr,),
            in_specs=[pl.BlockSpec((tr, d), lambda i, plan, nt: (used(i, nt), 0)),
                      pl.BlockSpec(memory_space=pl.ANY),
                      pl.BlockSpec(memory_space=pl.ANY),
                      pl.BlockSpec(memory_space=pl.ANY)],
            out_specs=pl.BlockSpec((tr, d), lambda i, plan, nt: (i, 0)),
            scratch_shapes=[pltpu.VMEM((2, d, ff), F32), pltpu.VMEM((2, d, ff), F32),
                            pltpu.VMEM((2, ff, d), F32), pltpu.SemaphoreType.DMA((2,))],
        ),
        out_shape=jax.ShapeDtypeStruct((p, d), F32),
        compiler_params=_params("arbitrary"),
        name="moe_experts",
    )(plan, n_tiles, sorted_rows, w_gate, w_up, w_down)


def _combine_body(pos_ref, pos_next_ref, x_ref, info_ref, gf_ref, y_sorted_ref, out_ref,
                  rows_ref, sems):
    step = pl.program_id(0)
    tm = x_ref.shape[0]
    slot = lax.rem(step, 2)

    def fetch(table_ref, buf):
        def issue(r, carry):
            for k in range(MOE_TOPK):
                src = table_ref[k, r]
                pltpu.make_async_copy(y_sorted_ref.at[pl.ds(src, 1)],
                                      rows_ref.at[buf, k, pl.ds(r, 1)], sems.at[buf]).start()
            return carry
        lax.fori_loop(0, tm, issue, 0, unroll=8)

    pl.when(step == 0)(lambda: fetch(pos_ref, 0))
    pl.when(step + 1 < pl.num_programs(0))(lambda: fetch(pos_next_ref, 1 - slot))
    for k in range(MOE_TOPK):
        _row_copies_wait(y_sorted_ref.at[pl.ds(0, tm)], rows_ref.at[slot, k], sems.at[slot])

    info = info_ref[...]
    y = info[:, 4:5] * rows_ref[slot, 0] + info[:, 5:6] * rows_ref[slot, 1]
    out_ref[...] = _rms(x_ref[...] + y, gf_ref[...])


def _combine(x, info, pos, y_sorted, g_final, *, tm):
    t, d = x.shape
    assert t % tm == 0
    return pl.pallas_call(
        _combine_body,
        grid=(t // tm,),
        in_specs=[_slot_block(tm),
                  pl.BlockSpec((MOE_TOPK, tm), lambda i: (0, jnp.minimum(i + 1, t // tm - 1)),
                               memory_space=pltpu.SMEM),
                  pl.BlockSpec((tm, d), lambda i: (i, 0)),
                  pl.BlockSpec((tm, ROUTE_COLS), lambda i: (i, 0)),
                  pl.BlockSpec((1, d), lambda i: (0, 0)),
                  pl.BlockSpec(memory_space=pl.ANY)],
        out_specs=pl.BlockSpec((tm, d), lambda i: (i, 0)),
        scratch_shapes=[pltpu.VMEM((2, MOE_TOPK, tm, d), F32), pltpu.SemaphoreType.DMA((2,))],
        out_shape=jax.ShapeDtypeStruct((t, d), F32),
        compiler_params=pltpu.CompilerParams(dimension_semantics=("arbitrary",),
                                             vmem_limit_bytes=VMEM_LIMIT_BYTES,
                                             disable_bounds_checks=True),
        name="moe_combine_final_norm",
    )(pos, pos, x, info, g_final.reshape(1, d), y_sorted)


def _moe(x, g, w_route, b_route, w_gate, w_up, w_down, g_final):
    t, d = x.shape
    tr = MOE_ROW_TILE
    assert (MOE_TOPK * t) % tr == 0
    info, counts = _route(x, g, w_route, b_route, tm=min(512, t))

    expert = info[:, 0:MOE_TOPK].astype(jnp.int32)
    rank = info[:, MOE_TOPK:2 * MOE_TOPK].astype(jnp.int32)
    count = counts[0, N_GROUPS:].astype(jnp.int32)
    seg_tiles = (count + (tr - 1)) // tr
    seg_end = jnp.cumsum(seg_tiles)
    seg_start_row = (seg_end - seg_tiles) * tr
    pos = (seg_start_row[expert] + rank).T
    max_tiles = (MOE_TOPK * t) // tr + N_EXPERTS
    tile_id = jnp.arange(max_tiles, dtype=jnp.int32)
    tile_expert = jnp.minimum(
        jnp.sum((seg_end[None, :] <= tile_id[:, None]).astype(jnp.int32), axis=1), N_EXPERTS - 1)
    n_tiles = seg_end[-1:].astype(jnp.int32)

    tail = n_tiles + jnp.arange(N_EXPERTS, dtype=jnp.int32)
    fill_tiles = jnp.concatenate([jnp.where(seg_tiles > 0, seg_end - 1, -1),
                                  jnp.where(tail < max_tiles, tail, -1)]).astype(jnp.int32)

    in_use = tile_id < n_tiles[0]
    run_start = (in_use & (tile_expert != jnp.append(-1, tile_expert[:-1]))).astype(jnp.int32)
    run_buf = jnp.maximum(jnp.cumsum(run_start) - 1, 0) % 2
    next_tile = seg_end[tile_expert]
    next_expert = jnp.where(next_tile < n_tiles[0],
                            tile_expert[jnp.minimum(next_tile, max_tiles - 1)], -1)
    plan = jnp.stack([tile_expert, run_start, run_buf, next_expert]).astype(jnp.int32)

    sorted_rows = _dispatch(x, g, pos, fill_tiles, max_tiles * tr, tm=min(512, t))
    y_sorted = _experts(sorted_rows, plan, n_tiles, w_gate, w_up, w_down)
    return _combine(x, info, pos, y_sorted, g_final, tm=min(512, t))


def _layer(x, mem, attn_norm, w_in, w_up_dil, w_up_moba, w_branch_gate, w_out, cross_norm,
           mem_norm, w_q_mem, w_kv_mem, w_o_mem):
    b, s, d = x.shape
    t = b * s
    xt = x.reshape(t, d)
    h = _norm(xt, attn_norm, tm=min(512, t), name="attn_norm")
    proj, gates = _in_proj(h, w_in, w_branch_gate, tm=min(2048, t))
    proj = proj.reshape(b, s, IN_WIDTH)
    dil_slopes = _alibi_slopes(DIL_HEADS)
    dil = [_dilated_group(proj, dil_slopes, g, dilation)
           for g, (_, dilation) in enumerate(DIL_CONFIGS)]
    y_moba = _moba(proj, _alibi_slopes(MOBA_HEADS))
    mem_len = mem.shape[1]
    kv = _norm_matmul(mem.reshape(b * mem_len, d), mem_norm, w_kv_mem.astype(BF16),
                      tm=b * mem_len, tn=512, name="norm_mem_kv").reshape(b, mem_len, 2 * MEM_WIDTH)
    return _merge_cross(xt, [o for o, _ in dil], [l for _, l in dil], y_moba, gates,
                        w_up_dil.astype(BF16), w_up_moba.astype(BF16), w_out.astype(BF16),
                        cross_norm, w_q_mem.astype(BF16), kv, w_o_mem.astype(BF16), tm=min(256, s))


def kernel(x, mem, attn_norm, w_in, w_up_dil, w_up_moba, w_branch_gate, w_out, cross_norm, mem_norm,
           w_q_mem, w_kv_mem, w_o_mem, ffn_norm, w_router_group, b_router_group, w_router_expert,
           b_router_expert, w_expert_gate, w_expert_up, w_expert_down, final_norm):
    b, s, d = x.shape
    depth = attn_norm.shape[0]
    assert depth == 1, "the final norm is fused into the last layer's MoE call"
    l = 0
    x2 = _layer(x, mem, attn_norm[l], w_in[l], w_up_dil[l], w_up_moba[l], w_branch_gate[l], w_out[l],
                cross_norm[l], mem_norm[l], w_q_mem[l], w_kv_mem[l], w_o_mem[l])
    w_route = jnp.concatenate([w_router_group[l], w_router_expert[l]], axis=1)
    b_route = jnp.concatenate([b_router_group[l], b_router_expert[l]], axis=0)
    out = _moe(x2, ffn_norm[l], w_route, b_route, w_expert_gate[l], w_expert_up[l],
               w_expert_down[l], final_norm)
    return out.reshape(b, s, d)
```

```python
import functools

import numpy as np
import jax
import jax.numpy as jnp
from jax import lax
from jax.experimental import pallas as pl
from jax.experimental.pallas import tpu as pltpu

F32 = jnp.float32
BF16 = jnp.bfloat16

HEAD_DIM = 128
DIL_CONFIGS = ((128, 1), (512, 4), (2048, 16))
DIL_HEADS_PER_GROUP = 4
DIL_HEADS = DIL_HEADS_PER_GROUP * len(DIL_CONFIGS)
DIL_WIDTH = DIL_HEADS * HEAD_DIM
DIL_OUT = DIL_HEADS_PER_GROUP * HEAD_DIM
DIL_STEPS = 128
MOBA_HEADS = 8
MOBA_WIDTH = MOBA_HEADS * HEAD_DIM
MOBA_BLOCK = 256
MOBA_TOPK = 3
IN_WIDTH = 3 * (DIL_WIDTH + MOBA_WIDTH)
MEM_HEADS = 4
MEM_WIDTH = MEM_HEADS * HEAD_DIM
N_GROUPS = 4
EXPERTS_PER_GROUP = 8
N_EXPERTS = N_GROUPS * EXPERTS_PER_GROUP
MOE_TOPK = 2
RMS_EPS = 1e-6
SCALE = HEAD_DIM ** -0.5
NEG_INF = float("-inf")
LOG2E = 1.4426950408889634
MOBA_GROUP = 4
MOBA_HEADS_PER_STEP = 4
MOBA_Q_TILES = (9, 11)
IN_PROJ_ROW_CHUNK = 512

VMEM_LIMIT_BYTES = 56 * 1024 * 1024
BF16_ROWS = 16
MXU_COLUMNS = 256
_NT = (((1,), (1,)), ((), ()))


def _alibi_slopes(n):
    return jnp.asarray(2.0 ** (-8.0 * np.arange(1, n + 1) / n), dtype=F32)


def _rms(x, g):
    return x * lax.rsqrt(jnp.mean(x * x, axis=-1, keepdims=True) + RMS_EPS) * g


def _dot_split3(a, b):
    a_hi = a.astype(BF16)
    b_hi = b.astype(BF16)
    a_lo = (a - a_hi.astype(F32)).astype(BF16)
    b_lo = (b - b_hi.astype(F32)).astype(BF16)
    dot = functools.partial(jnp.dot, preferred_element_type=F32)
    n = b.shape[1]
    if 2 * n <= MXU_COLUMNS:
        both = dot(a_hi, jnp.concatenate([b_hi, b_lo], axis=1))
        return both[:, :n] + (dot(a_lo, b_hi) + both[:, n:])
    return dot(a_hi, b_hi) + (dot(a_lo, b_hi) + dot(a_hi, b_lo))


def _params(*sem):
    return pltpu.CompilerParams(dimension_semantics=sem, vmem_limit_bytes=VMEM_LIMIT_BYTES)


def _resident(shape):
    nd = len(shape)
    return pl.BlockSpec(shape, lambda *_: (0,) * nd, pipeline_mode=pl.Buffered(1))


def _norm_matmul_body(x_ref, g_ref, w_ref, o_ref, h_ref):
    @pl.when(pl.program_id(1) == 0)
    def _():
        h_ref[...] = _rms(x_ref[...], g_ref[...]).astype(BF16)

    o_ref[...] = jnp.dot(h_ref[...], w_ref[...], preferred_element_type=F32).astype(o_ref.dtype)


def _norm_matmul(x, g, w, *, tm, tn, name):
    m, d = x.shape
    n = w.shape[1]
    assert m % tm == 0 and n % tn == 0
    return pl.pallas_call(
        _norm_matmul_body,
        grid=(m // tm, n // tn),
        in_specs=[
            pl.BlockSpec((tm, d), lambda i, j: (i, 0)),
            pl.BlockSpec((1, d), lambda i, j: (0, 0)),
            pl.BlockSpec((d, tn), lambda i, j: (0, j)),
        ],
        out_specs=pl.BlockSpec((tm, tn), lambda i, j: (i, j)),
        out_shape=jax.ShapeDtypeStruct((m, n), BF16),
        scratch_shapes=[pltpu.VMEM((tm, d), BF16)],
        compiler_params=_params("parallel", "arbitrary"),
        name=name,
    )(x, g.reshape(1, d), w)


def _norm_body(x_ref, g_ref, o_ref):
    o_ref[...] = _rms(x_ref[...], g_ref[...]).astype(o_ref.dtype)


def _norm(x, g, *, tm, name):
    m, d = x.shape
    assert m % tm == 0
    return pl.pallas_call(
        _norm_body,
        grid=(m // tm,),
        in_specs=[pl.BlockSpec((tm, d), lambda i: (i, 0)), pl.BlockSpec((1, d), lambda i: (0, 0))],
        out_specs=pl.BlockSpec((tm, d), lambda i: (i, 0)),
        out_shape=jax.ShapeDtypeStruct((m, d), BF16),
        compiler_params=_params("parallel"),
        name=name,
    )(x, g.reshape(1, d))


def _in_proj_body(h_ref, win_ref, wbg_ref, proj_ref, gates_ref, perm_ref, *, n_proj_tiles):
    j = pl.program_id(1)
    tm = h_ref.shape[0]
    n = DIL_STEPS
    groups = len(DIL_CONFIGS)

    def store_regrouped(acc, dilation):
        tile = n * dilation
        for c in range(acc.shape[1] // HEAD_DIM):
            cols = slice(c * HEAD_DIM, (c + 1) * HEAD_DIM)
            perm_ref[c] = acc[:, cols]
            for t0 in range(0, tm, tile):
                for r in range(dilation):
                    rows = perm_ref[c, pl.ds(t0 + r, n, stride=dilation), :]
                    proj_ref[t0 + r * n:t0 + (r + 1) * n, cols] = rows.astype(BF16)

    @pl.when(j < n_proj_tiles)
    def _():
        acc = jnp.dot(h_ref[...], win_ref[...].astype(BF16), preferred_element_type=F32)
        group = jnp.where(j < 3 * groups, lax.rem(j, groups), 0)
        moba_q = (j >= MOBA_Q_TILES[0]) & (j < MOBA_Q_TILES[1])
        scale = jnp.where(moba_q, SCALE * LOG2E, 1.0)

        def store_natural():
            proj_ref[...] = (acc * scale).astype(BF16)

        for g, (_, dilation) in enumerate(DIL_CONFIGS):
            store = store_natural if dilation == 1 else functools.partial(store_regrouped, acc, dilation)
            pl.when(group == g)(store)

    @pl.when(j >= n_proj_tiles)
    def _():
        w = wbg_ref[...].astype(BF16)
        for r0 in range(0, tm, IN_PROJ_ROW_CHUNK):
            rows = slice(r0, r0 + IN_PROJ_ROW_CHUNK)
            acc = jnp.dot(h_ref[rows, :], w, preferred_element_type=F32)
            gates_ref[rows, :] = jax.nn.sigmoid(acc).astype(BF16)


def _in_proj(h, w_in, w_bg, *, tm):
    t, d = h.shape
    tn = DIL_OUT
    assert MOBA_Q_TILES == (3 * DIL_WIDTH // tn, (3 * DIL_WIDTH + MOBA_WIDTH) // tn)
    assert t % tm == 0 and tm % (DIL_STEPS * max(dl for _, dl in DIL_CONFIGS)) == 0
    assert DIL_WIDTH == len(DIL_CONFIGS) * tn and w_in.shape[1] % tn == 0 and w_bg.shape[1] % tn == 0
    n_proj = w_in.shape[1] // tn
    n_gate = w_bg.shape[1] // tn
    return pl.pallas_call(
        functools.partial(_in_proj_body, n_proj_tiles=n_proj),
        grid=(t // tm, n_proj + n_gate),
        in_specs=[
            pl.BlockSpec((tm, d), lambda i, j: (i, 0)),
            pl.BlockSpec((d, tn), lambda i, j: (0, jnp.minimum(j, n_proj - 1))),
            pl.BlockSpec((d, tn), lambda i, j: (0, jnp.maximum(j - n_proj, 0))),
        ],
        out_specs=[
            pl.BlockSpec((tm, tn), lambda i, j: (i, jnp.minimum(j, n_proj - 1))),
            pl.BlockSpec((tm, tn), lambda i, j: (i, jnp.maximum(j - n_proj, 0))),
        ],
        out_shape=[jax.ShapeDtypeStruct((t, w_in.shape[1]), BF16),
                   jax.ShapeDtypeStruct((t, w_bg.shape[1]), BF16)],
        scratch_shapes=[pltpu.VMEM((tn // HEAD_DIM, tm, HEAD_DIM), F32)],
        compiler_params=_params("parallel", "arbitrary"),
        name="in_proj_gates",
    )(h, w_in, w_bg)


DIL_SUBBLOCKS = 4


def _dilated_body(slope_ref, q_ref, kp_ref, kc_ref, vp_ref, vc_ref, o_ref, lse_ref, *scratch,
                  dilation, group):
    n = DIL_STEPS
    first_tile = pl.program_id(1) == 0
    chunk = pl.program_id(2)
    chunks = max(dilation // DIL_SUBBLOCKS, 1)
    qi = lax.broadcasted_iota(jnp.int32, (n, n), 0)
    kj = lax.broadcasted_iota(jnp.int32, (n, n), 1)
    steps_cur = qi - kj
    valid_cur = steps_cur >= 0
    dist_cur = (steps_cur * dilation).astype(F32)
    dist_prev = ((steps_cur + n) * dilation).astype(F32)
    limit_across_tiles = jnp.where(first_tile, -n, 0)
    heads = range(DIL_HEADS_PER_GROUP)
    cols_of = [slice(h * HEAD_DIM, (h + 1) * HEAD_DIM) for h in heads]
    rows_of = [slice(s * n, (s + 1) * n) for s in range(DIL_SUBBLOCKS)]

    def prev_rows(s):
        if dilation == 1:
            return (slice(0, n), False, limit_across_tiles) if s == 0 else (rows_of[s - 1], True, 0)
        return rows_of[s], False, limit_across_tiles

    def scores(s):
        rows, from_cur, _ = prev_rows(s)
        out = []
        for h in heads:
            q = q_ref[rows_of[s], cols_of[h]]
            k_prev = (kc_ref if from_cur else kp_ref)[rows, cols_of[h]]
            out.append((lax.dot_general(q, kc_ref[rows_of[s], cols_of[h]], _NT,
                                        preferred_element_type=F32),
                        lax.dot_general(q, k_prev, _NT, preferred_element_type=F32)))
        return out

    def softmax(s, raw):
        valid_prev = steps_cur <= prev_rows(s)[2]
        out = []
        for h in heads:
            slope = slope_ref[group * DIL_HEADS_PER_GROUP + h]
            s_cur = jnp.where(valid_cur, raw[h][0] * SCALE - slope * dist_cur, NEG_INF)
            s_prev = jnp.where(valid_prev, raw[h][1] * SCALE - slope * dist_prev, NEG_INF)
            m = jnp.max(jnp.maximum(s_cur, s_prev), axis=-1, keepdims=True)
            p_cur = jnp.exp(s_cur - m)
            p_prev = jnp.exp(s_prev - m)
            den = jnp.sum(p_cur + p_prev, axis=-1, keepdims=True)
            out.append((p_cur.astype(BF16), p_prev.astype(BF16), m, den))
        return out

    def values(s, probs):
        rows, from_cur, _ = prev_rows(s)
        for h in heads:
            p_cur, p_prev, m, den = probs[h]
            v_prev = (vc_ref if from_cur else vp_ref)[rows, cols_of[h]]
            o = (jnp.dot(p_cur, vc_ref[rows_of[s], cols_of[h]], preferred_element_type=F32)
                 + jnp.dot(p_prev, v_prev, preferred_element_type=F32)) / den
            lse = jnp.broadcast_to(m + jnp.log(den), (n, HEAD_DIM))
            if dilation == 1:
                dst_o, dst_lse, rows_out = o_ref, lse_ref, rows_of[s]
            elif chunks == 1:
                dst_o, dst_lse, rows_out = o_ref, lse_ref, pl.ds(s, n, stride=dilation)
            else:
                dst_o, dst_lse = scratch
                rows_out = pl.ds(pl.multiple_of((chunk * DIL_SUBBLOCKS + s) * n, n), n)
            dst_o[h, rows_out, :] = o
            dst_lse[h, rows_out, :] = lse

    raw, probs = {0: scores(0)}, {}
    for s in range(DIL_SUBBLOCKS + 1):
        if s + 1 < DIL_SUBBLOCKS:
            raw[s + 1] = scores(s + 1)
        if s < DIL_SUBBLOCKS:
            probs[s] = softmax(s, raw.pop(s))
        if s >= 1:
            values(s - 1, probs.pop(s - 1))

    if chunks > 1:
        o_tile, lse_tile = scratch

        @pl.when(chunk == chunks - 1)
        def _():
            for h in heads:
                for r in range(dilation):
                    natural = pl.ds(r, n, stride=dilation)
                    o_ref[h, natural, :] = o_tile[h, r * n:(r + 1) * n, :]
                    lse_ref[h, natural, :] = lse_tile[h, r * n:(r + 1) * n, :]


def _dilated_group(proj, slopes, group, dilation):
    b, s, _ = proj.shape
    n = DIL_STEPS
    step_rows = DIL_SUBBLOCKS * n
    assert dilation == 1 or dilation % DIL_SUBBLOCKS == 0
    chunks = max(dilation // DIL_SUBBLOCKS, 1)
    tile = step_rows * chunks
    assert s % tile == 0
    groups = len(DIL_CONFIGS)
    prev_rows = n if dilation == 1 else step_rows
    blocks_per_step = step_rows // prev_rows

    def cur(section):
        return pl.BlockSpec((None, step_rows, DIL_OUT),
                            lambda bi, i, c: (bi, i * chunks + c, section * groups + group))

    def prev(section):
        def index(bi, i, c):
            if dilation == 1:
                return (bi, jnp.maximum(i * blocks_per_step - 1, 0), section * groups + group)
            return (bi, jnp.maximum(i - 1, 0) * chunks + c, section * groups + group)
        return pl.BlockSpec((None, prev_rows, DIL_OUT), index)

    out_block = (DIL_HEADS_PER_GROUP, tile, HEAD_DIM)
    out_spec = pl.BlockSpec((None,) + out_block, lambda bi, i, c: (bi, 0, i, 0))
    out_sds = jax.ShapeDtypeStruct((b, DIL_HEADS_PER_GROUP, s, HEAD_DIM), F32)
    scratch = [pltpu.VMEM(out_block, F32)] * 2 if chunks > 1 else []
    o, lse = pl.pallas_call(
        functools.partial(_dilated_body, dilation=dilation, group=group),
        grid=(b, s // tile, chunks),
        in_specs=[pl.BlockSpec(memory_space=pltpu.SMEM),
                  cur(0), prev(1), cur(1), prev(2), cur(2)],
        out_specs=[out_spec, out_spec],
        out_shape=[out_sds, out_sds],
        scratch_shapes=scratch,
        compiler_params=_params("parallel", "arbitrary", "arbitrary"),
        name=f"dilated_attn_g{group}",
    )(slopes, proj, proj, proj, proj, proj)
    return o, lse


def _moba_body(slope_ref, q_ref, k_ref, v_ref, o_ref, kmean_ref, vt_ref, bias_ref, sel_ref,
               m_ref, acc_ref, even_ref, odd_ref, *, nblk, group):
    blk = MOBA_BLOCK
    hd = HEAD_DIM
    heads = q_ref.shape[1] // hd
    own = pl.program_id(2)
    key_off = lax.broadcasted_iota(jnp.int32, (blk, blk), 0)
    qry_off = lax.broadcasted_iota(jnp.int32, (blk, blk), 1)
    slope2 = [slope_ref[pl.program_id(1) * heads + h] * LOG2E for h in range(heads)]

    @pl.when(own == 0)
    def _():
        def fill(jb, carry):
            start = pl.multiple_of(jb * blk, blk)
            rows = k_ref[pl.ds(start, blk), :].astype(F32)
            kmean_ref[pl.ds(jb, 1), :] = jnp.mean(rows, axis=0, keepdims=True)
            vrows = v_ref[pl.ds(start, blk), :].astype(F32)
            for h in range(heads):
                vt_ref[h, :hd, pl.ds(start, blk)] = vrows[:, h * hd:(h + 1) * hd].T.astype(BF16)
            return carry
        lax.fori_loop(0, nblk, fill, 0)
        for h in range(heads):
            vt_ref[h, hd:, :] = jnp.ones((vt_ref.shape[1] - hd, vt_ref.shape[2]), BF16)
        for h in range(heads):
            bias_ref[h] = -slope2[h] * (qry_off - key_off).astype(F32)

    q = [q_ref[:, h * hd:(h + 1) * hd] for h in range(heads)]

    blk_id = lax.broadcasted_iota(jnp.int32, (nblk, blk), 0).astype(F32)
    kmean = kmean_ref[...]
    km_hi = kmean.astype(BF16)
    rest = kmean - km_hi.astype(F32)
    km_mid = rest.astype(BF16)
    km_lo = (rest - km_mid.astype(F32)).astype(BF16)
    km3 = jnp.concatenate([km_hi, km_mid, km_lo], axis=0)
    gates = []
    for h in range(heads):
        parts = lax.dot_general(km3[:, h * hd:(h + 1) * hd], q[h], _NT, preferred_element_type=F32)
        gate = parts[:nblk] + (parts[nblk:2 * nblk] + parts[2 * nblk:])
        gates.append(jnp.where(blk_id < own.astype(F32), gate, NEG_INF))

    def scores(h, start, rows):
        k = k_ref[pl.ds(start, rows), h * hd:(h + 1) * hd]
        return lax.dot_general(k, q[h], _NT, preferred_element_type=F32)

    own_start = pl.multiple_of(own * blk, blk)
    own_x = [scores(h, own_start, blk) for h in range(heads)]
    for h in range(heads):
        even_ref[h] = scores(h, 0, group * blk)

    sels = [jnp.zeros((nblk, blk), F32) for _ in range(heads)]
    for _ in range(MOBA_TOPK):
        for h in range(heads):
            best = jnp.max(gates[h], axis=0, keepdims=True)
            is_best = (gates[h] == best) & (gates[h] > NEG_INF)
            pick = jnp.min(jnp.where(is_best, blk_id, float(nblk)), axis=0, keepdims=True)
            picked = blk_id == pick
            sels[h] = jnp.where(picked, 1.0, sels[h])
            gates[h] = jnp.where(picked, NEG_INF, gates[h])
    for h in range(heads):
        sel_ref[h] = sels[h]

    own_p = []
    for h in range(heads):
        x = jnp.where(qry_off >= key_off, own_x[h] + bias_ref[h], NEG_INF)
        m0 = jnp.max(x, axis=0, keepdims=True)
        own_p.append((m0, jnp.exp2(x - m0).astype(BF16)))
    for h in range(heads):
        m0, p = own_p[h]
        m_ref[h] = m0
        acc_ref[h] = jnp.dot(vt_ref[h, :, pl.ds(own_start, blk)], p, preferred_element_type=F32)

    def softmax_update(h, i, x, m):
        xs, chosen, shift = [], [], []
        m_new = m
        for g in range(group):
            j = i * group + g
            xs.append(x[g * blk:(g + 1) * blk] + bias_ref[h])
            chosen.append(sel_ref[h, pl.ds(j, 1), :] > 0.5)
            shift.append(-slope2[h] * ((own - j) * blk).astype(F32))
            top = jnp.max(xs[g], axis=0, keepdims=True) + shift[g]
            m_new = jnp.maximum(m_new, jnp.where(chosen[g], top, NEG_INF))
        ps = []
        for g in range(group):
            ref_g = jnp.where(chosen[g], m_new - shift[g], jnp.inf)
            ps.append(jnp.exp2(xs[g] - ref_g).astype(BF16))
        return m_new, jnp.exp2(m - m_new), jnp.concatenate(ps, axis=0)

    def group_start(i):
        return pl.multiple_of(i * (group * blk), group * blk)

    def key_group(i, cur_ref, next_ref):
        sm = {}
        for h in range(heads + 1):
            if next_ref is not None and h < heads:
                next_ref[h] = scores(h, group_start(i + 1), group * blk)
            if h < heads:
                sm[h] = softmax_update(h, i, cur_ref[h], m_ref[h])
            if h >= 1:
                m_new, alpha, p = sm[h - 1]
                m_ref[h - 1] = m_new
                acc_ref[h - 1] = alpha * acc_ref[h - 1] + jnp.dot(
                    vt_ref[h - 1, :, pl.ds(group_start(i), group * blk)], p,
                    preferred_element_type=F32)

    def two_groups(pair, carry):
        key_group(2 * pair, even_ref, odd_ref)
        key_group(2 * pair + 1, odd_ref, even_ref)
        return carry

    n_groups = lax.div(own + (group - 1), group)
    ahead = n_groups - 1
    lax.fori_loop(0, lax.div(ahead, 2), two_groups, 0)
    last_is_odd = lax.rem(ahead, 2) == 1

    @pl.when((n_groups > 0) & last_is_odd)
    def _():
        key_group(ahead - 1, even_ref, odd_ref)
        key_group(ahead, odd_ref, None)

    @pl.when((n_groups > 0) & jnp.logical_not(last_is_odd))
    def _():
        key_group(ahead, even_ref, None)

    for h in range(heads):
        acc = acc_ref[h]
        o_ref[:, h * hd:(h + 1) * hd] = (acc[:hd] / acc[hd:hd + 1]).T.astype(o_ref.dtype)


def _moba(proj, slopes):
    b, s, _ = proj.shape
    assert s % (MOBA_BLOCK * MOBA_GROUP) == 0
    nblk = s // MOBA_BLOCK
    hp = MOBA_HEADS_PER_STEP
    width = hp * HEAD_DIM
    q0 = 3 * DIL_WIDTH // width
    k0 = q0 + MOBA_WIDTH // width
    v0 = k0 + MOBA_WIDTH // width
    y = pl.pallas_call(
        functools.partial(_moba_body, nblk=nblk, group=MOBA_GROUP),
        grid=(b, MOBA_HEADS // hp, nblk),
        in_specs=[
            pl.BlockSpec(memory_space=pltpu.SMEM),
            pl.BlockSpec((None, MOBA_BLOCK, width), lambda bi, h, i: (bi, i, q0 + h)),
            pl.BlockSpec((None, s, width), lambda bi, h, i: (bi, 0, k0 + h),
                         pipeline_mode=pl.Buffered(1)),
            pl.BlockSpec((None, s, width), lambda bi, h, i: (bi, 0, v0 + h),
                         pipeline_mode=pl.Buffered(1)),
        ],
        out_specs=pl.BlockSpec((None, MOBA_BLOCK, width), lambda bi, h, i: (bi, i, h)),
        out_shape=jax.ShapeDtypeStruct((b, s, MOBA_WIDTH), BF16),
        scratch_shapes=[pltpu.VMEM((nblk, width), F32),
                        pltpu.VMEM((hp, HEAD_DIM + BF16_ROWS, s), BF16),
                        pltpu.VMEM((hp, MOBA_BLOCK, MOBA_BLOCK), F32),
                        pltpu.VMEM((hp, nblk, MOBA_BLOCK), F32),
                        pltpu.VMEM((hp, 1, MOBA_BLOCK), F32),
                        pltpu.VMEM((hp, HEAD_DIM + BF16_ROWS, MOBA_BLOCK), F32),
                        pltpu.VMEM((hp, MOBA_GROUP * MOBA_BLOCK, MOBA_BLOCK), F32),
                        pltpu.VMEM((hp, MOBA_GROUP * MOBA_BLOCK, MOBA_BLOCK), F32)],
        compiler_params=_params("parallel", "parallel", "arbitrary"),
        name="moba_attn",
    )(slopes, proj, proj, proj)
    return y.reshape(b * s, MOBA_WIDTH)


def _merged_branches(x, o_refs, l_refs, ym_ref, gd_ref, gm_ref, wud_ref, wum_ref, wo_ref):
    heads = []
    for h in range(DIL_HEADS_PER_GROUP):
        l0, l1, l2 = (l_ref[h] for l_ref in l_refs)
        m = jnp.maximum(jnp.maximum(l0, l1), l2)
        e0, e1, e2 = jnp.exp(l0 - m), jnp.exp(l1 - m), jnp.exp(l2 - m)
        den = e0 + e1 + e2
        mixed = ((e0 / den) * o_refs[0][h] + (e1 / den) * o_refs[1][h] + (e2 / den) * o_refs[2][h])
        heads.append(mixed.astype(BF16))
    y_dil = jnp.concatenate(heads, axis=-1)
    lift_dil = jnp.dot(y_dil, wud_ref[...], preferred_element_type=F32)
    lift_moba = jnp.dot(ym_ref[...], wum_ref[...], preferred_element_type=F32)
    merged = gd_ref[...].astype(F32) * lift_dil + gm_ref[...].astype(F32) * lift_moba
    return x + jnp.dot(merged.astype(BF16), wo_ref[...], preferred_element_type=F32)


def _cross_attended(x, g_ref, wq_ref, kv_ref, wo_ref):
    h = _rms(x, g_ref[...]).astype(BF16)
    q = jnp.dot(h, wq_ref[...], preferred_element_type=F32).astype(BF16)
    heads = []
    for hd in range(MEM_HEADS):
        k = kv_ref[:, hd * HEAD_DIM:(hd + 1) * HEAD_DIM]
        v = kv_ref[:, MEM_WIDTH + hd * HEAD_DIM:MEM_WIDTH + (hd + 1) * HEAD_DIM]
        s = lax.dot_general(q[:, hd * HEAD_DIM:(hd + 1) * HEAD_DIM], k, _NT,
                            preferred_element_type=F32) * SCALE
        p = jnp.exp(s - jnp.max(s, axis=-1, keepdims=True))
        den = jnp.sum(p, axis=-1, keepdims=True)
        heads.append((jnp.dot(p.astype(BF16), v, preferred_element_type=F32) / den).astype(BF16))
    o = jnp.concatenate(heads, axis=-1)
    return x + jnp.dot(o, wo_ref[...], preferred_element_type=F32)


def _merge_cross_body(x_ref, o0_ref, o1_ref, o2_ref, l0_ref, l1_ref, l2_ref, ym_ref, gd_ref, gm_ref,
                      wud_ref, wum_ref, wo_ref, gc_ref, wq_ref, kv_ref, wom_ref, out_ref):
    x1 = _merged_branches(x_ref[...], (o0_ref, o1_ref, o2_ref), (l0_ref, l1_ref, l2_ref),
                          ym_ref, gd_ref, gm_ref, wud_ref, wum_ref, wo_ref)
    out_ref[...] = _cross_attended(x1, gc_ref, wq_ref, kv_ref, wom_ref)


def _merge_cross(x, dil_outs, dil_lses, y_moba, gates, w_up_dil, w_up_moba, w_out,
                 g_cross, w_q, kv, w_o, *, tm):
    t, d = x.shape
    s = dil_outs[0].shape[2]
    mem_len = kv.shape[1]
    assert t % tm == 0 and s % tm == 0
    per_batch = s // tm
    row = lambda w: pl.BlockSpec((tm, w), lambda i: (i, 0))
    dil = pl.BlockSpec((None, DIL_HEADS_PER_GROUP, tm, HEAD_DIM),
                       lambda i: (i // per_batch, 0, i % per_batch, 0))
    return pl.pallas_call(
        _merge_cross_body,
        grid=(t // tm,),
        in_specs=[row(d)] + [dil] * 6 + [
            row(MOBA_WIDTH),
            pl.BlockSpec((tm, d), lambda i: (i, 0)),
            pl.BlockSpec((tm, d), lambda i: (i, 1)),
            _resident((DIL_OUT, d)), _resident((MOBA_WIDTH, d)), _resident((d, d)),
            _resident((1, d)), _resident((d, MEM_WIDTH)),
            pl.BlockSpec((None, mem_len, 2 * MEM_WIDTH), lambda i: (i // per_batch, 0, 0)),
            _resident((MEM_WIDTH, d)),
        ],
        out_specs=row(d),
        out_shape=jax.ShapeDtypeStruct((t, d), F32),
        compiler_params=_params("parallel"),
        name="merge_out_proj_cross_attn",
    )(x, *dil_outs, *dil_lses, y_moba, gates, gates, w_up_dil, w_up_moba, w_out,
      g_cross.reshape(1, d), w_q, kv, w_o)


ROUTE_COLS = 8
MOE_ROW_TILE = 256


def _route_body(x_ref, g_ref, wr_ref, br_ref, info_ref, counts_ref, run_ref, tri_ref):
    tm = x_ref.shape[0]
    n_route = N_GROUPS + N_EXPERTS
    lane = lax.broadcasted_iota(jnp.int32, (tm, n_route), 1).astype(F32)

    @pl.when(pl.program_id(0) == 0)
    def _():
        run_ref[...] = jnp.zeros_like(run_ref)
        earlier = (lax.broadcasted_iota(jnp.int32, (tm, tm), 0)
                   > lax.broadcasted_iota(jnp.int32, (tm, tm), 1))
        tri_ref[...] = jnp.where(earlier, 1.0, 0.0).astype(BF16)

    t = _rms(x_ref[...], g_ref[...])
    logits = _dot_split3(t, wr_ref[...]) + br_ref[...]
    none = float(n_route)
    glog = jnp.where(lane < N_GROUPS, logits, NEG_INF)
    gmax = jnp.max(glog, axis=-1, keepdims=True)
    gsel = jnp.min(jnp.where(glog == gmax, lane, none), axis=-1, keepdims=True)
    pg = 1.0 / jnp.sum(jnp.exp(glog - gmax), axis=-1, keepdims=True)
    first = N_GROUPS + gsel * EXPERTS_PER_GROUP
    in_group = (lane >= first) & (lane < first + EXPERTS_PER_GROUP)
    elog = jnp.where(in_group, logits, NEG_INF)
    top1 = jnp.max(elog, axis=-1, keepdims=True)
    i1 = jnp.min(jnp.where(elog == top1, lane, none), axis=-1, keepdims=True)
    rest = jnp.where(lane == i1, NEG_INF, elog)
    top2 = jnp.max(rest, axis=-1, keepdims=True)
    i2 = jnp.min(jnp.where(rest == top2, lane, none), axis=-1, keepdims=True)
    e2 = jnp.exp(top2 - top1)
    w1 = pg / (1.0 + e2)
    w2 = pg * e2 / (1.0 + e2)

    hit1 = lane == i1
    hit2 = lane == i2
    assigned = jnp.where(hit1 | hit2, 1.0, 0.0)
    before = jnp.dot(tri_ref[...], assigned.astype(BF16), preferred_element_type=F32) + run_ref[...]
    rank1 = jnp.sum(jnp.where(hit1, before, 0.0), axis=-1, keepdims=True)
    rank2 = jnp.sum(jnp.where(hit2, before, 0.0), axis=-1, keepdims=True)
    run_ref[...] += jnp.sum(assigned, axis=0, keepdims=True)
    counts_ref[...] = run_ref[...]

    col = lax.broadcasted_iota(jnp.int32, (tm, ROUTE_COLS), 1)
    fields = (i1 - N_GROUPS, i2 - N_GROUPS, rank1, rank2, w1, w2)
    info = jnp.zeros((tm, ROUTE_COLS), F32)
    for c, field in enumerate(fields):
        info = jnp.where(col == c, field, info)
    info_ref[...] = info


def _route(x, g, w_route, b_route, *, tm):
    t, d = x.shape
    n_route = N_GROUPS + N_EXPERTS
    assert t % tm == 0
    return pl.pallas_call(
        _route_body,
        grid=(t // tm,),
        in_specs=[
            pl.BlockSpec((tm, d), lambda i: (i, 0)),
            pl.BlockSpec((1, d), lambda i: (0, 0)),
            pl.BlockSpec((d, n_route), lambda i: (0, 0)),
            pl.BlockSpec((1, n_route), lambda i: (0, 0)),
        ],
        out_specs=[pl.BlockSpec((tm, ROUTE_COLS), lambda i: (i, 0)),
                   pl.BlockSpec((1, n_route), lambda i: (0, 0))],
        out_shape=[jax.ShapeDtypeStruct((t, ROUTE_COLS), F32),
                   jax.ShapeDtypeStruct((1, n_route), F32)],
        scratch_shapes=[pltpu.VMEM((1, n_route), F32), pltpu.VMEM((tm, tm), BF16)],
        compiler_params=_params("arbitrary"),
        name="moe_route",
    )(x, g.reshape(1, d), w_route, b_route.reshape(1, n_route))


def _row_copies_wait(src_rows, dst_rows, sem):
    pltpu.make_async_copy(src_rows, dst_rows, sem).wait()


def _slot_block(tm):
    return pl.BlockSpec((MOE_TOPK, tm), lambda i, *_: (0, i), memory_space=pltpu.SMEM)


def _dispatch_body(fill_ref, pos_ref, x_ref, g_ref, sorted_ref, t_ref, zero_ref, sems, fill_sem):
    step = pl.program_id(0)
    tm = x_ref.shape[0]
    tr = zero_ref.shape[0]
    slot = lax.rem(step, 2)

    @pl.when(pl.program_id(0) == 0)
    def _():
        zero_ref[...] = jnp.zeros_like(zero_ref)

        def fill_copy(tile):
            return pltpu.make_async_copy(zero_ref, sorted_ref.at[pl.ds(tile * tr, tr)], fill_sem)

        def start(idx, carry):
            pl.when(fill_ref[idx] >= 0)(lambda: fill_copy(fill_ref[idx]).start())
            return carry

        def finish(idx, carry):
            pl.when(fill_ref[idx] >= 0)(lambda: fill_copy(fill_ref[idx]).wait())
            return carry

        lax.fori_loop(0, fill_ref.shape[0], start, 0)
        lax.fori_loop(0, fill_ref.shape[0], finish, 0)

    t_ref[slot] = _rms(x_ref[...], g_ref[...])

    def issue(r, carry):
        for k in range(MOE_TOPK):
            dst = pos_ref[k, r]
            pltpu.make_async_copy(t_ref.at[slot, pl.ds(r, 1)], sorted_ref.at[pl.ds(dst, 1)],
                                  sems.at[slot]).start()
        return carry

    def copies_done(buf):
        for _ in range(MOE_TOPK):
            _row_copies_wait(t_ref.at[buf], sorted_ref.at[pl.ds(0, tm)], sems.at[buf])

    lax.fori_loop(0, tm, issue, 0, unroll=8)
    pl.when(step > 0)(lambda: copies_done(1 - slot))
    pl.when(step == pl.num_programs(0) - 1)(lambda: copies_done(slot))


def _dispatch(x, g, pos, fill_tiles, n_rows, *, tm):
    t, d = x.shape
    assert t % tm == 0
    return pl.pallas_call(
        _dispatch_body,
        grid_spec=pltpu.PrefetchScalarGridSpec(
            num_scalar_prefetch=1,
            grid=(t // tm,),
            in_specs=[_slot_block(tm),
                      pl.BlockSpec((tm, d), lambda i, fill: (i, 0)),
                      pl.BlockSpec((1, d), lambda i, fill: (0, 0))],
            out_specs=pl.BlockSpec(memory_space=pl.ANY),
            scratch_shapes=[pltpu.VMEM((2, tm, d), F32), pltpu.VMEM((MOE_ROW_TILE, d), F32),
                            pltpu.SemaphoreType.DMA((2,)), pltpu.SemaphoreType.DMA(())],
        ),
        out_shape=jax.ShapeDtypeStruct((n_rows, d), F32),
        compiler_params=pltpu.CompilerParams(dimension_semantics=("arbitrary",),
                                             vmem_limit_bytes=VMEM_LIMIT_BYTES,
                                             disable_bounds_checks=True),
        name="moe_dispatch",
    )(fill_tiles, pos, x, g.reshape(1, d))


def _experts_body(tile_expert_ref, n_tiles_ref, x_ref, wg_ref, wu_ref, wd_ref, y_ref):
    del tile_expert_ref
    in_use = pl.program_id(0) < n_tiles_ref[0]

    @pl.when(in_use)
    def _():
        t = x_ref[...].astype(BF16)
        gate = jnp.dot(t, wg_ref[...].astype(BF16), preferred_element_type=F32)
        up = jnp.dot(t, wu_ref[...].astype(BF16), preferred_element_type=F32)
        a = jax.nn.silu(gate) * up
        y_ref[...] = jnp.dot(a.astype(BF16), wd_ref[...].astype(BF16), preferred_element_type=F32)

    @pl.when(jnp.logical_not(in_use))
    def _():
        y_ref[...] = jnp.zeros_like(y_ref)


def _experts(sorted_rows, tile_expert, n_tiles, w_gate, w_up, w_down):
    p, d = sorted_rows.shape
    ff = w_gate.shape[-1]
    tr = MOE_ROW_TILE
    assert p % tr == 0
    used = lambda i, nt: jnp.minimum(i, nt[0] - 1)
    return pl.pallas_call(
        _experts_body,
        grid_spec=pltpu.PrefetchScalarGridSpec(
            num_scalar_prefetch=2,
            grid=(p // tr,),
            in_specs=[pl.BlockSpec((tr, d), lambda i, te, nt: (used(i, nt), 0)),
                      pl.BlockSpec((None, d, ff), lambda i, te, nt: (te[used(i, nt)], 0, 0)),
                      pl.BlockSpec((None, d, ff), lambda i, te, nt: (te[used(i, nt)], 0, 0)),
                      pl.BlockSpec((None, ff, d), lambda i, te, nt: (te[used(i, nt)], 0, 0))],
            out_specs=pl.BlockSpec((tr, d), lambda i, te, nt: (i, 0)),
        ),
        out_shape=jax.ShapeDtypeStruct((p, d), F32),
        compiler_params=_params("arbitrary"),
        name="moe_experts",
    )(tile_expert, n_tiles, sorted_rows, w_gate, w_up, w_down)


def _combine_body(pos_ref, pos_next_ref, x_ref, info_ref, gf_ref, y_sorted_ref, out_ref,
                  rows_ref, sems):
    step = pl.program_id(0)
    tm = x_ref.shape[0]
    slot = lax.rem(step, 2)

    def fetch(table_ref, buf):
        def issue(r, carry):
            for k in range(MOE_TOPK):
                src = table_ref[k, r]
                pltpu.make_async_copy(y_sorted_ref.at[pl.ds(src, 1)],
                                      rows_ref.at[buf, k, pl.ds(r, 1)], sems.at[buf]).start()
            return carry
        lax.fori_loop(0, tm, issue, 0, unroll=8)

    pl.when(step == 0)(lambda: fetch(pos_ref, 0))
    pl.when(step + 1 < pl.num_programs(0))(lambda: fetch(pos_next_ref, 1 - slot))
    for k in range(MOE_TOPK):
        _row_copies_wait(y_sorted_ref.at[pl.ds(0, tm)], rows_ref.at[slot, k], sems.at[slot])

    info = info_ref[...]
    y = info[:, 4:5] * rows_ref[slot, 0] + info[:, 5:6] * rows_ref[slot, 1]
    out_ref[...] = _rms(x_ref[...] + y, gf_ref[...])


def _combine(x, info, pos, y_sorted, g_final, *, tm):
    t, d = x.shape
    assert t % tm == 0
    return pl.pallas_call(
        _combine_body,
        grid=(t // tm,),
        in_specs=[_slot_block(tm),
                  pl.BlockSpec((MOE_TOPK, tm), lambda i: (0, jnp.minimum(i + 1, t // tm - 1)),
                               memory_space=pltpu.SMEM),
                  pl.BlockSpec((tm, d), lambda i: (i, 0)),
                  pl.BlockSpec((tm, ROUTE_COLS), lambda i: (i, 0)),
                  pl.BlockSpec((1, d), lambda i: (0, 0)),
                  pl.BlockSpec(memory_space=pl.ANY)],
        out_specs=pl.BlockSpec((tm, d), lambda i: (i, 0)),
        scratch_shapes=[pltpu.VMEM((2, MOE_TOPK, tm, d), F32), pltpu.SemaphoreType.DMA((2,))],
        out_shape=jax.ShapeDtypeStruct((t, d), F32),
        compiler_params=pltpu.CompilerParams(dimension_semantics=("arbitrary",),
                                             vmem_limit_bytes=VMEM_LIMIT_BYTES,
                                             disable_bounds_checks=True),
        name="moe_combine_final_norm",
    )(pos, pos, x, info, g_final.reshape(1, d), y_sorted)


def _moe(x, g, w_route, b_route, w_gate, w_up, w_down, g_final):
    t, d = x.shape
    tr = MOE_ROW_TILE
    assert (MOE_TOPK * t) % tr == 0
    info, counts = _route(x, g, w_route, b_route, tm=min(512, t))

    expert = info[:, 0:MOE_TOPK].astype(jnp.int32)
    rank = info[:, MOE_TOPK:2 * MOE_TOPK].astype(jnp.int32)
    count = counts[0, N_GROUPS:].astype(jnp.int32)
    seg_tiles = (count + (tr - 1)) // tr
    seg_end = jnp.cumsum(seg_tiles)
    seg_start_row = (seg_end - seg_tiles) * tr
    pos = (seg_start_row[expert] + rank).T
    max_tiles = (MOE_TOPK * t) // tr + N_EXPERTS
    tile_id = jnp.arange(max_tiles, dtype=jnp.int32)
    tile_expert = jnp.minimum(
        jnp.sum((seg_end[None, :] <= tile_id[:, None]).astype(jnp.int32), axis=1), N_EXPERTS - 1)
    n_tiles = seg_end[-1:].astype(jnp.int32)

    tail = n_tiles + jnp.arange(N_EXPERTS, dtype=jnp.int32)
    fill_tiles = jnp.concatenate([jnp.where(seg_tiles > 0, seg_end - 1, -1),
                                  jnp.where(tail < max_tiles, tail, -1)]).astype(jnp.int32)

    sorted_rows = _dispatch(x, g, pos, fill_tiles, max_tiles * tr, tm=min(512, t))
    y_sorted = _experts(sorted_rows, tile_expert, n_tiles, w_gate, w_up, w_down)
    return _combine(x, info, pos, y_sorted, g_final, tm=min(512, t))


def _layer(x, mem, attn_norm, w_in, w_up_dil, w_up_moba, w_branch_gate, w_out, cross_norm,
           mem_norm, w_q_mem, w_kv_mem, w_o_mem):
    b, s, d = x.shape
    t = b * s
    xt = x.reshape(t, d)
    h = _norm(xt, attn_norm, tm=min(512, t), name="attn_norm")
    proj, gates = _in_proj(h, w_in, w_branch_gate, tm=min(2048, t))
    proj = proj.reshape(b, s, IN_WIDTH)
    dil_slopes = _alibi_slopes(DIL_HEADS)
    dil = [_dilated_group(proj, dil_slopes, g, dilation)
           for g, (_, dilation) in enumerate(DIL_CONFIGS)]
    y_moba = _moba(proj, _alibi_slopes(MOBA_HEADS))
    mem_len = mem.shape[1]
    kv = _norm_matmul(mem.reshape(b * mem_len, d), mem_norm, w_kv_mem.astype(BF16),
                      tm=b * mem_len, tn=512, name="norm_mem_kv").reshape(b, mem_len, 2 * MEM_WIDTH)
    return _merge_cross(xt, [o for o, _ in dil], [l for _, l in dil], y_moba, gates,
                        w_up_dil.astype(BF16), w_up_moba.astype(BF16), w_out.astype(BF16),
                        cross_norm, w_q_mem.astype(BF16), kv, w_o_mem.astype(BF16), tm=min(256, s))


def kernel(x, mem, attn_norm, w_in, w_up_dil, w_up_moba, w_branch_gate, w_out, cross_norm, mem_norm,
           w_q_mem, w_kv_mem, w_o_mem, ffn_norm, w_router_group, b_router_group, w_router_expert,
           b_router_expert, w_expert_gate, w_expert_up, w_expert_down, final_norm):
    b, s, d = x.shape
    depth = attn_norm.shape[0]
    assert depth == 1, "the final norm is fused into the last layer's MoE call"
    l = 0
    x2 = _layer(x, mem, attn_norm[l], w_in[l], w_up_dil[l], w_up_moba[l], w_branch_gate[l], w_out[l],
                cross_norm[l], mem_norm[l], w_q_mem[l], w_kv_mem[l], w_o_mem[l])
    w_route = jnp.concatenate([w_router_group[l], w_router_expert[l]], axis=1)
    b_route = jnp.concatenate([b_router_group[l], b_router_expert[l]], axis=0)
    out = _moe(x2, ffn_norm[l], w_route, b_route, w_expert_gate[l], w_expert_up[l],
               w_expert_down[l], final_norm)
    return out.reshape(b, s, d)
```
